```python
import math
import jax, jax.numpy as jnp
from jax import lax
import numpy as np

D_MODEL = 1024
BATCH = 8
SEQ = 4096
DEPTH = 1
DEC_BATCH = 128
DEC_SEQ = 1
PAST_LEN = 16384
PAGE_SIZE = 128

D_INNER = 2 * D_MODEL
SSM_HEAD_DIM = 64
SSM_HEADS = D_INNER // SSM_HEAD_DIM
SSM_GROUPS = 4
SSM_HPG = SSM_HEADS // SSM_GROUPS
D_STATE = 128
CONV_W = 4
CONV_DIM = D_INNER + 2 * SSM_GROUPS * D_STATE
CHUNK = 128
ATTN_HEAD_DIM = 64
N_Q_HEADS = D_MODEL // ATTN_HEAD_DIM
N_KV_HEADS = 4
Q_PER_KV = N_Q_HEADS // N_KV_HEADS
D_ATTN = N_Q_HEADS * ATTN_HEAD_DIM
D_KV = N_KV_HEADS * ATTN_HEAD_DIM
WINDOW = 128
N_EXPERTS = 32
TOP_K = 4
D_FF = D_MODEL
SWIGLU_LIMIT = 7.0
SWIGLU_ALPHA = 1.702
EPS = 1e-5
NEG_BIG = -1e30
SPLITS = (D_INNER, CONV_DIM, SSM_HEADS, D_ATTN, D_KV, D_KV, D_MODEL, D_MODEL)
D_IN_PROJ = sum(SPLITS)

kernel_name = "hybrid_ssd_swa_moe_decode_step"


def rmsnorm(x, w):
    xf = x.astype(jnp.float32)
    y = xf * lax.rsqrt(jnp.mean(xf * xf, axis=-1, keepdims=True) + EPS)
    return (y * w.astype(jnp.float32)).astype(x.dtype)


def alibi_slopes():
    h = jnp.arange(1, N_Q_HEADS + 1, dtype=jnp.float32)
    return jnp.exp2(-8.0 * h / N_Q_HEADS).reshape(N_KV_HEADS, Q_PER_KV)


def causal_conv(x_hist, w, bias, L):
    out = bias + x_hist[:, 0:L] * w[0]
    for j in range(1, CONV_W):
        out = out + x_hist[:, j:j + L] * w[j]
    return out


def ssd_scan(x, dt, a_log, bm, cm, d_skip, h0):
    f32 = jnp.float32
    b, L = x.shape[:2]
    q = CHUNK if L % CHUNK == 0 else L
    nc = L // q
    A = -jnp.exp(a_log.astype(f32))
    la = (dt * A).reshape(b, nc, q, SSM_GROUPS, SSM_HPG)
    xf = x.astype(f32).reshape(b, nc, q, SSM_GROUPS, SSM_HPG, SSM_HEAD_DIM)
    xdt = xf * dt.reshape(b, nc, q, SSM_GROUPS, SSM_HPG)[..., None]
    bm = bm.astype(f32).reshape(b, nc, q, SSM_GROUPS, D_STATE)
    cm = cm.astype(f32).reshape(b, nc, q, SSM_GROUPS, D_STATE)
    cum = jnp.cumsum(la, axis=2)
    seg = cum[:, :, :, None] - cum[:, :, None, :]
    causal = jnp.tril(jnp.ones((q, q), dtype=bool))[:, :, None, None]
    decay = jnp.where(causal, jnp.exp(jnp.where(causal, seg, 0.0)), 0.0)
    cb = jnp.einsum("bclgn,bcsgn->bclsg", cm, bm)
    y_diag = jnp.einsum("bclsgh,bcsghp->bclghp", cb[..., None] * decay, xdt)
    decay_to_end = jnp.exp(cum[:, :, -1:] - cum)
    states = jnp.einsum("bcsgn,bcsgh,bcsghp->bcghpn", bm, decay_to_end, xdt)
    chunk_decay = jnp.exp(cum[:, :, -1])
    h_init = h0.astype(f32).reshape(b, SSM_GROUPS, SSM_HPG, SSM_HEAD_DIM, D_STATE)

    def step(h, inp):
        st, dec = inp
        return h * dec[..., None, None] + st, h

    h_final, h_starts = lax.scan(step, h_init, (jnp.moveaxis(states, 1, 0), jnp.moveaxis(chunk_decay, 1, 0)))
    h_starts = jnp.moveaxis(h_starts, 0, 1)
    y_off = jnp.einsum("bclgn,bcghpn,bclgh->bclghp", cm, h_starts, jnp.exp(cum))
    d = d_skip.astype(f32).reshape(SSM_GROUPS, SSM_HPG, 1)
    y = y_diag + y_off + xf * d
    return y.reshape(b, L, D_INNER), h_final.reshape(b, SSM_HEADS, SSM_HEAD_DIM, D_STATE)


def ssm_branch(z, conv_in, dt_raw, conv_w, conv_b, dt_bias, a_log, d_skip, norm_w, h0):
    f32 = jnp.float32
    b, L, _ = z.shape
    xbc = jax.nn.silu(causal_conv(conv_in, conv_w, conv_b, L))
    xs, bm, cm = jnp.split(xbc, [D_INNER, D_INNER + SSM_GROUPS * D_STATE], axis=-1)
    dt = jax.nn.softplus(dt_raw.astype(f32) + dt_bias.astype(f32))
    y, h = ssd_scan(xs.reshape(b, L, SSM_HEADS, SSM_HEAD_DIM), dt, a_log,
                    bm.reshape(b, L, SSM_GROUPS, D_STATE), cm.reshape(b, L, SSM_GROUPS, D_STATE), d_skip, h0)
    y = y * jax.nn.silu(z.astype(f32))
    yg = y.reshape(b, L, SSM_GROUPS, D_INNER // SSM_GROUPS)
    yg = yg * lax.rsqrt(jnp.mean(yg * yg, axis=-1, keepdims=True) + EPS)
    y = yg.reshape(b, L, D_INNER) * norm_w.astype(f32)
    return y.astype(z.dtype), h.astype(h0.dtype)


def sink_attend(q, k, v, dist, valid, sinks):
    f32 = jnp.float32
    s = jnp.einsum("bnqkgd,bnskd->bnkgqs", q, k).astype(f32) * (ATTN_HEAD_DIM ** -0.5)
    s = s - alibi_slopes()[:, :, None, None] * dist.astype(f32)
    s = jnp.where(valid[None, :, None, None], s, NEG_BIG)
    sink = jnp.broadcast_to(sinks.astype(f32).reshape(N_KV_HEADS, Q_PER_KV)[:, :, None, None], s.shape[:-1] + (1,))
    p = jax.nn.softmax(jnp.concatenate([s, sink], axis=-1), axis=-1)[..., :-1]
    return jnp.einsum("bnkgqs,bnskd->bnqkgd", p.astype(v.dtype), v)


def swa_prompt(q, k, v, sinks):
    b, L = q.shape[:2]
    nb = L // WINDOW
    qb = q.reshape(b, nb, WINDOW, N_KV_HEADS, Q_PER_KV, ATTN_HEAD_DIM)
    kb = k.reshape(b, nb, WINDOW, N_KV_HEADS, ATTN_HEAD_DIM)
    vb = v.reshape(b, nb, WINDOW, N_KV_HEADS, ATTN_HEAD_DIM)
    pad = ((0, 0), (1, 0), (0, 0), (0, 0), (0, 0))
    kk = jnp.concatenate([jnp.pad(kb[:, :-1], pad), kb], axis=2)
    vv = jnp.concatenate([jnp.pad(vb[:, :-1], pad), vb], axis=2)
    i = jnp.arange(WINDOW)[:, None]
    j = jnp.arange(2 * WINDOW)[None, :]
    dist = i + WINDOW - j
    key_pos = jnp.arange(nb)[:, None, None] * WINDOW - WINDOW + j[None]
    valid = (dist >= 0)[None] & (dist < WINDOW)[None] & (key_pos >= 0)
    out = sink_attend(qb, kk, vv, dist, valid, sinks)
    return out.reshape(b, L, D_ATTN)


def swa_sample(q, kk, vv, sinks):
    b, T = q.shape[:2]
    i = jnp.arange(T)[:, None]
    j = jnp.arange(WINDOW + T)[None, :]
    dist = i + WINDOW - j
    valid = ((dist >= 0) & (dist < WINDOW))[None]
    out = sink_attend(q[:, None], kk[:, None], vv[:, None], dist, valid, sinks)
    return out.reshape(b, T, D_ATTN)


def moe(xn, w_router, b_router, w_gate, b_gate, w_up, b_up, w_down, b_down):
    f32 = jnp.float32
    shape = xn.shape
    t = xn.reshape(-1, D_MODEL)
    logits = (t @ w_router + b_router).astype(f32)
    top_v, top_i = lax.top_k(logits, TOP_K)
    probs = jax.nn.softmax(top_v, axis=-1)
    combine = jnp.sum(jax.nn.one_hot(top_i, N_EXPERTS, dtype=f32) * probs[..., None], axis=1)
    out = jnp.zeros(t.shape, f32)
    for e in range(N_EXPERTS):
        g = jnp.minimum((t @ w_gate[e] + b_gate[e]).astype(f32), SWIGLU_LIMIT)
        u = jnp.clip((t @ w_up[e] + b_up[e]).astype(f32), -SWIGLU_LIMIT, SWIGLU_LIMIT)
        act = ((u + 1.0) * g * jax.nn.sigmoid(SWIGLU_ALPHA * g)).astype(t.dtype)
        out = out + combine[:, e:e + 1] * (act @ w_down[e] + b_down[e]).astype(f32)
    return out.astype(xn.dtype).reshape(shape)


def layer_forward(x, conv_hist, h0, kv_buf, params, prompt):
    (attn_norm_w, w_in, conv_w, conv_b, dt_bias, a_log, d_skip, ssm_norm_w, attn_sinks,
     w_ssm_proj, w_attn_proj, w_o, ffn_norm_w, w_router, b_router,
     w_gate, b_gate, w_up, b_up, w_down, b_down) = params
    b, L, _ = x.shape
    xn = rmsnorm(x, attn_norm_w)
    proj = jnp.einsum("bsd,de->bse", xn, w_in)
    idx = np.cumsum(SPLITS)[:-1].tolist()
    z, xbc, dt_raw, q, k, v, g_ssm, g_attn = jnp.split(proj, idx, axis=-1)
    conv_in = jnp.concatenate([conv_hist, xbc], axis=1)
    y_ssm, h_new = ssm_branch(z, conv_in, dt_raw, conv_w, conv_b, dt_bias, a_log, d_skip, ssm_norm_w, h0)
    new_conv = conv_in[:, -(CONV_W - 1):]
    q = q.reshape(b, L, N_KV_HEADS, Q_PER_KV, ATTN_HEAD_DIM)
    k = k.reshape(b, L, N_KV_HEADS, ATTN_HEAD_DIM)
    v = v.reshape(b, L, N_KV_HEADS, ATTN_HEAD_DIM)
    if prompt:
        y_attn = swa_prompt(q, k, v, attn_sinks)
        new_k, new_v = k[:, -WINDOW:], v[:, -WINDOW:]
    else:
        k_buf, v_buf = kv_buf
        kk = jnp.concatenate([k_buf, k], axis=1)
        vv = jnp.concatenate([v_buf, v], axis=1)
        y_attn = swa_sample(q, kk, vv, attn_sinks)
        new_k, new_v = kk[:, -WINDOW:], vv[:, -WINDOW:]
    merged = (jax.nn.sigmoid(g_ssm) * (y_ssm @ w_ssm_proj)
              + jax.nn.sigmoid(g_attn) * (y_attn @ w_attn_proj))
    h = x + merged @ w_o
    h = h + moe(rmsnorm(h, ffn_norm_w), w_router, b_router, w_gate, b_gate, w_up, b_up, w_down, b_down)
    return h, (new_conv, h_new, new_k, new_v)


def setup_inputs(seed: int = 0) -> dict:
    key = jax.random.key(seed)
    ks = jax.random.split(key, 32)
    f32 = jnp.float32
    nrm = lambda k, s, sc: jax.random.normal(k, s, f32) * sc
    dt0 = jnp.exp(jax.random.uniform(ks[8], (DEPTH, SSM_HEADS), f32) * (math.log(0.1) - math.log(0.001)) + math.log(0.001))
    return {
        "x_prompt": nrm(ks[0], (BATCH, SEQ, D_MODEL), 1.0),
        "x_sample": nrm(ks[1], (DEC_BATCH, DEC_SEQ, D_MODEL), 1.0),
        "state_conv": nrm(ks[2], (DEPTH, DEC_BATCH, CONV_W - 1, CONV_DIM), 1.0),
        "state_ssm": nrm(ks[3], (DEPTH, DEC_BATCH, SSM_HEADS, SSM_HEAD_DIM, D_STATE), 0.1),
        "cache_swa_k": nrm(ks[4], (DEPTH, DEC_BATCH, WINDOW, N_KV_HEADS, ATTN_HEAD_DIM), 1.0),
        "cache_swa_v": nrm(ks[5], (DEPTH, DEC_BATCH, WINDOW, N_KV_HEADS, ATTN_HEAD_DIM), 1.0),
        "attn_norm_w": 1.0 + nrm(ks[6], (DEPTH, D_MODEL), 0.02),
        "w_in": nrm(ks[7], (DEPTH, D_MODEL, D_IN_PROJ), D_MODEL ** -0.5),
        "conv_w": nrm(ks[9], (DEPTH, CONV_W, CONV_DIM), CONV_W ** -0.5),
        "conv_b": nrm(ks[10], (DEPTH, CONV_DIM), 0.01),
        "dt_bias": dt0 + jnp.log(-jnp.expm1(-dt0)),
        "a_log": jnp.log(jax.random.uniform(ks[11], (DEPTH, SSM_HEADS), f32, 1.0, 16.0)),
        "d_skip": 1.0 + nrm(ks[12], (DEPTH, SSM_HEADS), 0.02),
        "ssm_norm_w": 1.0 + nrm(ks[13], (DEPTH, D_INNER), 0.02),
        "attn_sinks": nrm(ks[14], (DEPTH, N_Q_HEADS), 1.0),
        "w_ssm_proj": nrm(ks[15], (DEPTH, D_INNER, D_MODEL), D_INNER ** -0.5),
        "w_attn_proj": nrm(ks[16], (DEPTH, D_ATTN, D_MODEL), D_ATTN ** -0.5),
        "w_o": nrm(ks[17], (DEPTH, D_MODEL, D_MODEL), D_MODEL ** -0.5),
        "ffn_norm_w": 1.0 + nrm(ks[18], (DEPTH, D_MODEL), 0.02),
        "w_router": nrm(ks[19], (DEPTH, D_MODEL, N_EXPERTS), D_MODEL ** -0.5),
        "b_router": nrm(ks[20], (DEPTH, N_EXPERTS), 0.01),
        "w_gate": nrm(ks[21], (DEPTH, N_EXPERTS, D_MODEL, D_FF), D_MODEL ** -0.5),
        "b_gate": nrm(ks[22], (DEPTH, N_EXPERTS, D_FF), 0.01),
        "w_up": nrm(ks[23], (DEPTH, N_EXPERTS, D_MODEL, D_FF), D_MODEL ** -0.5),
        "b_up": nrm(ks[24], (DEPTH, N_EXPERTS, D_FF), 0.01),
        "w_down": nrm(ks[25], (DEPTH, N_EXPERTS, D_FF, D_MODEL), D_FF ** -0.5),
        "b_down": nrm(ks[26], (DEPTH, N_EXPERTS, D_MODEL), 0.01),
        "final_norm_w": 1.0 + nrm(ks[27], (D_MODEL,), 0.02),
    }


def reference(x_prompt, x_sample, state_conv, state_ssm, cache_swa_k, cache_swa_v,
              attn_norm_w, w_in, conv_w, conv_b, dt_bias, a_log, d_skip, ssm_norm_w, attn_sinks,
              w_ssm_proj, w_attn_proj, w_o, ffn_norm_w, w_router, b_router,
              w_gate, b_gate, w_up, b_up, w_down, b_down, final_norm_w):
    layer_weights = (attn_norm_w, w_in, conv_w, conv_b, dt_bias, a_log, d_skip, ssm_norm_w, attn_sinks,
                     w_ssm_proj, w_attn_proj, w_o, ffn_norm_w, w_router, b_router,
                     w_gate, b_gate, w_up, b_up, w_down, b_down)
    yp, ys = x_prompt, x_sample
    conv0 = jnp.zeros((BATCH, CONV_W - 1, CONV_DIM), x_prompt.dtype)
    h00 = jnp.zeros((BATCH, SSM_HEADS, SSM_HEAD_DIM, D_STATE), state_ssm.dtype)
    sp_all, ss_all = [], []
    for l in range(DEPTH):
        p = tuple(a[l] for a in layer_weights)
        yp, sp = layer_forward(yp, conv0, h00, None, p, True)
        ys, ss = layer_forward(ys, state_conv[l], state_ssm[l], (cache_swa_k[l], cache_swa_v[l]), p, False)
        sp_all.append(sp)
        ss_all.append(ss)
    y_prompt = rmsnorm(yp, final_norm_w)
    y_sample = rmsnorm(ys, final_norm_w)
    conv_p = jnp.stack([s[0] for s in sp_all])
    ssm_p = jnp.stack([s[1] for s in sp_all])
    k_p = jnp.stack([s[2] for s in sp_all])
    v_p = jnp.stack([s[3] for s in sp_all])
    conv_s = jnp.stack([s[0] for s in ss_all])
    ssm_s = jnp.stack([s[1] for s in ss_all])
    k_s = jnp.stack([s[2] for s in ss_all])
    v_s = jnp.stack([s[3] for s in ss_all])
    return (y_prompt, y_sample, conv_p, ssm_p, k_p, v_p, conv_s, ssm_s, k_s, v_s)
```

```python
import functools

import jax
import jax.numpy as jnp
from jax import lax
from jax.experimental import pallas as pl
from jax.experimental.pallas import tpu as pltpu

F32, BF16, I32 = jnp.float32, jnp.bfloat16, jnp.int32

D_MODEL = 1024
D_INNER = 2 * D_MODEL
SSM_HEAD_DIM = 64
SSM_HEADS = D_INNER // SSM_HEAD_DIM
SSM_GROUPS = 4
SSM_HPG = SSM_HEADS // SSM_GROUPS
D_STATE = 128
CONV_W = 4
CONV_DIM = D_INNER + 2 * SSM_GROUPS * D_STATE
CHUNK = 128
ATTN_HEAD_DIM = 64
N_Q_HEADS = D_MODEL // ATTN_HEAD_DIM
N_KV_HEADS = 4
Q_PER_KV = N_Q_HEADS // N_KV_HEADS
D_ATTN = N_Q_HEADS * ATTN_HEAD_DIM
D_KV = N_KV_HEADS * ATTN_HEAD_DIM
WINDOW = 128
TOP_K = 4
D_FF = D_MODEL
SWIGLU_LIMIT = 7.0
SWIGLU_ALPHA = 1.702
EPS = 1e-5
NEG_BIG = -1e30

LANES = 128
SUBLANES = 8
GROUP_W = D_INNER // SSM_GROUPS
VMEM_LIMIT = 56 * 1024 * 1024

NT = (((1,), (1,)), ((), ()))
TN = (((0,), (0,)), ((), ()))


def _const_spec(shape):
    return pl.BlockSpec(shape, lambda *_: (0,) * len(shape))


def _resident_spec(shape):
    return pl.BlockSpec(shape, lambda *_: (0,) * len(shape), pipeline_mode=pl.Buffered(1))


def _split3(x):
    hi = x.astype(BF16)
    r1 = x - hi.astype(F32)
    mid = r1.astype(BF16)
    lo = (r1 - mid.astype(F32)).astype(BF16)
    return hi, mid, lo


def _softplus(x):
    return jnp.maximum(x, 0.0) + jnp.log(1.0 + jnp.exp(-jnp.abs(x)))


def _inproj_kernel(x_ref, nw_ref, wz_ref, wxbc_ref, wdt_ref, wq_ref, wk_ref, wv_ref, wg_ref,
                   z_ref, xbc_ref, dt_ref, q_ref, k_ref, v_ref, g_ref):
    x = x_ref[...]
    xn = x * lax.rsqrt(jnp.mean(x * x, axis=-1, keepdims=True) + EPS)
    xn = (xn * nw_ref[...]).astype(BF16)
    for w_ref, o_ref in ((wz_ref, z_ref), (wxbc_ref, xbc_ref), (wdt_ref, dt_ref), (wq_ref, q_ref),
                         (wk_ref, k_ref), (wv_ref, v_ref), (wg_ref, g_ref)):
        o_ref[...] = jnp.dot(xn, w_ref[...], preferred_element_type=F32).astype(o_ref.dtype)


def _in_proj(x, norm_w, ws, tm, act_dtype):
    t = x.shape[0]
    widths = (D_INNER, CONV_DIM, SSM_HEADS, D_ATTN, D_KV, D_KV, 2 * D_MODEL)
    dtypes = (act_dtype, act_dtype, F32, act_dtype, act_dtype, act_dtype, act_dtype)
    row = lambda n: pl.BlockSpec((tm, n), lambda i: (i, 0))
    return pl.pallas_call(
        _inproj_kernel,
        grid=(t // tm,),
        in_specs=[row(D_MODEL), _const_spec((1, D_MODEL))] + [_resident_spec((D_MODEL, n)) for n in widths],
        out_specs=[row(n) for n in widths],
        out_shape=[jax.ShapeDtypeStruct((t, n), d) for n, d in zip(widths, dtypes)],
        compiler_params=pltpu.CompilerParams(dimension_semantics=("arbitrary",), vmem_limit_bytes=VMEM_LIMIT),
        name="in_proj",
    )(x, norm_w, *ws)


def _ssd_kernel(xbc_ref, z_ref, dt_ref, dtT_ref, hist_ref, h0_ref, cw_ref, cb_ref, dtb_ref, dtbT_ref,
                a_ref, aT_ref, dsk_ref, nw_ref, e_ref, y_ref, h_ref, buf_ref, *, lc, n_valid):
    c = pl.program_id(1)

    @pl.when(c == 0)
    def _():
        buf_ref[0:SUBLANES, :] = hist_ref[...]
        h_ref[...] = h0_ref[...]

    cur = xbc_ref[...].astype(F32)
    buf_ref[SUBLANES:SUBLANES + lc, :] = cur
    conv = cb_ref[...] + buf_ref[SUBLANES - 3:SUBLANES - 3 + lc, :] * cw_ref[0:1, :]
    conv = conv + buf_ref[SUBLANES - 2:SUBLANES - 2 + lc, :] * cw_ref[1:2, :]
    conv = conv + buf_ref[SUBLANES - 1:SUBLANES - 1 + lc, :] * cw_ref[2:3, :]
    conv = conv + cur * cw_ref[3:4, :]
    buf_ref[0:SUBLANES, :] = buf_ref[lc:lc + SUBLANES, :]
    act = conv * jax.nn.sigmoid(conv)
    xs = act[:, :D_INNER]
    bm = act[:, D_INNER:D_INNER + SSM_GROUPS * D_STATE].astype(BF16)
    cm = act[:, D_INNER + SSM_GROUPS * D_STATE:].astype(BF16)

    dt = _softplus(dt_ref[...] + dtb_ref[...])
    dtT = _softplus(dtT_ref[0] + dtbT_ref[...])
    if n_valid < lc:
        dt = jnp.where(lax.broadcasted_iota(I32, dt.shape, 0) < n_valid, dt, 0.0)
        dtT = jnp.where(lax.broadcasted_iota(I32, dtT.shape, 1) < n_valid, dtT, 0.0)
    la = dt * a_ref[...]
    laT = dtT * aT_ref[...]
    li = lax.broadcasted_iota(I32, (lc, lc), 0)
    si = lax.broadcasted_iota(I32, (lc, lc), 1)
    causal = li >= si
    tril = jnp.where(causal, 1.0, 0.0).astype(BF16)
    triu = jnp.where(li <= si, 1.0, 0.0).astype(BF16)
    cum = sum(jnp.dot(tril, p, preferred_element_type=F32) for p in _split3(la))
    cumT = sum(jnp.dot(p, triu, preferred_element_type=F32) for p in _split3(laT))
    ec = jnp.exp(cum)
    dte = jnp.exp(cum[lc - 1:lc, :] - cum)
    cd = jnp.exp(cumT[:, lc - 1:lc])

    def expand(v):
        hi, mid, _ = _split3(v)
        return (jnp.dot(hi, e_ref[...], preferred_element_type=F32)
                + jnp.dot(mid, e_ref[...], preferred_element_type=F32))

    dt_x, ec_x, dte_x = expand(dt), expand(ec), expand(dte)
    xdt = xs * dt_x
    xdt_b = xdt.astype(BF16)
    xdte_b = (xdt * dte_x).astype(BF16)
    lane = lax.broadcasted_iota(I32, (lc, LANES), 1)
    low_half = lane < SSM_HEAD_DIM

    for g in range(SSM_GROUPS):
        gs = slice(g * GROUP_W, (g + 1) * GROUP_W)
        bm_g = bm[:, g * D_STATE:(g + 1) * D_STATE]
        cm_g = cm[:, g * D_STATE:(g + 1) * D_STATE]
        cb = lax.dot_general(cm_g, bm_g, NT, preferred_element_type=F32)
        cbm = jnp.where(causal, cb, 0.0)
        h_g = h_ref[0, g * SSM_HPG:(g + 1) * SSM_HPG].reshape(GROUP_W, D_STATE)
        y_off = lax.dot_general(cm_g, h_g.astype(BF16), NT, preferred_element_type=F32) * ec_x[:, gs]
        tiles = []
        for j in range(GROUP_W // LANES):
            col = g * GROUP_W + j * LANES
            x_pair = xdt_b[:, col:col + LANES]
            acc = None
            for half in range(2):
                h = col // SSM_HEAD_DIM + half
                seg = cum[:, h:h + 1] - cumT[h:h + 1, :]
                m = (cbm * jnp.exp(jnp.where(causal, seg, 0.0))).astype(BF16)
                x_h = jnp.where(low_half if half == 0 else jnp.logical_not(low_half), x_pair, jnp.zeros_like(x_pair))
                d = jnp.dot(m, x_h, preferred_element_type=F32)
                acc = d if acc is None else acc + d
            tiles.append(acc)
        y_g = jnp.concatenate(tiles, axis=1) + y_off + xs[:, gs] * dsk_ref[:, gs]
        zg = z_ref[:, gs].astype(F32)
        y_g = y_g * (zg * jax.nn.sigmoid(zg))
        y_g = y_g * lax.rsqrt(jnp.mean(y_g * y_g, axis=-1, keepdims=True) + EPS)
        y_ref[:, gs] = (y_g * nw_ref[:, gs]).astype(y_ref.dtype)
        st = lax.dot_general(xdte_b[:, gs], bm_g, TN, preferred_element_type=F32)
        for hh in range(SSM_HPG):
            h = g * SSM_HPG + hh
            rows = slice(hh * SSM_HEAD_DIM, (hh + 1) * SSM_HEAD_DIM)
            h_ref[0, h] = h_g[rows, :] * cd[h:h + 1, 0:1] + st[rows, :]


def _ssd(xbc, z, dt, dtT, hist, h0, consts, nb, nc, lc, n_valid):
    t = xbc.shape[0]
    row = lambda n: pl.BlockSpec((lc, n), lambda b, c: (b * nc + c, 0))
    in_specs = [
        row(CONV_DIM), row(D_INNER), row(SSM_HEADS),
        pl.BlockSpec((1, SSM_HEADS, lc), lambda b, c: (b * nc + c, 0, 0)),
        pl.BlockSpec((SUBLANES, CONV_DIM), lambda b, c: (b, 0)),
        pl.BlockSpec((1, SSM_HEADS, SSM_HEAD_DIM, D_STATE), lambda b, c: (b, 0, 0, 0)),
    ] + [_const_spec(a.shape) for a in consts]
    return pl.pallas_call(
        functools.partial(_ssd_kernel, lc=lc, n_valid=n_valid),
        grid=(nb, nc),
        in_specs=in_specs,
        out_specs=[row(D_INNER), pl.BlockSpec((1, SSM_HEADS, SSM_HEAD_DIM, D_STATE), lambda b, c: (b, 0, 0, 0))],
        out_shape=[jax.ShapeDtypeStruct((t, D_INNER), BF16),
                   jax.ShapeDtypeStruct((nb, SSM_HEADS, SSM_HEAD_DIM, D_STATE), F32)],
        scratch_shapes=[pltpu.VMEM((SUBLANES + lc, CONV_DIM), F32)],
        compiler_params=pltpu.CompilerParams(dimension_semantics=("arbitrary", "arbitrary"),
                                             vmem_limit_bytes=VMEM_LIMIT),
        name="ssd",
    )(xbc, z, dt, dtT, hist, h0, *consts)


def _swa_kernel(slope_ref, sink_ref, q_ref, kc_ref, vc_ref, kp_ref, vp_ref, y_ref, *, tq, prev_always):
    nk = WINDOW + tq
    r = lax.broadcasted_iota(I32, (tq, nk), 0)
    j = lax.broadcasted_iota(I32, (tq, nk), 1)
    dist = r + WINDOW - j
    valid = (dist >= 0) & (dist < WINDOW)
    if not prev_always:
        valid = valid & ((j >= WINDOW) | (pl.program_id(1) > 0))
    distf = dist.astype(F32)
    lane = lax.broadcasted_iota(I32, (nk, LANES), 1)
    out_tiles = [None] * (D_ATTN // LANES)
    for t in range(D_KV // LANES):
        cols = slice(t * LANES, (t + 1) * LANES)
        kt = jnp.concatenate([kp_ref[:, cols].astype(F32), kc_ref[:, cols].astype(F32)], axis=0)
        vt = jnp.concatenate([vp_ref[:, cols].astype(F32), vc_ref[:, cols].astype(F32)], axis=0)
        for b in range(2):
            g = 2 * t + b
            mine = (lane >= ATTN_HEAD_DIM) if b else (lane < ATTN_HEAD_DIM)
            k_same = jnp.where(mine, kt, 0.0)
            v_same = jnp.where(mine, vt, 0.0)
            k_half = {b: k_same.astype(BF16), 1 - b: pltpu.roll(k_same, ATTN_HEAD_DIM, 1).astype(BF16)}
            v_half = {b: v_same.astype(BF16), 1 - b: pltpu.roll(v_same, ATTN_HEAD_DIM, 1).astype(BF16)}
            for qi in range(Q_PER_KV):
                h = g * Q_PER_KV + qi
                jq, a = h // 2, h % 2
                qt = q_ref[:, jq * LANES:(jq + 1) * LANES].astype(BF16)
                s = lax.dot_general(qt, k_half[a], NT, preferred_element_type=F32)
                s = s * (ATTN_HEAD_DIM ** -0.5) - slope_ref[h] * distf
                s = jnp.where(valid, s, NEG_BIG)
                sink = sink_ref[h]
                m = jnp.maximum(jnp.max(s, axis=-1, keepdims=True), sink)
                e = jnp.exp(s - m)
                den = jnp.sum(e, axis=-1, keepdims=True) + jnp.exp(sink - m)
                p = (e / den).astype(BF16)
                o = jnp.dot(p, v_half[a], preferred_element_type=F32)
                out_tiles[jq] = o if out_tiles[jq] is None else out_tiles[jq] + o
    for jq, o in enumerate(out_tiles):
        y_ref[:, jq * LANES:(jq + 1) * LANES] = o.astype(y_ref.dtype)


def _swa(slopes, sinks, q, k, v, k_prev, v_prev, nb, nblk, tq, prev_always):
    t = q.shape[0]
    cur = lambda n: pl.BlockSpec((tq, n), lambda b, i: (b * nblk + i, 0))
    if prev_always:
        prev = pl.BlockSpec((WINDOW, D_KV), lambda b, i: (b, 0))
    else:
        prev = pl.BlockSpec((WINDOW, D_KV), lambda b, i: (b * nblk + jnp.maximum(i - 1, 0), 0))
    smem = pl.BlockSpec(memory_space=pltpu.SMEM)
    return pl.pallas_call(
        functools.partial(_swa_kernel, tq=tq, prev_always=prev_always),
        grid=(nb, nblk),
        in_specs=[smem, smem, cur(D_ATTN), cur(D_KV), cur(D_KV), prev, prev],
        out_specs=cur(D_ATTN),
        out_shape=jax.ShapeDtypeStruct((t, D_ATTN), BF16),
        compiler_params=pltpu.CompilerParams(dimension_semantics=("arbitrary", "arbitrary"),
                                             vmem_limit_bytes=VMEM_LIMIT),
        name="swa",
    )(slopes, sinks, q, k, v, k_prev, v_prev)


def _post_kernel(x_ref, ys_ref, ya_ref, g_ref, wsp_ref, wap_ref, wo_ref, nw_ref, wr_ref, br_ref,
                 h_ref, hn_ref, ids_ref, pr_ref, *, n_exp):
    a = jnp.dot(ys_ref[...].astype(BF16), wsp_ref[...], preferred_element_type=F32)
    b = jnp.dot(ya_ref[...].astype(BF16), wap_ref[...], preferred_element_type=F32)
    g = g_ref[...].astype(F32)
    merged = jax.nn.sigmoid(g[:, :D_MODEL]) * a + jax.nn.sigmoid(g[:, D_MODEL:]) * b
    h = x_ref[...] + jnp.dot(merged.astype(BF16), wo_ref[...], preferred_element_type=F32)
    h_ref[...] = h
    hn = h * lax.rsqrt(jnp.mean(h * h, axis=-1, keepdims=True) + EPS) * nw_ref[...]
    hn_ref[...] = hn
    w_hi, w_mid, _ = _split3(wr_ref[...])
    x_hi, x_mid, _ = _split3(hn)
    logits = (lax.dot_general(w_hi, x_hi, NT, preferred_element_type=F32)
              + lax.dot_general(w_hi, x_mid, NT, preferred_element_type=F32)
              + lax.dot_general(w_mid, x_hi, NT, preferred_element_type=F32)) + br_ref[...]
    eidx = lax.broadcasted_iota(I32, logits.shape, 0).astype(F32)
    work = logits
    vals, ids = [], []
    for _ in range(TOP_K):
        m = jnp.max(work, axis=0, keepdims=True)
        first = jnp.min(jnp.where(work == m, eidx, float(n_exp)), axis=0, keepdims=True)
        vals.append(m)
        ids.append(first)
        work = jnp.where(eidx == first, -jnp.inf, work)
    es = [jnp.exp(v - vals[0]) for v in vals]
    den = es[0] + es[1] + es[2] + es[3]
    ids_ref[...] = jnp.concatenate(ids, axis=0).astype(I32)
    pr_ref[...] = jnp.concatenate([e / den for e in es], axis=0)


def _post(x, y_ssm, y_attn, gates, wsp, wap, wo, ffn_nw, w_rT, b_r, tm):
    t = x.shape[0]
    n_exp = w_rT.shape[0]
    row = lambda n: pl.BlockSpec((tm, n), lambda i: (i, 0))
    col = pl.BlockSpec((TOP_K, tm), lambda i: (0, i))
    return pl.pallas_call(
        functools.partial(_post_kernel, n_exp=n_exp),
        grid=(t // tm,),
        in_specs=[row(D_MODEL), row(D_INNER), row(D_ATTN), row(2 * D_MODEL),
                  _resident_spec(wsp.shape), _resident_spec(wap.shape), _resident_spec(wo.shape),
                  _const_spec(ffn_nw.shape), _const_spec(w_rT.shape), _const_spec(b_r.shape)],
        out_specs=[row(D_MODEL), row(D_MODEL), col, col],
        out_shape=[jax.ShapeDtypeStruct((t, D_MODEL), F32), jax.ShapeDtypeStruct((t, D_MODEL), F32),
                   jax.ShapeDtypeStruct((TOP_K, t), I32), jax.ShapeDtypeStruct((TOP_K, t), F32)],
        compiler_params=pltpu.CompilerParams(dimension_semantics=("arbitrary",), vmem_limit_bytes=VMEM_LIMIT),
        name="post",
    )(x, y_ssm, y_attn, gates, wsp, wap, wo, ffn_nw, w_rT, b_r)


def _row_gather(idx_ref, n, src_hbm, dst, sem):
    def body(r, carry):
        pltpu.make_async_copy(src_hbm.at[pl.ds(idx_ref[0, 0, r], 1)], dst.at[pl.ds(r, 1)], sem).start()
        return carry
    lax.fori_loop(0, n, body, 0, unroll=8)


def _moe_kernel(te_ref, nu_ref, idx0_ref, idx1_ref, hn_hbm, wg_ref, wu_ref, wd_ref, bg_ref, bu_ref, bd_ref,
                y_ref, xbuf, sem, *, tme):
    i = pl.program_id(0)
    n_used = nu_ref[0]
    slot = i % 2

    @pl.when(i == 0)
    def _():
        _row_gather(idx0_ref, tme, hn_hbm, xbuf.at[0], sem.at[0])

    @pl.when(i + 1 < n_used)
    def _():
        _row_gather(idx1_ref, tme, hn_hbm, xbuf.at[1 - slot], sem.at[1 - slot])

    @pl.when(i < n_used)
    def _():
        pltpu.make_async_copy(hn_hbm.at[pl.ds(0, tme)], xbuf.at[slot], sem.at[slot]).wait()
        x = xbuf[slot].astype(BF16)
        g = jnp.minimum(jnp.dot(x, wg_ref[0], preferred_element_type=F32) + bg_ref[0], SWIGLU_LIMIT)
        u = jnp.clip(jnp.dot(x, wu_ref[0], preferred_element_type=F32) + bu_ref[0], -SWIGLU_LIMIT, SWIGLU_LIMIT)
        act = ((u + 1.0) * g * jax.nn.sigmoid(SWIGLU_ALPHA * g)).astype(BF16)
        y_ref[...] = jnp.dot(act, wd_ref[0], preferred_element_type=F32) + bd_ref[0]

    @pl.when(i >= n_used)
    def _():
        y_ref[...] = jnp.zeros_like(y_ref)


def _moe(tile_expert, n_used, sorted_tok, hn, wg, wu, wd, bg, bu, bd, tme):
    n_tiles = tile_expert.shape[0]
    idx = sorted_tok.reshape(n_tiles, 1, tme)
    wspec = pl.BlockSpec((1, D_MODEL, D_FF), lambda i, te, nu: (te[i], 0, 0))
    bspec = pl.BlockSpec((1, 1, D_FF), lambda i, te, nu: (te[i], 0, 0))
    grid_spec = pltpu.PrefetchScalarGridSpec(
        num_scalar_prefetch=2,
        grid=(n_tiles,),
        in_specs=[
            pl.BlockSpec((1, 1, tme), lambda i, te, nu: (0, 0, 0), memory_space=pltpu.SMEM),
            pl.BlockSpec((1, 1, tme), lambda i, te, nu: (jnp.minimum(i + 1, n_tiles - 1), 0, 0),
                         memory_space=pltpu.SMEM),
            pl.BlockSpec(memory_space=pl.ANY),
            wspec, wspec, wspec, bspec, bspec, bspec,
        ],
        out_specs=pl.BlockSpec((tme, D_MODEL), lambda i, te, nu: (i, 0)),
        scratch_shapes=[pltpu.VMEM((2, tme, D_MODEL), F32), pltpu.SemaphoreType.DMA((2,))],
    )
    return pl.pallas_call(
        functools.partial(_moe_kernel, tme=tme),
        grid_spec=grid_spec,
        out_shape=jax.ShapeDtypeStruct((n_tiles * tme, D_MODEL), F32),
        compiler_params=pltpu.CompilerParams(dimension_semantics=("arbitrary",), vmem_limit_bytes=VMEM_LIMIT),
        name="moe",
    )(tile_expert, n_used, idx, idx, hn, wg, wu, wd, bg, bu, bd)


def _combine_kernel(pos0_ref, pos1_ref, h_ref, pr_ref, y_hbm, nw_ref, o_ref, ybuf, sem, *, tt):
    i = pl.program_id(0)
    n = pl.num_programs(0)
    slot = i % 2
    rows = TOP_K * tt

    @pl.when(i == 0)
    def _():
        _row_gather(pos0_ref, rows, y_hbm, ybuf.at[0], sem.at[0])

    @pl.when(i + 1 < n)
    def _():
        _row_gather(pos1_ref, rows, y_hbm, ybuf.at[1 - slot], sem.at[1 - slot])

    pltpu.make_async_copy(y_hbm.at[pl.ds(0, rows)], ybuf.at[slot], sem.at[slot]).wait()
    moe = None
    for k in range(TOP_K):
        term = pr_ref[:, k:k + 1] * ybuf[slot, k * tt:(k + 1) * tt, :]
        moe = term if moe is None else moe + term
    h = h_ref[...] + moe
    o_ref[...] = h * lax.rsqrt(jnp.mean(h * h, axis=-1, keepdims=True) + EPS) * nw_ref[...]


def _combine(pos, h, probs, y_sorted, final_nw, tt):
    t = h.shape[0]
    n_tt = t // tt
    rows = TOP_K * tt
    return pl.pallas_call(
        functools.partial(_combine_kernel, tt=tt),
        grid=(n_tt,),
        in_specs=[
            pl.BlockSpec((1, 1, rows), lambda i: (0, 0, 0), memory_space=pltpu.SMEM),
            pl.BlockSpec((1, 1, rows), lambda i: (jnp.minimum(i + 1, n_tt - 1), 0, 0), memory_space=pltpu.SMEM),
            pl.BlockSpec((tt, D_MODEL), lambda i: (i, 0)),
            pl.BlockSpec((tt, TOP_K), lambda i: (i, 0)),
            pl.BlockSpec(memory_space=pl.ANY),
            _const_spec(final_nw.shape),
        ],
        out_specs=pl.BlockSpec((tt, D_MODEL), lambda i: (i, 0)),
        out_shape=jax.ShapeDtypeStruct((t, D_MODEL), F32),
        scratch_shapes=[pltpu.VMEM((2, rows, D_MODEL), F32), pltpu.SemaphoreType.DMA((2,))],
        compiler_params=pltpu.CompilerParams(dimension_semantics=("arbitrary",), vmem_limit_bytes=VMEM_LIMIT),
        name="combine",
    )(pos, pos, h, probs, y_sorted, final_nw)


def _route(ids, n_exp, tme, tt):
    k, t = ids.shape
    n_assign = k * t
    n_tiles = n_assign // tme + n_exp
    flat = ids.reshape(n_assign)
    onehot = (flat[:, None] == jnp.arange(n_exp, dtype=I32)[None, :]).astype(I32)
    csum = jnp.cumsum(onehot, axis=0)
    rank = jnp.take_along_axis(csum, flat[:, None], axis=1)[:, 0] - 1
    counts = csum[-1]
    tiles_per = (counts + tme - 1) // tme
    tile_end = jnp.cumsum(tiles_per)
    start_row = (tile_end - tiles_per) * tme
    pos = start_row[flat] + rank
    tok = jnp.tile(jnp.arange(t, dtype=I32), k)
    sorted_tok = jnp.zeros((n_tiles * tme,), I32).at[pos].set(tok)
    n_used = tile_end[-1]
    tile_ids = jnp.arange(n_tiles, dtype=I32)
    tile_expert = jnp.searchsorted(tile_end, jnp.minimum(tile_ids, n_used - 1), side="right").astype(I32)
    pos_tiles = pos.reshape(k, t // tt, tt).transpose(1, 0, 2).reshape(t // tt, 1, k * tt)
    return tile_expert, n_used.reshape(1).astype(I32), sorted_tok, pos_tiles


def _pick_tile(n, pref):
    while n % pref:
        pref //= 2
    return pref


def kernel(x_prompt, x_sample, state_conv, state_ssm, cache_swa_k, cache_swa_v, attn_norm_w, w_in, conv_w, conv_b, dt_bias, a_log, d_skip, ssm_norm_w, attn_sinks, w_ssm_proj, w_attn_proj, w_o, ffn_norm_w, w_router, b_router, w_gate, b_gate, w_up, b_up, w_down, b_down, final_norm_w):
    assert w_in.shape[0] == 1, "single-layer step"
    nb, seq, _ = x_prompt.shape
    nbs = x_sample.shape[0]
    n_exp = w_router.shape[-1]
    tp, ts = nb * seq, nbs
    pad = SUBLANES

    cuts = [0]
    for n in (D_INNER, CONV_DIM, SSM_HEADS, D_ATTN, D_KV, D_KV, 2 * D_MODEL):
        cuts.append(cuts[-1] + n)
    w_in_b = w_in[0].astype(BF16)
    ws = [w_in_b[:, cuts[i]:cuts[i + 1]] for i in range(7)]
    attn_nw = attn_norm_w[0].reshape(1, D_MODEL)
    a_neg = -jnp.exp(a_log[0].astype(F32))
    head_of = jnp.arange(D_INNER, dtype=I32) // SSM_HEAD_DIM
    expand = (jnp.arange(SSM_HEADS, dtype=I32)[:, None] == head_of[None, :]).astype(BF16)
    ssd_consts = (conv_w[0], conv_b[0].reshape(1, CONV_DIM), dt_bias[0].reshape(1, SSM_HEADS),
                  dt_bias[0].reshape(SSM_HEADS, 1), a_neg.reshape(1, SSM_HEADS), a_neg.reshape(SSM_HEADS, 1),
                  d_skip[0][head_of].reshape(1, D_INNER), ssm_norm_w[0].reshape(1, D_INNER), expand)
    slopes = jnp.exp2(-8.0 * jnp.arange(1, N_Q_HEADS + 1, dtype=F32) / N_Q_HEADS)
    sinks = attn_sinks[0].astype(F32)
    wsp, wap, wo = w_ssm_proj[0].astype(BF16), w_attn_proj[0].astype(BF16), w_o[0].astype(BF16)
    ffn_nw = ffn_norm_w[0].reshape(1, D_MODEL)
    w_rT = w_router[0].T
    b_r = b_router[0].reshape(n_exp, 1)
    wg, wu, wd = w_gate[0].astype(BF16), w_up[0].astype(BF16), w_down[0].astype(BF16)
    bg, bu, bd = (b[0].reshape(n_exp, 1, -1) for b in (b_gate, b_up, b_down))
    final_nw = final_norm_w.reshape(1, D_MODEL)

    xp = x_prompt.reshape(tp, D_MODEL)
    z, xbc, dt, q, k, v, gates = _in_proj(xp, attn_nw, ws, _pick_tile(tp, 512), BF16)
    nc = seq // CHUNK
    dtT = dt.reshape(nb * nc, CHUNK, SSM_HEADS).transpose(0, 2, 1)
    y_ssm, ssm_p = _ssd(xbc, z, dt, dtT, jnp.zeros((nb * SUBLANES, CONV_DIM), F32),
                        jnp.zeros((nb, SSM_HEADS, SSM_HEAD_DIM, D_STATE), F32), ssd_consts, nb, nc, CHUNK, CHUNK)
    nblk = seq // WINDOW
    y_attn = _swa(slopes, sinks, q, k, v, k, v, nb, nblk, WINDOW, False)
    tm_post = _pick_tile(tp, 512)
    h_p, hn_p, ids_p, pr_p = _post(xp, y_ssm, y_attn, gates, wsp, wap, wo, ffn_nw, w_rT, b_r, tm_post)

    xs_pad = jnp.pad(x_sample.reshape(ts, 1, D_MODEL), ((0, 0), (0, pad - 1), (0, 0))).reshape(ts * pad, D_MODEL)
    z_s, xbc_s, dt_s, q_s, k_s, v_s, gates_s = _in_proj(xs_pad, attn_nw, ws, _pick_tile(ts * pad, 256), F32)
    dtT_s = dt_s.reshape(ts, pad, SSM_HEADS).transpose(0, 2, 1)
    hist_s = jnp.pad(state_conv[0], ((0, 0), (pad - (CONV_W - 1), 0), (0, 0))).reshape(ts * pad, CONV_DIM)
    y_ssm_s, ssm_s = _ssd(xbc_s, z_s, dt_s, dtT_s, hist_s, state_ssm[0], ssd_consts, ts, 1, pad, 1)
    kc = cache_swa_k[0].reshape(ts * WINDOW, D_KV)
    vc = cache_swa_v[0].reshape(ts * WINDOW, D_KV)
    y_attn_s = _swa(slopes, sinks, q_s, k_s, v_s, kc, vc, ts, 1, pad, True)
    real = lambda a: a.reshape(ts, pad, -1)[:, 0]
    h_s, hn_s, ids_s, pr_s = _post(x_sample.reshape(ts, D_MODEL), real(y_ssm_s), real(y_attn_s), real(gates_s),
                                   wsp, wap, wo, ffn_nw, w_rT, b_r, _pick_tile(ts, 512))

    h_all = jnp.concatenate([h_p, h_s], axis=0)
    hn_all = jnp.concatenate([hn_p, hn_s], axis=0)
    ids = jnp.concatenate([ids_p, ids_s], axis=1)
    probs = jnp.concatenate([pr_p, pr_s], axis=1).T
    t_all = tp + ts
    tme = _pick_tile(TOP_K * t_all, 512)
    tt = _pick_tile(t_all, 128)
    tile_expert, n_used, sorted_tok, pos_tiles = _route(ids, n_exp, tme, tt)
    y_sorted = _moe(tile_expert, n_used, sorted_tok, hn_all, wg, wu, wd, bg, bu, bd, tme)
    out = _combine(pos_tiles, h_all, probs, y_sorted, final_nw, tt)

    y_prompt = out[:tp].reshape(nb, seq, D_MODEL)
    y_sample = out[tp:].reshape(nbs, 1, D_MODEL)
    conv_p = xbc.reshape(nb, seq, CONV_DIM)[:, seq - (CONV_W - 1):].astype(F32)[None]
    k_p = k.reshape(nb, seq, N_KV_HEADS, ATTN_HEAD_DIM)[:, seq - WINDOW:].astype(F32)[None]
    v_p = v.reshape(nb, seq, N_KV_HEADS, ATTN_HEAD_DIM)[:, seq - WINDOW:].astype(F32)[None]
    conv_s = jnp.concatenate([state_conv[0][:, 1:], real(xbc_s)[:, None]], axis=1)[None]
    k_new = real(k_s).reshape(ts, 1, N_KV_HEADS, ATTN_HEAD_DIM)
    v_new = real(v_s).reshape(ts, 1, N_KV_HEADS, ATTN_HEAD_DIM)
    ks_out = jnp.concatenate([cache_swa_k[0][:, 1:], k_new], axis=1)[None]
    vs_out = jnp.concatenate([cache_swa_v[0][:, 1:], v_new], axis=1)[None]
    return (y_prompt, y_sample, conv_p, ssm_p[None], k_p, v_p, conv_s, ssm_s[None], ks_out, vs_out)
```

```python
import functools

import jax
import jax.numpy as jnp
from jax import lax
from jax.experimental import pallas as pl
from jax.experimental.pallas import tpu as pltpu

F32, BF16, I32 = jnp.float32, jnp.bfloat16, jnp.int32

D_MODEL = 1024
D_INNER = 2 * D_MODEL
SSM_HEAD_DIM = 64
SSM_HEADS = D_INNER // SSM_HEAD_DIM
SSM_GROUPS = 4
SSM_HPG = SSM_HEADS // SSM_GROUPS
D_STATE = 128
CONV_W = 4
CONV_DIM = D_INNER + 2 * SSM_GROUPS * D_STATE
CHUNK = 128
ATTN_HEAD_DIM = 64
N_Q_HEADS = D_MODEL // ATTN_HEAD_DIM
N_KV_HEADS = 4
Q_PER_KV = N_Q_HEADS // N_KV_HEADS
D_ATTN = N_Q_HEADS * ATTN_HEAD_DIM
D_KV = N_KV_HEADS * ATTN_HEAD_DIM
WINDOW = 128
TOP_K = 4
D_FF = D_MODEL
SWIGLU_LIMIT = 7.0
SWIGLU_ALPHA = 1.702
EPS = 1e-5
NEG_BIG = -1e30

LANES = 128
SUBLANES = 8
GROUP_W = D_INNER // SSM_GROUPS
PIECE = 2 * SUBLANES
PIECES_PER_TILE = 32
VMEM_LIMIT = 56 * 1024 * 1024

NT = (((1,), (1,)), ((), ()))
TN = (((0,), (0,)), ((), ()))


def _const_spec(shape):
    return pl.BlockSpec(shape, lambda *_: (0,) * len(shape))


def _resident_spec(shape):
    return pl.BlockSpec(shape, lambda *_: (0,) * len(shape), pipeline_mode=pl.Buffered(1))


def _split3(x):
    hi = x.astype(BF16)
    r1 = x - hi.astype(F32)
    mid = r1.astype(BF16)
    lo = (r1 - mid.astype(F32)).astype(BF16)
    return hi, mid, lo


def _softplus(x):
    return jnp.maximum(x, 0.0) + jnp.log(1.0 + jnp.exp(-jnp.abs(x)))


def _inproj_kernel(x_ref, nw_ref, wz_ref, wxbc_ref, wdt_ref, wq_ref, wk_ref, wv_ref, wg_ref,
                   z_ref, xbc_ref, dt_ref, q_ref, k_ref, v_ref, g_ref):
    x = x_ref[...]
    xn = x * lax.rsqrt(jnp.mean(x * x, axis=-1, keepdims=True) + EPS)
    xn = (xn * nw_ref[...]).astype(BF16)
    for w_ref, o_ref in ((wz_ref, z_ref), (wxbc_ref, xbc_ref), (wdt_ref, dt_ref), (wq_ref, q_ref),
                         (wk_ref, k_ref), (wv_ref, v_ref), (wg_ref, g_ref)):
        o_ref[...] = jnp.dot(xn, w_ref[...], preferred_element_type=F32).astype(o_ref.dtype)


def _in_proj(x, norm_w, ws, tm, act_dtype):
    t = x.shape[0]
    widths = (D_INNER, CONV_DIM, SSM_HEADS, D_ATTN, D_KV, D_KV, 2 * D_MODEL)
    dtypes = (act_dtype, act_dtype, F32, act_dtype, act_dtype, act_dtype, act_dtype)
    row = lambda n: pl.BlockSpec((tm, n), lambda i: (i, 0))
    return pl.pallas_call(
        _inproj_kernel,
        grid=(t // tm,),
        in_specs=[row(D_MODEL), _const_spec((1, D_MODEL))] + [_resident_spec((D_MODEL, n)) for n in widths],
        out_specs=[row(n) for n in widths],
        out_shape=[jax.ShapeDtypeStruct((t, n), d) for n, d in zip(widths, dtypes)],
        compiler_params=pltpu.CompilerParams(dimension_semantics=("arbitrary",), vmem_limit_bytes=VMEM_LIMIT),
        name="in_proj",
    )(x, norm_w, *ws)


def _ssd_kernel(xbc_ref, z_ref, dt_ref, dtT_ref, hist_ref, h0_ref, cw_ref, cb_ref, dtb_ref, dtbT_ref,
                a_ref, aT_ref, dsk_ref, nw_ref, e_ref, y_ref, h_ref, buf_ref, *, lc, n_valid):
    c = pl.program_id(1)

    @pl.when(c == 0)
    def _():
        buf_ref[0:SUBLANES, :] = hist_ref[...]
        h_ref[...] = h0_ref[...]

    cur = xbc_ref[...].astype(F32)
    buf_ref[SUBLANES:SUBLANES + lc, :] = cur
    conv = cb_ref[...] + buf_ref[SUBLANES - 3:SUBLANES - 3 + lc, :] * cw_ref[0:1, :]
    conv = conv + buf_ref[SUBLANES - 2:SUBLANES - 2 + lc, :] * cw_ref[1:2, :]
    conv = conv + buf_ref[SUBLANES - 1:SUBLANES - 1 + lc, :] * cw_ref[2:3, :]
    conv = conv + cur * cw_ref[3:4, :]
    buf_ref[0:SUBLANES, :] = buf_ref[lc:lc + SUBLANES, :]
    act = conv * jax.nn.sigmoid(conv)
    xs = act[:, :D_INNER]
    bm = act[:, D_INNER:D_INNER + SSM_GROUPS * D_STATE].astype(BF16)
    cm = act[:, D_INNER + SSM_GROUPS * D_STATE:].astype(BF16)

    dt = _softplus(dt_ref[...] + dtb_ref[...])
    dtT = _softplus(dtT_ref[0] + dtbT_ref[...])
    if n_valid < lc:
        dt = jnp.where(lax.broadcasted_iota(I32, dt.shape, 0) < n_valid, dt, 0.0)
        dtT = jnp.where(lax.broadcasted_iota(I32, dtT.shape, 1) < n_valid, dtT, 0.0)
    la = dt * a_ref[...]
    laT = dtT * aT_ref[...]
    li = lax.broadcasted_iota(I32, (lc, lc), 0)
    si = lax.broadcasted_iota(I32, (lc, lc), 1)
    causal = li >= si
    tril = jnp.where(causal, 1.0, 0.0).astype(BF16)
    triu = jnp.where(li <= si, 1.0, 0.0).astype(BF16)
    cum = sum(jnp.dot(tril, p, preferred_element_type=F32) for p in _split3(la))
    cumT = sum(jnp.dot(p, triu, preferred_element_type=F32) for p in _split3(laT))
    ec = jnp.exp(cum)
    dte = jnp.exp(cum[lc - 1:lc, :] - cum)
    cd = jnp.exp(cumT[:, lc - 1:lc])

    def expand(v):
        hi, mid, _ = _split3(v)
        return (jnp.dot(hi, e_ref[...], preferred_element_type=F32)
                + jnp.dot(mid, e_ref[...], preferred_element_type=F32))

    dt_x, ec_x, dte_x = expand(dt), expand(ec), expand(dte)
    xdt = xs * dt_x
    xdt_b = xdt.astype(BF16)
    xdte_b = (xdt * dte_x).astype(BF16)
    lane = lax.broadcasted_iota(I32, (lc, LANES), 1)
    low_half = lane < SSM_HEAD_DIM

    for g in range(SSM_GROUPS):
        gs = slice(g * GROUP_W, (g + 1) * GROUP_W)
        bm_g = bm[:, g * D_STATE:(g + 1) * D_STATE]
        cm_g = cm[:, g * D_STATE:(g + 1) * D_STATE]
        cb = lax.dot_general(cm_g, bm_g, NT, preferred_element_type=F32)
        cbm = jnp.where(causal, cb, 0.0)
        h_g = h_ref[0, g * SSM_HPG:(g + 1) * SSM_HPG].reshape(GROUP_W, D_STATE)
        y_off = lax.dot_general(cm_g, h_g.astype(BF16), NT, preferred_element_type=F32) * ec_x[:, gs]
        tiles = []
        for j in range(GROUP_W // LANES):
            col = g * GROUP_W + j * LANES
            x_pair = xdt_b[:, col:col + LANES]
            acc = None
            for half in range(2):
                h = col // SSM_HEAD_DIM + half
                seg = cum[:, h:h + 1] - cumT[h:h + 1, :]
                m = (cbm * jnp.exp(jnp.where(causal, seg, 0.0))).astype(BF16)
                x_h = jnp.where(low_half if half == 0 else jnp.logical_not(low_half), x_pair, jnp.zeros_like(x_pair))
                d = jnp.dot(m, x_h, preferred_element_type=F32)
                acc = d if acc is None else acc + d
            tiles.append(acc)
        y_g = jnp.concatenate(tiles, axis=1) + y_off + xs[:, gs] * dsk_ref[:, gs]
        zg = z_ref[:, gs].astype(F32)
        y_g = y_g * (zg * jax.nn.sigmoid(zg))
        y_g = y_g * lax.rsqrt(jnp.mean(y_g * y_g, axis=-1, keepdims=True) + EPS)
        y_ref[:, gs] = (y_g * nw_ref[:, gs]).astype(y_ref.dtype)
        st = lax.dot_general(xdte_b[:, gs], bm_g, TN, preferred_element_type=F32)
        for hh in range(SSM_HPG):
            h = g * SSM_HPG + hh
            rows = slice(hh * SSM_HEAD_DIM, (hh + 1) * SSM_HEAD_DIM)
            h_ref[0, h] = h_g[rows, :] * cd[h:h + 1, 0:1] + st[rows, :]


def _ssd(xbc, z, dt, dtT, hist, h0, consts, nb, nc, lc, n_valid):
    t = xbc.shape[0]
    row = lambda n: pl.BlockSpec((lc, n), lambda b, c: (b * nc + c, 0))
    in_specs = [
        row(CONV_DIM), row(D_INNER), row(SSM_HEADS),
        pl.BlockSpec((1, SSM_HEADS, lc), lambda b, c: (b * nc + c, 0, 0)),
        pl.BlockSpec((SUBLANES, CONV_DIM), lambda b, c: (b, 0)),
        pl.BlockSpec((1, SSM_HEADS, SSM_HEAD_DIM, D_STATE), lambda b, c: (b, 0, 0, 0)),
    ] + [_const_spec(a.shape) for a in consts]
    return pl.pallas_call(
        functools.partial(_ssd_kernel, lc=lc, n_valid=n_valid),
        grid=(nb, nc),
        in_specs=in_specs,
        out_specs=[row(D_INNER), pl.BlockSpec((1, SSM_HEADS, SSM_HEAD_DIM, D_STATE), lambda b, c: (b, 0, 0, 0))],
        out_shape=[jax.ShapeDtypeStruct((t, D_INNER), BF16),
                   jax.ShapeDtypeStruct((nb, SSM_HEADS, SSM_HEAD_DIM, D_STATE), F32)],
        scratch_shapes=[pltpu.VMEM((SUBLANES + lc, CONV_DIM), F32)],
        compiler_params=pltpu.CompilerParams(dimension_semantics=("arbitrary", "arbitrary"),
                                             vmem_limit_bytes=VMEM_LIMIT),
        name="ssd",
    )(xbc, z, dt, dtT, hist, h0, *consts)


def _swa_kernel(slope_ref, sink_ref, q_ref, kc_ref, vc_ref, kp_ref, vp_ref, y_ref, *, tq, prev_always):
    nk = WINDOW + tq
    r = lax.broadcasted_iota(I32, (tq, nk), 0)
    j = lax.broadcasted_iota(I32, (tq, nk), 1)
    dist = r + WINDOW - j
    valid = (dist >= 0) & (dist < WINDOW)
    if not prev_always:
        valid = valid & ((j >= WINDOW) | (pl.program_id(1) > 0))
    distf = dist.astype(F32)
    lane = lax.broadcasted_iota(I32, (nk, LANES), 1)
    out_tiles = [None] * (D_ATTN // LANES)
    for t in range(D_KV // LANES):
        cols = slice(t * LANES, (t + 1) * LANES)
        kt = jnp.concatenate([kp_ref[:, cols].astype(F32), kc_ref[:, cols].astype(F32)], axis=0)
        vt = jnp.concatenate([vp_ref[:, cols].astype(F32), vc_ref[:, cols].astype(F32)], axis=0)
        for b in range(2):
            g = 2 * t + b
            mine = (lane >= ATTN_HEAD_DIM) if b else (lane < ATTN_HEAD_DIM)
            k_same = jnp.where(mine, kt, 0.0)
            v_same = jnp.where(mine, vt, 0.0)
            k_half = {b: k_same.astype(BF16), 1 - b: pltpu.roll(k_same, ATTN_HEAD_DIM, 1).astype(BF16)}
            v_half = {b: v_same.astype(BF16), 1 - b: pltpu.roll(v_same, ATTN_HEAD_DIM, 1).astype(BF16)}
            for qi in range(Q_PER_KV):
                h = g * Q_PER_KV + qi
                jq, a = h // 2, h % 2
                qt = q_ref[:, jq * LANES:(jq + 1) * LANES].astype(BF16)
                s = lax.dot_general(qt, k_half[a], NT, preferred_element_type=F32)
                s = s * (ATTN_HEAD_DIM ** -0.5) - slope_ref[h] * distf
                s = jnp.where(valid, s, NEG_BIG)
                sink = sink_ref[h]
                m = jnp.maximum(jnp.max(s, axis=-1, keepdims=True), sink)
                e = jnp.exp(s - m)
                den = jnp.sum(e, axis=-1, keepdims=True) + jnp.exp(sink - m)
                p = (e / den).astype(BF16)
                o = jnp.dot(p, v_half[a], preferred_element_type=F32)
                out_tiles[jq] = o if out_tiles[jq] is None else out_tiles[jq] + o
    for jq, o in enumerate(out_tiles):
        y_ref[:, jq * LANES:(jq + 1) * LANES] = o.astype(y_ref.dtype)


def _swa(slopes, sinks, q, k, v, k_prev, v_prev, nb, nblk, tq, prev_always):
    t = q.shape[0]
    cur = lambda n: pl.BlockSpec((tq, n), lambda b, i: (b * nblk + i, 0))
    if prev_always:
        prev = pl.BlockSpec((WINDOW, D_KV), lambda b, i: (b, 0))
    else:
        prev = pl.BlockSpec((WINDOW, D_KV), lambda b, i: (b * nblk + jnp.maximum(i - 1, 0), 0))
    smem = pl.BlockSpec(memory_space=pltpu.SMEM)
    return pl.pallas_call(
        functools.partial(_swa_kernel, tq=tq, prev_always=prev_always),
        grid=(nb, nblk),
        in_specs=[smem, smem, cur(D_ATTN), cur(D_KV), cur(D_KV), prev, prev],
        out_specs=cur(D_ATTN),
        out_shape=jax.ShapeDtypeStruct((t, D_ATTN), BF16),
        compiler_params=pltpu.CompilerParams(dimension_semantics=("arbitrary", "arbitrary"),
                                             vmem_limit_bytes=VMEM_LIMIT),
        name="swa",
    )(slopes, sinks, q, k, v, k_prev, v_prev)


def _post_kernel(x_ref, ys_ref, ya_ref, g_ref, wsp_ref, wap_ref, wo_ref, nw_ref, wr_ref, br_ref, *rest,
                 n_exp, lr, nt, aliased):
    if aliased:
        rest = rest[1:]
    h_ref, xs_ref, lpos_ref, pr_ref, pc_ref = rest
    tp = x_ref.shape[0]

    @pl.when(pl.program_id(0) >= nt)
    def _():
        xs_ref[...] = jnp.zeros_like(xs_ref)

    @pl.when(pl.program_id(0) < nt)
    def _():
        a = jnp.dot(ys_ref[...].astype(BF16), wsp_ref[...], preferred_element_type=F32)
        b = jnp.dot(ya_ref[...].astype(BF16), wap_ref[...], preferred_element_type=F32)
        g = g_ref[...].astype(F32)
        merged = jax.nn.sigmoid(g[:, :D_MODEL]) * a + jax.nn.sigmoid(g[:, D_MODEL:]) * b
        h = x_ref[...] + jnp.dot(merged.astype(BF16), wo_ref[...], preferred_element_type=F32)
        h_ref[...] = h
        hn = h * lax.rsqrt(jnp.mean(h * h, axis=-1, keepdims=True) + EPS) * nw_ref[...]
        w_hi, w_mid, _ = _split3(wr_ref[...])
        x_hi, x_mid, _ = _split3(hn)
        logits = (lax.dot_general(w_hi, x_hi, NT, preferred_element_type=F32)
                  + lax.dot_general(w_hi, x_mid, NT, preferred_element_type=F32)
                  + lax.dot_general(w_mid, x_hi, NT, preferred_element_type=F32)) + br_ref[...]
        eidx = lax.broadcasted_iota(I32, logits.shape, 0).astype(F32)
        work = logits
        vals, ids = [], []
        for _ in range(TOP_K):
            m = jnp.max(work, axis=0, keepdims=True)
            first = jnp.min(jnp.where(work == m, eidx, float(n_exp)), axis=0, keepdims=True)
            vals.append(m)
            ids.append(first)
            work = jnp.where(eidx == first, -jnp.inf, work)
        es = [jnp.exp(v - vals[0]) for v in vals]
        den = es[0] + es[1] + es[2] + es[3]
        pr_ref[...] = jnp.concatenate([e / den for e in es], axis=0)

        onehot = [jnp.where(eidx == i, 1.0, 0.0) for i in ids]
        counts = [jnp.sum(o, axis=1, keepdims=True) for o in onehot]
        total = counts[0] + counts[1] + counts[2] + counts[3]
        padded = jnp.floor((total + (PIECE - 1)) * (1.0 / PIECE)) * PIECE
        ei = lax.broadcasted_iota(I32, (n_exp, n_exp), 0)
        ej = lax.broadcasted_iota(I32, (n_exp, n_exp), 1)
        below = jnp.where(ej < ei, 1.0, 0.0).astype(BF16)
        padded_b = jnp.broadcast_to(padded, (n_exp, LANES))
        seg_off = sum(jnp.dot(below, p, preferred_element_type=F32) for p in _split3(padded_b))[:, 0:1]
        ti = lax.broadcasted_iota(I32, (tp, tp), 0)
        tj = lax.broadcasted_iota(I32, (tp, tp), 1)
        before = jnp.where(ti < tj, 1.0, 0.0).astype(BF16)
        base = seg_off
        lpos = []
        for k in range(TOP_K):
            prefix = jnp.dot(onehot[k].astype(BF16), before, preferred_element_type=F32)
            lpos.append(jnp.sum(onehot[k] * (base + prefix), axis=0, keepdims=True))
            base = base + counts[k]
        lpos_ref[...] = jnp.concatenate(lpos, axis=0).astype(I32)
        pc_ref[0] = padded_b.astype(I32)

        hn_b = hn.astype(BF16)
        rc = lr // 4
        for c in range(4):
            ri = (lax.broadcasted_iota(I32, (rc, tp), 0) + c * rc).astype(F32)
            sel = None
            for k in range(TOP_K):
                hit = jnp.where(ri == lpos[k], 1.0, 0.0)
                sel = hit if sel is None else sel + hit
            xs_ref[c * rc:(c + 1) * rc, :] = jnp.dot(sel.astype(BF16), hn_b, preferred_element_type=F32).astype(BF16)


def _post(x, y_ssm, y_attn, gates, wsp, wap, wo, ffn_nw, w_rT, b_r, tm, lr, xs_rows, xs_block0, pad_steps,
          xs_prev=None):
    t = x.shape[0]
    n_exp = w_rT.shape[0]
    nt = t // tm
    last = nt - 1
    row = lambda n: pl.BlockSpec((tm, n), lambda i: (jnp.minimum(i, last), 0))
    col = pl.BlockSpec((TOP_K, tm), lambda i: (0, jnp.minimum(i, last)))
    in_specs = [row(D_MODEL), row(D_INNER), row(D_ATTN), row(2 * D_MODEL),
                _resident_spec(wsp.shape), _resident_spec(wap.shape), _resident_spec(wo.shape),
                _const_spec(ffn_nw.shape), _const_spec(w_rT.shape), _const_spec(b_r.shape)]
    args = [x, y_ssm, y_attn, gates, wsp, wap, wo, ffn_nw, w_rT, b_r]
    aliases = {}
    if xs_prev is not None:
        in_specs.append(pl.BlockSpec(memory_space=pl.ANY))
        args.append(xs_prev)
        aliases = {len(args) - 1: 1}
    return pl.pallas_call(
        functools.partial(_post_kernel, n_exp=n_exp, lr=lr, nt=nt, aliased=xs_prev is not None),
        grid=(nt + pad_steps,),
        in_specs=in_specs,
        out_specs=[row(D_MODEL), pl.BlockSpec((lr, D_MODEL), lambda i: (xs_block0 + i, 0)), col, col,
                   pl.BlockSpec((1, n_exp, LANES), lambda i: (jnp.minimum(i, last), 0, 0))],
        out_shape=[jax.ShapeDtypeStruct((t, D_MODEL), F32), jax.ShapeDtypeStruct((xs_rows, D_MODEL), BF16),
                   jax.ShapeDtypeStruct((TOP_K, t), I32), jax.ShapeDtypeStruct((TOP_K, t), F32),
                   jax.ShapeDtypeStruct((nt, n_exp, LANES), I32)],
        input_output_aliases=aliases,
        compiler_params=pltpu.CompilerParams(dimension_semantics=("arbitrary",), vmem_limit_bytes=VMEM_LIMIT),
        name="post",
    )(*args)


def _piece(ref, p):
    return ref.at[pl.ds(pl.multiple_of(p * PIECE, PIECE), PIECE)]


def _moe_kernel(te_ref, nu_ref, nr_ref, src0_ref, src1_ref, srcc_ref, xs_hbm, wg_ref, wu_ref, wd_ref,
                bg_ref, bu_ref, bd_ref, ys_hbm, xbuf, obuf, gsem, ssem):
    i = pl.program_id(0)
    n_used = nu_ref[0]
    slot = i % 2

    def gather(src_ref, n, s):
        def body(r, carry):
            pltpu.make_async_copy(_piece(xs_hbm, src_ref[0, 0, r]), _piece(xbuf.at[s], r), gsem.at[s]).start()
            return carry
        lax.fori_loop(0, n, body, 0)

    def wait_gather(s, n):
        def body(r, carry):
            pltpu.make_async_copy(_piece(xs_hbm, 0), _piece(xbuf.at[s], 0), gsem.at[s]).wait()
            return carry
        lax.fori_loop(0, n, body, 0)

    def wait_put(s, n):
        def body(r, carry):
            pltpu.make_async_copy(_piece(obuf.at[s], 0), _piece(ys_hbm, 0), ssem.at[s]).wait()
            return carry
        lax.fori_loop(0, n, body, 0)

    @pl.when(i == 0)
    def _():
        xbuf[...] = jnp.zeros_like(xbuf)
        gather(src0_ref, nr_ref[0], 0)

    @pl.when(i + 1 < n_used)
    def _():
        gather(src1_ref, nr_ref[i + 1], 1 - slot)

    @pl.when(i < n_used)
    def _():
        n_real = nr_ref[i]
        wait_gather(slot, n_real)

        @pl.when(i >= 2)
        def _():
            wait_put(slot, nr_ref[i - 2])

        x = xbuf[slot]
        g = jnp.minimum(jnp.dot(x, wg_ref[0], preferred_element_type=F32) + bg_ref[0], SWIGLU_LIMIT)
        u = jnp.clip(jnp.dot(x, wu_ref[0], preferred_element_type=F32) + bu_ref[0], -SWIGLU_LIMIT, SWIGLU_LIMIT)
        act = ((u + 1.0) * g * jax.nn.sigmoid(SWIGLU_ALPHA * g)).astype(BF16)
        obuf[slot] = (jnp.dot(act, wd_ref[0], preferred_element_type=F32) + bd_ref[0]).astype(BF16)

        def put(r, carry):
            pltpu.make_async_copy(_piece(obuf.at[slot], r), _piece(ys_hbm, srcc_ref[0, 0, r]), ssem.at[slot]).start()
            return carry
        lax.fori_loop(0, n_real, put, 0)

        @pl.when(i == n_used - 1)
        def _():
            wait_put(slot, n_real)

            @pl.when(i >= 1)
            def _():
                wait_put(1 - slot, nr_ref[i - 1])


def _moe(tile_expert, n_used, n_real, src, xs, wg, wu, wd, bg, bu, bd):
    n_tiles = tile_expert.shape[0]
    idx = src.reshape(n_tiles, 1, PIECES_PER_TILE)
    tme = PIECES_PER_TILE * PIECE
    wspec = pl.BlockSpec((1, D_MODEL, D_FF), lambda i, te, nu, nr: (te[i], 0, 0))
    bspec = pl.BlockSpec((1, 1, D_FF), lambda i, te, nu, nr: (te[i], 0, 0))
    ispec = lambda f: pl.BlockSpec((1, 1, PIECES_PER_TILE), lambda i, te, nu, nr: (f(i), 0, 0),
                                   memory_space=pltpu.SMEM)
    grid_spec = pltpu.PrefetchScalarGridSpec(
        num_scalar_prefetch=3,
        grid=(n_tiles,),
        in_specs=[ispec(lambda i: 0), ispec(lambda i: jnp.minimum(i + 1, n_tiles - 1)), ispec(lambda i: i),
                  pl.BlockSpec(memory_space=pl.ANY), wspec, wspec, wspec, bspec, bspec, bspec],
        out_specs=pl.BlockSpec(memory_space=pl.ANY),
        scratch_shapes=[pltpu.VMEM((2, tme, D_MODEL), BF16), pltpu.VMEM((2, tme, D_MODEL), BF16),
                        pltpu.SemaphoreType.DMA((2,)), pltpu.SemaphoreType.DMA((2,))],
    )
    return pl.pallas_call(
        _moe_kernel,
        grid_spec=grid_spec,
        out_shape=jax.ShapeDtypeStruct(xs.shape, xs.dtype),
        input_output_aliases={6: 0},
        compiler_params=pltpu.CompilerParams(dimension_semantics=("arbitrary",), vmem_limit_bytes=VMEM_LIMIT),
        name="moe",
    )(tile_expert, n_used, n_real, idx, idx, idx, xs, wg, wu, wd, bg, bu, bd)


def _combine_kernel(h_ref, ys_ref, lpos_ref, pr_ref, nw_ref, o_ref):
    tp = h_ref.shape[0]
    lr = ys_ref.shape[0]
    ri = lax.broadcasted_iota(I32, (tp, lr), 1)
    lp = lpos_ref[...]
    pr = pr_ref[...]
    pw = None
    for k in range(TOP_K):
        term = jnp.where(ri == lp[:, k:k + 1], pr[:, k:k + 1], 0.0)
        pw = term if pw is None else pw + term
    hi, mid, _ = _split3(pw)
    ys = ys_ref[...]
    moe = jnp.dot(hi, ys, preferred_element_type=F32) + jnp.dot(mid, ys, preferred_element_type=F32)
    h = h_ref[...] + moe
    o_ref[...] = h * lax.rsqrt(jnp.mean(h * h, axis=-1, keepdims=True) + EPS) * nw_ref[...]


def _combine(h, ys, lpos_t, probs_t, final_nw, tm, lr, ys_block0):
    t = h.shape[0]
    return pl.pallas_call(
        _combine_kernel,
        grid=(t // tm,),
        in_specs=[pl.BlockSpec((tm, D_MODEL), lambda i: (i, 0)),
                  pl.BlockSpec((lr, D_MODEL), lambda i: (ys_block0 + i, 0)),
                  pl.BlockSpec((tm, TOP_K), lambda i: (i, 0)),
                  pl.BlockSpec((tm, TOP_K), lambda i: (i, 0)),
                  _const_spec(final_nw.shape)],
        out_specs=pl.BlockSpec((tm, D_MODEL), lambda i: (i, 0)),
        out_shape=jax.ShapeDtypeStruct((t, D_MODEL), F32),
        compiler_params=pltpu.CompilerParams(dimension_semantics=("arbitrary",), vmem_limit_bytes=VMEM_LIMIT),
        name="combine",
    )(h, ys, lpos_t, probs_t, final_nw)


def _piece_table(padded_counts, tile_row0, n_tiles):
    n_pieces = (padded_counts // PIECE).T
    seg_row = tile_row0[:, None] + jnp.cumsum(padded_counts, axis=1) - padded_counts
    seg_piece = (seg_row // PIECE).T
    per_expert = n_pieces.sum(axis=1)
    tiles_per = (per_expert + PIECES_PER_TILE - 1) // PIECES_PER_TILE
    tile_end = jnp.cumsum(tiles_per)
    n_used = tile_end[-1]
    slot0 = (tile_end - tiles_per) * PIECES_PER_TILE
    seg_slot = (slot0[:, None] + jnp.cumsum(n_pieces, axis=1) - n_pieces).reshape(-1)
    seg_n = n_pieces.reshape(-1)
    seg_src = seg_piece.reshape(-1)
    slots = jnp.arange(n_tiles * PIECES_PER_TILE, dtype=I32)
    seg = jnp.sum((seg_slot[None, :] <= slots[:, None]).astype(I32), axis=1) - 1
    within = slots - seg_slot[seg]
    real = within < seg_n[seg]
    src = jnp.where(real, seg_src[seg] + within, 0).astype(I32)
    n_real = jnp.sum(real.reshape(n_tiles, PIECES_PER_TILE).astype(I32), axis=1)
    tile_ids = jnp.arange(n_tiles, dtype=I32)
    tile_expert = jnp.sum((tile_end[None, :] <= jnp.minimum(tile_ids, n_used - 1)[:, None]).astype(I32), axis=1)
    return tile_expert.astype(I32), n_used.reshape(1).astype(I32), n_real, src


def _pick_tile(n, pref):
    while n % pref:
        pref //= 2
    return pref


def _local_rows(tm, n_exp):
    need = TOP_K * tm + n_exp * (PIECE - 1)
    return -(-need // 64) * 64


def kernel(x_prompt, x_sample, state_conv, state_ssm, cache_swa_k, cache_swa_v, attn_norm_w, w_in, conv_w, conv_b, dt_bias, a_log, d_skip, ssm_norm_w, attn_sinks, w_ssm_proj, w_attn_proj, w_o, ffn_norm_w, w_router, b_router, w_gate, b_gate, w_up, b_up, w_down, b_down, final_norm_w):
    assert w_in.shape[0] == 1, "single-layer step"
    nb, seq, _ = x_prompt.shape
    nbs = x_sample.shape[0]
    n_exp = w_router.shape[-1]
    tp, ts = nb * seq, nbs
    pad = SUBLANES

    cuts = [0]
    for n in (D_INNER, CONV_DIM, SSM_HEADS, D_ATTN, D_KV, D_KV, 2 * D_MODEL):
        cuts.append(cuts[-1] + n)
    w_in_b = w_in[0].astype(BF16)
    ws = [w_in_b[:, cuts[i]:cuts[i + 1]] for i in range(7)]
    attn_nw = attn_norm_w[0].reshape(1, D_MODEL)
    a_neg = -jnp.exp(a_log[0].astype(F32))
    head_of = jnp.arange(D_INNER, dtype=I32) // SSM_HEAD_DIM
    expand = (jnp.arange(SSM_HEADS, dtype=I32)[:, None] == head_of[None, :]).astype(BF16)
    ssd_consts = (conv_w[0], conv_b[0].reshape(1, CONV_DIM), dt_bias[0].reshape(1, SSM_HEADS),
                  dt_bias[0].reshape(SSM_HEADS, 1), a_neg.reshape(1, SSM_HEADS), a_neg.reshape(SSM_HEADS, 1),
                  d_skip[0][head_of].reshape(1, D_INNER), ssm_norm_w[0].reshape(1, D_INNER), expand)
    slopes = jnp.exp2(-8.0 * jnp.arange(1, N_Q_HEADS + 1, dtype=F32) / N_Q_HEADS)
    sinks = attn_sinks[0].astype(F32)
    wsp, wap, wo = w_ssm_proj[0].astype(BF16), w_attn_proj[0].astype(BF16), w_o[0].astype(BF16)
    ffn_nw = ffn_norm_w[0].reshape(1, D_MODEL)
    w_rT = w_router[0].T
    b_r = b_router[0].reshape(n_exp, 1)
    wg, wu, wd = w_gate[0].astype(BF16), w_up[0].astype(BF16), w_down[0].astype(BF16)
    bg, bu, bd = (b[0].reshape(n_exp, 1, -1) for b in (b_gate, b_up, b_down))
    final_nw = final_norm_w.reshape(1, D_MODEL)

    xp = x_prompt.reshape(tp, D_MODEL)
    z, xbc, dt, q, k, v, gates = _in_proj(xp, attn_nw, ws, _pick_tile(tp, 512), BF16)
    nc = seq // CHUNK
    dtT = dt.reshape(nb * nc, CHUNK, SSM_HEADS).transpose(0, 2, 1)
    y_ssm, ssm_p = _ssd(xbc, z, dt, dtT, jnp.zeros((nb * SUBLANES, CONV_DIM), F32),
                        jnp.zeros((nb, SSM_HEADS, SSM_HEAD_DIM, D_STATE), F32), ssd_consts, nb, nc, CHUNK, CHUNK)
    nblk = seq // WINDOW
    y_attn = _swa(slopes, sinks, q, k, v, k, v, nb, nblk, WINDOW, False)

    xs_pad = jnp.pad(x_sample.reshape(ts, 1, D_MODEL), ((0, 0), (0, pad - 1), (0, 0))).reshape(ts * pad, D_MODEL)
    z_s, xbc_s, dt_s, q_s, k_s, v_s, gates_s = _in_proj(xs_pad, attn_nw, ws, _pick_tile(ts * pad, 256), F32)
    dtT_s = dt_s.reshape(ts, pad, SSM_HEADS).transpose(0, 2, 1)
    hist_s = jnp.pad(state_conv[0], ((0, 0), (pad - (CONV_W - 1), 0), (0, 0))).reshape(ts * pad, CONV_DIM)
    y_ssm_s, ssm_s = _ssd(xbc_s, z_s, dt_s, dtT_s, hist_s, state_ssm[0], ssd_consts, ts, 1, pad, 1)
    kc = cache_swa_k[0].reshape(ts * WINDOW, D_KV)
    vc = cache_swa_v[0].reshape(ts * WINDOW, D_KV)
    y_attn_s = _swa(slopes, sinks, q_s, k_s, v_s, kc, vc, ts, 1, pad, True)
    real = lambda a: a.reshape(ts, pad, -1)[:, 0]

    tm_p, tm_s = _pick_tile(tp, 512), ts
    nt_p = tp // tm_p
    lr_p, lr_s = _local_rows(tm_p, n_exp), _local_rows(tm_s, n_exp)
    if (nt_p * lr_p) % lr_s or lr_s > lr_p:
        lr_s = lr_p
    xs_rows = (nt_p + 1) * lr_p
    block_s = nt_p * lr_p // lr_s
    post_w = (wsp, wap, wo, ffn_nw, w_rT, b_r)
    h_p, xs, lpos_p, pr_p, pc_p = _post(xp, y_ssm, y_attn, gates, *post_w, tm_p, lr_p, xs_rows, 0, 1)
    h_s, xs, lpos_s, pr_s, pc_s = _post(x_sample.reshape(ts, D_MODEL), real(y_ssm_s), real(y_attn_s), real(gates_s),
                                        *post_w, tm_s, lr_s, xs_rows, block_s, 0, xs_prev=xs)

    padded_counts = jnp.concatenate([pc_p[:, :, 0], pc_s[:, :, 0]], axis=0)
    tile_row0 = jnp.concatenate([jnp.arange(nt_p, dtype=I32) * lr_p, jnp.full((1,), nt_p * lr_p, I32)])
    max_pieces = (TOP_K * (tp + ts) + (PIECE - 1) * n_exp * (nt_p + 1)) // PIECE + n_exp * (PIECES_PER_TILE - 1)
    n_tiles = -(-max_pieces // PIECES_PER_TILE)
    tile_expert, n_used, n_real, src = _piece_table(padded_counts, tile_row0, n_tiles)
    ys = _moe(tile_expert, n_used, n_real, src, xs, wg, wu, wd, bg, bu, bd)
    out_p = _combine(h_p, ys, lpos_p.T, pr_p.T, final_nw, tm_p, lr_p, 0)
    out_s = _combine(h_s, ys, lpos_s.T, pr_s.T, final_nw, tm_s, lr_s, block_s)

    y_prompt = out_p.reshape(nb, seq, D_MODEL)
    y_sample = out_s.reshape(nbs, 1, D_MODEL)
    conv_p = xbc.reshape(nb, seq, CONV_DIM)[:, seq - (CONV_W - 1):].astype(F32)[None]
    k_p = k.reshape(nb, seq, N_KV_HEADS, ATTN_HEAD_DIM)[:, seq - WINDOW:].astype(F32)[None]
    v_p = v.reshape(nb, seq, N_KV_HEADS, ATTN_HEAD_DIM)[:, seq - WINDOW:].astype(F32)[None]
    conv_s = jnp.concatenate([state_conv[0][:, 1:], real(xbc_s)[:, None]], axis=1)[None]
    k_new = real(k_s).reshape(ts, 1, N_KV_HEADS, ATTN_HEAD_DIM)
    v_new = real(v_s).reshape(ts, 1, N_KV_HEADS, ATTN_HEAD_DIM)
    ks_out = jnp.concatenate([cache_swa_k[0][:, 1:], k_new], axis=1)[None]
    vs_out = jnp.concatenate([cache_swa_v[0][:, 1:], v_new], axis=1)[None]
    return (y_prompt, y_sample, conv_p, ssm_p[None], k_p, v_p, conv_s, ssm_s[None], ks_out, vs_out)
```

```python
import functools

import jax
import jax.numpy as jnp
from jax import lax
from jax.experimental import pallas as pl
from jax.experimental.pallas import tpu as pltpu

F32, BF16, I32 = jnp.float32, jnp.bfloat16, jnp.int32

D_MODEL = 1024
D_INNER = 2 * D_MODEL
SSM_HEAD_DIM = 64
SSM_HEADS = D_INNER // SSM_HEAD_DIM
SSM_GROUPS = 4
SSM_HPG = SSM_HEADS // SSM_GROUPS
D_STATE = 128
CONV_W = 4
CONV_DIM = D_INNER + 2 * SSM_GROUPS * D_STATE
CHUNK = 128
ATTN_HEAD_DIM = 64
N_Q_HEADS = D_MODEL // ATTN_HEAD_DIM
N_KV_HEADS = 4
Q_PER_KV = N_Q_HEADS // N_KV_HEADS
D_ATTN = N_Q_HEADS * ATTN_HEAD_DIM
D_KV = N_KV_HEADS * ATTN_HEAD_DIM
WINDOW = 128
TOP_K = 4
D_FF = D_MODEL
SWIGLU_LIMIT = 7.0
SWIGLU_ALPHA = 1.702
EPS = 1e-5
NEG_BIG = -1e30

LANES = 128
SUBLANES = 8
GROUP_W = D_INNER // SSM_GROUPS
PIECE = 2 * SUBLANES
PIECES_PER_TILE = 32
VMEM_LIMIT = 56 * 1024 * 1024

NT = (((1,), (1,)), ((), ()))
TN = (((0,), (0,)), ((), ()))


def _const_spec(shape):
    return pl.BlockSpec(shape, lambda *_: (0,) * len(shape))


def _resident_spec(shape):
    return pl.BlockSpec(shape, lambda *_: (0,) * len(shape), pipeline_mode=pl.Buffered(1))


def _split3(x):
    hi = x.astype(BF16)
    r1 = x - hi.astype(F32)
    mid = r1.astype(BF16)
    lo = (r1 - mid.astype(F32)).astype(BF16)
    return hi, mid, lo


def _softplus(x):
    return jnp.maximum(x, 0.0) + jnp.log(1.0 + jnp.exp(-jnp.abs(x)))


def _causal_conv_silu(buf_ref, n, cw_ref, cb_ref):
    full = buf_ref[...]
    conv = cb_ref[...]
    for j in range(CONV_W):
        shifted = full if j == CONV_W - 1 else pltpu.roll(full, CONV_W - 1 - j, 0)
        conv = conv + shifted[SUBLANES:SUBLANES + n, :] * cw_ref[j:j + 1, :]
    buf_ref[0:SUBLANES, :] = full[n:n + SUBLANES, :]
    return conv * jax.nn.sigmoid(conv)


def _inproj_kernel(x_ref, nw_ref, wz_ref, wxbc_ref, wdt_ref, wq_ref, wk_ref, wv_ref, wg_ref, *rest,
                   fuse_conv, tiles_per_seq):
    if fuse_conv:
        hist_ref, cw_ref, cb_ref, z_ref, xbc_ref, dt_ref, q_ref, k_ref, v_ref, g_ref, tail_ref, buf_ref = rest
    else:
        z_ref, xbc_ref, dt_ref, q_ref, k_ref, v_ref, g_ref = rest
    tm = x_ref.shape[0]
    if fuse_conv:
        pos = pl.program_id(0) % tiles_per_seq

        @pl.when(pos == 0)
        def _():
            buf_ref[0:SUBLANES, :] = hist_ref[...]

    x = x_ref[...]
    xn = x * lax.rsqrt(jnp.mean(x * x, axis=-1, keepdims=True) + EPS)
    xn = (xn * nw_ref[...]).astype(BF16)
    xbc = jnp.dot(xn, wxbc_ref[...], preferred_element_type=F32)
    if fuse_conv:
        buf_ref[SUBLANES:SUBLANES + tm, :] = xbc
        xbc_ref[...] = _causal_conv_silu(buf_ref, tm, cw_ref, cb_ref).astype(xbc_ref.dtype)
    else:
        xbc_ref[...] = xbc.astype(xbc_ref.dtype)
    for w_ref, o_ref in ((wz_ref, z_ref), (wdt_ref, dt_ref), (wq_ref, q_ref),
                         (wk_ref, k_ref), (wv_ref, v_ref), (wg_ref, g_ref)):
        o_ref[...] = jnp.dot(xn, w_ref[...], preferred_element_type=F32).astype(o_ref.dtype)
    if fuse_conv:
        @pl.when(pos == tiles_per_seq - 1)
        def _():
            tail_ref[0] = buf_ref[0:SUBLANES, :]


def _in_proj(x, norm_w, ws, tm, act_dtype, conv=None):
    t = x.shape[0]
    widths = (D_INNER, CONV_DIM, SSM_HEADS, D_ATTN, D_KV, D_KV, 2 * D_MODEL)
    dtypes = (act_dtype, act_dtype, F32, act_dtype, act_dtype, act_dtype, act_dtype)
    row = lambda n: pl.BlockSpec((tm, n), lambda i: (i, 0))
    in_specs = [row(D_MODEL), _const_spec((1, D_MODEL))] + [_resident_spec((D_MODEL, n)) for n in widths]
    out_specs = [row(n) for n in widths]
    out_shape = [jax.ShapeDtypeStruct((t, n), d) for n, d in zip(widths, dtypes)]
    args, scratch, tiles_per_seq = [x, norm_w, *ws], [], 1
    if conv is not None:
        hist, cw, cb, tiles_per_seq = conv
        n_seq = t // (tm * tiles_per_seq)
        in_specs += [pl.BlockSpec((SUBLANES, CONV_DIM), lambda i: (i // tiles_per_seq, 0)),
                     _const_spec(cw.shape), _const_spec(cb.shape)]
        out_specs.append(pl.BlockSpec((1, SUBLANES, CONV_DIM), lambda i: (i // tiles_per_seq, 0, 0)))
        out_shape.append(jax.ShapeDtypeStruct((n_seq, SUBLANES, CONV_DIM), F32))
        args += [hist, cw, cb]
        scratch = [pltpu.VMEM((SUBLANES + tm, CONV_DIM), F32)]
    return pl.pallas_call(
        functools.partial(_inproj_kernel, fuse_conv=conv is not None, tiles_per_seq=tiles_per_seq),
        grid=(t // tm,),
        in_specs=in_specs,
        out_specs=out_specs,
        out_shape=out_shape,
        scratch_shapes=scratch,
        compiler_params=pltpu.CompilerParams(dimension_semantics=("arbitrary",), vmem_limit_bytes=VMEM_LIMIT),
        name="in_proj",
    )(*args)


def _ssd_kernel(xbc_ref, z_ref, dt_ref, dtT_ref, h0_ref, dtb_ref, dtbT_ref, a_ref, aT_ref, dsk_ref, nw_ref, e_ref,
                *rest, lc, n_valid, conv):
    if conv:
        hist_ref, cw_ref, cb_ref, y_ref, h_ref, buf_ref = rest
    else:
        y_ref, h_ref = rest
    c = pl.program_id(1)

    @pl.when(c == 0)
    def _():
        h_ref[...] = h0_ref[...]

    if conv:
        @pl.when(c == 0)
        def _():
            buf_ref[0:SUBLANES, :] = hist_ref[...]

        buf_ref[SUBLANES:SUBLANES + lc, :] = xbc_ref[...].astype(F32)
        act = _causal_conv_silu(buf_ref, lc, cw_ref, cb_ref)
    else:
        act = xbc_ref[...].astype(F32)
    xs = act[:, :D_INNER]
    bm = act[:, D_INNER:D_INNER + SSM_GROUPS * D_STATE].astype(BF16)
    cm = act[:, D_INNER + SSM_GROUPS * D_STATE:].astype(BF16)

    dt = _softplus(dt_ref[...] + dtb_ref[...])
    dtT = _softplus(dtT_ref[0] + dtbT_ref[...])
    if n_valid < lc:
        dt = jnp.where(lax.broadcasted_iota(I32, dt.shape, 0) < n_valid, dt, 0.0)
        dtT = jnp.where(lax.broadcasted_iota(I32, dtT.shape, 1) < n_valid, dtT, 0.0)
    la = dt * a_ref[...]
    laT = dtT * aT_ref[...]
    li = lax.broadcasted_iota(I32, (lc, lc), 0)
    si = lax.broadcasted_iota(I32, (lc, lc), 1)
    causal = li >= si
    tril = jnp.where(causal, 1.0, 0.0).astype(BF16)
    triu = jnp.where(li <= si, 1.0, 0.0).astype(BF16)
    cum = sum(jnp.dot(tril, p, preferred_element_type=F32) for p in _split3(la))
    cumT = sum(jnp.dot(p, triu, preferred_element_type=F32) for p in _split3(laT))
    ec = jnp.exp(cum)
    dte = jnp.exp(cum[lc - 1:lc, :] - cum)
    cd = jnp.exp(cumT[:, lc - 1:lc])

    def expand(v):
        hi, mid, _ = _split3(v)
        return (jnp.dot(hi, e_ref[...], preferred_element_type=F32)
                + jnp.dot(mid, e_ref[...], preferred_element_type=F32))

    dt_x, ec_x, dte_x = expand(dt), expand(ec), expand(dte)
    xdt = xs * dt_x
    xdt_b = xdt.astype(BF16)
    xdte_b = (xdt * dte_x).astype(BF16)
    lane = lax.broadcasted_iota(I32, (lc, LANES), 1)
    low_half = lane < SSM_HEAD_DIM

    for g in range(SSM_GROUPS):
        gs = slice(g * GROUP_W, (g + 1) * GROUP_W)
        bm_g = bm[:, g * D_STATE:(g + 1) * D_STATE]
        cm_g = cm[:, g * D_STATE:(g + 1) * D_STATE]
        cb = lax.dot_general(cm_g, bm_g, NT, preferred_element_type=F32)
        cbm = jnp.where(causal, cb, 0.0)
        h_g = h_ref[0, g * SSM_HPG:(g + 1) * SSM_HPG].reshape(GROUP_W, D_STATE)
        y_off = lax.dot_general(cm_g, h_g.astype(BF16), NT, preferred_element_type=F32) * ec_x[:, gs]
        tiles = []
        for j in range(GROUP_W // LANES):
            col = g * GROUP_W + j * LANES
            x_pair = xdt_b[:, col:col + LANES]
            acc = None
            for half in range(2):
                h = col // SSM_HEAD_DIM + half
                seg = cum[:, h:h + 1] - cumT[h:h + 1, :]
                m = (cbm * jnp.exp(jnp.where(causal, seg, 0.0))).astype(BF16)
                x_h = jnp.where(low_half if half == 0 else jnp.logical_not(low_half), x_pair, jnp.zeros_like(x_pair))
                d = jnp.dot(m, x_h, preferred_element_type=F32)
                acc = d if acc is None else acc + d
            tiles.append(acc)
        y_g = jnp.concatenate(tiles, axis=1) + y_off + xs[:, gs] * dsk_ref[:, gs]
        zg = z_ref[:, gs].astype(F32)
        y_g = y_g * (zg * jax.nn.sigmoid(zg))
        y_g = y_g * lax.rsqrt(jnp.mean(y_g * y_g, axis=-1, keepdims=True) + EPS)
        y_ref[:, gs] = (y_g * nw_ref[:, gs]).astype(y_ref.dtype)
        st = lax.dot_general(xdte_b[:, gs], bm_g, TN, preferred_element_type=F32)
        for hh in range(SSM_HPG):
            h = g * SSM_HPG + hh
            rows = slice(hh * SSM_HEAD_DIM, (hh + 1) * SSM_HEAD_DIM)
            h_ref[0, h] = h_g[rows, :] * cd[h:h + 1, 0:1] + st[rows, :]


def _ssd(xbc, z, dt, dtT, h0, consts, nb, nc, lc, n_valid, conv=None):
    t = xbc.shape[0]
    row = lambda n: pl.BlockSpec((lc, n), lambda b, c: (b * nc + c, 0))
    in_specs = [
        row(CONV_DIM), row(D_INNER), row(SSM_HEADS),
        pl.BlockSpec((1, SSM_HEADS, lc), lambda b, c: (b * nc + c, 0, 0)),
        pl.BlockSpec((1, SSM_HEADS, SSM_HEAD_DIM, D_STATE), lambda b, c: (b, 0, 0, 0)),
    ] + [_const_spec(a.shape) for a in consts]
    args, scratch = [xbc, z, dt, dtT, h0, *consts], []
    if conv is not None:
        hist, cw, cb = conv
        in_specs += [pl.BlockSpec((SUBLANES, CONV_DIM), lambda b, c: (b, 0)), _const_spec(cw.shape),
                     _const_spec(cb.shape)]
        args += [hist, cw, cb]
        scratch = [pltpu.VMEM((SUBLANES + lc, CONV_DIM), F32)]
    return pl.pallas_call(
        functools.partial(_ssd_kernel, lc=lc, n_valid=n_valid, conv=conv is not None),
        grid=(nb, nc),
        in_specs=in_specs,
        out_specs=[row(D_INNER), pl.BlockSpec((1, SSM_HEADS, SSM_HEAD_DIM, D_STATE), lambda b, c: (b, 0, 0, 0))],
        out_shape=[jax.ShapeDtypeStruct((t, D_INNER), BF16),
                   jax.ShapeDtypeStruct((nb, SSM_HEADS, SSM_HEAD_DIM, D_STATE), F32)],
        scratch_shapes=scratch,
        compiler_params=pltpu.CompilerParams(dimension_semantics=("arbitrary", "arbitrary"),
                                             vmem_limit_bytes=VMEM_LIMIT),
        name="ssd",
    )(*args)


def _swa_kernel(slope_ref, sink_ref, q_ref, kc_ref, vc_ref, kp_ref, vp_ref, y_ref, *, tq, prev_always):
    nk = WINDOW + tq
    j = lax.broadcasted_iota(I32, (nk, tq), 0)
    r = lax.broadcasted_iota(I32, (nk, tq), 1)
    dist = r + WINDOW - j
    valid = (dist >= 0) & (dist < WINDOW)
    if not prev_always:
        valid = valid & ((j >= WINDOW) | (pl.program_id(1) > 0))
    distf = dist.astype(F32)
    lane = lax.broadcasted_iota(I32, (nk, LANES), 1)
    transposed_out = tq == WINDOW
    zeros_half = jnp.zeros((ATTN_HEAD_DIM, nk), BF16)
    heads = []
    for t in range(D_KV // LANES):
        cols = slice(t * LANES, (t + 1) * LANES)
        kt = jnp.concatenate([kp_ref[:, cols].astype(F32), kc_ref[:, cols].astype(F32)], axis=0)
        vt = jnp.concatenate([vp_ref[:, cols].astype(F32), vc_ref[:, cols].astype(F32)], axis=0)
        if transposed_out:
            vt_t = vt.T.astype(BF16)
        for b in range(2):
            mine = (lane >= ATTN_HEAD_DIM) if b else (lane < ATTN_HEAD_DIM)
            k_same = jnp.where(mine, kt, 0.0)
            k_half = {b: k_same.astype(BF16), 1 - b: pltpu.roll(k_same, ATTN_HEAD_DIM, 1).astype(BF16)}
            if transposed_out:
                v_g = vt_t[b * ATTN_HEAD_DIM:(b + 1) * ATTN_HEAD_DIM, :]
                v_half = {0: jnp.concatenate([v_g, zeros_half], axis=0),
                          1: jnp.concatenate([zeros_half, v_g], axis=0)}
            else:
                v_same = jnp.where(mine, vt, 0.0)
                v_half = {b: v_same.astype(BF16), 1 - b: pltpu.roll(v_same, ATTN_HEAD_DIM, 1).astype(BF16)}
            for qi in range(Q_PER_KV):
                a = qi % 2
                heads.append((k_half[a], v_half[a]))

    def scores(h):
        qt = q_ref[:, (h // 2) * LANES:(h // 2 + 1) * LANES].astype(BF16)
        s = lax.dot_general(heads[h][0], qt, NT, preferred_element_type=F32)
        s = s * (ATTN_HEAD_DIM ** -0.5) - slope_ref[h] * distf
        return jnp.where(valid, s, NEG_BIG)

    def attend(h, s):
        sink = sink_ref[h]
        m = jnp.maximum(jnp.max(s, axis=0, keepdims=True), sink)
        e = jnp.exp(s - m)
        rden = 1.0 / (jnp.sum(e, axis=0, keepdims=True) + jnp.exp(sink - m))
        if transposed_out:
            return jnp.dot(heads[h][1], e.astype(BF16), preferred_element_type=F32) * rden
        return lax.dot_general((e * rden).astype(BF16), heads[h][1], TN, preferred_element_type=F32)

    out_tiles = [None] * (D_ATTN // LANES)
    s_next = scores(0)
    for h in range(N_Q_HEADS):
        s_cur = s_next
        if h + 1 < N_Q_HEADS:
            s_next = scores(h + 1)
        o = attend(h, s_cur)
        out_tiles[h // 2] = o if out_tiles[h // 2] is None else out_tiles[h // 2] + o
    for jq, o in enumerate(out_tiles):
        y_ref[:, jq * LANES:(jq + 1) * LANES] = (o.T if transposed_out else o).astype(y_ref.dtype)


def _swa(slopes, sinks, q, k, v, k_prev, v_prev, nb, nblk, tq, prev_always):
    t = q.shape[0]
    cur = lambda n: pl.BlockSpec((tq, n), lambda b, i: (b * nblk + i, 0))
    if prev_always:
        prev = pl.BlockSpec((WINDOW, D_KV), lambda b, i: (b, 0))
    else:
        prev = pl.BlockSpec((WINDOW, D_KV), lambda b, i: (b * nblk + jnp.maximum(i - 1, 0), 0))
    smem = pl.BlockSpec(memory_space=pltpu.SMEM)
    return pl.pallas_call(
        functools.partial(_swa_kernel, tq=tq, prev_always=prev_always),
        grid=(nb, nblk),
        in_specs=[smem, smem, cur(D_ATTN), cur(D_KV), cur(D_KV), prev, prev],
        out_specs=cur(D_ATTN),
        out_shape=jax.ShapeDtypeStruct((t, D_ATTN), BF16),
        compiler_params=pltpu.CompilerParams(dimension_semantics=("arbitrary", "arbitrary"),
                                             vmem_limit_bytes=VMEM_LIMIT),
        name="swa",
    )(slopes, sinks, q, k, v, k_prev, v_prev)


def _post_kernel(x_ref, ys_ref, ya_ref, g_ref, wsp_ref, wap_ref, wo_ref, nw_ref, wr_ref, br_ref, *rest,
                 n_exp, lr, nt, aliased):
    if aliased:
        rest = rest[1:]
    h_ref, xs_ref, lpos_ref, pr_ref, pc_ref = rest
    tp = x_ref.shape[0]

    @pl.when(pl.program_id(0) >= nt)
    def _():
        xs_ref[...] = jnp.zeros_like(xs_ref)

    @pl.when(pl.program_id(0) < nt)
    def _():
        a = jnp.dot(ys_ref[...].astype(BF16), wsp_ref[...], preferred_element_type=F32)
        b = jnp.dot(ya_ref[...].astype(BF16), wap_ref[...], preferred_element_type=F32)
        g = g_ref[...].astype(F32)
        merged = jax.nn.sigmoid(g[:, :D_MODEL]) * a + jax.nn.sigmoid(g[:, D_MODEL:]) * b
        h = x_ref[...] + jnp.dot(merged.astype(BF16), wo_ref[...], preferred_element_type=F32)
        h_ref[...] = h
        hn = h * lax.rsqrt(jnp.mean(h * h, axis=-1, keepdims=True) + EPS) * nw_ref[...]
        w_hi, w_mid, _ = _split3(wr_ref[...])
        x_hi, x_mid, _ = _split3(hn)
        logits = (lax.dot_general(w_hi, x_hi, NT, preferred_element_type=F32)
                  + lax.dot_general(w_hi, x_mid, NT, preferred_element_type=F32)
                  + lax.dot_general(w_mid, x_hi, NT, preferred_element_type=F32)) + br_ref[...]
        eidx = lax.broadcasted_iota(I32, logits.shape, 0).astype(F32)
        work = logits
        vals, ids = [], []
        for _ in range(TOP_K):
            m = jnp.max(work, axis=0, keepdims=True)
            first = jnp.min(jnp.where(work == m, eidx, float(n_exp)), axis=0, keepdims=True)
            vals.append(m)
            ids.append(first)
            work = jnp.where(eidx == first, -jnp.inf, work)
        es = [jnp.exp(v - vals[0]) for v in vals]
        den = es[0] + es[1] + es[2] + es[3]
        pr_ref[...] = jnp.concatenate([e / den for e in es], axis=0)

        onehot = [jnp.where(eidx == i, 1.0, 0.0) for i in ids]
        counts = [jnp.sum(o, axis=1, keepdims=True) for o in onehot]
        total = counts[0] + counts[1] + counts[2] + counts[3]
        padded = jnp.floor((total + (PIECE - 1)) * (1.0 / PIECE)) * PIECE
        ei = lax.broadcasted_iota(I32, (n_exp, n_exp), 0)
        ej = lax.broadcasted_iota(I32, (n_exp, n_exp), 1)
        below = jnp.where(ej < ei, 1.0, 0.0).astype(BF16)
        padded_b = jnp.broadcast_to(padded, (n_exp, LANES))
        seg_off = sum(jnp.dot(below, p, preferred_element_type=F32) for p in _split3(padded_b))[:, 0:1]
        ti = lax.broadcasted_iota(I32, (tp, tp), 0)
        tj = lax.broadcasted_iota(I32, (tp, tp), 1)
        before = jnp.where(ti < tj, 1.0, 0.0).astype(BF16)
        base = seg_off
        lpos = []
        for k in range(TOP_K):
            prefix = jnp.dot(onehot[k].astype(BF16), before, preferred_element_type=F32)
            lpos.append(jnp.sum(onehot[k] * (base + prefix), axis=0, keepdims=True))
            base = base + counts[k]
        lpos_ref[...] = jnp.concatenate(lpos, axis=0).astype(I32)
        pc_ref[0] = padded_b.astype(I32)

        hn_b = hn.astype(BF16)
        rc = lr // 4
        for c in range(4):
            ri = (lax.broadcasted_iota(I32, (rc, tp), 0) + c * rc).astype(F32)
            sel = None
            for k in range(TOP_K):
                hit = jnp.where(ri == lpos[k], 1.0, 0.0)
                sel = hit if sel is None else sel + hit
            xs_ref[c * rc:(c + 1) * rc, :] = jnp.dot(sel.astype(BF16), hn_b, preferred_element_type=F32).astype(BF16)


def _post(x, y_ssm, y_attn, gates, wsp, wap, wo, ffn_nw, w_rT, b_r, tm, lr, xs_rows, xs_block0, pad_steps,
          xs_prev=None):
    t = x.shape[0]
    n_exp = w_rT.shape[0]
    nt = t // tm
    last = nt - 1
    row = lambda n: pl.BlockSpec((tm, n), lambda i: (jnp.minimum(i, last), 0))
    col = pl.BlockSpec((TOP_K, tm), lambda i: (0, jnp.minimum(i, last)))
    in_specs = [row(D_MODEL), row(D_INNER), row(D_ATTN), row(2 * D_MODEL),
                _resident_spec(wsp.shape), _resident_spec(wap.shape), _resident_spec(wo.shape),
                _const_spec(ffn_nw.shape), _const_spec(w_rT.shape), _const_spec(b_r.shape)]
    args = [x, y_ssm, y_attn, gates, wsp, wap, wo, ffn_nw, w_rT, b_r]
    aliases = {}
    if xs_prev is not None:
        in_specs.append(pl.BlockSpec(memory_space=pl.ANY))
        args.append(xs_prev)
        aliases = {len(args) - 1: 1}
    return pl.pallas_call(
        functools.partial(_post_kernel, n_exp=n_exp, lr=lr, nt=nt, aliased=xs_prev is not None),
        grid=(nt + pad_steps,),
        in_specs=in_specs,
        out_specs=[row(D_MODEL), pl.BlockSpec((lr, D_MODEL), lambda i: (xs_block0 + i, 0)), col, col,
                   pl.BlockSpec((1, n_exp, LANES), lambda i: (jnp.minimum(i, last), 0, 0))],
        out_shape=[jax.ShapeDtypeStruct((t, D_MODEL), F32), jax.ShapeDtypeStruct((xs_rows, D_MODEL), BF16),
                   jax.ShapeDtypeStruct((TOP_K, t), I32), jax.ShapeDtypeStruct((TOP_K, t), F32),
                   jax.ShapeDtypeStruct((nt, n_exp, LANES), I32)],
        input_output_aliases=aliases,
        compiler_params=pltpu.CompilerParams(dimension_semantics=("arbitrary",), vmem_limit_bytes=VMEM_LIMIT),
        name="post",
    )(*args)


def _piece(ref, p):
    return ref.at[pl.ds(pl.multiple_of(p * PIECE, PIECE), PIECE)]


def _moe_kernel(te_ref, nu_ref, nr_ref, src0_ref, src1_ref, srcc_ref, xs_hbm, wg_ref, wu_ref, wd_ref,
                bg_ref, bu_ref, bd_ref, ys_hbm, xbuf, obuf, gsem, ssem):
    i = pl.program_id(0)
    n_used = nu_ref[0]
    slot = i % 2

    def gather(src_ref, n, s):
        def body(r, carry):
            pltpu.make_async_copy(_piece(xs_hbm, src_ref[0, 0, r]), _piece(xbuf.at[s], r), gsem.at[s]).start()
            return carry
        lax.fori_loop(0, n, body, 0)

    def wait_gather(s, n):
        def body(r, carry):
            pltpu.make_async_copy(_piece(xs_hbm, 0), _piece(xbuf.at[s], 0), gsem.at[s]).wait()
            return carry
        lax.fori_loop(0, n, body, 0)

    def wait_put(s, n):
        def body(r, carry):
            pltpu.make_async_copy(_piece(obuf.at[s], 0), _piece(ys_hbm, 0), ssem.at[s]).wait()
            return carry
        lax.fori_loop(0, n, body, 0)

    @pl.when(i == 0)
    def _():
        xbuf[...] = jnp.zeros_like(xbuf)
        gather(src0_ref, nr_ref[0], 0)

    @pl.when(i + 1 < n_used)
    def _():
        gather(src1_ref, nr_ref[i + 1], 1 - slot)

    @pl.when(i < n_used)
    def _():
        n_real = nr_ref[i]
        wait_gather(slot, n_real)

        @pl.when(i >= 2)
        def _():
            wait_put(slot, nr_ref[i - 2])

        x = xbuf[slot]
        g = jnp.minimum(jnp.dot(x, wg_ref[0], preferred_element_type=F32) + bg_ref[0], SWIGLU_LIMIT)
        u = jnp.clip(jnp.dot(x, wu_ref[0], preferred_element_type=F32) + bu_ref[0], -SWIGLU_LIMIT, SWIGLU_LIMIT)
        act = ((u + 1.0) * g * jax.nn.sigmoid(SWIGLU_ALPHA * g)).astype(BF16)
        obuf[slot] = (jnp.dot(act, wd_ref[0], preferred_element_type=F32) + bd_ref[0]).astype(BF16)

        def put(r, carry):
            pltpu.make_async_copy(_piece(obuf.at[slot], r), _piece(ys_hbm, srcc_ref[0, 0, r]), ssem.at[slot]).start()
            return carry
        lax.fori_loop(0, n_real, put, 0)

        @pl.when(i == n_used - 1)
        def _():
            wait_put(slot, n_real)

            @pl.when(i >= 1)
            def _():
                wait_put(1 - slot, nr_ref[i - 1])


def _moe(tile_expert, n_used, n_real, src, xs, wg, wu, wd, bg, bu, bd):
    n_tiles = tile_expert.shape[0]
    idx = src.reshape(n_tiles, 1, PIECES_PER_TILE)
    tme = PIECES_PER_TILE * PIECE
    wspec = pl.BlockSpec((1, D_MODEL, D_FF), lambda i, te, nu, nr: (te[i], 0, 0))
    bspec = pl.BlockSpec((1, 1, D_FF), lambda i, te, nu, nr: (te[i], 0, 0))
    ispec = lambda f: pl.BlockSpec((1, 1, PIECES_PER_TILE), lambda i, te, nu, nr: (f(i), 0, 0),
                                   memory_space=pltpu.SMEM)
    grid_spec = pltpu.PrefetchScalarGridSpec(
        num_scalar_prefetch=3,
        grid=(n_tiles,),
        in_specs=[ispec(lambda i: 0), ispec(lambda i: jnp.minimum(i + 1, n_tiles - 1)), ispec(lambda i: i),
                  pl.BlockSpec(memory_space=pl.ANY), wspec, wspec, wspec, bspec, bspec, bspec],
        out_specs=pl.BlockSpec(memory_space=pl.ANY),
        scratch_shapes=[pltpu.VMEM((2, tme, D_MODEL), BF16), pltpu.VMEM((2, tme, D_MODEL), BF16),
                        pltpu.SemaphoreType.DMA((2,)), pltpu.SemaphoreType.DMA((2,))],
    )
    return pl.pallas_call(
        _moe_kernel,
        grid_spec=grid_spec,
        out_shape=jax.ShapeDtypeStruct(xs.shape, xs.dtype),
        input_output_aliases={6: 0},
        compiler_params=pltpu.CompilerParams(dimension_semantics=("arbitrary",), vmem_limit_bytes=VMEM_LIMIT),
        name="moe",
    )(tile_expert, n_used, n_real, idx, idx, idx, xs, wg, wu, wd, bg, bu, bd)


def _combine_kernel(h_ref, ys_ref, lpos_ref, pr_ref, nw_ref, o_ref):
    tp = h_ref.shape[0]
    lr = ys_ref.shape[0]
    ri = lax.broadcasted_iota(I32, (tp, lr), 1)
    lp = lpos_ref[...]
    pr = pr_ref[...]
    pw = jnp.zeros((tp, lr), F32)
    for k in range(TOP_K):
        pw = jnp.where(ri == lp[:, k:k + 1], pr[:, k:k + 1], pw)
    moe = jnp.dot(pw.astype(BF16), ys_ref[...], preferred_element_type=F32)
    h = h_ref[...] + moe
    o_ref[...] = h * lax.rsqrt(jnp.mean(h * h, axis=-1, keepdims=True) + EPS) * nw_ref[...]


def _combine(h, ys, lpos_t, probs_t, final_nw, tm, lr, ys_block0):
    t = h.shape[0]
    return pl.pallas_call(
        _combine_kernel,
        grid=(t // tm,),
        in_specs=[pl.BlockSpec((tm, D_MODEL), lambda i: (i, 0)),
                  pl.BlockSpec((lr, D_MODEL), lambda i: (ys_block0 + i, 0)),
                  pl.BlockSpec((tm, TOP_K), lambda i: (i, 0)),
                  pl.BlockSpec((tm, TOP_K), lambda i: (i, 0)),
                  _const_spec(final_nw.shape)],
        out_specs=pl.BlockSpec((tm, D_MODEL), lambda i: (i, 0)),
        out_shape=jax.ShapeDtypeStruct((t, D_MODEL), F32),
        compiler_params=pltpu.CompilerParams(dimension_semantics=("arbitrary",), vmem_limit_bytes=VMEM_LIMIT),
        name="combine",
    )(h, ys, lpos_t, probs_t, final_nw)


def _piece_table(padded_counts, tile_row0, n_tiles):
    n_pieces = (padded_counts // PIECE).T
    seg_row = tile_row0[:, None] + jnp.cumsum(padded_counts, axis=1) - padded_counts
    seg_piece = (seg_row // PIECE).T
    per_expert = n_pieces.sum(axis=1)
    tiles_per = (per_expert + PIECES_PER_TILE - 1) // PIECES_PER_TILE
    tile_end = jnp.cumsum(tiles_per)
    n_used = tile_end[-1]
    slot0 = (tile_end - tiles_per) * PIECES_PER_TILE
    seg_slot = (slot0[:, None] + jnp.cumsum(n_pieces, axis=1) - n_pieces).reshape(-1)
    seg_n = n_pieces.reshape(-1)
    seg_src = seg_piece.reshape(-1)
    slots = jnp.arange(n_tiles * PIECES_PER_TILE, dtype=I32)

    def at_segment_of_slot(f):
        df = f - jnp.concatenate([jnp.zeros((1,), I32), f[:-1]])
        return jnp.sum(jnp.where(seg_slot[None, :] <= slots[:, None], df[None, :], 0), axis=1)

    real = slots < at_segment_of_slot(seg_slot + seg_n)
    src = jnp.where(real, slots + at_segment_of_slot(seg_src - seg_slot), 0).astype(I32)
    n_real = jnp.sum(real.reshape(n_tiles, PIECES_PER_TILE).astype(I32), axis=1)
    tile_ids = jnp.arange(n_tiles, dtype=I32)
    tile_expert = jnp.sum((tile_end[None, :] <= jnp.minimum(tile_ids, n_used - 1)[:, None]).astype(I32), axis=1)
    return tile_expert.astype(I32), n_used.reshape(1).astype(I32), n_real, src


def _pick_tile(n, pref):
    while n % pref:
        pref //= 2
    return pref


def _local_rows(tm, n_exp):
    need = TOP_K * tm + n_exp * (PIECE - 1)
    return -(-need // 64) * 64


def kernel(x_prompt, x_sample, state_conv, state_ssm, cache_swa_k, cache_swa_v, attn_norm_w, w_in, conv_w, conv_b, dt_bias, a_log, d_skip, ssm_norm_w, attn_sinks, w_ssm_proj, w_attn_proj, w_o, ffn_norm_w, w_router, b_router, w_gate, b_gate, w_up, b_up, w_down, b_down, final_norm_w):
    assert w_in.shape[0] == 1, "single-layer step"
    nb, seq, _ = x_prompt.shape
    nbs = x_sample.shape[0]
    n_exp = w_router.shape[-1]
    tp, ts = nb * seq, nbs
    pad = SUBLANES

    cuts = [0]
    for n in (D_INNER, CONV_DIM, SSM_HEADS, D_ATTN, D_KV, D_KV, 2 * D_MODEL):
        cuts.append(cuts[-1] + n)
    w_in_b = w_in[0].astype(BF16)
    ws = [w_in_b[:, cuts[i]:cuts[i + 1]] for i in range(7)]
    attn_nw = attn_norm_w[0].reshape(1, D_MODEL)
    a_neg = -jnp.exp(a_log[0].astype(F32))
    head_of = jnp.arange(D_INNER, dtype=I32) // SSM_HEAD_DIM
    expand = (jnp.arange(SSM_HEADS, dtype=I32)[:, None] == head_of[None, :]).astype(BF16)
    conv_consts = (conv_w[0], conv_b[0].reshape(1, CONV_DIM))
    ssd_consts = (dt_bias[0].reshape(1, SSM_HEADS), dt_bias[0].reshape(SSM_HEADS, 1),
                  a_neg.reshape(1, SSM_HEADS), a_neg.reshape(SSM_HEADS, 1),
                  d_skip[0][head_of].reshape(1, D_INNER), ssm_norm_w[0].reshape(1, D_INNER), expand)
    slopes = jnp.exp2(-8.0 * jnp.arange(1, N_Q_HEADS + 1, dtype=F32) / N_Q_HEADS)
    sinks = attn_sinks[0].astype(F32)
    wsp, wap, wo = w_ssm_proj[0].astype(BF16), w_attn_proj[0].astype(BF16), w_o[0].astype(BF16)
    ffn_nw = ffn_norm_w[0].reshape(1, D_MODEL)
    w_rT = w_router[0].T
    b_r = b_router[0].reshape(n_exp, 1)
    wg, wu, wd = w_gate[0].astype(BF16), w_up[0].astype(BF16), w_down[0].astype(BF16)
    bg, bu, bd = (b[0].reshape(n_exp, 1, -1) for b in (b_gate, b_up, b_down))
    final_nw = final_norm_w.reshape(1, D_MODEL)

    xp = x_prompt.reshape(tp, D_MODEL)
    tm_in = _pick_tile(seq, 512)
    z, xbc, dt, q, k, v, gates, conv_tail = _in_proj(
        xp, attn_nw, ws, tm_in, BF16,
        conv=(jnp.zeros((nb * SUBLANES, CONV_DIM), F32), *conv_consts, seq // tm_in))
    nc = seq // CHUNK
    dtT = dt.reshape(nb * nc, CHUNK, SSM_HEADS).transpose(0, 2, 1)
    y_ssm, ssm_p = _ssd(xbc, z, dt, dtT, jnp.zeros((nb, SSM_HEADS, SSM_HEAD_DIM, D_STATE), F32), ssd_consts,
                        nb, nc, CHUNK, CHUNK)
    nblk = seq // WINDOW
    y_attn = _swa(slopes, sinks, q, k, v, k, v, nb, nblk, WINDOW, False)

    xs_pad = jnp.pad(x_sample.reshape(ts, 1, D_MODEL), ((0, 0), (0, pad - 1), (0, 0))).reshape(ts * pad, D_MODEL)
    z_s, xbc_s, dt_s, q_s, k_s, v_s, gates_s = _in_proj(xs_pad, attn_nw, ws, _pick_tile(ts * pad, 256), F32)
    dtT_s = dt_s.reshape(ts, pad, SSM_HEADS).transpose(0, 2, 1)
    hist_s = jnp.pad(state_conv[0], ((0, 0), (pad - (CONV_W - 1), 0), (0, 0))).reshape(ts * pad, CONV_DIM)
    y_ssm_s, ssm_s = _ssd(xbc_s, z_s, dt_s, dtT_s, state_ssm[0], ssd_consts, ts, 1, pad, 1,
                          conv=(hist_s, *conv_consts))
    kc = cache_swa_k[0].reshape(ts * WINDOW, D_KV)
    vc = cache_swa_v[0].reshape(ts * WINDOW, D_KV)
    y_attn_s = _swa(slopes, sinks, q_s, k_s, v_s, kc, vc, ts, 1, pad, True)
    real = lambda a: a.reshape(ts, pad, -1)[:, 0]

    tm_p, tm_s = _pick_tile(tp, 512), ts
    nt_p = tp // tm_p
    lr_p, lr_s = _local_rows(tm_p, n_exp), _local_rows(tm_s, n_exp)
    if (nt_p * lr_p) % lr_s or lr_s > lr_p:
        lr_s = lr_p
    xs_rows = (nt_p + 1) * lr_p
    block_s = nt_p * lr_p // lr_s
    post_w = (wsp, wap, wo, ffn_nw, w_rT, b_r)
    h_p, xs, lpos_p, pr_p, pc_p = _post(xp, y_ssm, y_attn, gates, *post_w, tm_p, lr_p, xs_rows, 0, 1)
    h_s, xs, lpos_s, pr_s, pc_s = _post(x_sample.reshape(ts, D_MODEL), real(y_ssm_s), real(y_attn_s), real(gates_s),
                                        *post_w, tm_s, lr_s, xs_rows, block_s, 0, xs_prev=xs)

    padded_counts = jnp.concatenate([pc_p[:, :, 0], pc_s[:, :, 0]], axis=0)
    tile_row0 = jnp.concatenate([jnp.arange(nt_p, dtype=I32) * lr_p, jnp.full((1,), nt_p * lr_p, I32)])
    max_pieces = (TOP_K * (tp + ts) + (PIECE - 1) * n_exp * (nt_p + 1)) // PIECE + n_exp * (PIECES_PER_TILE - 1)
    n_tiles = -(-max_pieces // PIECES_PER_TILE)
    tile_expert, n_used, n_real, src = _piece_table(padded_counts, tile_row0, n_tiles)
    ys = _moe(tile_expert, n_used, n_real, src, xs, wg, wu, wd, bg, bu, bd)
    out_p = _combine(h_p, ys, lpos_p.T, pr_p.T, final_nw, tm_p, lr_p, 0)
    out_s = _combine(h_s, ys, lpos_s.T, pr_s.T, final_nw, tm_s, lr_s, block_s)

    y_prompt = out_p.reshape(nb, seq, D_MODEL)
    y_sample = out_s.reshape(nbs, 1, D_MODEL)
    conv_p = conv_tail[:, SUBLANES - (CONV_W - 1):][None]
    k_p = k.reshape(nb, seq, N_KV_HEADS, ATTN_HEAD_DIM)[:, seq - WINDOW:].astype(F32)[None]
    v_p = v.reshape(nb, seq, N_KV_HEADS, ATTN_HEAD_DIM)[:, seq - WINDOW:].astype(F32)[None]
    conv_s = jnp.concatenate([state_conv[0][:, 1:], real(xbc_s)[:, None]], axis=1)[None]
    k_new = real(k_s).reshape(ts, 1, N_KV_HEADS, ATTN_HEAD_DIM)
    v_new = real(v_s).reshape(ts, 1, N_KV_HEADS, ATTN_HEAD_DIM)
    ks_out = jnp.concatenate([cache_swa_k[0][:, 1:], k_new], axis=1)[None]
    vs_out = jnp.concatenate([cache_swa_v[0][:, 1:], v_new], axis=1)[None]
    return (y_prompt, y_sample, conv_p, ssm_p[None], k_p, v_p, conv_s, ssm_s[None], ks_out, vs_out)
```

```python
import functools

import jax
import jax.numpy as jnp
from jax import lax
from jax.experimental import pallas as pl
from jax.experimental.pallas import tpu as pltpu

F32, BF16, I32 = jnp.float32, jnp.bfloat16, jnp.int32

D_MODEL = 1024
D_INNER = 2 * D_MODEL
SSM_HEAD_DIM = 64
SSM_HEADS = D_INNER // SSM_HEAD_DIM
SSM_GROUPS = 4
SSM_HPG = SSM_HEADS // SSM_GROUPS
D_STATE = 128
CONV_W = 4
CONV_DIM = D_INNER + 2 * SSM_GROUPS * D_STATE
CHUNK = 128
ATTN_HEAD_DIM = 64
N_Q_HEADS = D_MODEL // ATTN_HEAD_DIM
N_KV_HEADS = 4
Q_PER_KV = N_Q_HEADS // N_KV_HEADS
D_ATTN = N_Q_HEADS * ATTN_HEAD_DIM
D_KV = N_KV_HEADS * ATTN_HEAD_DIM
WINDOW = 128
TOP_K = 4
D_FF = D_MODEL
SWIGLU_LIMIT = 7.0
SWIGLU_ALPHA = 1.702
EPS = 1e-5
NEG_BIG = -1e30

LANES = 128
SUBLANES = 8
GROUP_W = D_INNER // SSM_GROUPS
PIECE = 2 * SUBLANES
PIECES_PER_TILE = 32
VMEM_LIMIT = 56 * 1024 * 1024

NT = (((1,), (1,)), ((), ()))
TN = (((0,), (0,)), ((), ()))


def _const_spec(shape):
    return pl.BlockSpec(shape, lambda *_: (0,) * len(shape))


def _resident_spec(shape):
    return pl.BlockSpec(shape, lambda *_: (0,) * len(shape), pipeline_mode=pl.Buffered(1))


def _split3(x):
    hi = x.astype(BF16)
    r1 = x - hi.astype(F32)
    mid = r1.astype(BF16)
    lo = (r1 - mid.astype(F32)).astype(BF16)
    return hi, mid, lo


def _softplus(x):
    return jnp.maximum(x, 0.0) + jnp.log(1.0 + jnp.exp(-jnp.abs(x)))


def _causal_conv_silu(buf_ref, n, cw_ref, cb_ref):
    full = buf_ref[...]
    conv = cb_ref[...]
    for j in range(CONV_W):
        shifted = full if j == CONV_W - 1 else pltpu.roll(full, CONV_W - 1 - j, 0)
        conv = conv + shifted[SUBLANES:SUBLANES + n, :] * cw_ref[j:j + 1, :]
    buf_ref[0:SUBLANES, :] = full[n:n + SUBLANES, :]
    return conv * jax.nn.sigmoid(conv)


def _inproj_kernel(x_ref, nw_ref, wz_ref, wxbc_ref, wdt_ref, wq_ref, wk_ref, wv_ref, wg_ref, *rest,
                   fuse_conv, tiles_per_seq):
    if fuse_conv:
        (hist_ref, cw_ref, cb_ref, wdtT_ref, z_ref, xbc_ref, dt_ref, q_ref, k_ref, v_ref, g_ref, tail_ref, dtT_ref,
         buf_ref) = rest
    else:
        z_ref, xbc_ref, dt_ref, q_ref, k_ref, v_ref, g_ref = rest
    tm = x_ref.shape[0]
    if fuse_conv:
        pos = pl.program_id(0) % tiles_per_seq

        @pl.when(pos == 0)
        def _():
            buf_ref[0:SUBLANES, :] = hist_ref[...]

    x = x_ref[...]
    xn = x * lax.rsqrt(jnp.mean(x * x, axis=-1, keepdims=True) + EPS)
    xn = (xn * nw_ref[...]).astype(BF16)
    xbc = jnp.dot(xn, wxbc_ref[...], preferred_element_type=F32)
    if fuse_conv:
        buf_ref[SUBLANES:SUBLANES + tm, :] = xbc
        xbc_ref[...] = _causal_conv_silu(buf_ref, tm, cw_ref, cb_ref).astype(xbc_ref.dtype)
    else:
        xbc_ref[...] = xbc.astype(xbc_ref.dtype)
    for w_ref, o_ref in ((wz_ref, z_ref), (wdt_ref, dt_ref), (wq_ref, q_ref),
                         (wk_ref, k_ref), (wv_ref, v_ref), (wg_ref, g_ref)):
        o_ref[...] = jnp.dot(xn, w_ref[...], preferred_element_type=F32).astype(o_ref.dtype)
    if fuse_conv:
        dtT_ref[...] = lax.dot_general(wdtT_ref[...], xn, NT, preferred_element_type=F32)

        @pl.when(pos == tiles_per_seq - 1)
        def _():
            tail_ref[0] = buf_ref[0:SUBLANES, :]


def _in_proj(x, norm_w, ws, tm, act_dtype, conv=None):
    t = x.shape[0]
    widths = (D_INNER, CONV_DIM, SSM_HEADS, D_ATTN, D_KV, D_KV, 2 * D_MODEL)
    dtypes = (act_dtype, act_dtype, F32, act_dtype, act_dtype, act_dtype, act_dtype)
    row = lambda n: pl.BlockSpec((tm, n), lambda i: (i, 0))
    in_specs = [row(D_MODEL), _const_spec((1, D_MODEL))] + [_resident_spec((D_MODEL, n)) for n in widths]
    out_specs = [row(n) for n in widths]
    out_shape = [jax.ShapeDtypeStruct((t, n), d) for n, d in zip(widths, dtypes)]
    args, scratch, tiles_per_seq = [x, norm_w, *ws], [], 1
    if conv is not None:
        hist, cw, cb, tiles_per_seq = conv
        n_seq = t // (tm * tiles_per_seq)
        in_specs += [pl.BlockSpec((SUBLANES, CONV_DIM), lambda i: (i // tiles_per_seq, 0)),
                     _const_spec(cw.shape), _const_spec(cb.shape), _const_spec((SSM_HEADS, D_MODEL))]
        out_specs += [pl.BlockSpec((1, SUBLANES, CONV_DIM), lambda i: (i // tiles_per_seq, 0, 0)),
                      pl.BlockSpec((SSM_HEADS, tm), lambda i: (0, i))]
        out_shape += [jax.ShapeDtypeStruct((n_seq, SUBLANES, CONV_DIM), F32),
                      jax.ShapeDtypeStruct((SSM_HEADS, t), F32)]
        args += [hist, cw, cb, ws[2].T]
        scratch = [pltpu.VMEM((SUBLANES + tm, CONV_DIM), F32)]
    return pl.pallas_call(
        functools.partial(_inproj_kernel, fuse_conv=conv is not None, tiles_per_seq=tiles_per_seq),
        grid=(t // tm,),
        in_specs=in_specs,
        out_specs=out_specs,
        out_shape=out_shape,
        scratch_shapes=scratch,
        compiler_params=pltpu.CompilerParams(dimension_semantics=("arbitrary",), vmem_limit_bytes=VMEM_LIMIT),
        name="in_proj",
    )(*args)


def _ssd_kernel(xbc_ref, z_ref, dt_ref, dtT_ref, h0_ref, dtb_ref, dtbT_ref, a_ref, aT_ref, dsk_ref, nw_ref, e_ref,
                *rest, lc, n_valid, conv):
    if conv:
        hist_ref, cw_ref, cb_ref, y_ref, h_ref, buf_ref = rest
    else:
        y_ref, h_ref = rest
    c = pl.program_id(1)

    @pl.when(c == 0)
    def _():
        h_ref[...] = h0_ref[...]

    if conv:
        @pl.when(c == 0)
        def _():
            buf_ref[0:SUBLANES, :] = hist_ref[...]

        buf_ref[SUBLANES:SUBLANES + lc, :] = xbc_ref[...].astype(F32)
        act = _causal_conv_silu(buf_ref, lc, cw_ref, cb_ref)
    else:
        act = xbc_ref[...].astype(F32)
    xs = act[:, :D_INNER]
    bm = act[:, D_INNER:D_INNER + SSM_GROUPS * D_STATE].astype(BF16)
    cm = act[:, D_INNER + SSM_GROUPS * D_STATE:].astype(BF16)

    dt = _softplus(dt_ref[...] + dtb_ref[...])
    dtT_raw = dtT_ref[0] if len(dtT_ref.shape) == 3 else dtT_ref[...]
    dtT = _softplus(dtT_raw + dtbT_ref[...])
    if n_valid < lc:
        dt = jnp.where(lax.broadcasted_iota(I32, dt.shape, 0) < n_valid, dt, 0.0)
        dtT = jnp.where(lax.broadcasted_iota(I32, dtT.shape, 1) < n_valid, dtT, 0.0)
    la = dt * a_ref[...]
    laT = dtT * aT_ref[...]
    li = lax.broadcasted_iota(I32, (lc, lc), 0)
    si = lax.broadcasted_iota(I32, (lc, lc), 1)
    causal = li >= si
    tril = jnp.where(causal, 1.0, 0.0).astype(BF16)
    triu = jnp.where(li <= si, 1.0, 0.0).astype(BF16)
    cum = sum(jnp.dot(tril, p, preferred_element_type=F32) for p in _split3(la))
    cumT = sum(jnp.dot(p, triu, preferred_element_type=F32) for p in _split3(laT))
    ec = jnp.exp(cum)
    dte = jnp.exp(cum[lc - 1:lc, :] - cum)
    cd = jnp.exp(cumT[:, lc - 1:lc])

    def expand(v):
        hi, mid, _ = _split3(v)
        return (jnp.dot(hi, e_ref[...], preferred_element_type=F32)
                + jnp.dot(mid, e_ref[...], preferred_element_type=F32))

    dt_x, ec_x, dte_x = expand(dt), expand(ec), expand(dte)
    xdt = xs * dt_x
    xdt_b = xdt.astype(BF16)
    xdte_b = (xdt * dte_x).astype(BF16)
    lane = lax.broadcasted_iota(I32, (lc, LANES), 1)
    low_half = lane < SSM_HEAD_DIM

    for g in range(SSM_GROUPS):
        gs = slice(g * GROUP_W, (g + 1) * GROUP_W)
        bm_g = bm[:, g * D_STATE:(g + 1) * D_STATE]
        cm_g = cm[:, g * D_STATE:(g + 1) * D_STATE]
        cb = lax.dot_general(cm_g, bm_g, NT, preferred_element_type=F32)
        cbm = jnp.where(causal, cb, 0.0)
        h_g = h_ref[0, g * SSM_HPG:(g + 1) * SSM_HPG].reshape(GROUP_W, D_STATE)
        y_off = lax.dot_general(cm_g, h_g.astype(BF16), NT, preferred_element_type=F32) * ec_x[:, gs]
        tiles = []
        for j in range(GROUP_W // LANES):
            col = g * GROUP_W + j * LANES
            x_pair = xdt_b[:, col:col + LANES]
            acc = None
            for half in range(2):
                h = col // SSM_HEAD_DIM + half
                seg = cum[:, h:h + 1] - cumT[h:h + 1, :]
                m = (cbm * jnp.exp(jnp.where(causal, seg, 0.0))).astype(BF16)
                x_h = jnp.where(low_half if half == 0 else jnp.logical_not(low_half), x_pair, jnp.zeros_like(x_pair))
                d = jnp.dot(m, x_h, preferred_element_type=F32)
                acc = d if acc is None else acc + d
            tiles.append(acc)
        y_g = jnp.concatenate(tiles, axis=1) + y_off + xs[:, gs] * dsk_ref[:, gs]
        zg = z_ref[:, gs].astype(F32)
        y_g = y_g * (zg * jax.nn.sigmoid(zg))
        y_g = y_g * lax.rsqrt(jnp.mean(y_g * y_g, axis=-1, keepdims=True) + EPS)
        y_ref[:, gs] = (y_g * nw_ref[:, gs]).astype(y_ref.dtype)
        st = lax.dot_general(xdte_b[:, gs], bm_g, TN, preferred_element_type=F32)
        for hh in range(SSM_HPG):
            h = g * SSM_HPG + hh
            rows = slice(hh * SSM_HEAD_DIM, (hh + 1) * SSM_HEAD_DIM)
            h_ref[0, h] = h_g[rows, :] * cd[h:h + 1, 0:1] + st[rows, :]


def _ssd(xbc, z, dt, dtT, h0, consts, nb, nc, lc, n_valid, conv=None):
    t = xbc.shape[0]
    row = lambda n: pl.BlockSpec((lc, n), lambda b, c: (b * nc + c, 0))
    if dtT.ndim == 2:
        dtT_spec = pl.BlockSpec((SSM_HEADS, lc), lambda b, c: (0, b * nc + c))
    else:
        dtT_spec = pl.BlockSpec((1, SSM_HEADS, lc), lambda b, c: (b * nc + c, 0, 0))
    in_specs = [
        row(CONV_DIM), row(D_INNER), row(SSM_HEADS), dtT_spec,
        pl.BlockSpec((1, SSM_HEADS, SSM_HEAD_DIM, D_STATE), lambda b, c: (b, 0, 0, 0)),
    ] + [_const_spec(a.shape) for a in consts]
    args, scratch = [xbc, z, dt, dtT, h0, *consts], []
    if conv is not None:
        hist, cw, cb = conv
        in_specs += [pl.BlockSpec((SUBLANES, CONV_DIM), lambda b, c: (b, 0)), _const_spec(cw.shape),
                     _const_spec(cb.shape)]
        args += [hist, cw, cb]
        scratch = [pltpu.VMEM((SUBLANES + lc, CONV_DIM), F32)]
    return pl.pallas_call(
        functools.partial(_ssd_kernel, lc=lc, n_valid=n_valid, conv=conv is not None),
        grid=(nb, nc),
        in_specs=in_specs,
        out_specs=[row(D_INNER), pl.BlockSpec((1, SSM_HEADS, SSM_HEAD_DIM, D_STATE), lambda b, c: (b, 0, 0, 0))],
        out_shape=[jax.ShapeDtypeStruct((t, D_INNER), BF16),
                   jax.ShapeDtypeStruct((nb, SSM_HEADS, SSM_HEAD_DIM, D_STATE), F32)],
        scratch_shapes=scratch,
        compiler_params=pltpu.CompilerParams(dimension_semantics=("arbitrary", "arbitrary"),
                                             vmem_limit_bytes=VMEM_LIMIT),
        name="ssd",
    )(*args)


def _swa_kernel(slope_ref, sink_ref, q_ref, kc_ref, vc_ref, kp_ref, vp_ref, y_ref, *, tq, prev_always):
    nk = WINDOW + tq
    j = lax.broadcasted_iota(I32, (nk, tq), 0)
    r = lax.broadcasted_iota(I32, (nk, tq), 1)
    dist = r + WINDOW - j
    valid = (dist >= 0) & (dist < WINDOW)
    if not prev_always:
        valid = valid & ((j >= WINDOW) | (pl.program_id(1) > 0))
    distf = dist.astype(F32)
    lane = lax.broadcasted_iota(I32, (nk, LANES), 1)
    transposed_out = tq == WINDOW
    zeros_half = jnp.zeros((ATTN_HEAD_DIM, nk), BF16)
    heads = []
    for t in range(D_KV // LANES):
        cols = slice(t * LANES, (t + 1) * LANES)
        kt = jnp.concatenate([kp_ref[:, cols].astype(F32), kc_ref[:, cols].astype(F32)], axis=0)
        vt = jnp.concatenate([vp_ref[:, cols].astype(F32), vc_ref[:, cols].astype(F32)], axis=0)
        if transposed_out:
            vt_t = vt.T.astype(BF16)
        for b in range(2):
            mine = (lane >= ATTN_HEAD_DIM) if b else (lane < ATTN_HEAD_DIM)
            k_same = jnp.where(mine, kt, 0.0)
            k_half = {b: k_same.astype(BF16), 1 - b: pltpu.roll(k_same, ATTN_HEAD_DIM, 1).astype(BF16)}
            if transposed_out:
                v_g = vt_t[b * ATTN_HEAD_DIM:(b + 1) * ATTN_HEAD_DIM, :]
                v_half = {0: jnp.concatenate([v_g, zeros_half], axis=0),
                          1: jnp.concatenate([zeros_half, v_g], axis=0)}
            else:
                v_same = jnp.where(mine, vt, 0.0)
                v_half = {b: v_same.astype(BF16), 1 - b: pltpu.roll(v_same, ATTN_HEAD_DIM, 1).astype(BF16)}
            for qi in range(Q_PER_KV):
                a = qi % 2
                heads.append((k_half[a], v_half[a]))

    def scores(h):
        qt = q_ref[:, (h // 2) * LANES:(h // 2 + 1) * LANES].astype(BF16)
        s = lax.dot_general(heads[h][0], qt, NT, preferred_element_type=F32)
        s = s * (ATTN_HEAD_DIM ** -0.5) - slope_ref[h] * distf
        return jnp.where(valid, s, NEG_BIG)

    def attend(h, s):
        sink = sink_ref[h]
        m = jnp.maximum(jnp.max(s, axis=0, keepdims=True), sink)
        e = jnp.exp(s - m)
        rden = 1.0 / (jnp.sum(e, axis=0, keepdims=True) + jnp.exp(sink - m))
        if transposed_out:
            return jnp.dot(heads[h][1], e.astype(BF16), preferred_element_type=F32) * rden
        return lax.dot_general((e * rden).astype(BF16), heads[h][1], TN, preferred_element_type=F32)

    out_tiles = [None] * (D_ATTN // LANES)
    s_next = scores(0)
    for h in range(N_Q_HEADS):
        s_cur = s_next
        if h + 1 < N_Q_HEADS:
            s_next = scores(h + 1)
        o = attend(h, s_cur)
        out_tiles[h // 2] = o if out_tiles[h // 2] is None else out_tiles[h // 2] + o
    for jq, o in enumerate(out_tiles):
        y_ref[:, jq * LANES:(jq + 1) * LANES] = (o.T if transposed_out else o).astype(y_ref.dtype)


def _swa(slopes, sinks, q, k, v, k_prev, v_prev, nb, nblk, tq, prev_always):
    t = q.shape[0]
    cur = lambda n: pl.BlockSpec((tq, n), lambda b, i: (b * nblk + i, 0))
    if prev_always:
        prev = pl.BlockSpec((WINDOW, D_KV), lambda b, i: (b, 0))
    else:
        prev = pl.BlockSpec((WINDOW, D_KV), lambda b, i: (b * nblk + jnp.maximum(i - 1, 0), 0))
    smem = pl.BlockSpec(memory_space=pltpu.SMEM)
    return pl.pallas_call(
        functools.partial(_swa_kernel, tq=tq, prev_always=prev_always),
        grid=(nb, nblk),
        in_specs=[smem, smem, cur(D_ATTN), cur(D_KV), cur(D_KV), prev, prev],
        out_specs=cur(D_ATTN),
        out_shape=jax.ShapeDtypeStruct((t, D_ATTN), BF16),
        compiler_params=pltpu.CompilerParams(dimension_semantics=("arbitrary", "arbitrary"),
                                             vmem_limit_bytes=VMEM_LIMIT),
        name="swa",
    )(slopes, sinks, q, k, v, k_prev, v_prev)


def _post_kernel(x_ref, ys_ref, ya_ref, g_ref, wsp_ref, wap_ref, wo_ref, nw_ref, wr_ref, br_ref, *rest,
                 n_exp, lr, nt, aliased):
    if aliased:
        rest = rest[1:]
    h_ref, xs_ref, lpos_ref, pr_ref, pc_ref = rest
    tp = x_ref.shape[0]

    @pl.when(pl.program_id(0) >= nt)
    def _():
        xs_ref[...] = jnp.zeros_like(xs_ref)

    @pl.when(pl.program_id(0) < nt)
    def _():
        a = jnp.dot(ys_ref[...].astype(BF16), wsp_ref[...], preferred_element_type=F32)
        b = jnp.dot(ya_ref[...].astype(BF16), wap_ref[...], preferred_element_type=F32)
        g = g_ref[...].astype(F32)
        merged = jax.nn.sigmoid(g[:, :D_MODEL]) * a + jax.nn.sigmoid(g[:, D_MODEL:]) * b
        h = x_ref[...] + jnp.dot(merged.astype(BF16), wo_ref[...], preferred_element_type=F32)
        h_ref[...] = h
        hn = h * lax.rsqrt(jnp.mean(h * h, axis=-1, keepdims=True) + EPS) * nw_ref[...]
        w_hi, w_mid, _ = _split3(wr_ref[...])
        x_hi, x_mid, _ = _split3(hn)
        logits = (lax.dot_general(w_hi, x_hi, NT, preferred_element_type=F32)
                  + lax.dot_general(w_hi, x_mid, NT, preferred_element_type=F32)
                  + lax.dot_general(w_mid, x_hi, NT, preferred_element_type=F32)) + br_ref[...]
        eidx = lax.broadcasted_iota(I32, logits.shape, 0).astype(F32)
        work = logits
        vals, ids = [], []
        for _ in range(TOP_K):
            m = jnp.max(work, axis=0, keepdims=True)
            first = jnp.min(jnp.where(work == m, eidx, float(n_exp)), axis=0, keepdims=True)
            vals.append(m)
            ids.append(first)
            work = jnp.where(eidx == first, -jnp.inf, work)
        es = [jnp.exp(v - vals[0]) for v in vals]
        den = es[0] + es[1] + es[2] + es[3]
        pr_ref[...] = jnp.concatenate([e / den for e in es], axis=0)

        onehot = [jnp.where(eidx == i, 1.0, 0.0) for i in ids]
        counts = [jnp.sum(o, axis=1, keepdims=True) for o in onehot]
        total = counts[0] + counts[1] + counts[2] + counts[3]
        padded = jnp.floor((total + (PIECE - 1)) * (1.0 / PIECE)) * PIECE
        ei = lax.broadcasted_iota(I32, (n_exp, n_exp), 0)
        ej = lax.broadcasted_iota(I32, (n_exp, n_exp), 1)
        below = jnp.where(ej < ei, 1.0, 0.0).astype(BF16)
        padded_b = jnp.broadcast_to(padded, (n_exp, LANES))
        seg_off = sum(jnp.dot(below, p, preferred_element_type=F32) for p in _split3(padded_b))[:, 0:1]
        ti = lax.broadcasted_iota(I32, (tp, tp), 0)
        tj = lax.broadcasted_iota(I32, (tp, tp), 1)
        before = jnp.where(ti < tj, 1.0, 0.0).astype(BF16)
        base = seg_off
        lpos = []
        for k in range(TOP_K):
            prefix = jnp.dot(onehot[k].astype(BF16), before, preferred_element_type=F32)
            lpos.append(jnp.sum(onehot[k] * (base + prefix), axis=0, keepdims=True))
            base = base + counts[k]
        lpos_ref[...] = jnp.concatenate(lpos, axis=0).astype(I32)
        pc_ref[0] = padded_b.astype(I32)

        hn_b = hn.astype(BF16)
        rc = lr // 4
        for c in range(4):
            ri = (lax.broadcasted_iota(I32, (rc, tp), 0) + c * rc).astype(F32)
            sel = None
            for k in range(TOP_K):
                hit = jnp.where(ri == lpos[k], 1.0, 0.0)
                sel = hit if sel is None else sel + hit
            xs_ref[c * rc:(c + 1) * rc, :] = jnp.dot(sel.astype(BF16), hn_b, preferred_element_type=F32).astype(BF16)


def _post(x, y_ssm, y_attn, gates, wsp, wap, wo, ffn_nw, w_rT, b_r, tm, lr, xs_rows, xs_block0, pad_steps,
          xs_prev=None):
    t = x.shape[0]
    n_exp = w_rT.shape[0]
    nt = t // tm
    last = nt - 1
    row = lambda n: pl.BlockSpec((tm, n), lambda i: (jnp.minimum(i, last), 0))
    col = pl.BlockSpec((TOP_K, tm), lambda i: (0, jnp.minimum(i, last)))
    in_specs = [row(D_MODEL), row(D_INNER), row(D_ATTN), row(2 * D_MODEL),
                _resident_spec(wsp.shape), _resident_spec(wap.shape), _resident_spec(wo.shape),
                _const_spec(ffn_nw.shape), _const_spec(w_rT.shape), _const_spec(b_r.shape)]
    args = [x, y_ssm, y_attn, gates, wsp, wap, wo, ffn_nw, w_rT, b_r]
    aliases = {}
    if xs_prev is not None:
        in_specs.append(pl.BlockSpec(memory_space=pl.ANY))
        args.append(xs_prev)
        aliases = {len(args) - 1: 1}
    return pl.pallas_call(
        functools.partial(_post_kernel, n_exp=n_exp, lr=lr, nt=nt, aliased=xs_prev is not None),
        grid=(nt + pad_steps,),
        in_specs=in_specs,
        out_specs=[row(D_MODEL), pl.BlockSpec((lr, D_MODEL), lambda i: (xs_block0 + i, 0)), col, col,
                   pl.BlockSpec((1, n_exp, LANES), lambda i: (jnp.minimum(i, last), 0, 0))],
        out_shape=[jax.ShapeDtypeStruct((t, D_MODEL), F32), jax.ShapeDtypeStruct((xs_rows, D_MODEL), BF16),
                   jax.ShapeDtypeStruct((TOP_K, t), I32), jax.ShapeDtypeStruct((TOP_K, t), F32),
                   jax.ShapeDtypeStruct((nt, n_exp, LANES), I32)],
        input_output_aliases=aliases,
        compiler_params=pltpu.CompilerParams(dimension_semantics=("arbitrary",), vmem_limit_bytes=VMEM_LIMIT),
        name="post",
    )(*args)


def _piece(ref, p):
    return ref.at[pl.ds(pl.multiple_of(p * PIECE, PIECE), PIECE)]


def _moe_kernel(te_ref, nu_ref, src0_ref, src1_ref, srcc_ref, xs_hbm, wg_ref, wu_ref, wd_ref,
                bg_ref, bu_ref, bd_ref, ys_hbm, xbuf, obuf, wbf, gsem, ssem):
    i = pl.program_id(0)
    n_used = nu_ref[0]
    slot = i % 2
    tme = PIECES_PER_TILE * PIECE

    def gather(src_ref, s):
        for r in range(PIECES_PER_TILE):
            pltpu.make_async_copy(_piece(xs_hbm, src_ref[0, 0, r]), xbuf.at[s, pl.ds(r * PIECE, PIECE)],
                                  gsem.at[s]).start()

    def wait_gather(s):
        pltpu.make_async_copy(xs_hbm.at[pl.ds(0, tme)], xbuf.at[s], gsem.at[s]).wait()

    def wait_put(s):
        pltpu.make_async_copy(obuf.at[s], ys_hbm.at[pl.ds(0, tme)], ssem.at[s]).wait()

    @pl.when(i == 0)
    def _():
        gather(src0_ref, 0)

    @pl.when(i < n_used)
    def _():
        gather(src1_ref, 1 - slot)
        wait_gather(slot)

        @pl.when(i >= 2)
        def _():
            wait_put(slot)

        @pl.when((i == 0) | (te_ref[i] != te_ref[jnp.maximum(i - 1, 0)]))
        def _():
            wbf[0] = wg_ref[0].astype(BF16)
            wbf[1] = wu_ref[0].astype(BF16)
            wbf[2] = wd_ref[0].astype(BF16)

        x = xbuf[slot]
        g = jnp.minimum(jnp.dot(x, wbf[0], preferred_element_type=F32) + bg_ref[0], SWIGLU_LIMIT)
        u = jnp.clip(jnp.dot(x, wbf[1], preferred_element_type=F32) + bu_ref[0], -SWIGLU_LIMIT, SWIGLU_LIMIT)
        act = ((u + 1.0) * g * jax.nn.sigmoid(SWIGLU_ALPHA * g)).astype(BF16)
        obuf[slot] = (jnp.dot(act, wbf[2], preferred_element_type=F32) + bd_ref[0]).astype(BF16)
        for r in range(PIECES_PER_TILE):
            pltpu.make_async_copy(obuf.at[slot, pl.ds(r * PIECE, PIECE)], _piece(ys_hbm, srcc_ref[0, 0, r]),
                                  ssem.at[slot]).start()

        @pl.when(i == n_used - 1)
        def _():
            wait_put(slot)
            wait_gather(1 - slot)

            @pl.when(i >= 1)
            def _():
                wait_put(1 - slot)


def _moe(tile_expert, n_used, src, xs, wg, wu, wd, bg, bu, bd):
    n_tiles = tile_expert.shape[0]
    idx = src.reshape(n_tiles, 1, PIECES_PER_TILE)
    tme = PIECES_PER_TILE * PIECE
    wspec = pl.BlockSpec((1, D_MODEL, D_FF), lambda i, te, nu: (te[i], 0, 0))
    bspec = pl.BlockSpec((1, 1, D_FF), lambda i, te, nu: (te[i], 0, 0))
    ispec = lambda f: pl.BlockSpec((1, 1, PIECES_PER_TILE), lambda i, te, nu: (f(i), 0, 0),
                                   memory_space=pltpu.SMEM)
    grid_spec = pltpu.PrefetchScalarGridSpec(
        num_scalar_prefetch=2,
        grid=(n_tiles,),
        in_specs=[ispec(lambda i: 0), ispec(lambda i: jnp.minimum(i + 1, n_tiles - 1)), ispec(lambda i: i),
                  pl.BlockSpec(memory_space=pl.ANY), wspec, wspec, wspec, bspec, bspec, bspec],
        out_specs=pl.BlockSpec(memory_space=pl.ANY),
        scratch_shapes=[pltpu.VMEM((2, tme, D_MODEL), BF16), pltpu.VMEM((2, tme, D_MODEL), BF16),
                        pltpu.VMEM((3, D_MODEL, D_FF), BF16),
                        pltpu.SemaphoreType.DMA((2,)), pltpu.SemaphoreType.DMA((2,))],
    )
    return pl.pallas_call(
        _moe_kernel,
        grid_spec=grid_spec,
        out_shape=jax.ShapeDtypeStruct(xs.shape, xs.dtype),
        input_output_aliases={5: 0},
        compiler_params=pltpu.CompilerParams(dimension_semantics=("arbitrary",), vmem_limit_bytes=VMEM_LIMIT),
        name="moe",
    )(tile_expert, n_used, idx, idx, idx, xs, wg, wu, wd, bg, bu, bd)


def _combine_kernel(h_ref, ys_ref, lpos_ref, pr_ref, nw_ref, o_ref):
    tp = h_ref.shape[0]
    lr = ys_ref.shape[0]
    ri = lax.broadcasted_iota(I32, (tp, lr), 1)
    lp = lpos_ref[...]
    pr = pr_ref[...]
    pw = jnp.zeros((tp, lr), F32)
    for k in range(TOP_K):
        pw = jnp.where(ri == lp[:, k:k + 1], pr[:, k:k + 1], pw)
    moe = jnp.dot(pw.astype(BF16), ys_ref[...], preferred_element_type=F32)
    h = h_ref[...] + moe
    o_ref[...] = h * lax.rsqrt(jnp.mean(h * h, axis=-1, keepdims=True) + EPS) * nw_ref[...]


def _combine(h, ys, lpos_t, probs_t, final_nw, tm, lr, ys_block0):
    t = h.shape[0]
    return pl.pallas_call(
        _combine_kernel,
        grid=(t // tm,),
        in_specs=[pl.BlockSpec((tm, D_MODEL), lambda i: (i, 0)),
                  pl.BlockSpec((lr, D_MODEL), lambda i: (ys_block0 + i, 0)),
                  pl.BlockSpec((tm, TOP_K), lambda i: (i, 0)),
                  pl.BlockSpec((tm, TOP_K), lambda i: (i, 0)),
                  _const_spec(final_nw.shape)],
        out_specs=pl.BlockSpec((tm, D_MODEL), lambda i: (i, 0)),
        out_shape=jax.ShapeDtypeStruct((t, D_MODEL), F32),
        compiler_params=pltpu.CompilerParams(dimension_semantics=("arbitrary",), vmem_limit_bytes=VMEM_LIMIT),
        name="combine",
    )(h, ys, lpos_t, probs_t, final_nw)


def _piece_table(padded_counts, tile_row0, n_tiles, spare_piece0):
    n_pieces = (padded_counts // PIECE).T
    seg_row = tile_row0[:, None] + jnp.cumsum(padded_counts, axis=1) - padded_counts
    seg_piece = (seg_row // PIECE).T
    per_expert = n_pieces.sum(axis=1)
    tiles_per = (per_expert + PIECES_PER_TILE - 1) // PIECES_PER_TILE
    tile_end = jnp.cumsum(tiles_per)
    n_used = tile_end[-1]
    slot0 = (tile_end - tiles_per) * PIECES_PER_TILE
    seg_slot = (slot0[:, None] + jnp.cumsum(n_pieces, axis=1) - n_pieces).reshape(-1)
    seg_n = n_pieces.reshape(-1)
    seg_src = seg_piece.reshape(-1)
    slots = jnp.arange(n_tiles * PIECES_PER_TILE, dtype=I32)

    def at_segment_of_slot(f):
        df = f - jnp.concatenate([jnp.zeros((1,), I32), f[:-1]])
        return jnp.sum(jnp.where(seg_slot[None, :] <= slots[:, None], df[None, :], 0), axis=1)

    real = slots < at_segment_of_slot(seg_slot + seg_n)
    padding = jnp.logical_not(real) & (slots < n_used * PIECES_PER_TILE)
    spare = spare_piece0 + jnp.where(padding, jnp.cumsum(padding.astype(I32)), 0)
    src = jnp.where(real, slots + at_segment_of_slot(seg_src - seg_slot), spare).astype(I32)
    tile_ids = jnp.arange(n_tiles, dtype=I32)
    tile_expert = jnp.sum((tile_end[None, :] <= jnp.minimum(tile_ids, n_used - 1)[:, None]).astype(I32), axis=1)
    return tile_expert.astype(I32), n_used.reshape(1).astype(I32), src


def _pick_tile(n, pref):
    while n % pref:
        pref //= 2
    return pref


def _local_rows(tm, n_exp):
    need = TOP_K * tm + n_exp * (PIECE - 1)
    return -(-need // 64) * 64


def kernel(x_prompt, x_sample, state_conv, state_ssm, cache_swa_k, cache_swa_v, attn_norm_w, w_in, conv_w, conv_b, dt_bias, a_log, d_skip, ssm_norm_w, attn_sinks, w_ssm_proj, w_attn_proj, w_o, ffn_norm_w, w_router, b_router, w_gate, b_gate, w_up, b_up, w_down, b_down, final_norm_w):
    assert w_in.shape[0] == 1, "single-layer step"
    nb, seq, _ = x_prompt.shape
    nbs = x_sample.shape[0]
    n_exp = w_router.shape[-1]
    tp, ts = nb * seq, nbs
    pad = SUBLANES

    cuts = [0]
    for n in (D_INNER, CONV_DIM, SSM_HEADS, D_ATTN, D_KV, D_KV, 2 * D_MODEL):
        cuts.append(cuts[-1] + n)
    w_in_b = w_in[0].astype(BF16)
    ws = [w_in_b[:, cuts[i]:cuts[i + 1]] for i in range(7)]
    attn_nw = attn_norm_w[0].reshape(1, D_MODEL)
    a_neg = -jnp.exp(a_log[0].astype(F32))
    head_of = jnp.arange(D_INNER, dtype=I32) // SSM_HEAD_DIM
    expand = (jnp.arange(SSM_HEADS, dtype=I32)[:, None] == head_of[None, :]).astype(BF16)
    conv_consts = (conv_w[0], conv_b[0].reshape(1, CONV_DIM))
    ssd_consts = (dt_bias[0].reshape(1, SSM_HEADS), dt_bias[0].reshape(SSM_HEADS, 1),
                  a_neg.reshape(1, SSM_HEADS), a_neg.reshape(SSM_HEADS, 1),
                  d_skip[0][head_of].reshape(1, D_INNER), ssm_norm_w[0].reshape(1, D_INNER), expand)
    slopes = jnp.exp2(-8.0 * jnp.arange(1, N_Q_HEADS + 1, dtype=F32) / N_Q_HEADS)
    sinks = attn_sinks[0].astype(F32)
    wsp, wap, wo = w_ssm_proj[0].astype(BF16), w_attn_proj[0].astype(BF16), w_o[0].astype(BF16)
    ffn_nw = ffn_norm_w[0].reshape(1, D_MODEL)
    w_rT = w_router[0].T
    b_r = b_router[0].reshape(n_exp, 1)
    bg, bu, bd = (b[0].reshape(n_exp, 1, -1) for b in (b_gate, b_up, b_down))
    final_nw = final_norm_w.reshape(1, D_MODEL)

    xp = x_prompt.reshape(tp, D_MODEL)
    tm_in = _pick_tile(seq, 512)
    z, xbc, dt, q, k, v, gates, conv_tail, dtT = _in_proj(
        xp, attn_nw, ws, tm_in, BF16,
        conv=(jnp.zeros((nb * SUBLANES, CONV_DIM), F32), *conv_consts, seq // tm_in))
    nc = seq // CHUNK
    y_ssm, ssm_p = _ssd(xbc, z, dt, dtT, jnp.zeros((nb, SSM_HEADS, SSM_HEAD_DIM, D_STATE), F32), ssd_consts,
                        nb, nc, CHUNK, CHUNK)
    nblk = seq // WINDOW
    y_attn = _swa(slopes, sinks, q, k, v, k, v, nb, nblk, WINDOW, False)

    xs_pad = jnp.pad(x_sample.reshape(ts, 1, D_MODEL), ((0, 0), (0, pad - 1), (0, 0))).reshape(ts * pad, D_MODEL)
    z_s, xbc_s, dt_s, q_s, k_s, v_s, gates_s = _in_proj(xs_pad, attn_nw, ws, _pick_tile(ts * pad, 256), F32)
    dtT_s = dt_s.reshape(ts, pad, SSM_HEADS).transpose(0, 2, 1)
    hist_s = jnp.pad(state_conv[0], ((0, 0), (pad - (CONV_W - 1), 0), (0, 0))).reshape(ts * pad, CONV_DIM)
    y_ssm_s, ssm_s = _ssd(xbc_s, z_s, dt_s, dtT_s, state_ssm[0], ssd_consts, ts, 1, pad, 1,
                          conv=(hist_s, *conv_consts))
    kc = cache_swa_k[0].reshape(ts * WINDOW, D_KV)
    vc = cache_swa_v[0].reshape(ts * WINDOW, D_KV)
    y_attn_s = _swa(slopes, sinks, q_s, k_s, v_s, kc, vc, ts, 1, pad, True)
    real = lambda a: a.reshape(ts, pad, -1)[:, 0]

    tm_p, tm_s = _pick_tile(tp, 512), ts
    nt_p = tp // tm_p
    lr_p, lr_s = _local_rows(tm_p, n_exp), _local_rows(tm_s, n_exp)
    if (nt_p * lr_p) % lr_s or lr_s > lr_p:
        lr_s = lr_p
    spare_rows = lr_s + (1 + n_exp * (PIECES_PER_TILE - 1)) * PIECE
    pad_steps = -(-spare_rows // lr_p)
    xs_rows = (nt_p + pad_steps) * lr_p
    block_s = nt_p * lr_p // lr_s
    post_w = (wsp, wap, wo, ffn_nw, w_rT, b_r)
    h_p, xs, lpos_p, pr_p, pc_p = _post(xp, y_ssm, y_attn, gates, *post_w, tm_p, lr_p, xs_rows, 0, pad_steps)
    h_s, xs, lpos_s, pr_s, pc_s = _post(x_sample.reshape(ts, D_MODEL), real(y_ssm_s), real(y_attn_s), real(gates_s),
                                        *post_w, tm_s, lr_s, xs_rows, block_s, 0, xs_prev=xs)

    padded_counts = jnp.concatenate([pc_p[:, :, 0], pc_s[:, :, 0]], axis=0)
    tile_row0 = jnp.concatenate([jnp.arange(nt_p, dtype=I32) * lr_p, jnp.full((1,), nt_p * lr_p, I32)])
    max_pieces = (TOP_K * (tp + ts) + (PIECE - 1) * n_exp * (nt_p + 1)) // PIECE + n_exp * (PIECES_PER_TILE - 1)
    n_tiles = -(-max_pieces // PIECES_PER_TILE) + 1
    tile_expert, n_used, src = _piece_table(padded_counts, tile_row0, n_tiles, (nt_p * lr_p + lr_s) // PIECE)
    ys = _moe(tile_expert, n_used, src, xs, w_gate[0], w_up[0], w_down[0], bg, bu, bd)
    out_p = _combine(h_p, ys, lpos_p.T, pr_p.T, final_nw, tm_p, lr_p, 0)
    out_s = _combine(h_s, ys, lpos_s.T, pr_s.T, final_nw, tm_s, lr_s, block_s)

    y_prompt = out_p.reshape(nb, seq, D_MODEL)
    y_sample = out_s.reshape(nbs, 1, D_MODEL)
    conv_p = conv_tail[:, SUBLANES - (CONV_W - 1):][None]
    k_p = k.reshape(nb, seq, N_KV_HEADS, ATTN_HEAD_DIM)[:, seq - WINDOW:].astype(F32)[None]
    v_p = v.reshape(nb, seq, N_KV_HEADS, ATTN_HEAD_DIM)[:, seq - WINDOW:].astype(F32)[None]
    conv_s = jnp.concatenate([state_conv[0][:, 1:], real(xbc_s)[:, None]], axis=1)[None]
    k_new = real(k_s).reshape(ts, 1, N_KV_HEADS, ATTN_HEAD_DIM)
    v_new = real(v_s).reshape(ts, 1, N_KV_HEADS, ATTN_HEAD_DIM)
    ks_out = jnp.concatenate([cache_swa_k[0][:, 1:], k_new], axis=1)[None]
    vs_out = jnp.concatenate([cache_swa_v[0][:, 1:], v_new], axis=1)[None]
    return (y_prompt, y_sample, conv_p, ssm_p[None], k_p, v_p, conv_s, ssm_s[None], ks_out, vs_out)
```

```python
import functools

import jax
import jax.numpy as jnp
from jax import lax
from jax.experimental import pallas as pl
from jax.experimental.pallas import tpu as pltpu

F32, BF16, I32 = jnp.float32, jnp.bfloat16, jnp.int32

D_MODEL = 1024
D_INNER = 2 * D_MODEL
SSM_HEAD_DIM = 64
SSM_HEADS = D_INNER // SSM_HEAD_DIM
SSM_GROUPS = 4
SSM_HPG = SSM_HEADS // SSM_GROUPS
D_STATE = 128
CONV_W = 4
CONV_DIM = D_INNER + 2 * SSM_GROUPS * D_STATE
CHUNK = 128
ATTN_HEAD_DIM = 64
N_Q_HEADS = D_MODEL // ATTN_HEAD_DIM
N_KV_HEADS = 4
Q_PER_KV = N_Q_HEADS // N_KV_HEADS
D_ATTN = N_Q_HEADS * ATTN_HEAD_DIM
D_KV = N_KV_HEADS * ATTN_HEAD_DIM
WINDOW = 128
TOP_K = 4
D_FF = D_MODEL
SWIGLU_LIMIT = 7.0
SWIGLU_ALPHA = 1.702
EPS = 1e-5
NEG_BIG = -1e30

LANES = 128
SUBLANES = 8
GROUP_W = D_INNER // SSM_GROUPS
PIECE = 2 * SUBLANES
PIECES_PER_TILE = 32
VMEM_LIMIT = 56 * 1024 * 1024

NT = (((1,), (1,)), ((), ()))
TN = (((0,), (0,)), ((), ()))


def _const_spec(shape):
    return pl.BlockSpec(shape, lambda *_: (0,) * len(shape))


def _resident_spec(shape):
    return pl.BlockSpec(shape, lambda *_: (0,) * len(shape), pipeline_mode=pl.Buffered(1))


def _split3(x):
    hi = x.astype(BF16)
    r1 = x - hi.astype(F32)
    mid = r1.astype(BF16)
    lo = (r1 - mid.astype(F32)).astype(BF16)
    return hi, mid, lo


def _softplus(x):
    return jnp.maximum(x, 0.0) + jnp.log(1.0 + jnp.exp(-jnp.abs(x)))


def _causal_conv_silu(buf_ref, n, cw_ref, cb_ref):
    full = buf_ref[...]
    conv = cb_ref[...]
    for j in range(CONV_W):
        shifted = full if j == CONV_W - 1 else pltpu.roll(full, CONV_W - 1 - j, 0)
        conv = conv + shifted[SUBLANES:SUBLANES + n, :] * cw_ref[j:j + 1, :]
    buf_ref[0:SUBLANES, :] = full[n:n + SUBLANES, :]
    return conv * jax.nn.sigmoid(conv)


def _inproj_kernel(x_ref, nw_ref, wz_ref, wxbc_ref, wdt_ref, wq_ref, wk_ref, wv_ref, wg_ref, *rest,
                   fuse_conv, tiles_per_seq):
    if fuse_conv:
        (hist_ref, cw_ref, cb_ref, wdtT_ref, z_ref, xbc_ref, dt_ref, q_ref, k_ref, v_ref, g_ref, tail_ref, dtT_ref,
         buf_ref) = rest
    else:
        z_ref, xbc_ref, dt_ref, q_ref, k_ref, v_ref, g_ref = rest
    tm = x_ref.shape[0]
    if fuse_conv:
        pos = pl.program_id(0) % tiles_per_seq

        @pl.when(pos == 0)
        def _():
            buf_ref[0:SUBLANES, :] = hist_ref[...]

    x = x_ref[...]
    xn = x * lax.rsqrt(jnp.mean(x * x, axis=-1, keepdims=True) + EPS)
    xn = (xn * nw_ref[...]).astype(BF16)
    xbc = jnp.dot(xn, wxbc_ref[...], preferred_element_type=F32)
    if fuse_conv:
        buf_ref[SUBLANES:SUBLANES + tm, :] = xbc
        xbc_ref[...] = _causal_conv_silu(buf_ref, tm, cw_ref, cb_ref).astype(xbc_ref.dtype)
    else:
        xbc_ref[...] = xbc.astype(xbc_ref.dtype)
    for w_ref, o_ref in ((wz_ref, z_ref), (wdt_ref, dt_ref), (wq_ref, q_ref),
                         (wk_ref, k_ref), (wv_ref, v_ref), (wg_ref, g_ref)):
        o_ref[...] = jnp.dot(xn, w_ref[...], preferred_element_type=F32).astype(o_ref.dtype)
    if fuse_conv:
        dtT_ref[...] = lax.dot_general(wdtT_ref[...], xn, NT, preferred_element_type=F32)

        @pl.when(pos == tiles_per_seq - 1)
        def _():
            tail_ref[0] = buf_ref[0:SUBLANES, :]


def _in_proj(x, norm_w, ws, tm, act_dtype, conv=None):
    t = x.shape[0]
    widths = (D_INNER, CONV_DIM, SSM_HEADS, D_ATTN, D_KV, D_KV, 2 * D_MODEL)
    dtypes = (act_dtype, act_dtype, F32, act_dtype, act_dtype, act_dtype, act_dtype)
    row = lambda n: pl.BlockSpec((tm, n), lambda i: (i, 0))
    in_specs = [row(D_MODEL), _const_spec((1, D_MODEL))] + [_resident_spec((D_MODEL, n)) for n in widths]
    out_specs = [row(n) for n in widths]
    out_shape = [jax.ShapeDtypeStruct((t, n), d) for n, d in zip(widths, dtypes)]
    args, scratch, tiles_per_seq = [x, norm_w, *ws], [], 1
    if conv is not None:
        hist, cw, cb, tiles_per_seq = conv
        n_seq = t // (tm * tiles_per_seq)
        in_specs += [pl.BlockSpec((SUBLANES, CONV_DIM), lambda i: (i // tiles_per_seq, 0)),
                     _const_spec(cw.shape), _const_spec(cb.shape), _const_spec((SSM_HEADS, D_MODEL))]
        out_specs += [pl.BlockSpec((1, SUBLANES, CONV_DIM), lambda i: (i // tiles_per_seq, 0, 0)),
                      pl.BlockSpec((SSM_HEADS, tm), lambda i: (0, i))]
        out_shape += [jax.ShapeDtypeStruct((n_seq, SUBLANES, CONV_DIM), F32),
                      jax.ShapeDtypeStruct((SSM_HEADS, t), F32)]
        args += [hist, cw, cb, ws[2].T]
        scratch = [pltpu.VMEM((SUBLANES + tm, CONV_DIM), F32)]
    return pl.pallas_call(
        functools.partial(_inproj_kernel, fuse_conv=conv is not None, tiles_per_seq=tiles_per_seq),
        grid=(t // tm,),
        in_specs=in_specs,
        out_specs=out_specs,
        out_shape=out_shape,
        scratch_shapes=scratch,
        compiler_params=pltpu.CompilerParams(dimension_semantics=("arbitrary",), vmem_limit_bytes=VMEM_LIMIT),
        name="in_proj",
    )(*args)


def _ssd_kernel(xbc_ref, z_ref, dt_ref, dtT_ref, h0_ref, dtb_ref, dtbT_ref, a_ref, aT_ref, dsk_ref, nw_ref, e_ref,
                *rest, lc, n_valid, conv):
    if conv:
        hist_ref, cw_ref, cb_ref, y_ref, h_ref, buf_ref = rest
    else:
        y_ref, h_ref = rest
    c = pl.program_id(1)

    @pl.when(c == 0)
    def _():
        h_ref[...] = h0_ref[...]

    if conv:
        @pl.when(c == 0)
        def _():
            buf_ref[0:SUBLANES, :] = hist_ref[...]

        buf_ref[SUBLANES:SUBLANES + lc, :] = xbc_ref[...].astype(F32)
        act = _causal_conv_silu(buf_ref, lc, cw_ref, cb_ref)
    else:
        act = xbc_ref[...].astype(F32)
    xs = act[:, :D_INNER]
    bm = act[:, D_INNER:D_INNER + SSM_GROUPS * D_STATE].astype(BF16)
    cm = act[:, D_INNER + SSM_GROUPS * D_STATE:].astype(BF16)

    dt = _softplus(dt_ref[...] + dtb_ref[...])
    dtT_raw = dtT_ref[0] if len(dtT_ref.shape) == 3 else dtT_ref[...]
    dtT = _softplus(dtT_raw + dtbT_ref[...])
    if n_valid < lc:
        dt = jnp.where(lax.broadcasted_iota(I32, dt.shape, 0) < n_valid, dt, 0.0)
        dtT = jnp.where(lax.broadcasted_iota(I32, dtT.shape, 1) < n_valid, dtT, 0.0)
    la = dt * a_ref[...]
    laT = dtT * aT_ref[...]
    li = lax.broadcasted_iota(I32, (lc, lc), 0)
    si = lax.broadcasted_iota(I32, (lc, lc), 1)
    causal = li >= si
    tril = jnp.where(causal, 1.0, 0.0).astype(BF16)
    triu = jnp.where(li <= si, 1.0, 0.0).astype(BF16)
    cum = sum(jnp.dot(tril, p, preferred_element_type=F32) for p in _split3(la))
    cumT = sum(jnp.dot(p, triu, preferred_element_type=F32) for p in _split3(laT))
    ec = jnp.exp(cum)
    dte = jnp.exp(cum[lc - 1:lc, :] - cum)
    cd = jnp.exp(cumT[:, lc - 1:lc])

    def expand(v):
        hi, mid, _ = _split3(v)
        return jnp.dot(jnp.concatenate([hi, mid], axis=1), e_ref[...], preferred_element_type=F32)

    dt_x, ec_x, dte_x = expand(dt), expand(ec), expand(dte)
    xdt = xs * dt_x
    xdt_b = xdt.astype(BF16)
    xdte_b = (xdt * dte_x).astype(BF16)
    lane = lax.broadcasted_iota(I32, (lc, LANES), 1)
    low_half = lane < SSM_HEAD_DIM

    for g in range(SSM_GROUPS):
        gs = slice(g * GROUP_W, (g + 1) * GROUP_W)
        bm_g = bm[:, g * D_STATE:(g + 1) * D_STATE]
        cm_g = cm[:, g * D_STATE:(g + 1) * D_STATE]
        cb = lax.dot_general(cm_g, bm_g, NT, preferred_element_type=F32)
        cbm = jnp.where(causal, cb, 0.0)
        h_g = h_ref[0, g * SSM_HPG:(g + 1) * SSM_HPG].reshape(GROUP_W, D_STATE)
        y_off = lax.dot_general(cm_g, h_g.astype(BF16), NT, preferred_element_type=F32) * ec_x[:, gs]
        tiles = []
        for j in range(GROUP_W // LANES):
            col = g * GROUP_W + j * LANES
            x_pair = xdt_b[:, col:col + LANES]
            acc = None
            for half in range(2):
                h = col // SSM_HEAD_DIM + half
                seg = cum[:, h:h + 1] - cumT[h:h + 1, :]
                m = (cbm * jnp.exp(jnp.where(causal, seg, 0.0))).astype(BF16)
                x_h = jnp.where(low_half if half == 0 else jnp.logical_not(low_half), x_pair, jnp.zeros_like(x_pair))
                d = jnp.dot(m, x_h, preferred_element_type=F32)
                acc = d if acc is None else acc + d
            tiles.append(acc)
        y_g = jnp.concatenate(tiles, axis=1) + y_off + xs[:, gs] * dsk_ref[:, gs]
        zg = z_ref[:, gs].astype(F32)
        y_g = y_g * (zg * jax.nn.sigmoid(zg))
        y_g = y_g * lax.rsqrt(jnp.mean(y_g * y_g, axis=-1, keepdims=True) + EPS)
        y_ref[:, gs] = (y_g * nw_ref[:, gs]).astype(y_ref.dtype)
        st = lax.dot_general(xdte_b[:, gs], bm_g, TN, preferred_element_type=F32)
        for hh in range(SSM_HPG):
            h = g * SSM_HPG + hh
            rows = slice(hh * SSM_HEAD_DIM, (hh + 1) * SSM_HEAD_DIM)
            h_ref[0, h] = h_g[rows, :] * cd[h:h + 1, 0:1] + st[rows, :]


def _ssd(xbc, z, dt, dtT, h0, consts, nb, nc, lc, n_valid, conv=None):
    t = xbc.shape[0]
    row = lambda n: pl.BlockSpec((lc, n), lambda b, c: (b * nc + c, 0))
    if dtT.ndim == 2:
        dtT_spec = pl.BlockSpec((SSM_HEADS, lc), lambda b, c: (0, b * nc + c))
    else:
        dtT_spec = pl.BlockSpec((1, SSM_HEADS, lc), lambda b, c: (b * nc + c, 0, 0))
    in_specs = [
        row(CONV_DIM), row(D_INNER), row(SSM_HEADS), dtT_spec,
        pl.BlockSpec((1, SSM_HEADS, SSM_HEAD_DIM, D_STATE), lambda b, c: (b, 0, 0, 0)),
    ] + [_const_spec(a.shape) for a in consts]
    args, scratch = [xbc, z, dt, dtT, h0, *consts], []
    if conv is not None:
        hist, cw, cb = conv
        in_specs += [pl.BlockSpec((SUBLANES, CONV_DIM), lambda b, c: (b, 0)), _const_spec(cw.shape),
                     _const_spec(cb.shape)]
        args += [hist, cw, cb]
        scratch = [pltpu.VMEM((SUBLANES + lc, CONV_DIM), F32)]
    return pl.pallas_call(
        functools.partial(_ssd_kernel, lc=lc, n_valid=n_valid, conv=conv is not None),
        grid=(nb, nc),
        in_specs=in_specs,
        out_specs=[row(D_INNER), pl.BlockSpec((1, SSM_HEADS, SSM_HEAD_DIM, D_STATE), lambda b, c: (b, 0, 0, 0))],
        out_shape=[jax.ShapeDtypeStruct((t, D_INNER), BF16),
                   jax.ShapeDtypeStruct((nb, SSM_HEADS, SSM_HEAD_DIM, D_STATE), F32)],
        scratch_shapes=scratch,
        compiler_params=pltpu.CompilerParams(dimension_semantics=("arbitrary", "arbitrary"),
                                             vmem_limit_bytes=VMEM_LIMIT),
        name="ssd",
    )(*args)


def _swa_kernel(slope_ref, sink_ref, q_ref, kc_ref, vc_ref, kp_ref, vp_ref, y_ref, *, tq, prev_always):
    nk = WINDOW + tq
    j = lax.broadcasted_iota(I32, (nk, tq), 0)
    r = lax.broadcasted_iota(I32, (nk, tq), 1)
    dist = r + WINDOW - j
    valid = (dist >= 0) & (dist < WINDOW)
    if not prev_always:
        valid = valid & ((j >= WINDOW) | (pl.program_id(1) > 0))
    distf = dist.astype(F32)
    lane = lax.broadcasted_iota(I32, (nk, LANES), 1)
    transposed_out = tq == WINDOW
    zeros_half = jnp.zeros((ATTN_HEAD_DIM, nk), BF16)
    sink_row = lax.broadcasted_iota(I32, (SUBLANES, tq), 0) == 0
    ones_keys = jnp.ones((nk + SUBLANES, LANES), BF16)
    heads = []
    for t in range(D_KV // LANES):
        cols = slice(t * LANES, (t + 1) * LANES)
        kt = jnp.concatenate([kp_ref[:, cols].astype(F32), kc_ref[:, cols].astype(F32)], axis=0)
        vt = jnp.concatenate([vp_ref[:, cols].astype(F32), vc_ref[:, cols].astype(F32)], axis=0)
        if transposed_out:
            vt_t = vt.T.astype(BF16)
        for b in range(2):
            mine = (lane >= ATTN_HEAD_DIM) if b else (lane < ATTN_HEAD_DIM)
            k_same = jnp.where(mine, kt, 0.0)
            k_half = {b: k_same.astype(BF16), 1 - b: pltpu.roll(k_same, ATTN_HEAD_DIM, 1).astype(BF16)}
            if transposed_out:
                v_g = vt_t[b * ATTN_HEAD_DIM:(b + 1) * ATTN_HEAD_DIM, :]
                v_half = {0: jnp.concatenate([v_g, zeros_half], axis=0),
                          1: jnp.concatenate([zeros_half, v_g], axis=0)}
            else:
                v_same = jnp.concatenate([jnp.where(mine, vt, 0.0), jnp.zeros((SUBLANES, LANES), F32)], axis=0)
                v_half = {b: v_same.astype(BF16), 1 - b: pltpu.roll(v_same, ATTN_HEAD_DIM, 1).astype(BF16)}
            for qi in range(Q_PER_KV):
                a = qi % 2
                heads.append((k_half[a], v_half[a]))

    def scores(h):
        qt = q_ref[:, (h // 2) * LANES:(h // 2 + 1) * LANES].astype(BF16)
        s = lax.dot_general(heads[h][0], qt, NT, preferred_element_type=F32)
        s = s * (ATTN_HEAD_DIM ** -0.5) - slope_ref[h] * distf
        return jnp.where(valid, s, NEG_BIG)

    def attend(h, s):
        sink = sink_ref[h]
        m = jnp.maximum(jnp.max(s, axis=0, keepdims=True), sink)
        e = jnp.exp(s - m)
        e_sink = jnp.exp(sink - m)
        if transposed_out:
            rden = 1.0 / (jnp.sum(e, axis=0, keepdims=True) + e_sink)
            return jnp.dot(heads[h][1], e.astype(BF16), preferred_element_type=F32) * rden
        p = jnp.concatenate([e, jnp.where(sink_row, e_sink, 0.0)], axis=0).astype(BF16)
        num = lax.dot_general(p, heads[h][1], TN, preferred_element_type=F32)
        den = lax.dot_general(p, ones_keys, TN, preferred_element_type=F32)
        return num / den

    out_tiles = [None] * (D_ATTN // LANES)
    s_next = scores(0)
    for h in range(N_Q_HEADS):
        s_cur = s_next
        if h + 1 < N_Q_HEADS:
            s_next = scores(h + 1)
        o = attend(h, s_cur)
        out_tiles[h // 2] = o if out_tiles[h // 2] is None else out_tiles[h // 2] + o
    for jq, o in enumerate(out_tiles):
        y_ref[:, jq * LANES:(jq + 1) * LANES] = (o.T if transposed_out else o).astype(y_ref.dtype)


def _swa(slopes, sinks, q, k, v, k_prev, v_prev, nb, nblk, tq, prev_always):
    t = q.shape[0]
    cur = lambda n: pl.BlockSpec((tq, n), lambda b, i: (b * nblk + i, 0))
    if prev_always:
        prev = pl.BlockSpec((WINDOW, D_KV), lambda b, i: (b, 0))
    else:
        prev = pl.BlockSpec((WINDOW, D_KV), lambda b, i: (b * nblk + jnp.maximum(i - 1, 0), 0))
    smem = pl.BlockSpec(memory_space=pltpu.SMEM)
    return pl.pallas_call(
        functools.partial(_swa_kernel, tq=tq, prev_always=prev_always),
        grid=(nb, nblk),
        in_specs=[smem, smem, cur(D_ATTN), cur(D_KV), cur(D_KV), prev, prev],
        out_specs=cur(D_ATTN),
        out_shape=jax.ShapeDtypeStruct((t, D_ATTN), BF16),
        compiler_params=pltpu.CompilerParams(dimension_semantics=("arbitrary", "arbitrary"),
                                             vmem_limit_bytes=VMEM_LIMIT),
        name="swa",
    )(slopes, sinks, q, k, v, k_prev, v_prev)


def _post_kernel(x_ref, ys_ref, ya_ref, g_ref, wsp_ref, wap_ref, wo_ref, nw_ref, wr_ref, br_ref, *rest,
                 n_exp, lr, nt, aliased):
    if aliased:
        rest = rest[1:]
    h_ref, xs_ref, lpos_ref, pr_ref, pc_ref = rest
    tp = x_ref.shape[0]

    @pl.when(pl.program_id(0) >= nt)
    def _():
        xs_ref[...] = jnp.zeros_like(xs_ref)

    @pl.when(pl.program_id(0) < nt)
    def _():
        a = jnp.dot(ys_ref[...].astype(BF16), wsp_ref[...], preferred_element_type=F32)
        b = jnp.dot(ya_ref[...].astype(BF16), wap_ref[...], preferred_element_type=F32)
        g = g_ref[...].astype(F32)
        merged = jax.nn.sigmoid(g[:, :D_MODEL]) * a + jax.nn.sigmoid(g[:, D_MODEL:]) * b
        h = x_ref[...] + jnp.dot(merged.astype(BF16), wo_ref[...], preferred_element_type=F32)
        h_ref[...] = h
        hn = h * lax.rsqrt(jnp.mean(h * h, axis=-1, keepdims=True) + EPS) * nw_ref[...]
        w_hi, w_mid, _ = _split3(wr_ref[...])
        x_hi, x_mid, _ = _split3(hn)
        logits = (lax.dot_general(w_hi, x_hi, NT, preferred_element_type=F32)
                  + lax.dot_general(w_hi, x_mid, NT, preferred_element_type=F32)
                  + lax.dot_general(w_mid, x_hi, NT, preferred_element_type=F32)) + br_ref[...]
        eidx = lax.broadcasted_iota(I32, logits.shape, 0).astype(F32)
        work = logits
        vals, ids = [], []
        for _ in range(TOP_K):
            m = jnp.max(work, axis=0, keepdims=True)
            first = jnp.min(jnp.where(work == m, eidx, float(n_exp)), axis=0, keepdims=True)
            vals.append(m)
            ids.append(first)
            work = jnp.where(eidx == first, -jnp.inf, work)
        es = [jnp.exp(v - vals[0]) for v in vals]
        den = es[0] + es[1] + es[2] + es[3]
        eye = jnp.where(lax.broadcasted_iota(I32, (TOP_K, TOP_K), 0) == lax.broadcasted_iota(I32, (TOP_K, TOP_K), 1),
                        1.0, 0.0).astype(BF16)

        def to_columns(rows):
            return sum(lax.dot_general(p, eye, TN, preferred_element_type=F32) for p in _split3(rows))

        pr_ref[...] = to_columns(jnp.concatenate([e / den for e in es], axis=0))

        onehot = [jnp.where(eidx == i, 1.0, 0.0) for i in ids]
        counts = [jnp.sum(o, axis=1, keepdims=True) for o in onehot]
        total = counts[0] + counts[1] + counts[2] + counts[3]
        padded = jnp.floor((total + (PIECE - 1)) * (1.0 / PIECE)) * PIECE
        ei = lax.broadcasted_iota(I32, (n_exp, n_exp), 0)
        ej = lax.broadcasted_iota(I32, (n_exp, n_exp), 1)
        below = jnp.where(ej < ei, 1.0, 0.0).astype(BF16)
        padded_b = jnp.broadcast_to(padded, (n_exp, LANES))
        seg_off = sum(jnp.dot(below, p, preferred_element_type=F32) for p in _split3(padded_b))[:, 0:1]
        ti = lax.broadcasted_iota(I32, (tp, tp), 0)
        tj = lax.broadcasted_iota(I32, (tp, tp), 1)
        before = jnp.where(ti < tj, 1.0, 0.0).astype(BF16)
        base = seg_off
        lpos = []
        for k in range(TOP_K):
            prefix = jnp.dot(onehot[k].astype(BF16), before, preferred_element_type=F32)
            lpos.append(jnp.sum(onehot[k] * (base + prefix), axis=0, keepdims=True))
            base = base + counts[k]
        lpos_ref[...] = to_columns(jnp.concatenate(lpos, axis=0)).astype(I32)
        pc_ref[0] = padded_b.astype(I32)

        hn_b = hn.astype(BF16)
        rc = lr // 4
        for c in range(4):
            ri = (lax.broadcasted_iota(I32, (rc, tp), 0) + c * rc).astype(F32)
            sel = None
            for k in range(TOP_K):
                hit = jnp.where(ri == lpos[k], 1.0, 0.0)
                sel = hit if sel is None else sel + hit
            xs_ref[c * rc:(c + 1) * rc, :] = jnp.dot(sel.astype(BF16), hn_b, preferred_element_type=F32).astype(BF16)


def _post(x, y_ssm, y_attn, gates, wsp, wap, wo, ffn_nw, w_rT, b_r, tm, lr, xs_rows, xs_block0, pad_steps,
          xs_prev=None):
    t = x.shape[0]
    n_exp = w_rT.shape[0]
    nt = t // tm
    last = nt - 1
    row = lambda n: pl.BlockSpec((tm, n), lambda i: (jnp.minimum(i, last), 0))
    col = row(TOP_K)
    in_specs = [row(D_MODEL), row(D_INNER), row(D_ATTN), row(2 * D_MODEL),
                _resident_spec(wsp.shape), _resident_spec(wap.shape), _resident_spec(wo.shape),
                _const_spec(ffn_nw.shape), _const_spec(w_rT.shape), _const_spec(b_r.shape)]
    args = [x, y_ssm, y_attn, gates, wsp, wap, wo, ffn_nw, w_rT, b_r]
    aliases = {}
    if xs_prev is not None:
        in_specs.append(pl.BlockSpec(memory_space=pl.ANY))
        args.append(xs_prev)
        aliases = {len(args) - 1: 1}
    return pl.pallas_call(
        functools.partial(_post_kernel, n_exp=n_exp, lr=lr, nt=nt, aliased=xs_prev is not None),
        grid=(nt + pad_steps,),
        in_specs=in_specs,
        out_specs=[row(D_MODEL), pl.BlockSpec((lr, D_MODEL), lambda i: (xs_block0 + i, 0)), col, col,
                   pl.BlockSpec((1, n_exp, LANES), lambda i: (jnp.minimum(i, last), 0, 0))],
        out_shape=[jax.ShapeDtypeStruct((t, D_MODEL), F32), jax.ShapeDtypeStruct((xs_rows, D_MODEL), BF16),
                   jax.ShapeDtypeStruct((t, TOP_K), I32), jax.ShapeDtypeStruct((t, TOP_K), F32),
                   jax.ShapeDtypeStruct((nt, n_exp, LANES), I32)],
        input_output_aliases=aliases,
        compiler_params=pltpu.CompilerParams(dimension_semantics=("arbitrary",), vmem_limit_bytes=VMEM_LIMIT),
        name="post",
    )(*args)


def _piece(ref, p):
    return ref.at[pl.ds(pl.multiple_of(p * PIECE, PIECE), PIECE)]


def _moe_kernel(te_ref, nu_ref, src0_ref, src1_ref, srcc_ref, xs_hbm, wg_ref, wu_ref, wd_ref,
                bg_ref, bu_ref, bd_ref, ys_hbm, xbuf, obuf, wbf, gsem, ssem):
    i = pl.program_id(0)
    n_used = nu_ref[0]
    slot = i % 2
    tme = PIECES_PER_TILE * PIECE

    def gather(src_ref, s):
        for r in range(PIECES_PER_TILE):
            pltpu.make_async_copy(_piece(xs_hbm, src_ref[0, 0, r]), xbuf.at[s, pl.ds(r * PIECE, PIECE)],
                                  gsem.at[s]).start()

    def wait_gather(s):
        pltpu.make_async_copy(xs_hbm.at[pl.ds(0, tme)], xbuf.at[s], gsem.at[s]).wait()

    def wait_put(s):
        pltpu.make_async_copy(obuf.at[s], ys_hbm.at[pl.ds(0, tme)], ssem.at[s]).wait()

    @pl.when(i == 0)
    def _():
        gather(src0_ref, 0)

    @pl.when(i < n_used)
    def _():
        wait_gather(slot)

        @pl.when(i >= 2)
        def _():
            wait_put(slot)

        @pl.when((i == 0) | (te_ref[i] != te_ref[jnp.maximum(i - 1, 0)]))
        def _():
            wbf[0] = wg_ref[0].astype(BF16)
            wbf[1] = wu_ref[0].astype(BF16)
            wbf[2] = wd_ref[0].astype(BF16)

        x = xbuf[slot]
        g = jnp.minimum(jnp.dot(x, wbf[0], preferred_element_type=F32) + bg_ref[0], SWIGLU_LIMIT)
        u = jnp.clip(jnp.dot(x, wbf[1], preferred_element_type=F32) + bu_ref[0], -SWIGLU_LIMIT, SWIGLU_LIMIT)
        gather(src1_ref, 1 - slot)
        act = ((u + 1.0) * g * jax.nn.sigmoid(SWIGLU_ALPHA * g)).astype(BF16)
        obuf[slot] = (jnp.dot(act, wbf[2], preferred_element_type=F32) + bd_ref[0]).astype(BF16)
        for r in range(PIECES_PER_TILE):
            pltpu.make_async_copy(obuf.at[slot, pl.ds(r * PIECE, PIECE)], _piece(ys_hbm, srcc_ref[0, 0, r]),
                                  ssem.at[slot]).start()

        @pl.when(i == n_used - 1)
        def _():
            wait_put(slot)
            wait_gather(1 - slot)

            @pl.when(i >= 1)
            def _():
                wait_put(1 - slot)


def _moe(tile_expert, n_used, src, xs, wg, wu, wd, bg, bu, bd):
    n_tiles = tile_expert.shape[0]
    idx = src.reshape(n_tiles, 1, PIECES_PER_TILE)
    tme = PIECES_PER_TILE * PIECE
    wspec = pl.BlockSpec((1, D_MODEL, D_FF), lambda i, te, nu: (te[i], 0, 0))
    bspec = pl.BlockSpec((1, 1, D_FF), lambda i, te, nu: (te[i], 0, 0))
    ispec = lambda f: pl.BlockSpec((1, 1, PIECES_PER_TILE), lambda i, te, nu: (f(i), 0, 0),
                                   memory_space=pltpu.SMEM)
    grid_spec = pltpu.PrefetchScalarGridSpec(
        num_scalar_prefetch=2,
        grid=(n_tiles,),
        in_specs=[ispec(lambda i: 0), ispec(lambda i: jnp.minimum(i + 1, n_tiles - 1)), ispec(lambda i: i),
                  pl.BlockSpec(memory_space=pl.ANY), wspec, wspec, wspec, bspec, bspec, bspec],
        out_specs=pl.BlockSpec(memory_space=pl.ANY),
        scratch_shapes=[pltpu.VMEM((2, tme, D_MODEL), BF16), pltpu.VMEM((2, tme, D_MODEL), BF16),
                        pltpu.VMEM((3, D_MODEL, D_FF), BF16),
                        pltpu.SemaphoreType.DMA((2,)), pltpu.SemaphoreType.DMA((2,))],
    )
    return pl.pallas_call(
        _moe_kernel,
        grid_spec=grid_spec,
        out_shape=jax.ShapeDtypeStruct(xs.shape, xs.dtype),
        input_output_aliases={5: 0},
        compiler_params=pltpu.CompilerParams(dimension_semantics=("arbitrary",), vmem_limit_bytes=VMEM_LIMIT),
        name="moe",
    )(tile_expert, n_used, idx, idx, idx, xs, wg, wu, wd, bg, bu, bd)


def _combine_kernel(h_ref, ys_ref, lpos_ref, pr_ref, nw_ref, o_ref):
    tp = h_ref.shape[0]
    lr = ys_ref.shape[0]
    ri = lax.broadcasted_iota(I32, (tp, lr), 1)
    lp = lpos_ref[...]
    pr = pr_ref[...]
    pw = jnp.zeros((tp, lr), F32)
    for k in range(TOP_K):
        pw = jnp.where(ri == lp[:, k:k + 1], pr[:, k:k + 1], pw)
    moe = jnp.dot(pw.astype(BF16), ys_ref[...], preferred_element_type=F32)
    h = h_ref[...] + moe
    o_ref[...] = h * lax.rsqrt(jnp.mean(h * h, axis=-1, keepdims=True) + EPS) * nw_ref[...]


def _combine(h, ys, lpos_t, probs_t, final_nw, tm, lr, ys_block0):
    t = h.shape[0]
    return pl.pallas_call(
        _combine_kernel,
        grid=(t // tm,),
        in_specs=[pl.BlockSpec((tm, D_MODEL), lambda i: (i, 0)),
                  pl.BlockSpec((lr, D_MODEL), lambda i: (ys_block0 + i, 0)),
                  pl.BlockSpec((tm, TOP_K), lambda i: (i, 0)),
                  pl.BlockSpec((tm, TOP_K), lambda i: (i, 0)),
                  _const_spec(final_nw.shape)],
        out_specs=pl.BlockSpec((tm, D_MODEL), lambda i: (i, 0)),
        out_shape=jax.ShapeDtypeStruct((t, D_MODEL), F32),
        compiler_params=pltpu.CompilerParams(dimension_semantics=("arbitrary",), vmem_limit_bytes=VMEM_LIMIT),
        name="combine",
    )(h, ys, lpos_t, probs_t, final_nw)


def _piece_table(padded_counts, tile_row0, n_tiles, spare_piece0):
    n_pieces = (padded_counts // PIECE).T
    seg_row = tile_row0[:, None] + jnp.cumsum(padded_counts, axis=1) - padded_counts
    seg_piece = (seg_row // PIECE).T
    per_expert = n_pieces.sum(axis=1)
    tiles_per = (per_expert + PIECES_PER_TILE - 1) // PIECES_PER_TILE
    tile_end = jnp.cumsum(tiles_per)
    n_used = tile_end[-1]
    slot0 = (tile_end - tiles_per) * PIECES_PER_TILE
    seg_slot = (slot0[:, None] + jnp.cumsum(n_pieces, axis=1) - n_pieces).reshape(-1)
    seg_n = n_pieces.reshape(-1)
    seg_src = seg_piece.reshape(-1)
    slots = jnp.arange(n_tiles * PIECES_PER_TILE, dtype=I32)

    def at_segment_of_slot(f):
        df = f - jnp.concatenate([jnp.zeros((1,), I32), f[:-1]])
        return jnp.sum(jnp.where(seg_slot[None, :] <= slots[:, None], df[None, :], 0), axis=1)

    real = slots < at_segment_of_slot(seg_slot + seg_n)
    padding = jnp.logical_not(real) & (slots < n_used * PIECES_PER_TILE)
    spare = spare_piece0 + jnp.where(padding, jnp.cumsum(padding.astype(I32)), 0)
    src = jnp.where(real, slots + at_segment_of_slot(seg_src - seg_slot), spare).astype(I32)
    tile_ids = jnp.arange(n_tiles, dtype=I32)
    tile_expert = jnp.sum((tile_end[None, :] <= jnp.minimum(tile_ids, n_used - 1)[:, None]).astype(I32), axis=1)
    return tile_expert.astype(I32), n_used.reshape(1).astype(I32), src


def _pick_tile(n, pref):
    while n % pref:
        pref //= 2
    return pref


def _local_rows(tm, n_exp):
    need = TOP_K * tm + n_exp * (PIECE - 1)
    return -(-need // 64) * 64


def kernel(x_prompt, x_sample, state_conv, state_ssm, cache_swa_k, cache_swa_v, attn_norm_w, w_in, conv_w, conv_b, dt_bias, a_log, d_skip, ssm_norm_w, attn_sinks, w_ssm_proj, w_attn_proj, w_o, ffn_norm_w, w_router, b_router, w_gate, b_gate, w_up, b_up, w_down, b_down, final_norm_w):
    assert w_in.shape[0] == 1, "single-layer step"
    nb, seq, _ = x_prompt.shape
    nbs = x_sample.shape[0]
    n_exp = w_router.shape[-1]
    tp, ts = nb * seq, nbs
    pad = SUBLANES

    cuts = [0]
    for n in (D_INNER, CONV_DIM, SSM_HEADS, D_ATTN, D_KV, D_KV, 2 * D_MODEL):
        cuts.append(cuts[-1] + n)
    w_in_b = w_in[0].astype(BF16)
    ws = [w_in_b[:, cuts[i]:cuts[i + 1]] for i in range(7)]
    attn_nw = attn_norm_w[0].reshape(1, D_MODEL)
    a_neg = -jnp.exp(a_log[0].astype(F32))
    head_of = jnp.arange(D_INNER, dtype=I32) // SSM_HEAD_DIM
    expand = jnp.tile((jnp.arange(SSM_HEADS, dtype=I32)[:, None] == head_of[None, :]).astype(BF16), (2, 1))
    conv_consts = (conv_w[0], conv_b[0].reshape(1, CONV_DIM))
    ssd_consts = (dt_bias[0].reshape(1, SSM_HEADS), dt_bias[0].reshape(SSM_HEADS, 1),
                  a_neg.reshape(1, SSM_HEADS), a_neg.reshape(SSM_HEADS, 1),
                  d_skip[0][head_of].reshape(1, D_INNER), ssm_norm_w[0].reshape(1, D_INNER), expand)
    slopes = jnp.exp2(-8.0 * jnp.arange(1, N_Q_HEADS + 1, dtype=F32) / N_Q_HEADS)
    sinks = attn_sinks[0].astype(F32)
    wsp, wap, wo = w_ssm_proj[0].astype(BF16), w_attn_proj[0].astype(BF16), w_o[0].astype(BF16)
    ffn_nw = ffn_norm_w[0].reshape(1, D_MODEL)
    w_rT = w_router[0].T
    b_r = b_router[0].reshape(n_exp, 1)
    bg, bu, bd = (b[0].reshape(n_exp, 1, -1) for b in (b_gate, b_up, b_down))
    final_nw = final_norm_w.reshape(1, D_MODEL)

    xp = x_prompt.reshape(tp, D_MODEL)
    tm_in = _pick_tile(seq, 512)
    z, xbc, dt, q, k, v, gates, conv_tail, dtT = _in_proj(
        xp, attn_nw, ws, tm_in, BF16,
        conv=(jnp.zeros((nb * SUBLANES, CONV_DIM), F32), *conv_consts, seq // tm_in))
    nc = seq // CHUNK
    y_ssm, ssm_p = _ssd(xbc, z, dt, dtT, jnp.zeros((nb, SSM_HEADS, SSM_HEAD_DIM, D_STATE), F32), ssd_consts,
                        nb, nc, CHUNK, CHUNK)
    nblk = seq // WINDOW
    y_attn = _swa(slopes, sinks, q, k, v, k, v, nb, nblk, WINDOW, False)

    xs_pad = jnp.pad(x_sample.reshape(ts, 1, D_MODEL), ((0, 0), (0, pad - 1), (0, 0))).reshape(ts * pad, D_MODEL)
    z_s, xbc_s, dt_s, q_s, k_s, v_s, gates_s = _in_proj(xs_pad, attn_nw, ws, _pick_tile(ts * pad, 256), F32)
    dtT_s = dt_s.reshape(ts, pad, SSM_HEADS).transpose(0, 2, 1)
    hist_s = jnp.pad(state_conv[0], ((0, 0), (pad - (CONV_W - 1), 0), (0, 0))).reshape(ts * pad, CONV_DIM)
    y_ssm_s, ssm_s = _ssd(xbc_s, z_s, dt_s, dtT_s, state_ssm[0], ssd_consts, ts, 1, pad, 1,
                          conv=(hist_s, *conv_consts))
    kc = cache_swa_k[0].reshape(ts * WINDOW, D_KV)
    vc = cache_swa_v[0].reshape(ts * WINDOW, D_KV)
    y_attn_s = _swa(slopes, sinks, q_s, k_s, v_s, kc, vc, ts, 1, pad, True)
    real = lambda a: a.reshape(ts, pad, -1)[:, 0]

    tm_p, tm_s = _pick_tile(tp, 512), ts
    nt_p = tp // tm_p
    lr_p, lr_s = _local_rows(tm_p, n_exp), _local_rows(tm_s, n_exp)
    if (nt_p * lr_p) % lr_s or lr_s > lr_p:
        lr_s = lr_p
    spare_rows = lr_s + (1 + n_exp * (PIECES_PER_TILE - 1)) * PIECE
    pad_steps = -(-spare_rows // lr_p)
    xs_rows = (nt_p + pad_steps) * lr_p
    block_s = nt_p * lr_p // lr_s
    post_w = (wsp, wap, wo, ffn_nw, w_rT, b_r)
    h_p, xs, lpos_p, pr_p, pc_p = _post(xp, y_ssm, y_attn, gates, *post_w, tm_p, lr_p, xs_rows, 0, pad_steps)
    h_s, xs, lpos_s, pr_s, pc_s = _post(x_sample.reshape(ts, D_MODEL), real(y_ssm_s), real(y_attn_s), real(gates_s),
                                        *post_w, tm_s, lr_s, xs_rows, block_s, 0, xs_prev=xs)

    padded_counts = jnp.concatenate([pc_p[:, :, 0], pc_s[:, :, 0]], axis=0)
    tile_row0 = jnp.concatenate([jnp.arange(nt_p, dtype=I32) * lr_p, jnp.full((1,), nt_p * lr_p, I32)])
    max_pieces = (TOP_K * (tp + ts) + (PIECE - 1) * n_exp * (nt_p + 1)) // PIECE + n_exp * (PIECES_PER_TILE - 1)
    n_tiles = -(-max_pieces // PIECES_PER_TILE) + 1
    tile_expert, n_used, src = _piece_table(padded_counts, tile_row0, n_tiles, (nt_p * lr_p + lr_s) // PIECE)
    ys = _moe(tile_expert, n_used, src, xs, w_gate[0], w_up[0], w_down[0], bg, bu, bd)
    out_p = _combine(h_p, ys, lpos_p, pr_p, final_nw, tm_p, lr_p, 0)
    out_s = _combine(h_s, ys, lpos_s, pr_s, final_nw, tm_s, lr_s, block_s)

    y_prompt = out_p.reshape(nb, seq, D_MODEL)
    y_sample = out_s.reshape(nbs, 1, D_MODEL)
    conv_p = conv_tail[:, SUBLANES - (CONV_W - 1):][None]
    k_p = k.reshape(nb, seq, N_KV_HEADS, ATTN_HEAD_DIM)[:, seq - WINDOW:].astype(F32)[None]
    v_p = v.reshape(nb, seq, N_KV_HEADS, ATTN_HEAD_DIM)[:, seq - WINDOW:].astype(F32)[None]
    conv_s = jnp.concatenate([state_conv[0][:, 1:], real(xbc_s)[:, None]], axis=1)[None]
    k_new = real(k_s).reshape(ts, 1, N_KV_HEADS, ATTN_HEAD_DIM)
    v_new = real(v_s).reshape(ts, 1, N_KV_HEADS, ATTN_HEAD_DIM)
    ks_out = jnp.concatenate([cache_swa_k[0][:, 1:], k_new], axis=1)[None]
    vs_out = jnp.concatenate([cache_swa_v[0][:, 1:], v_new], axis=1)[None]
    return (y_prompt, y_sample, conv_p, ssm_p[None], k_p, v_p, conv_s, ssm_s[None], ks_out, vs_out)
```

```python
import functools

import jax
import jax.numpy as jnp
from jax import lax
from jax.experimental import pallas as pl
from jax.experimental.pallas import tpu as pltpu

F32, BF16, I32 = jnp.float32, jnp.bfloat16, jnp.int32

D_MODEL = 1024
D_INNER = 2 * D_MODEL
SSM_HEAD_DIM = 64
SSM_HEADS = D_INNER // SSM_HEAD_DIM
SSM_GROUPS = 4
SSM_HPG = SSM_HEADS // SSM_GROUPS
D_STATE = 128
CONV_W = 4
CONV_DIM = D_INNER + 2 * SSM_GROUPS * D_STATE
CHUNK = 128
ATTN_HEAD_DIM = 64
N_Q_HEADS = D_MODEL // ATTN_HEAD_DIM
N_KV_HEADS = 4
Q_PER_KV = N_Q_HEADS // N_KV_HEADS
D_ATTN = N_Q_HEADS * ATTN_HEAD_DIM
D_KV = N_KV_HEADS * ATTN_HEAD_DIM
WINDOW = 128
TOP_K = 4
D_FF = D_MODEL
SWIGLU_LIMIT = 7.0
SWIGLU_ALPHA = 1.702
EPS = 1e-5
NEG_BIG = -1e30

LANES = 128
SUBLANES = 8
GROUP_W = D_INNER // SSM_GROUPS
PIECE = 2 * SUBLANES
PIECES_PER_TILE = 32
VMEM_LIMIT = 56 * 1024 * 1024

NT = (((1,), (1,)), ((), ()))
TN = (((0,), (0,)), ((), ()))


def _const_spec(shape):
    return pl.BlockSpec(shape, lambda *_: (0,) * len(shape))


def _resident_spec(shape):
    return pl.BlockSpec(shape, lambda *_: (0,) * len(shape), pipeline_mode=pl.Buffered(1))


def _split3(x):
    hi = x.astype(BF16)
    r1 = x - hi.astype(F32)
    mid = r1.astype(BF16)
    lo = (r1 - mid.astype(F32)).astype(BF16)
    return hi, mid, lo


def _softplus(x):
    return jnp.maximum(x, 0.0) + jnp.log(1.0 + jnp.exp(-jnp.abs(x)))


def _causal_conv_silu(buf_ref, n, cw_ref, cb_ref):
    full = buf_ref[...]
    conv = cb_ref[...]
    for j in range(CONV_W):
        shifted = full if j == CONV_W - 1 else pltpu.roll(full, CONV_W - 1 - j, 0)
        conv = conv + shifted[SUBLANES:SUBLANES + n, :] * cw_ref[j:j + 1, :]
    buf_ref[0:SUBLANES, :] = full[n:n + SUBLANES, :]
    return conv * jax.nn.sigmoid(conv)


def _inproj_kernel(x_ref, nw_ref, wz_ref, wxbc_ref, wdt_ref, wq_ref, wk_ref, wv_ref, wg_ref, *rest,
                   fuse_conv, tiles_per_seq):
    if fuse_conv:
        (hist_ref, cw_ref, cb_ref, wdtT_ref, z_ref, xbc_ref, dt_ref, q_ref, k_ref, v_ref, g_ref, tail_ref, dtT_ref,
         ktail_ref, vtail_ref, buf_ref) = rest
    else:
        z_ref, xbc_ref, dt_ref, q_ref, k_ref, v_ref, g_ref = rest
    tm = x_ref.shape[0]
    if fuse_conv:
        pos = pl.program_id(0) % tiles_per_seq

        @pl.when(pos == 0)
        def _():
            buf_ref[0:SUBLANES, :] = hist_ref[...]

    x = x_ref[...]
    xn = x * lax.rsqrt(jnp.mean(x * x, axis=-1, keepdims=True) + EPS)
    xn = (xn * nw_ref[...]).astype(BF16)
    xbc = jnp.dot(xn, wxbc_ref[...], preferred_element_type=F32)
    if fuse_conv:
        buf_ref[SUBLANES:SUBLANES + tm, :] = xbc
        xbc_ref[...] = _causal_conv_silu(buf_ref, tm, cw_ref, cb_ref).astype(xbc_ref.dtype)
    else:
        xbc_ref[...] = xbc.astype(xbc_ref.dtype)
    for w_ref, o_ref in ((wz_ref, z_ref), (wdt_ref, dt_ref), (wq_ref, q_ref), (wg_ref, g_ref)):
        o_ref[...] = jnp.dot(xn, w_ref[...], preferred_element_type=F32).astype(o_ref.dtype)
    k = jnp.dot(xn, wk_ref[...], preferred_element_type=F32)
    v = jnp.dot(xn, wv_ref[...], preferred_element_type=F32)
    k_ref[...] = k.astype(k_ref.dtype)
    v_ref[...] = v.astype(v_ref.dtype)
    if fuse_conv:
        dtT_ref[...] = lax.dot_general(wdtT_ref[...], xn, NT, preferred_element_type=F32)

        @pl.when(pos == tiles_per_seq - 1)
        def _():
            tail_ref[0] = buf_ref[0:SUBLANES, :]
            ktail_ref[0] = k[tm - WINDOW:, :]
            vtail_ref[0] = v[tm - WINDOW:, :]


def _in_proj(x, norm_w, ws, tm, act_dtype, conv=None):
    t = x.shape[0]
    widths = (D_INNER, CONV_DIM, SSM_HEADS, D_ATTN, D_KV, D_KV, 2 * D_MODEL)
    dtypes = (act_dtype, act_dtype, F32, act_dtype, act_dtype, act_dtype, act_dtype)
    row = lambda n: pl.BlockSpec((tm, n), lambda i: (i, 0))
    in_specs = [row(D_MODEL), _const_spec((1, D_MODEL))] + [_resident_spec((D_MODEL, n)) for n in widths]
    out_specs = [row(n) for n in widths]
    out_shape = [jax.ShapeDtypeStruct((t, n), d) for n, d in zip(widths, dtypes)]
    args, scratch, tiles_per_seq = [x, norm_w, *ws], [], 1
    if conv is not None:
        hist, cw, cb, tiles_per_seq = conv
        n_seq = t // (tm * tiles_per_seq)
        in_specs += [pl.BlockSpec((SUBLANES, CONV_DIM), lambda i: (i // tiles_per_seq, 0)),
                     _const_spec(cw.shape), _const_spec(cb.shape), _const_spec((SSM_HEADS, D_MODEL))]
        per_seq = lambda r, c: pl.BlockSpec((1, r, c), lambda i: (i // tiles_per_seq, 0, 0))
        out_specs += [per_seq(SUBLANES, CONV_DIM), pl.BlockSpec((SSM_HEADS, tm), lambda i: (0, i)),
                      per_seq(WINDOW, D_KV), per_seq(WINDOW, D_KV)]
        out_shape += [jax.ShapeDtypeStruct((n_seq, SUBLANES, CONV_DIM), F32),
                      jax.ShapeDtypeStruct((SSM_HEADS, t), F32),
                      jax.ShapeDtypeStruct((n_seq, WINDOW, D_KV), F32),
                      jax.ShapeDtypeStruct((n_seq, WINDOW, D_KV), F32)]
        args += [hist, cw, cb, ws[2].T]
        scratch = [pltpu.VMEM((SUBLANES + tm, CONV_DIM), F32)]
    return pl.pallas_call(
        functools.partial(_inproj_kernel, fuse_conv=conv is not None, tiles_per_seq=tiles_per_seq),
        grid=(t // tm,),
        in_specs=in_specs,
        out_specs=out_specs,
        out_shape=out_shape,
        scratch_shapes=scratch,
        compiler_params=pltpu.CompilerParams(dimension_semantics=("arbitrary",), vmem_limit_bytes=VMEM_LIMIT),
        name="in_proj",
    )(*args)


def _ssd_kernel(xbc_ref, z_ref, dt_ref, dtT_ref, h0_ref, dtb_ref, dtbT_ref, a_ref, aT_ref, dsk_ref, nw_ref, e_ref,
                *rest, lc, n_valid, conv):
    if conv:
        hist_ref, cw_ref, cb_ref, y_ref, h_ref, buf_ref = rest
    else:
        y_ref, h_ref = rest
    c = pl.program_id(1)

    @pl.when(c == 0)
    def _():
        h_ref[...] = h0_ref[...]

    if conv:
        @pl.when(c == 0)
        def _():
            buf_ref[0:SUBLANES, :] = hist_ref[...]

        buf_ref[SUBLANES:SUBLANES + lc, :] = xbc_ref[...].astype(F32)
        act = _causal_conv_silu(buf_ref, lc, cw_ref, cb_ref)
    else:
        act = xbc_ref[...].astype(F32)
    xs = act[:, :D_INNER]
    bm = act[:, D_INNER:D_INNER + SSM_GROUPS * D_STATE].astype(BF16)
    cm = act[:, D_INNER + SSM_GROUPS * D_STATE:].astype(BF16)

    dt = _softplus(dt_ref[...] + dtb_ref[...])
    dtT_raw = dtT_ref[0] if len(dtT_ref.shape) == 3 else dtT_ref[...]
    dtT = _softplus(dtT_raw + dtbT_ref[...])
    if n_valid < lc:
        dt = jnp.where(lax.broadcasted_iota(I32, dt.shape, 0) < n_valid, dt, 0.0)
        dtT = jnp.where(lax.broadcasted_iota(I32, dtT.shape, 1) < n_valid, dtT, 0.0)
    la = dt * a_ref[...]
    laT = dtT * aT_ref[...]
    li = lax.broadcasted_iota(I32, (lc, lc), 0)
    si = lax.broadcasted_iota(I32, (lc, lc), 1)
    causal = li >= si
    tril = jnp.where(causal, 1.0, 0.0).astype(BF16)
    triu = jnp.where(li <= si, 1.0, 0.0).astype(BF16)
    cum = sum(jnp.dot(tril, p, preferred_element_type=F32) for p in _split3(la))
    cumT = sum(jnp.dot(p, triu, preferred_element_type=F32) for p in _split3(laT))
    ec = jnp.exp(cum)
    dte = jnp.exp(cum[lc - 1:lc, :] - cum)
    cd = jnp.exp(cumT[:, lc - 1:lc])

    def expand(v):
        hi, mid, _ = _split3(v)
        return jnp.dot(jnp.concatenate([hi, mid], axis=1), e_ref[...], preferred_element_type=F32)

    dt_x, ec_x, dte_x = expand(dt), expand(ec), expand(dte)
    xdt = xs * dt_x
    xdt_b = xdt.astype(BF16)
    xdte_b = (xdt * dte_x).astype(BF16)
    lane = lax.broadcasted_iota(I32, (lc, LANES), 1)
    low_half = lane < SSM_HEAD_DIM

    for g in range(SSM_GROUPS):
        gs = slice(g * GROUP_W, (g + 1) * GROUP_W)
        bm_g = bm[:, g * D_STATE:(g + 1) * D_STATE]
        cm_g = cm[:, g * D_STATE:(g + 1) * D_STATE]
        cb = lax.dot_general(cm_g, bm_g, NT, preferred_element_type=F32)
        cbm = jnp.where(causal, cb, 0.0)
        h_g = h_ref[0, g * SSM_HPG:(g + 1) * SSM_HPG].reshape(GROUP_W, D_STATE)
        y_off = lax.dot_general(cm_g, h_g.astype(BF16), NT, preferred_element_type=F32) * ec_x[:, gs]
        tiles = []
        for j in range(GROUP_W // LANES):
            col = g * GROUP_W + j * LANES
            x_pair = xdt_b[:, col:col + LANES]
            acc = None
            for half in range(2):
                h = col // SSM_HEAD_DIM + half
                seg = cum[:, h:h + 1] - cumT[h:h + 1, :]
                m = (cbm * jnp.exp(jnp.where(causal, seg, 0.0))).astype(BF16)
                x_h = jnp.where(low_half if half == 0 else jnp.logical_not(low_half), x_pair, jnp.zeros_like(x_pair))
                d = jnp.dot(m, x_h, preferred_element_type=F32)
                acc = d if acc is None else acc + d
            tiles.append(acc)
        y_g = jnp.concatenate(tiles, axis=1) + y_off + xs[:, gs] * dsk_ref[:, gs]
        zg = z_ref[:, gs].astype(F32)
        y_g = y_g * (zg * jax.nn.sigmoid(zg))
        y_g = y_g * lax.rsqrt(jnp.mean(y_g * y_g, axis=-1, keepdims=True) + EPS)
        y_ref[:, gs] = (y_g * nw_ref[:, gs]).astype(y_ref.dtype)
        st = lax.dot_general(xdte_b[:, gs], bm_g, TN, preferred_element_type=F32)
        for hh in range(SSM_HPG):
            h = g * SSM_HPG + hh
            rows = slice(hh * SSM_HEAD_DIM, (hh + 1) * SSM_HEAD_DIM)
            h_ref[0, h] = h_g[rows, :] * cd[h:h + 1, 0:1] + st[rows, :]


def _ssd(xbc, z, dt, dtT, h0, consts, nb, nc, lc, n_valid, conv=None):
    t = xbc.shape[0]
    row = lambda n: pl.BlockSpec((lc, n), lambda b, c: (b * nc + c, 0))
    if dtT.ndim == 2:
        dtT_spec = pl.BlockSpec((SSM_HEADS, lc), lambda b, c: (0, b * nc + c))
    else:
        dtT_spec = pl.BlockSpec((1, SSM_HEADS, lc), lambda b, c: (b * nc + c, 0, 0))
    in_specs = [
        row(CONV_DIM), row(D_INNER), row(SSM_HEADS), dtT_spec,
        pl.BlockSpec((1, SSM_HEADS, SSM_HEAD_DIM, D_STATE), lambda b, c: (b, 0, 0, 0)),
    ] + [_const_spec(a.shape) for a in consts]
    args, scratch = [xbc, z, dt, dtT, h0, *consts], []
    if conv is not None:
        hist, cw, cb = conv
        in_specs += [pl.BlockSpec((SUBLANES, CONV_DIM), lambda b, c: (b, 0)), _const_spec(cw.shape),
                     _const_spec(cb.shape)]
        args += [hist, cw, cb]
        scratch = [pltpu.VMEM((SUBLANES + lc, CONV_DIM), F32)]
    return pl.pallas_call(
        functools.partial(_ssd_kernel, lc=lc, n_valid=n_valid, conv=conv is not None),
        grid=(nb, nc),
        in_specs=in_specs,
        out_specs=[row(D_INNER), pl.BlockSpec((1, SSM_HEADS, SSM_HEAD_DIM, D_STATE), lambda b, c: (b, 0, 0, 0))],
        out_shape=[jax.ShapeDtypeStruct((t, D_INNER), BF16),
                   jax.ShapeDtypeStruct((nb, SSM_HEADS, SSM_HEAD_DIM, D_STATE), F32)],
        scratch_shapes=scratch,
        compiler_params=pltpu.CompilerParams(dimension_semantics=("arbitrary", "arbitrary"),
                                             vmem_limit_bytes=VMEM_LIMIT),
        name="ssd",
    )(*args)


def _swa_kernel(slope_ref, sink_ref, q_ref, kc_ref, vc_ref, kp_ref, vp_ref, y_ref, *, tq, prev_always):
    nk = WINDOW + tq
    j = lax.broadcasted_iota(I32, (nk, tq), 0)
    r = lax.broadcasted_iota(I32, (nk, tq), 1)
    dist = r + WINDOW - j
    valid = (dist >= 0) & (dist < WINDOW)
    if not prev_always:
        valid = valid & ((j >= WINDOW) | (pl.program_id(1) > 0))
    distf = dist.astype(F32)
    lane = lax.broadcasted_iota(I32, (nk, LANES), 1)
    transposed_out = tq == WINDOW
    zeros_half = jnp.zeros((ATTN_HEAD_DIM, nk), BF16)
    sink_row = lax.broadcasted_iota(I32, (SUBLANES, tq), 0) == 0
    ones_keys = jnp.ones((nk + SUBLANES, LANES), BF16)
    heads = []
    for t in range(D_KV // LANES):
        cols = slice(t * LANES, (t + 1) * LANES)
        kt = jnp.concatenate([kp_ref[:, cols].astype(F32), kc_ref[:, cols].astype(F32)], axis=0)
        vt = jnp.concatenate([vp_ref[:, cols].astype(F32), vc_ref[:, cols].astype(F32)], axis=0)
        if transposed_out:
            vt_t = vt.T.astype(BF16)
        for b in range(2):
            mine = (lane >= ATTN_HEAD_DIM) if b else (lane < ATTN_HEAD_DIM)
            k_same = jnp.where(mine, kt, 0.0)
            k_half = {b: k_same.astype(BF16), 1 - b: pltpu.roll(k_same, ATTN_HEAD_DIM, 1).astype(BF16)}
            if transposed_out:
                v_g = vt_t[b * ATTN_HEAD_DIM:(b + 1) * ATTN_HEAD_DIM, :]
                v_half = {0: jnp.concatenate([v_g, zeros_half], axis=0),
                          1: jnp.concatenate([zeros_half, v_g], axis=0)}
            else:
                v_same = jnp.concatenate([jnp.where(mine, vt, 0.0), jnp.zeros((SUBLANES, LANES), F32)], axis=0)
                v_half = {b: v_same.astype(BF16), 1 - b: pltpu.roll(v_same, ATTN_HEAD_DIM, 1).astype(BF16)}
            for qi in range(Q_PER_KV):
                a = qi % 2
                heads.append((k_half[a], v_half[a]))

    def scores(h):
        qt = q_ref[:, (h // 2) * LANES:(h // 2 + 1) * LANES].astype(BF16)
        s = lax.dot_general(heads[h][0], qt, NT, preferred_element_type=F32)
        s = s * (ATTN_HEAD_DIM ** -0.5) - slope_ref[h] * distf
        return jnp.where(valid, s, NEG_BIG)

    def attend(h, s):
        sink = sink_ref[h]
        m = jnp.maximum(jnp.max(s, axis=0, keepdims=True), sink)
        e = jnp.exp(s - m)
        e_sink = jnp.exp(sink - m)
        if transposed_out:
            rden = 1.0 / (jnp.sum(e, axis=0, keepdims=True) + e_sink)
            return jnp.dot(heads[h][1], e.astype(BF16), preferred_element_type=F32) * rden
        p = jnp.concatenate([e, jnp.where(sink_row, e_sink, 0.0)], axis=0).astype(BF16)
        num = lax.dot_general(p, heads[h][1], TN, preferred_element_type=F32)
        den = lax.dot_general(p, ones_keys, TN, preferred_element_type=F32)
        return num / den

    out_tiles = [None] * (D_ATTN // LANES)
    s_next = scores(0)
    for h in range(N_Q_HEADS):
        s_cur = s_next
        if h + 1 < N_Q_HEADS:
            s_next = scores(h + 1)
        o = attend(h, s_cur)
        out_tiles[h // 2] = o if out_tiles[h // 2] is None else out_tiles[h // 2] + o
    for jq, o in enumerate(out_tiles):
        y_ref[:, jq * LANES:(jq + 1) * LANES] = (o.T if transposed_out else o).astype(y_ref.dtype)


def _swa(slopes, sinks, q, k, v, k_prev, v_prev, nb, nblk, tq, prev_always):
    t = q.shape[0]
    cur = lambda n: pl.BlockSpec((tq, n), lambda b, i: (b * nblk + i, 0))
    if prev_always:
        prev = pl.BlockSpec((WINDOW, D_KV), lambda b, i: (b, 0))
    else:
        prev = pl.BlockSpec((WINDOW, D_KV), lambda b, i: (b * nblk + jnp.maximum(i - 1, 0), 0))
    smem = pl.BlockSpec(memory_space=pltpu.SMEM)
    return pl.pallas_call(
        functools.partial(_swa_kernel, tq=tq, prev_always=prev_always),
        grid=(nb, nblk),
        in_specs=[smem, smem, cur(D_ATTN), cur(D_KV), cur(D_KV), prev, prev],
        out_specs=cur(D_ATTN),
        out_shape=jax.ShapeDtypeStruct((t, D_ATTN), BF16),
        compiler_params=pltpu.CompilerParams(dimension_semantics=("arbitrary", "arbitrary"),
                                             vmem_limit_bytes=VMEM_LIMIT),
        name="swa",
    )(slopes, sinks, q, k, v, k_prev, v_prev)


def _post_kernel(x_ref, ys_ref, ya_ref, g_ref, wsp_ref, wap_ref, wo_ref, nw_ref, wr_ref, br_ref, *rest,
                 n_exp, lr, nt, aliased):
    if aliased:
        rest = rest[1:]
    h_ref, xs_ref, lpos_ref, pr_ref, pc_ref = rest
    tp = x_ref.shape[0]

    @pl.when(pl.program_id(0) >= nt)
    def _():
        xs_ref[...] = jnp.zeros_like(xs_ref)

    @pl.when(pl.program_id(0) < nt)
    def _():
        a = jnp.dot(ys_ref[...].astype(BF16), wsp_ref[...], preferred_element_type=F32)
        b = jnp.dot(ya_ref[...].astype(BF16), wap_ref[...], preferred_element_type=F32)
        g = g_ref[...].astype(F32)
        merged = jax.nn.sigmoid(g[:, :D_MODEL]) * a + jax.nn.sigmoid(g[:, D_MODEL:]) * b
        h = x_ref[...] + jnp.dot(merged.astype(BF16), wo_ref[...], preferred_element_type=F32)
        h_ref[...] = h
        hn = h * lax.rsqrt(jnp.mean(h * h, axis=-1, keepdims=True) + EPS) * nw_ref[...]
        w_hi, w_mid, _ = _split3(wr_ref[...])
        x_hi, x_mid, _ = _split3(hn)
        logits = (lax.dot_general(w_hi, x_hi, NT, preferred_element_type=F32)
                  + lax.dot_general(w_hi, x_mid, NT, preferred_element_type=F32)
                  + lax.dot_general(w_mid, x_hi, NT, preferred_element_type=F32)) + br_ref[...]
        eidx = lax.broadcasted_iota(I32, logits.shape, 0).astype(F32)
        work = logits
        vals, ids = [], []
        for _ in range(TOP_K):
            m = jnp.max(work, axis=0, keepdims=True)
            first = jnp.min(jnp.where(work == m, eidx, float(n_exp)), axis=0, keepdims=True)
            vals.append(m)
            ids.append(first)
            work = jnp.where(eidx == first, -jnp.inf, work)
        es = [jnp.exp(v - vals[0]) for v in vals]
        den = es[0] + es[1] + es[2] + es[3]
        eye = jnp.where(lax.broadcasted_iota(I32, (TOP_K, TOP_K), 0) == lax.broadcasted_iota(I32, (TOP_K, TOP_K), 1),
                        1.0, 0.0).astype(BF16)

        def to_columns(rows):
            return sum(lax.dot_general(p, eye, TN, preferred_element_type=F32) for p in _split3(rows))

        pr_ref[...] = to_columns(jnp.concatenate([e / den for e in es], axis=0))

        onehot = [jnp.where(eidx == i, 1.0, 0.0) for i in ids]
        counts = [jnp.sum(o, axis=1, keepdims=True) for o in onehot]
        total = counts[0] + counts[1] + counts[2] + counts[3]
        padded = jnp.floor((total + (PIECE - 1)) * (1.0 / PIECE)) * PIECE
        ei = lax.broadcasted_iota(I32, (n_exp, n_exp), 0)
        ej = lax.broadcasted_iota(I32, (n_exp, n_exp), 1)
        below = jnp.where(ej < ei, 1.0, 0.0).astype(BF16)
        padded_b = jnp.broadcast_to(padded, (n_exp, LANES))
        seg_off = sum(jnp.dot(below, p, preferred_element_type=F32) for p in _split3(padded_b))[:, 0:1]
        ti = lax.broadcasted_iota(I32, (tp, tp), 0)
        tj = lax.broadcasted_iota(I32, (tp, tp), 1)
        before = jnp.where(ti < tj, 1.0, 0.0).astype(BF16)
        base = seg_off
        lpos = []
        for k in range(TOP_K):
            prefix = jnp.dot(onehot[k].astype(BF16), before, preferred_element_type=F32)
            lpos.append(jnp.sum(onehot[k] * (base + prefix), axis=0, keepdims=True))
            base = base + counts[k]
        lpos_ref[...] = to_columns(jnp.concatenate(lpos, axis=0)).astype(I32)
        pc_ref[0] = padded_b.astype(I32)

        hn_b = hn.astype(BF16)
        rc = lr // 4
        for c in range(4):
            ri = (lax.broadcasted_iota(I32, (rc, tp), 0) + c * rc).astype(F32)
            sel = None
            for k in range(TOP_K):
                hit = jnp.where(ri == lpos[k], 1.0, 0.0)
                sel = hit if sel is None else sel + hit
            xs_ref[c * rc:(c + 1) * rc, :] = jnp.dot(sel.astype(BF16), hn_b, preferred_element_type=F32).astype(BF16)


def _post(x, y_ssm, y_attn, gates, wsp, wap, wo, ffn_nw, w_rT, b_r, tm, lr, xs_rows, xs_block0, pad_steps,
          xs_prev=None):
    t = x.shape[0]
    n_exp = w_rT.shape[0]
    nt = t // tm
    last = nt - 1
    row = lambda n: pl.BlockSpec((tm, n), lambda i: (jnp.minimum(i, last), 0))
    col = row(TOP_K)
    in_specs = [row(D_MODEL), row(D_INNER), row(D_ATTN), row(2 * D_MODEL),
                _resident_spec(wsp.shape), _resident_spec(wap.shape), _resident_spec(wo.shape),
                _const_spec(ffn_nw.shape), _const_spec(w_rT.shape), _const_spec(b_r.shape)]
    args = [x, y_ssm, y_attn, gates, wsp, wap, wo, ffn_nw, w_rT, b_r]
    aliases = {}
    if xs_prev is not None:
        in_specs.append(pl.BlockSpec(memory_space=pl.ANY))
        args.append(xs_prev)
        aliases = {len(args) - 1: 1}
    return pl.pallas_call(
        functools.partial(_post_kernel, n_exp=n_exp, lr=lr, nt=nt, aliased=xs_prev is not None),
        grid=(nt + pad_steps,),
        in_specs=in_specs,
        out_specs=[row(D_MODEL), pl.BlockSpec((lr, D_MODEL), lambda i: (xs_block0 + i, 0)), col, col,
                   pl.BlockSpec((1, n_exp, LANES), lambda i: (jnp.minimum(i, last), 0, 0))],
        out_shape=[jax.ShapeDtypeStruct((t, D_MODEL), F32), jax.ShapeDtypeStruct((xs_rows, D_MODEL), BF16),
                   jax.ShapeDtypeStruct((t, TOP_K), I32), jax.ShapeDtypeStruct((t, TOP_K), F32),
                   jax.ShapeDtypeStruct((nt, n_exp, LANES), I32)],
        input_output_aliases=aliases,
        compiler_params=pltpu.CompilerParams(dimension_semantics=("arbitrary",), vmem_limit_bytes=VMEM_LIMIT),
        name="post",
    )(*args)


def _piece(ref, p):
    return ref.at[pl.ds(pl.multiple_of(p * PIECE, PIECE), PIECE)]


def _moe_kernel(te_ref, nu_ref, src0_ref, src1_ref, srcc_ref, xs_hbm, wg_ref, wu_ref, wd_ref,
                bg_ref, bu_ref, bd_ref, ys_hbm, xbuf, obuf, wbf, gsem, ssem):
    i = pl.program_id(0)
    n_used = nu_ref[0]
    slot = i % 2
    tme = PIECES_PER_TILE * PIECE

    def gather(src_ref, s):
        for r in range(PIECES_PER_TILE):
            pltpu.make_async_copy(_piece(xs_hbm, src_ref[0, 0, r]), xbuf.at[s, pl.ds(r * PIECE, PIECE)],
                                  gsem.at[s]).start()

    def wait_gather(s):
        pltpu.make_async_copy(xs_hbm.at[pl.ds(0, tme)], xbuf.at[s], gsem.at[s]).wait()

    def wait_put(s):
        pltpu.make_async_copy(obuf.at[s], ys_hbm.at[pl.ds(0, tme)], ssem.at[s]).wait()

    @pl.when(i == 0)
    def _():
        gather(src0_ref, 0)

    @pl.when(i < n_used)
    def _():
        wait_gather(slot)

        @pl.when(i >= 2)
        def _():
            wait_put(slot)

        @pl.when((i == 0) | (te_ref[i] != te_ref[jnp.maximum(i - 1, 0)]))
        def _():
            wbf[0] = wg_ref[0].astype(BF16)
            wbf[1] = wu_ref[0].astype(BF16)
            wbf[2] = wd_ref[0].astype(BF16)

        x = xbuf[slot]
        g = jnp.minimum(jnp.dot(x, wbf[0], preferred_element_type=F32) + bg_ref[0], SWIGLU_LIMIT)
        gather(src1_ref, 1 - slot)
        u = jnp.clip(jnp.dot(x, wbf[1], preferred_element_type=F32) + bu_ref[0], -SWIGLU_LIMIT, SWIGLU_LIMIT)
        act = ((u + 1.0) * g * jax.nn.sigmoid(SWIGLU_ALPHA * g)).astype(BF16)
        obuf[slot] = (jnp.dot(act, wbf[2], preferred_element_type=F32) + bd_ref[0]).astype(BF16)
        for r in range(PIECES_PER_TILE):
            pltpu.make_async_copy(obuf.at[slot, pl.ds(r * PIECE, PIECE)], _piece(ys_hbm, srcc_ref[0, 0, r]),
                                  ssem.at[slot]).start()

        @pl.when(i == n_used - 1)
        def _():
            wait_put(slot)
            wait_gather(1 - slot)

            @pl.when(i >= 1)
            def _():
                wait_put(1 - slot)


def _moe(tile_expert, n_used, src, xs, wg, wu, wd, bg, bu, bd):
    n_tiles = tile_expert.shape[0]
    idx = src.reshape(n_tiles, 1, PIECES_PER_TILE)
    tme = PIECES_PER_TILE * PIECE
    wspec = pl.BlockSpec((1, D_MODEL, D_FF), lambda i, te, nu: (te[i], 0, 0))
    bspec = pl.BlockSpec((1, 1, D_FF), lambda i, te, nu: (te[i], 0, 0))
    ispec = lambda f: pl.BlockSpec((1, 1, PIECES_PER_TILE), lambda i, te, nu: (f(i), 0, 0),
                                   memory_space=pltpu.SMEM)
    grid_spec = pltpu.PrefetchScalarGridSpec(
        num_scalar_prefetch=2,
        grid=(n_tiles,),
        in_specs=[ispec(lambda i: 0), ispec(lambda i: jnp.minimum(i + 1, n_tiles - 1)), ispec(lambda i: i),
                  pl.BlockSpec(memory_space=pl.ANY), wspec, wspec, wspec, bspec, bspec, bspec],
        out_specs=pl.BlockSpec(memory_space=pl.ANY),
        scratch_shapes=[pltpu.VMEM((2, tme, D_MODEL), BF16), pltpu.VMEM((2, tme, D_MODEL), BF16),
                        pltpu.VMEM((3, D_MODEL, D_FF), BF16),
                        pltpu.SemaphoreType.DMA((2,)), pltpu.SemaphoreType.DMA((2,))],
    )
    return pl.pallas_call(
        _moe_kernel,
        grid_spec=grid_spec,
        out_shape=jax.ShapeDtypeStruct(xs.shape, xs.dtype),
        input_output_aliases={5: 0},
        compiler_params=pltpu.CompilerParams(dimension_semantics=("arbitrary",), vmem_limit_bytes=VMEM_LIMIT),
        name="moe",
    )(tile_expert, n_used, idx, idx, idx, xs, wg, wu, wd, bg, bu, bd)


def _combine_kernel(h_ref, ys_ref, lpos_ref, pr_ref, nw_ref, o_ref):
    tp = h_ref.shape[0]
    lr = ys_ref.shape[0]
    ri = lax.broadcasted_iota(I32, (tp, lr), 1)
    lp = lpos_ref[...]
    pr = pr_ref[...]
    pw = jnp.zeros((tp, lr), F32)
    for k in range(TOP_K):
        pw = jnp.where(ri == lp[:, k:k + 1], pr[:, k:k + 1], pw)
    moe = jnp.dot(pw.astype(BF16), ys_ref[...], preferred_element_type=F32)
    h = h_ref[...] + moe
    o_ref[...] = h * lax.rsqrt(jnp.mean(h * h, axis=-1, keepdims=True) + EPS) * nw_ref[...]


def _combine(h, ys, lpos_t, probs_t, final_nw, tm, lr, ys_block0):
    t = h.shape[0]
    return pl.pallas_call(
        _combine_kernel,
        grid=(t // tm,),
        in_specs=[pl.BlockSpec((tm, D_MODEL), lambda i: (i, 0)),
                  pl.BlockSpec((lr, D_MODEL), lambda i: (ys_block0 + i, 0)),
                  pl.BlockSpec((tm, TOP_K), lambda i: (i, 0)),
                  pl.BlockSpec((tm, TOP_K), lambda i: (i, 0)),
                  _const_spec(final_nw.shape)],
        out_specs=pl.BlockSpec((tm, D_MODEL), lambda i: (i, 0)),
        out_shape=jax.ShapeDtypeStruct((t, D_MODEL), F32),
        compiler_params=pltpu.CompilerParams(dimension_semantics=("arbitrary",), vmem_limit_bytes=VMEM_LIMIT),
        name="combine",
    )(h, ys, lpos_t, probs_t, final_nw)


def _piece_table(padded_counts, tile_row0, n_tiles, spare_piece0):
    n_pieces = (padded_counts // PIECE).T
    seg_row = tile_row0[:, None] + jnp.cumsum(padded_counts, axis=1) - padded_counts
    seg_piece = (seg_row // PIECE).T
    per_expert = n_pieces.sum(axis=1)
    tiles_per = (per_expert + PIECES_PER_TILE - 1) // PIECES_PER_TILE
    tile_end = jnp.cumsum(tiles_per)
    n_used = tile_end[-1]
    slot0 = (tile_end - tiles_per) * PIECES_PER_TILE
    seg_slot = (slot0[:, None] + jnp.cumsum(n_pieces, axis=1) - n_pieces).reshape(-1)
    seg_n = n_pieces.reshape(-1)
    seg_src = seg_piece.reshape(-1)
    slots = jnp.arange(n_tiles * PIECES_PER_TILE, dtype=I32)

    def at_segment_of_slot(f):
        df = f - jnp.concatenate([jnp.zeros((1,), I32), f[:-1]])
        return jnp.sum(jnp.where(seg_slot[None, :] <= slots[:, None], df[None, :], 0), axis=1)

    real = slots < at_segment_of_slot(seg_slot + seg_n)
    padding = jnp.logical_not(real) & (slots < n_used * PIECES_PER_TILE)
    spare = spare_piece0 + jnp.where(padding, jnp.cumsum(padding.astype(I32)), 0)
    src = jnp.where(real, slots + at_segment_of_slot(seg_src - seg_slot), spare).astype(I32)
    tile_ids = jnp.arange(n_tiles, dtype=I32)
    tile_expert = jnp.sum((tile_end[None, :] <= jnp.minimum(tile_ids, n_used - 1)[:, None]).astype(I32), axis=1)
    return tile_expert.astype(I32), n_used.reshape(1).astype(I32), src


def _pick_tile(n, pref):
    while n % pref:
        pref //= 2
    return pref


def _local_rows(tm, n_exp):
    need = TOP_K * tm + n_exp * (PIECE - 1)
    return -(-need // 64) * 64


def kernel(x_prompt, x_sample, state_conv, state_ssm, cache_swa_k, cache_swa_v, attn_norm_w, w_in, conv_w, conv_b, dt_bias, a_log, d_skip, ssm_norm_w, attn_sinks, w_ssm_proj, w_attn_proj, w_o, ffn_norm_w, w_router, b_router, w_gate, b_gate, w_up, b_up, w_down, b_down, final_norm_w):
    assert w_in.shape[0] == 1, "single-layer step"
    nb, seq, _ = x_prompt.shape
    nbs = x_sample.shape[0]
    n_exp = w_router.shape[-1]
    tp, ts = nb * seq, nbs
    pad = SUBLANES

    cuts = [0]
    for n in (D_INNER, CONV_DIM, SSM_HEADS, D_ATTN, D_KV, D_KV, 2 * D_MODEL):
        cuts.append(cuts[-1] + n)
    w_in_b = w_in[0].astype(BF16)
    ws = [w_in_b[:, cuts[i]:cuts[i + 1]] for i in range(7)]
    attn_nw = attn_norm_w[0].reshape(1, D_MODEL)
    a_neg = -jnp.exp(a_log[0].astype(F32))
    head_of = jnp.arange(D_INNER, dtype=I32) // SSM_HEAD_DIM
    expand = jnp.tile((jnp.arange(SSM_HEADS, dtype=I32)[:, None] == head_of[None, :]).astype(BF16), (2, 1))
    conv_consts = (conv_w[0], conv_b[0].reshape(1, CONV_DIM))
    ssd_consts = (dt_bias[0].reshape(1, SSM_HEADS), dt_bias[0].reshape(SSM_HEADS, 1),
                  a_neg.reshape(1, SSM_HEADS), a_neg.reshape(SSM_HEADS, 1),
                  d_skip[0][head_of].reshape(1, D_INNER), ssm_norm_w[0].reshape(1, D_INNER), expand)
    slopes = jnp.exp2(-8.0 * jnp.arange(1, N_Q_HEADS + 1, dtype=F32) / N_Q_HEADS)
    sinks = attn_sinks[0].astype(F32)
    wsp, wap, wo = w_ssm_proj[0].astype(BF16), w_attn_proj[0].astype(BF16), w_o[0].astype(BF16)
    ffn_nw = ffn_norm_w[0].reshape(1, D_MODEL)
    w_rT = w_router[0].T
    b_r = b_router[0].reshape(n_exp, 1)
    bg, bu, bd = (b[0].reshape(n_exp, 1, -1) for b in (b_gate, b_up, b_down))
    final_nw = final_norm_w.reshape(1, D_MODEL)

    xp = x_prompt.reshape(tp, D_MODEL)
    tm_in = _pick_tile(seq, 512)
    z, xbc, dt, q, k, v, gates, conv_tail, dtT, k_tail, v_tail = _in_proj(
        xp, attn_nw, ws, tm_in, BF16,
        conv=(jnp.zeros((nb * SUBLANES, CONV_DIM), F32), *conv_consts, seq // tm_in))
    nc = seq // CHUNK
    y_ssm, ssm_p = _ssd(xbc, z, dt, dtT, jnp.zeros((nb, SSM_HEADS, SSM_HEAD_DIM, D_STATE), F32), ssd_consts,
                        nb, nc, CHUNK, CHUNK)
    nblk = seq // WINDOW
    y_attn = _swa(slopes, sinks, q, k, v, k, v, nb, nblk, WINDOW, False)

    xs_pad = jnp.pad(x_sample.reshape(ts, 1, D_MODEL), ((0, 0), (0, pad - 1), (0, 0))).reshape(ts * pad, D_MODEL)
    z_s, xbc_s, dt_s, q_s, k_s, v_s, gates_s = _in_proj(xs_pad, attn_nw, ws, _pick_tile(ts * pad, 256), F32)
    dtT_s = dt_s.reshape(ts, pad, SSM_HEADS).transpose(0, 2, 1)
    hist_s = jnp.pad(state_conv[0], ((0, 0), (pad - (CONV_W - 1), 0), (0, 0))).reshape(ts * pad, CONV_DIM)
    y_ssm_s, ssm_s = _ssd(xbc_s, z_s, dt_s, dtT_s, state_ssm[0], ssd_consts, ts, 1, pad, 1,
                          conv=(hist_s, *conv_consts))
    kc = cache_swa_k[0].reshape(ts * WINDOW, D_KV)
    vc = cache_swa_v[0].reshape(ts * WINDOW, D_KV)
    y_attn_s = _swa(slopes, sinks, q_s, k_s, v_s, kc, vc, ts, 1, pad, True)
    real = lambda a: a.reshape(ts, pad, -1)[:, 0]

    tm_p, tm_s = _pick_tile(tp, 512), ts
    nt_p = tp // tm_p
    lr_p, lr_s = _local_rows(tm_p, n_exp), _local_rows(tm_s, n_exp)
    if (nt_p * lr_p) % lr_s or lr_s > lr_p:
        lr_s = lr_p
    spare_rows = lr_s + (1 + n_exp * (PIECES_PER_TILE - 1)) * PIECE
    pad_steps = -(-spare_rows // lr_p)
    xs_rows = (nt_p + pad_steps) * lr_p
    block_s = nt_p * lr_p // lr_s
    post_w = (wsp, wap, wo, ffn_nw, w_rT, b_r)
    h_p, xs, lpos_p, pr_p, pc_p = _post(xp, y_ssm, y_attn, gates, *post_w, tm_p, lr_p, xs_rows, 0, pad_steps)
    h_s, xs, lpos_s, pr_s, pc_s = _post(x_sample.reshape(ts, D_MODEL), real(y_ssm_s), real(y_attn_s), real(gates_s),
                                        *post_w, tm_s, lr_s, xs_rows, block_s, 0, xs_prev=xs)

    padded_counts = jnp.concatenate([pc_p[:, :, 0], pc_s[:, :, 0]], axis=0)
    tile_row0 = jnp.concatenate([jnp.arange(nt_p, dtype=I32) * lr_p, jnp.full((1,), nt_p * lr_p, I32)])
    max_pieces = (TOP_K * (tp + ts) + (PIECE - 1) * n_exp * (nt_p + 1)) // PIECE + n_exp * (PIECES_PER_TILE - 1)
    n_tiles = -(-max_pieces // PIECES_PER_TILE) + 1
    tile_expert, n_used, src = _piece_table(padded_counts, tile_row0, n_tiles, (nt_p * lr_p + lr_s) // PIECE)
    ys = _moe(tile_expert, n_used, src, xs, w_gate[0], w_up[0], w_down[0], bg, bu, bd)
    out_p = _combine(h_p, ys, lpos_p, pr_p, final_nw, tm_p, lr_p, 0)
    out_s = _combine(h_s, ys, lpos_s, pr_s, final_nw, tm_s, lr_s, block_s)

    y_prompt = out_p.reshape(nb, seq, D_MODEL)
    y_sample = out_s.reshape(nbs, 1, D_MODEL)
    conv_p = conv_tail[:, SUBLANES - (CONV_W - 1):][None]
    k_p = k_tail.reshape(1, nb, WINDOW, N_KV_HEADS, ATTN_HEAD_DIM)
    v_p = v_tail.reshape(1, nb, WINDOW, N_KV_HEADS, ATTN_HEAD_DIM)
    conv_s = jnp.concatenate([state_conv[0][:, 1:], real(xbc_s)[:, None]], axis=1)[None]
    k_new = real(k_s).reshape(ts, 1, N_KV_HEADS, ATTN_HEAD_DIM)
    v_new = real(v_s).reshape(ts, 1, N_KV_HEADS, ATTN_HEAD_DIM)
    ks_out = jnp.concatenate([cache_swa_k[0][:, 1:], k_new], axis=1)[None]
    vs_out = jnp.concatenate([cache_swa_v[0][:, 1:], v_new], axis=1)[None]
    return (y_prompt, y_sample, conv_p, ssm_p[None], k_p, v_p, conv_s, ssm_s[None], ks_out, vs_out)
```

```python
import functools

import jax
import jax.numpy as jnp
from jax import lax
from jax.experimental import pallas as pl
from jax.experimental.pallas import tpu as pltpu

F32, BF16, I32 = jnp.float32, jnp.bfloat16, jnp.int32

D_MODEL = 1024
D_INNER = 2 * D_MODEL
SSM_HEAD_DIM = 64
SSM_HEADS = D_INNER // SSM_HEAD_DIM
SSM_GROUPS = 4
SSM_HPG = SSM_HEADS // SSM_GROUPS
D_STATE = 128
CONV_W = 4
CONV_DIM = D_INNER + 2 * SSM_GROUPS * D_STATE
CHUNK = 128
ATTN_HEAD_DIM = 64
N_Q_HEADS = D_MODEL // ATTN_HEAD_DIM
N_KV_HEADS = 4
Q_PER_KV = N_Q_HEADS // N_KV_HEADS
D_ATTN = N_Q_HEADS * ATTN_HEAD_DIM
D_KV = N_KV_HEADS * ATTN_HEAD_DIM
WINDOW = 128
TOP_K = 4
D_FF = D_MODEL
SWIGLU_LIMIT = 7.0
SWIGLU_ALPHA = 1.702
EPS = 1e-5
NEG_BIG = -1e30

LANES = 128
SUBLANES = 8
GROUP_W = D_INNER // SSM_GROUPS
PIECE = 2 * SUBLANES
PIECES_PER_TILE = 32
VMEM_LIMIT = 56 * 1024 * 1024

NT = (((1,), (1,)), ((), ()))
TN = (((0,), (0,)), ((), ()))


def _const_spec(shape):
    return pl.BlockSpec(shape, lambda *_: (0,) * len(shape))


def _resident_spec(shape):
    return pl.BlockSpec(shape, lambda *_: (0,) * len(shape), pipeline_mode=pl.Buffered(1))


def _split3(x):
    hi = x.astype(BF16)
    r1 = x - hi.astype(F32)
    mid = r1.astype(BF16)
    lo = (r1 - mid.astype(F32)).astype(BF16)
    return hi, mid, lo


def _softplus(x):
    return jnp.maximum(x, 0.0) + jnp.log(1.0 + jnp.exp(-jnp.abs(x)))


def _causal_conv_silu(buf_ref, n, cw_ref, cb_ref):
    full = buf_ref[...]
    conv = cb_ref[...]
    for j in range(CONV_W):
        shifted = full if j == CONV_W - 1 else pltpu.roll(full, CONV_W - 1 - j, 0)
        conv = conv + shifted[SUBLANES:SUBLANES + n, :] * cw_ref[j:j + 1, :]
    buf_ref[0:SUBLANES, :] = full[n:n + SUBLANES, :]
    return conv * jax.nn.sigmoid(conv)


def _inproj_kernel(x_ref, nw_ref, wz_ref, wxbc_ref, wdt_ref, wq_ref, wk_ref, wv_ref, wg_ref, *rest,
                   fuse_conv, tiles_per_seq):
    if fuse_conv:
        (hist_ref, cw_ref, cb_ref, wdtT_ref, z_ref, xbc_ref, dt_ref, q_ref, k_ref, v_ref, g_ref, tail_ref, dtT_ref,
         ktail_ref, vtail_ref, buf_ref) = rest
    else:
        z_ref, xbc_ref, dt_ref, q_ref, k_ref, v_ref, g_ref = rest
    tm = x_ref.shape[0]
    if fuse_conv:
        pos = pl.program_id(0) % tiles_per_seq

        @pl.when(pos == 0)
        def _():
            buf_ref[0:SUBLANES, :] = hist_ref[...]

    x = x_ref[...]
    xn = x * lax.rsqrt(jnp.mean(x * x, axis=-1, keepdims=True) + EPS)
    xn = (xn * nw_ref[...]).astype(BF16)
    xbc = jnp.dot(xn, wxbc_ref[...], preferred_element_type=F32)
    if fuse_conv:
        buf_ref[SUBLANES:SUBLANES + tm, :] = xbc
        xbc_ref[...] = _causal_conv_silu(buf_ref, tm, cw_ref, cb_ref).astype(xbc_ref.dtype)
    else:
        xbc_ref[...] = xbc.astype(xbc_ref.dtype)
    for w_ref, o_ref in ((wz_ref, z_ref), (wdt_ref, dt_ref), (wq_ref, q_ref), (wg_ref, g_ref)):
        o_ref[...] = jnp.dot(xn, w_ref[...], preferred_element_type=F32).astype(o_ref.dtype)
    k = jnp.dot(xn, wk_ref[...], preferred_element_type=F32)
    v = jnp.dot(xn, wv_ref[...], preferred_element_type=F32)
    k_ref[...] = k.astype(k_ref.dtype)
    v_ref[...] = v.astype(v_ref.dtype)
    if fuse_conv:
        dtT_ref[...] = lax.dot_general(wdtT_ref[...], xn, NT, preferred_element_type=F32)

        @pl.when(pos == tiles_per_seq - 1)
        def _():
            tail_ref[0] = buf_ref[0:SUBLANES, :]
            ktail_ref[0] = k[tm - WINDOW:, :]
            vtail_ref[0] = v[tm - WINDOW:, :]


def _in_proj(x, norm_w, ws, tm, act_dtype, conv=None):
    t = x.shape[0]
    widths = (D_INNER, CONV_DIM, SSM_HEADS, D_ATTN, D_KV, D_KV, 2 * D_MODEL)
    dtypes = (act_dtype, act_dtype, F32, act_dtype, act_dtype, act_dtype, act_dtype)
    row = lambda n: pl.BlockSpec((tm, n), lambda i: (i, 0))
    in_specs = [row(D_MODEL), _const_spec((1, D_MODEL))] + [_resident_spec((D_MODEL, n)) for n in widths]
    out_specs = [row(n) for n in widths]
    out_shape = [jax.ShapeDtypeStruct((t, n), d) for n, d in zip(widths, dtypes)]
    args, scratch, tiles_per_seq = [x, norm_w, *ws], [], 1
    if conv is not None:
        hist, cw, cb, tiles_per_seq = conv
        n_seq = t // (tm * tiles_per_seq)
        in_specs += [pl.BlockSpec((SUBLANES, CONV_DIM), lambda i: (i // tiles_per_seq, 0)),
                     _const_spec(cw.shape), _const_spec(cb.shape), _const_spec((SSM_HEADS, D_MODEL))]
        per_seq = lambda r, c: pl.BlockSpec((1, r, c), lambda i: (i // tiles_per_seq, 0, 0))
        out_specs += [per_seq(SUBLANES, CONV_DIM), pl.BlockSpec((SSM_HEADS, tm), lambda i: (0, i)),
                      per_seq(WINDOW, D_KV), per_seq(WINDOW, D_KV)]
        out_shape += [jax.ShapeDtypeStruct((n_seq, SUBLANES, CONV_DIM), F32),
                      jax.ShapeDtypeStruct((SSM_HEADS, t), F32),
                      jax.ShapeDtypeStruct((n_seq, WINDOW, D_KV), F32),
                      jax.ShapeDtypeStruct((n_seq, WINDOW, D_KV), F32)]
        args += [hist, cw, cb, ws[2].T]
        scratch = [pltpu.VMEM((SUBLANES + tm, CONV_DIM), F32)]
    return pl.pallas_call(
        functools.partial(_inproj_kernel, fuse_conv=conv is not None, tiles_per_seq=tiles_per_seq),
        grid=(t // tm,),
        in_specs=in_specs,
        out_specs=out_specs,
        out_shape=out_shape,
        scratch_shapes=scratch,
        compiler_params=pltpu.CompilerParams(dimension_semantics=("arbitrary",), vmem_limit_bytes=VMEM_LIMIT),
        name="in_proj",
    )(*args)


def _ssd_kernel(xbc_ref, z_ref, dt_ref, dtT_ref, h0_ref, dtb_ref, dtbT_ref, a_ref, aT_ref, dsk_ref, nw_ref, e_ref,
                *rest, lc, n_valid, conv):
    if conv:
        hist_ref, cw_ref, cb_ref, y_ref, h_ref, buf_ref = rest
    else:
        y_ref, h_ref = rest
    c = pl.program_id(1)

    @pl.when(c == 0)
    def _():
        h_ref[...] = h0_ref[...]

    if conv:
        @pl.when(c == 0)
        def _():
            buf_ref[0:SUBLANES, :] = hist_ref[...]

        buf_ref[SUBLANES:SUBLANES + lc, :] = xbc_ref[...].astype(F32)
        act = _causal_conv_silu(buf_ref, lc, cw_ref, cb_ref)
    else:
        act = xbc_ref[...].astype(F32)
    xs = act[:, :D_INNER]
    bm = act[:, D_INNER:D_INNER + SSM_GROUPS * D_STATE].astype(BF16)
    cm = act[:, D_INNER + SSM_GROUPS * D_STATE:].astype(BF16)

    dt = _softplus(dt_ref[...] + dtb_ref[...])
    dtT_raw = dtT_ref[0] if len(dtT_ref.shape) == 3 else dtT_ref[...]
    dtT = _softplus(dtT_raw + dtbT_ref[...])
    if n_valid < lc:
        dt = jnp.where(lax.broadcasted_iota(I32, dt.shape, 0) < n_valid, dt, 0.0)
        dtT = jnp.where(lax.broadcasted_iota(I32, dtT.shape, 1) < n_valid, dtT, 0.0)
    la = dt * a_ref[...]
    laT = dtT * aT_ref[...]
    li = lax.broadcasted_iota(I32, (lc, lc), 0)
    si = lax.broadcasted_iota(I32, (lc, lc), 1)
    causal = li >= si
    tril = jnp.where(causal, 1.0, 0.0).astype(BF16)
    triu = jnp.where(li <= si, 1.0, 0.0).astype(BF16)
    cum = sum(jnp.dot(tril, p, preferred_element_type=F32) for p in _split3(la))
    cumT = sum(jnp.dot(p, triu, preferred_element_type=F32) for p in _split3(laT))
    ec = jnp.exp(cum)
    dte = jnp.exp(cum[lc - 1:lc, :] - cum)
    cd = jnp.exp(cumT[:, lc - 1:lc])

    def expand(v):
        hi, mid, _ = _split3(v)
        return jnp.dot(jnp.concatenate([hi, mid], axis=1), e_ref[...], preferred_element_type=F32)

    dt_x, ec_x, dte_x = expand(dt), expand(ec), expand(dte)
    xdt = xs * dt_x
    xdt_b = xdt.astype(BF16)
    xdte_b = (xdt * dte_x).astype(BF16)
    lane = lax.broadcasted_iota(I32, (lc, LANES), 1)
    low_half = lane < SSM_HEAD_DIM

    for g in range(SSM_GROUPS):
        gs = slice(g * GROUP_W, (g + 1) * GROUP_W)
        bm_g = bm[:, g * D_STATE:(g + 1) * D_STATE]
        cm_g = cm[:, g * D_STATE:(g + 1) * D_STATE]
        cb = lax.dot_general(cm_g, bm_g, NT, preferred_element_type=F32)
        cbm = jnp.where(causal, cb, 0.0)
        h_g = h_ref[0, g * SSM_HPG:(g + 1) * SSM_HPG].reshape(GROUP_W, D_STATE)
        y_off = lax.dot_general(cm_g, h_g.astype(BF16), NT, preferred_element_type=F32) * ec_x[:, gs]
        tiles = []
        for j in range(GROUP_W // LANES):
            col = g * GROUP_W + j * LANES
            x_pair = xdt_b[:, col:col + LANES]
            acc = None
            for half in range(2):
                h = col // SSM_HEAD_DIM + half
                seg = cum[:, h:h + 1] - cumT[h:h + 1, :]
                m = (cbm * jnp.exp(jnp.where(causal, seg, 0.0))).astype(BF16)
                x_h = jnp.where(low_half if half == 0 else jnp.logical_not(low_half), x_pair, jnp.zeros_like(x_pair))
                d = jnp.dot(m, x_h, preferred_element_type=F32)
                acc = d if acc is None else acc + d
            tiles.append(acc)
        y_g = jnp.concatenate(tiles, axis=1) + y_off + xs[:, gs] * dsk_ref[:, gs]
        zg = z_ref[:, gs].astype(F32)
        y_g = y_g * (zg * jax.nn.sigmoid(zg))
        y_g = y_g * lax.rsqrt(jnp.mean(y_g * y_g, axis=-1, keepdims=True) + EPS)
        y_ref[:, gs] = (y_g * nw_ref[:, gs]).astype(y_ref.dtype)
        st = lax.dot_general(xdte_b[:, gs], bm_g, TN, preferred_element_type=F32)
        for hh in range(SSM_HPG):
            h = g * SSM_HPG + hh
            rows = slice(hh * SSM_HEAD_DIM, (hh + 1) * SSM_HEAD_DIM)
            h_ref[0, h] = h_g[rows, :] * cd[h:h + 1, 0:1] + st[rows, :]


def _ssd(xbc, z, dt, dtT, h0, consts, nb, nc, lc, n_valid, conv=None):
    t = xbc.shape[0]
    row = lambda n: pl.BlockSpec((lc, n), lambda b, c: (b * nc + c, 0))
    if dtT.ndim == 2:
        dtT_spec = pl.BlockSpec((SSM_HEADS, lc), lambda b, c: (0, b * nc + c))
    else:
        dtT_spec = pl.BlockSpec((1, SSM_HEADS, lc), lambda b, c: (b * nc + c, 0, 0))
    in_specs = [
        row(CONV_DIM), row(D_INNER), row(SSM_HEADS), dtT_spec,
        pl.BlockSpec((1, SSM_HEADS, SSM_HEAD_DIM, D_STATE), lambda b, c: (b, 0, 0, 0)),
    ] + [_const_spec(a.shape) for a in consts]
    args, scratch = [xbc, z, dt, dtT, h0, *consts], []
    if conv is not None:
        hist, cw, cb = conv
        in_specs += [pl.BlockSpec((SUBLANES, CONV_DIM), lambda b, c: (b, 0)), _const_spec(cw.shape),
                     _const_spec(cb.shape)]
        args += [hist, cw, cb]
        scratch = [pltpu.VMEM((SUBLANES + lc, CONV_DIM), F32)]
    return pl.pallas_call(
        functools.partial(_ssd_kernel, lc=lc, n_valid=n_valid, conv=conv is not None),
        grid=(nb, nc),
        in_specs=in_specs,
        out_specs=[row(D_INNER), pl.BlockSpec((1, SSM_HEADS, SSM_HEAD_DIM, D_STATE), lambda b, c: (b, 0, 0, 0))],
        out_shape=[jax.ShapeDtypeStruct((t, D_INNER), BF16),
                   jax.ShapeDtypeStruct((nb, SSM_HEADS, SSM_HEAD_DIM, D_STATE), F32)],
        scratch_shapes=scratch,
        compiler_params=pltpu.CompilerParams(dimension_semantics=("arbitrary", "arbitrary"),
                                             vmem_limit_bytes=VMEM_LIMIT),
        name="ssd",
    )(*args)


def _swa_kernel(slope_ref, sink_ref, q_ref, kc_ref, vc_ref, kp_ref, vp_ref, y_ref, bias_ref, *, tq, prev_always):
    nk = WINDOW + tq

    @pl.when((pl.program_id(0) == 0) & (pl.program_id(1) == 0))
    def _():
        j = lax.broadcasted_iota(I32, (nk, tq), 0)
        r = lax.broadcasted_iota(I32, (nk, tq), 1)
        dist = r + WINDOW - j
        valid = (dist >= 0) & (dist < WINDOW)
        distf = dist.astype(F32)
        for h in range(N_Q_HEADS):
            penalty = -slope_ref[h] * distf
            bias_ref[h] = jnp.where(valid, penalty, NEG_BIG)
            if not prev_always:
                bias_ref[N_Q_HEADS + h] = jnp.where(valid & (j >= WINDOW), penalty, NEG_BIG)

    first = 0 if prev_always else jnp.where(pl.program_id(1) == 0, N_Q_HEADS, 0)
    lane = lax.broadcasted_iota(I32, (nk, LANES), 1)
    transposed_out = tq == WINDOW
    zeros_half = jnp.zeros((ATTN_HEAD_DIM, nk), BF16)
    sink_row = lax.broadcasted_iota(I32, (SUBLANES, tq), 0) == 0
    ones_keys = jnp.ones((nk + SUBLANES, LANES), BF16)
    heads = []
    for t in range(D_KV // LANES):
        cols = slice(t * LANES, (t + 1) * LANES)
        kt = jnp.concatenate([kp_ref[:, cols].astype(F32), kc_ref[:, cols].astype(F32)], axis=0)
        vt = jnp.concatenate([vp_ref[:, cols].astype(F32), vc_ref[:, cols].astype(F32)], axis=0)
        if transposed_out:
            vt_t = vt.T.astype(BF16)
        for b in range(2):
            mine = (lane >= ATTN_HEAD_DIM) if b else (lane < ATTN_HEAD_DIM)
            k_same = jnp.where(mine, kt, 0.0)
            k_half = {b: k_same.astype(BF16), 1 - b: pltpu.roll(k_same, ATTN_HEAD_DIM, 1).astype(BF16)}
            if transposed_out:
                v_g = vt_t[b * ATTN_HEAD_DIM:(b + 1) * ATTN_HEAD_DIM, :]
                v_half = {0: jnp.concatenate([v_g, zeros_half], axis=0),
                          1: jnp.concatenate([zeros_half, v_g], axis=0)}
            else:
                v_same = jnp.concatenate([jnp.where(mine, vt, 0.0), jnp.zeros((SUBLANES, LANES), F32)], axis=0)
                v_half = {b: v_same.astype(BF16), 1 - b: pltpu.roll(v_same, ATTN_HEAD_DIM, 1).astype(BF16)}
            for qi in range(Q_PER_KV):
                a = qi % 2
                heads.append((k_half[a], v_half[a]))

    q_tiles = [(q_ref[:, jq * LANES:(jq + 1) * LANES] * (ATTN_HEAD_DIM ** -0.5)).astype(BF16)
               for jq in range(D_ATTN // LANES)]

    def scores(h):
        return lax.dot_general(heads[h][0], q_tiles[h // 2], NT, preferred_element_type=F32) + bias_ref[first + h]

    def attend(h, s):
        sink = sink_ref[h]
        m = jnp.maximum(jnp.max(s, axis=0, keepdims=True), sink)
        e = jnp.exp(s - m)
        e_sink = jnp.exp(sink - m)
        if transposed_out:
            rden = 1.0 / (jnp.sum(e, axis=0, keepdims=True) + e_sink)
            return jnp.dot(heads[h][1], e.astype(BF16), preferred_element_type=F32) * rden
        p = jnp.concatenate([e, jnp.where(sink_row, e_sink, 0.0)], axis=0).astype(BF16)
        num = lax.dot_general(p, heads[h][1], TN, preferred_element_type=F32)
        den = lax.dot_general(p, ones_keys, TN, preferred_element_type=F32)
        return num / den

    out_tiles = [None] * (D_ATTN // LANES)
    s_next = scores(0)
    for h in range(N_Q_HEADS):
        s_cur = s_next
        if h + 1 < N_Q_HEADS:
            s_next = scores(h + 1)
        o = attend(h, s_cur)
        out_tiles[h // 2] = o if out_tiles[h // 2] is None else out_tiles[h // 2] + o
    for jq, o in enumerate(out_tiles):
        y_ref[:, jq * LANES:(jq + 1) * LANES] = (o.T if transposed_out else o).astype(y_ref.dtype)


def _swa(slopes, sinks, q, k, v, k_prev, v_prev, nb, nblk, tq, prev_always):
    t = q.shape[0]
    cur = lambda n: pl.BlockSpec((tq, n), lambda b, i: (b * nblk + i, 0))
    if prev_always:
        prev = pl.BlockSpec((WINDOW, D_KV), lambda b, i: (b, 0))
    else:
        prev = pl.BlockSpec((WINDOW, D_KV), lambda b, i: (b * nblk + jnp.maximum(i - 1, 0), 0))
    smem = pl.BlockSpec(memory_space=pltpu.SMEM)
    return pl.pallas_call(
        functools.partial(_swa_kernel, tq=tq, prev_always=prev_always),
        grid=(nb, nblk),
        in_specs=[smem, smem, cur(D_ATTN), cur(D_KV), cur(D_KV), prev, prev],
        out_specs=cur(D_ATTN),
        out_shape=jax.ShapeDtypeStruct((t, D_ATTN), BF16),
        scratch_shapes=[pltpu.VMEM(((1 if prev_always else 2) * N_Q_HEADS, WINDOW + tq, tq), F32)],
        compiler_params=pltpu.CompilerParams(dimension_semantics=("arbitrary", "arbitrary"),
                                             vmem_limit_bytes=VMEM_LIMIT),
        name="swa",
    )(slopes, sinks, q, k, v, k_prev, v_prev)


def _post_kernel(x_ref, ys_ref, ya_ref, g_ref, wsp_ref, wap_ref, wo_ref, nw_ref, wr_ref, br_ref, *rest,
                 n_exp, lr, nt, aliased):
    if aliased:
        rest = rest[1:]
    h_ref, xs_ref, lpos_ref, pr_ref, pc_ref = rest
    tp = x_ref.shape[0]

    @pl.when(pl.program_id(0) >= nt)
    def _():
        xs_ref[...] = jnp.zeros_like(xs_ref)

    @pl.when(pl.program_id(0) < nt)
    def _():
        a = jnp.dot(ys_ref[...].astype(BF16), wsp_ref[...], preferred_element_type=F32)
        b = jnp.dot(ya_ref[...].astype(BF16), wap_ref[...], preferred_element_type=F32)
        g = g_ref[...].astype(F32)
        merged = jax.nn.sigmoid(g[:, :D_MODEL]) * a + jax.nn.sigmoid(g[:, D_MODEL:]) * b
        h = x_ref[...] + jnp.dot(merged.astype(BF16), wo_ref[...], preferred_element_type=F32)
        h_ref[...] = h
        hn = h * lax.rsqrt(jnp.mean(h * h, axis=-1, keepdims=True) + EPS) * nw_ref[...]
        w_hi, w_mid, _ = _split3(wr_ref[...])
        x_hi, x_mid, _ = _split3(hn)
        logits = (lax.dot_general(w_hi, x_hi, NT, preferred_element_type=F32)
                  + lax.dot_general(w_hi, x_mid, NT, preferred_element_type=F32)
                  + lax.dot_general(w_mid, x_hi, NT, preferred_element_type=F32)) + br_ref[...]
        eidx = lax.broadcasted_iota(I32, logits.shape, 0).astype(F32)
        work = logits
        vals, ids = [], []
        for _ in range(TOP_K):
            m = jnp.max(work, axis=0, keepdims=True)
            first = jnp.min(jnp.where(work == m, eidx, float(n_exp)), axis=0, keepdims=True)
            vals.append(m)
            ids.append(first)
            work = jnp.where(eidx == first, -jnp.inf, work)
        es = [jnp.exp(v - vals[0]) for v in vals]
        den = es[0] + es[1] + es[2] + es[3]
        eye = jnp.where(lax.broadcasted_iota(I32, (TOP_K, TOP_K), 0) == lax.broadcasted_iota(I32, (TOP_K, TOP_K), 1),
                        1.0, 0.0).astype(BF16)

        def to_columns(rows):
            return sum(lax.dot_general(p, eye, TN, preferred_element_type=F32) for p in _split3(rows))

        pr_ref[...] = to_columns(jnp.concatenate([e / den for e in es], axis=0))

        onehot = [jnp.where(eidx == i, 1.0, 0.0) for i in ids]
        counts = [jnp.sum(o, axis=1, keepdims=True) for o in onehot]
        total = counts[0] + counts[1] + counts[2] + counts[3]
        padded = jnp.floor((total + (PIECE - 1)) * (1.0 / PIECE)) * PIECE
        ei = lax.broadcasted_iota(I32, (n_exp, n_exp), 0)
        ej = lax.broadcasted_iota(I32, (n_exp, n_exp), 1)
        below = jnp.where(ej < ei, 1.0, 0.0).astype(BF16)
        padded_b = jnp.broadcast_to(padded, (n_exp, LANES))
        seg_off = sum(jnp.dot(below, p, preferred_element_type=F32) for p in _split3(padded_b))[:, 0:1]
        ti = lax.broadcasted_iota(I32, (tp, tp), 0)
        tj = lax.broadcasted_iota(I32, (tp, tp), 1)
        before = jnp.where(ti < tj, 1.0, 0.0).astype(BF16)
        base = seg_off
        lpos = []
        for k in range(TOP_K):
            prefix = jnp.dot(onehot[k].astype(BF16), before, preferred_element_type=F32)
            lpos.append(jnp.sum(onehot[k] * (base + prefix), axis=0, keepdims=True))
            base = base + counts[k]
        lpos_ref[...] = to_columns(jnp.concatenate(lpos, axis=0)).astype(I32)
        pc_ref[0] = padded_b.astype(I32)

        hn_b = hn.astype(BF16)
        rc = lr // 4
        for c in range(4):
            ri = (lax.broadcasted_iota(I32, (rc, tp), 0) + c * rc).astype(F32)
            sel = None
            for k in range(TOP_K):
                hit = jnp.where(ri == lpos[k], 1.0, 0.0)
                sel = hit if sel is None else sel + hit
            xs_ref[c * rc:(c + 1) * rc, :] = jnp.dot(sel.astype(BF16), hn_b, preferred_element_type=F32).astype(BF16)


def _post(x, y_ssm, y_attn, gates, wsp, wap, wo, ffn_nw, w_rT, b_r, tm, lr, xs_rows, xs_block0, pad_steps,
          xs_prev=None):
    t = x.shape[0]
    n_exp = w_rT.shape[0]
    nt = t // tm
    last = nt - 1
    row = lambda n: pl.BlockSpec((tm, n), lambda i: (jnp.minimum(i, last), 0))
    col = row(TOP_K)
    in_specs = [row(D_MODEL), row(D_INNER), row(D_ATTN), row(2 * D_MODEL),
                _resident_spec(wsp.shape), _resident_spec(wap.shape), _resident_spec(wo.shape),
                _const_spec(ffn_nw.shape), _const_spec(w_rT.shape), _const_spec(b_r.shape)]
    args = [x, y_ssm, y_attn, gates, wsp, wap, wo, ffn_nw, w_rT, b_r]
    aliases = {}
    if xs_prev is not None:
        in_specs.append(pl.BlockSpec(memory_space=pl.ANY))
        args.append(xs_prev)
        aliases = {len(args) - 1: 1}
    return pl.pallas_call(
        functools.partial(_post_kernel, n_exp=n_exp, lr=lr, nt=nt, aliased=xs_prev is not None),
        grid=(nt + pad_steps,),
        in_specs=in_specs,
        out_specs=[row(D_MODEL), pl.BlockSpec((lr, D_MODEL), lambda i: (xs_block0 + i, 0)), col, col,
                   pl.BlockSpec((1, n_exp, LANES), lambda i: (jnp.minimum(i, last), 0, 0))],
        out_shape=[jax.ShapeDtypeStruct((t, D_MODEL), F32), jax.ShapeDtypeStruct((xs_rows, D_MODEL), BF16),
                   jax.ShapeDtypeStruct((t, TOP_K), I32), jax.ShapeDtypeStruct((t, TOP_K), F32),
                   jax.ShapeDtypeStruct((nt, n_exp, LANES), I32)],
        input_output_aliases=aliases,
        compiler_params=pltpu.CompilerParams(dimension_semantics=("arbitrary",), vmem_limit_bytes=VMEM_LIMIT),
        name="post",
    )(*args)


def _piece(ref, p):
    return ref.at[pl.ds(pl.multiple_of(p * PIECE, PIECE), PIECE)]


def _moe_kernel(te_ref, nu_ref, src0_ref, src1_ref, srcc_ref, xs_hbm, wg_ref, wu_ref, wd_ref,
                bg_ref, bu_ref, bd_ref, ys_hbm, xbuf, obuf, wbf, gsem, ssem):
    i = pl.program_id(0)
    n_used = nu_ref[0]
    slot = i % 2
    tme = PIECES_PER_TILE * PIECE

    def gather(src_ref, s):
        for r in range(PIECES_PER_TILE):
            pltpu.make_async_copy(_piece(xs_hbm, src_ref[0, 0, r]), xbuf.at[s, pl.ds(r * PIECE, PIECE)],
                                  gsem.at[s]).start()

    def wait_gather(s):
        pltpu.make_async_copy(xs_hbm.at[pl.ds(0, tme)], xbuf.at[s], gsem.at[s]).wait()

    def wait_put(s):
        pltpu.make_async_copy(obuf.at[s], ys_hbm.at[pl.ds(0, tme)], ssem.at[s]).wait()

    @pl.when(i == 0)
    def _():
        gather(src0_ref, 0)

    @pl.when(i < n_used)
    def _():
        gather(src1_ref, 1 - slot)
        wait_gather(slot)

        @pl.when(i >= 2)
        def _():
            wait_put(slot)

        @pl.when((i == 0) | (te_ref[i] != te_ref[jnp.maximum(i - 1, 0)]))
        def _():
            wbf[0] = wg_ref[0].astype(BF16)
            wbf[1] = wu_ref[0].astype(BF16)
            wbf[2] = wd_ref[0].astype(BF16)

        x = xbuf[slot]
        g = jnp.minimum(jnp.dot(x, wbf[0], preferred_element_type=F32) + bg_ref[0], SWIGLU_LIMIT)
        u = jnp.clip(jnp.dot(x, wbf[1], preferred_element_type=F32) + bu_ref[0], -SWIGLU_LIMIT, SWIGLU_LIMIT)
        act = ((u + 1.0) * g * jax.nn.sigmoid(SWIGLU_ALPHA * g)).astype(BF16)
        obuf[slot] = (jnp.dot(act, wbf[2], preferred_element_type=F32) + bd_ref[0]).astype(BF16)
        for r in range(PIECES_PER_TILE):
            pltpu.make_async_copy(obuf.at[slot, pl.ds(r * PIECE, PIECE)], _piece(ys_hbm, srcc_ref[0, 0, r]),
                                  ssem.at[slot]).start()

        @pl.when(i == n_used - 1)
        def _():
            wait_put(slot)
            wait_gather(1 - slot)

            @pl.when(i >= 1)
            def _():
                wait_put(1 - slot)


def _moe(tile_expert, n_used, src, xs, wg, wu, wd, bg, bu, bd):
    n_tiles = tile_expert.shape[0]
    idx = src.reshape(n_tiles, 1, PIECES_PER_TILE)
    tme = PIECES_PER_TILE * PIECE
    wspec = pl.BlockSpec((1, D_MODEL, D_FF), lambda i, te, nu: (te[i], 0, 0))
    bspec = pl.BlockSpec((1, 1, D_FF), lambda i, te, nu: (te[i], 0, 0))
    ispec = lambda f: pl.BlockSpec((1, 1, PIECES_PER_TILE), lambda i, te, nu: (f(i), 0, 0),
                                   memory_space=pltpu.SMEM)
    grid_spec = pltpu.PrefetchScalarGridSpec(
        num_scalar_prefetch=2,
        grid=(n_tiles,),
        in_specs=[ispec(lambda i: 0), ispec(lambda i: jnp.minimum(i + 1, n_tiles - 1)), ispec(lambda i: i),
                  pl.BlockSpec(memory_space=pl.ANY), wspec, wspec, wspec, bspec, bspec, bspec],
        out_specs=pl.BlockSpec(memory_space=pl.ANY),
        scratch_shapes=[pltpu.VMEM((2, tme, D_MODEL), BF16), pltpu.VMEM((2, tme, D_MODEL), BF16),
                        pltpu.VMEM((3, D_MODEL, D_FF), BF16),
                        pltpu.SemaphoreType.DMA((2,)), pltpu.SemaphoreType.DMA((2,))],
    )
    return pl.pallas_call(
        _moe_kernel,
        grid_spec=grid_spec,
        out_shape=jax.ShapeDtypeStruct(xs.shape, xs.dtype),
        input_output_aliases={5: 0},
        compiler_params=pltpu.CompilerParams(dimension_semantics=("arbitrary",), vmem_limit_bytes=VMEM_LIMIT),
        name="moe",
    )(tile_expert, n_used, idx, idx, idx, xs, wg, wu, wd, bg, bu, bd)


def _combine_kernel(h_ref, ys_ref, lpos_ref, pr_ref, nw_ref, o_ref):
    tp = h_ref.shape[0]
    lr = ys_ref.shape[0]
    ri = lax.broadcasted_iota(I32, (tp, lr), 1)
    lp = lpos_ref[...]
    pr = pr_ref[...]
    pw = jnp.zeros((tp, lr), F32)
    for k in range(TOP_K):
        pw = jnp.where(ri == lp[:, k:k + 1], pr[:, k:k + 1], pw)
    moe = jnp.dot(pw.astype(BF16), ys_ref[...], preferred_element_type=F32)
    h = h_ref[...] + moe
    o_ref[...] = h * lax.rsqrt(jnp.mean(h * h, axis=-1, keepdims=True) + EPS) * nw_ref[...]


def _combine(h, ys, lpos_t, probs_t, final_nw, tm, lr, ys_block0):
    t = h.shape[0]
    return pl.pallas_call(
        _combine_kernel,
        grid=(t // tm,),
        in_specs=[pl.BlockSpec((tm, D_MODEL), lambda i: (i, 0)),
                  pl.BlockSpec((lr, D_MODEL), lambda i: (ys_block0 + i, 0)),
                  pl.BlockSpec((tm, TOP_K), lambda i: (i, 0)),
                  pl.BlockSpec((tm, TOP_K), lambda i: (i, 0)),
                  _const_spec(final_nw.shape)],
        out_specs=pl.BlockSpec((tm, D_MODEL), lambda i: (i, 0)),
        out_shape=jax.ShapeDtypeStruct((t, D_MODEL), F32),
        compiler_params=pltpu.CompilerParams(dimension_semantics=("arbitrary",), vmem_limit_bytes=VMEM_LIMIT),
        name="combine",
    )(h, ys, lpos_t, probs_t, final_nw)


def _piece_table(padded_counts, tile_row0, n_tiles, spare_piece0):
    n_pieces = (padded_counts // PIECE).T
    seg_row = tile_row0[:, None] + jnp.cumsum(padded_counts, axis=1) - padded_counts
    seg_piece = (seg_row // PIECE).T
    per_expert = n_pieces.sum(axis=1)
    tiles_per = (per_expert + PIECES_PER_TILE - 1) // PIECES_PER_TILE
    tile_end = jnp.cumsum(tiles_per)
    n_used = tile_end[-1]
    slot0 = (tile_end - tiles_per) * PIECES_PER_TILE
    seg_slot = (slot0[:, None] + jnp.cumsum(n_pieces, axis=1) - n_pieces).reshape(-1)
    seg_n = n_pieces.reshape(-1)
    seg_src = seg_piece.reshape(-1)
    slots = jnp.arange(n_tiles * PIECES_PER_TILE, dtype=I32)

    def at_segment_of_slot(f):
        df = f - jnp.concatenate([jnp.zeros((1,), I32), f[:-1]])
        return jnp.sum(jnp.where(seg_slot[None, :] <= slots[:, None], df[None, :], 0), axis=1)

    real = slots < at_segment_of_slot(seg_slot + seg_n)
    padding = jnp.logical_not(real) & (slots < n_used * PIECES_PER_TILE)
    spare = spare_piece0 + jnp.where(padding, jnp.cumsum(padding.astype(I32)), 0)
    src = jnp.where(real, slots + at_segment_of_slot(seg_src - seg_slot), spare).astype(I32)
    tile_ids = jnp.arange(n_tiles, dtype=I32)
    tile_expert = jnp.sum((tile_end[None, :] <= jnp.minimum(tile_ids, n_used - 1)[:, None]).astype(I32), axis=1)
    return tile_expert.astype(I32), n_used.reshape(1).astype(I32), src


def _pick_tile(n, pref):
    while n % pref:
        pref //= 2
    return pref


def _local_rows(tm, n_exp):
    need = TOP_K * tm + n_exp * (PIECE - 1)
    return -(-need // 64) * 64


def kernel(x_prompt, x_sample, state_conv, state_ssm, cache_swa_k, cache_swa_v, attn_norm_w, w_in, conv_w, conv_b, dt_bias, a_log, d_skip, ssm_norm_w, attn_sinks, w_ssm_proj, w_attn_proj, w_o, ffn_norm_w, w_router, b_router, w_gate, b_gate, w_up, b_up, w_down, b_down, final_norm_w):
    assert w_in.shape[0] == 1, "single-layer step"
    nb, seq, _ = x_prompt.shape
    nbs = x_sample.shape[0]
    n_exp = w_router.shape[-1]
    tp, ts = nb * seq, nbs
    pad = SUBLANES

    cuts = [0]
    for n in (D_INNER, CONV_DIM, SSM_HEADS, D_ATTN, D_KV, D_KV, 2 * D_MODEL):
        cuts.append(cuts[-1] + n)
    w_in_b = w_in[0].astype(BF16)
    ws = [w_in_b[:, cuts[i]:cuts[i + 1]] for i in range(7)]
    attn_nw = attn_norm_w[0].reshape(1, D_MODEL)
    a_neg = -jnp.exp(a_log[0].astype(F32))
    head_of = jnp.arange(D_INNER, dtype=I32) // SSM_HEAD_DIM
    expand = jnp.tile((jnp.arange(SSM_HEADS, dtype=I32)[:, None] == head_of[None, :]).astype(BF16), (2, 1))
    conv_consts = (conv_w[0], conv_b[0].reshape(1, CONV_DIM))
    ssd_consts = (dt_bias[0].reshape(1, SSM_HEADS), dt_bias[0].reshape(SSM_HEADS, 1),
                  a_neg.reshape(1, SSM_HEADS), a_neg.reshape(SSM_HEADS, 1),
                  d_skip[0][head_of].reshape(1, D_INNER), ssm_norm_w[0].reshape(1, D_INNER), expand)
    slopes = jnp.exp2(-8.0 * jnp.arange(1, N_Q_HEADS + 1, dtype=F32) / N_Q_HEADS)
    sinks = attn_sinks[0].astype(F32)
    wsp, wap, wo = w_ssm_proj[0].astype(BF16), w_attn_proj[0].astype(BF16), w_o[0].astype(BF16)
    ffn_nw = ffn_norm_w[0].reshape(1, D_MODEL)
    w_rT = w_router[0].T
    b_r = b_router[0].reshape(n_exp, 1)
    bg, bu, bd = (b[0].reshape(n_exp, 1, -1) for b in (b_gate, b_up, b_down))
    final_nw = final_norm_w.reshape(1, D_MODEL)

    xp = x_prompt.reshape(tp, D_MODEL)
    tm_in = _pick_tile(seq, 512)
    z, xbc, dt, q, k, v, gates, conv_tail, dtT, k_tail, v_tail = _in_proj(
        xp, attn_nw, ws, tm_in, BF16,
        conv=(jnp.zeros((nb * SUBLANES, CONV_DIM), F32), *conv_consts, seq // tm_in))
    nc = seq // CHUNK
    y_ssm, ssm_p = _ssd(xbc, z, dt, dtT, jnp.zeros((nb, SSM_HEADS, SSM_HEAD_DIM, D_STATE), F32), ssd_consts,
                        nb, nc, CHUNK, CHUNK)
    nblk = seq // WINDOW
    y_attn = _swa(slopes, sinks, q, k, v, k, v, nb, nblk, WINDOW, False)

    xs_pad = jnp.pad(x_sample.reshape(ts, 1, D_MODEL), ((0, 0), (0, pad - 1), (0, 0))).reshape(ts * pad, D_MODEL)
    z_s, xbc_s, dt_s, q_s, k_s, v_s, gates_s = _in_proj(xs_pad, attn_nw, ws, _pick_tile(ts * pad, 256), F32)
    dtT_s = dt_s.reshape(ts, pad, SSM_HEADS).transpose(0, 2, 1)
    hist_s = jnp.pad(state_conv[0], ((0, 0), (pad - (CONV_W - 1), 0), (0, 0))).reshape(ts * pad, CONV_DIM)
    y_ssm_s, ssm_s = _ssd(xbc_s, z_s, dt_s, dtT_s, state_ssm[0], ssd_consts, ts, 1, pad, 1,
                          conv=(hist_s, *conv_consts))
    kc = cache_swa_k[0].reshape(ts * WINDOW, D_KV)
    vc = cache_swa_v[0].reshape(ts * WINDOW, D_KV)
    y_attn_s = _swa(slopes, sinks, q_s, k_s, v_s, kc, vc, ts, 1, pad, True)
    real = lambda a: a.reshape(ts, pad, -1)[:, 0]

    tm_p, tm_s = _pick_tile(tp, 512), ts
    nt_p = tp // tm_p
    lr_p, lr_s = _local_rows(tm_p, n_exp), _local_rows(tm_s, n_exp)
    if (nt_p * lr_p) % lr_s or lr_s > lr_p:
        lr_s = lr_p
    spare_rows = lr_s + (1 + n_exp * (PIECES_PER_TILE - 1)) * PIECE
    pad_steps = -(-spare_rows // lr_p)
    xs_rows = (nt_p + pad_steps) * lr_p
    block_s = nt_p * lr_p // lr_s
    post_w = (wsp, wap, wo, ffn_nw, w_rT, b_r)
    h_p, xs, lpos_p, pr_p, pc_p = _post(xp, y_ssm, y_attn, gates, *post_w, tm_p, lr_p, xs_rows, 0, pad_steps)
    h_s, xs, lpos_s, pr_s, pc_s = _post(x_sample.reshape(ts, D_MODEL), real(y_ssm_s), real(y_attn_s), real(gates_s),
                                        *post_w, tm_s, lr_s, xs_rows, block_s, 0, xs_prev=xs)

    padded_counts = jnp.concatenate([pc_p[:, :, 0], pc_s[:, :, 0]], axis=0)
    tile_row0 = jnp.concatenate([jnp.arange(nt_p, dtype=I32) * lr_p, jnp.full((1,), nt_p * lr_p, I32)])
    max_pieces = (TOP_K * (tp + ts) + (PIECE - 1) * n_exp * (nt_p + 1)) // PIECE + n_exp * (PIECES_PER_TILE - 1)
    n_tiles = -(-max_pieces // PIECES_PER_TILE) + 1
    tile_expert, n_used, src = _piece_table(padded_counts, tile_row0, n_tiles, (nt_p * lr_p + lr_s) // PIECE)
    ys = _moe(tile_expert, n_used, src, xs, w_gate[0], w_up[0], w_down[0], bg, bu, bd)
    out_p = _combine(h_p, ys, lpos_p, pr_p, final_nw, tm_p, lr_p, 0)
    out_s = _combine(h_s, ys, lpos_s, pr_s, final_nw, tm_s, lr_s, block_s)

    y_prompt = out_p.reshape(nb, seq, D_MODEL)
    y_sample = out_s.reshape(nbs, 1, D_MODEL)
    conv_p = conv_tail[:, SUBLANES - (CONV_W - 1):][None]
    k_p = k_tail.reshape(1, nb, WINDOW, N_KV_HEADS, ATTN_HEAD_DIM)
    v_p = v_tail.reshape(1, nb, WINDOW, N_KV_HEADS, ATTN_HEAD_DIM)
    conv_s = jnp.concatenate([state_conv[0][:, 1:], real(xbc_s)[:, None]], axis=1)[None]
    k_new = real(k_s).reshape(ts, 1, N_KV_HEADS, ATTN_HEAD_DIM)
    v_new = real(v_s).reshape(ts, 1, N_KV_HEADS, ATTN_HEAD_DIM)
    ks_out = jnp.concatenate([cache_swa_k[0][:, 1:], k_new], axis=1)[None]
    vs_out = jnp.concatenate([cache_swa_v[0][:, 1:], v_new], axis=1)[None]
    return (y_prompt, y_sample, conv_p, ssm_p[None], k_p, v_p, conv_s, ssm_s[None], ks_out, vs_out)
```

```python
import functools

import jax
import jax.numpy as jnp
from jax import lax
from jax.experimental import pallas as pl
from jax.experimental.pallas import tpu as pltpu

F32, BF16, I32 = jnp.float32, jnp.bfloat16, jnp.int32

D_MODEL = 1024
D_INNER = 2 * D_MODEL
SSM_HEAD_DIM = 64
SSM_HEADS = D_INNER // SSM_HEAD_DIM
SSM_GROUPS = 4
SSM_HPG = SSM_HEADS // SSM_GROUPS
D_STATE = 128
CONV_W = 4
CONV_DIM = D_INNER + 2 * SSM_GROUPS * D_STATE
CHUNK = 128
ATTN_HEAD_DIM = 64
N_Q_HEADS = D_MODEL // ATTN_HEAD_DIM
N_KV_HEADS = 4
Q_PER_KV = N_Q_HEADS // N_KV_HEADS
D_ATTN = N_Q_HEADS * ATTN_HEAD_DIM
D_KV = N_KV_HEADS * ATTN_HEAD_DIM
WINDOW = 128
TOP_K = 4
D_FF = D_MODEL
SWIGLU_LIMIT = 7.0
SWIGLU_ALPHA = 1.702
EPS = 1e-5
NEG_BIG = -1e30

LANES = 128
SUBLANES = 8
GROUP_W = D_INNER // SSM_GROUPS
PIECE = 2 * SUBLANES
PIECES_PER_TILE = 32
VMEM_LIMIT = 56 * 1024 * 1024

NT = (((1,), (1,)), ((), ()))
TN = (((0,), (0,)), ((), ()))


def _const_spec(shape):
    return pl.BlockSpec(shape, lambda *_: (0,) * len(shape))


def _resident_spec(shape):
    return pl.BlockSpec(shape, lambda *_: (0,) * len(shape), pipeline_mode=pl.Buffered(1))


def _split3(x):
    hi = x.astype(BF16)
    r1 = x - hi.astype(F32)
    mid = r1.astype(BF16)
    lo = (r1 - mid.astype(F32)).astype(BF16)
    return hi, mid, lo


def _softplus(x):
    return jnp.maximum(x, 0.0) + jnp.log(1.0 + jnp.exp(-jnp.abs(x)))


def _causal_conv_silu(buf_ref, n, cw_ref, cb_ref):
    full = buf_ref[...]
    conv = cb_ref[...]
    for j in range(CONV_W):
        shifted = full if j == CONV_W - 1 else pltpu.roll(full, CONV_W - 1 - j, 0)
        conv = conv + shifted[SUBLANES:SUBLANES + n, :] * cw_ref[j:j + 1, :]
    buf_ref[0:SUBLANES, :] = full[n:n + SUBLANES, :]
    return conv * jax.nn.sigmoid(conv)


def _inproj_kernel(x_ref, nw_ref, wz_ref, wxbc_ref, wdt_ref, wq_ref, wk_ref, wv_ref, wg_ref, *rest, tiles_per_seq):
    if tiles_per_seq:
        wdtT_ref, z_ref, xbc_ref, dt_ref, q_ref, k_ref, v_ref, g_ref, tail_ref, dtT_ref, ktail_ref, vtail_ref = rest
    else:
        z_ref, xbc_ref, dt_ref, q_ref, k_ref, v_ref, g_ref = rest
    tm = x_ref.shape[0]
    x = x_ref[...]
    xn = x * lax.rsqrt(jnp.mean(x * x, axis=-1, keepdims=True) + EPS)
    xn = (xn * nw_ref[...]).astype(BF16)
    for w_ref, o_ref in ((wz_ref, z_ref), (wdt_ref, dt_ref), (wq_ref, q_ref), (wg_ref, g_ref)):
        o_ref[...] = jnp.dot(xn, w_ref[...], preferred_element_type=F32).astype(o_ref.dtype)
    xbc = jnp.dot(xn, wxbc_ref[...], preferred_element_type=F32)
    k = jnp.dot(xn, wk_ref[...], preferred_element_type=F32)
    v = jnp.dot(xn, wv_ref[...], preferred_element_type=F32)
    xbc_ref[...] = xbc.astype(xbc_ref.dtype)
    k_ref[...] = k.astype(k_ref.dtype)
    v_ref[...] = v.astype(v_ref.dtype)
    if tiles_per_seq:
        dtT_ref[...] = lax.dot_general(wdtT_ref[...], xn, NT, preferred_element_type=F32)

        @pl.when(pl.program_id(0) % tiles_per_seq == tiles_per_seq - 1)
        def _():
            tail_ref[0] = xbc[tm - SUBLANES:, :]
            ktail_ref[0] = k[tm - WINDOW:, :]
            vtail_ref[0] = v[tm - WINDOW:, :]


def _in_proj(x, norm_w, ws, tm, act_dtype, tiles_per_seq=None):
    t = x.shape[0]
    widths = (D_INNER, CONV_DIM, SSM_HEADS, D_ATTN, D_KV, D_KV, 2 * D_MODEL)
    dtypes = (act_dtype, act_dtype, F32, act_dtype, act_dtype, act_dtype, act_dtype)
    row = lambda n: pl.BlockSpec((tm, n), lambda i: (i, 0))
    in_specs = [row(D_MODEL), _const_spec((1, D_MODEL))] + [_resident_spec((D_MODEL, n)) for n in widths]
    out_specs = [row(n) for n in widths]
    out_shape = [jax.ShapeDtypeStruct((t, n), d) for n, d in zip(widths, dtypes)]
    args = [x, norm_w, *ws]
    if tiles_per_seq:
        n_seq = t // (tm * tiles_per_seq)
        in_specs.append(_const_spec((SSM_HEADS, D_MODEL)))
        args.append(ws[2].T)
        per_seq = lambda r, c: pl.BlockSpec((1, r, c), lambda i: (i // tiles_per_seq, 0, 0))
        out_specs += [per_seq(SUBLANES, CONV_DIM), pl.BlockSpec((SSM_HEADS, tm), lambda i: (0, i)),
                      per_seq(WINDOW, D_KV), per_seq(WINDOW, D_KV)]
        out_shape += [jax.ShapeDtypeStruct((n_seq, SUBLANES, CONV_DIM), F32),
                      jax.ShapeDtypeStruct((SSM_HEADS, t), F32),
                      jax.ShapeDtypeStruct((n_seq, WINDOW, D_KV), F32),
                      jax.ShapeDtypeStruct((n_seq, WINDOW, D_KV), F32)]
    return pl.pallas_call(
        functools.partial(_inproj_kernel, tiles_per_seq=tiles_per_seq),
        grid=(t // tm,),
        in_specs=in_specs,
        out_specs=out_specs,
        out_shape=out_shape,
        compiler_params=pltpu.CompilerParams(dimension_semantics=("arbitrary",), vmem_limit_bytes=VMEM_LIMIT),
        name="in_proj",
    )(*args)


def _ssd_kernel(xbc_ref, z_ref, dt_ref, dtT_ref, h0_ref, dtb_ref, dtbT_ref, a_ref, aT_ref, dsk_ref, nw_ref, e_ref,
                *rest, lc, n_valid, conv):
    if conv == "history":
        hist_ref, cw_ref, cb_ref, y_ref, h_ref, buf_ref = rest
    else:
        cw_ref, cb_ref, y_ref, h_ref, tail_ref = rest
    c = pl.program_id(1)

    @pl.when(c == 0)
    def _():
        h_ref[...] = h0_ref[...]

    if conv == "history":
        @pl.when(c == 0)
        def _():
            buf_ref[0:SUBLANES, :] = hist_ref[...]

        buf_ref[SUBLANES:SUBLANES + lc, :] = xbc_ref[...].astype(F32)
        act = _causal_conv_silu(buf_ref, lc, cw_ref, cb_ref)
    else:
        @pl.when(c == 0)
        def _():
            tail_ref[...] = jnp.zeros_like(tail_ref)

        raw = xbc_ref[...]
        nt = tail_ref.shape[0]
        ext = jnp.concatenate([tail_ref[...], raw], axis=0)
        to = lax.broadcasted_iota(I32, (lc, nt + lc), 0)
        frm = lax.broadcasted_iota(I32, (lc, nt + lc), 1)
        conv_acc = cb_ref[...]
        for j in range(CONV_W - 1):
            shift = jnp.where(frm == to + (nt - (CONV_W - 1) + j), 1.0, 0.0).astype(BF16)
            conv_acc = conv_acc + jnp.dot(shift, ext, preferred_element_type=F32) * cw_ref[j:j + 1, :]
        conv_acc = conv_acc + raw.astype(F32) * cw_ref[CONV_W - 1:CONV_W, :]
        tail_ref[...] = raw[lc - nt:, :]
        act = conv_acc * jax.nn.sigmoid(conv_acc)
    xs = act[:, :D_INNER]
    bm = act[:, D_INNER:D_INNER + SSM_GROUPS * D_STATE].astype(BF16)
    cm = act[:, D_INNER + SSM_GROUPS * D_STATE:].astype(BF16)

    dt = _softplus(dt_ref[...] + dtb_ref[...])
    dtT_raw = dtT_ref[0] if len(dtT_ref.shape) == 3 else dtT_ref[...]
    dtT = _softplus(dtT_raw + dtbT_ref[...])
    if n_valid < lc:
        dt = jnp.where(lax.broadcasted_iota(I32, dt.shape, 0) < n_valid, dt, 0.0)
        dtT = jnp.where(lax.broadcasted_iota(I32, dtT.shape, 1) < n_valid, dtT, 0.0)
    la = dt * a_ref[...]
    laT = dtT * aT_ref[...]
    li = lax.broadcasted_iota(I32, (lc, lc), 0)
    si = lax.broadcasted_iota(I32, (lc, lc), 1)
    causal = li >= si
    tril = jnp.where(causal, 1.0, 0.0).astype(BF16)
    triu = jnp.where(li <= si, 1.0, 0.0).astype(BF16)
    cum = sum(jnp.dot(tril, p, preferred_element_type=F32) for p in _split3(la))
    cumT = sum(jnp.dot(p, triu, preferred_element_type=F32) for p in _split3(laT))
    ec = jnp.exp(cum)
    dte = jnp.exp(cum[lc - 1:lc, :] - cum)
    cd = jnp.exp(cumT[:, lc - 1:lc])

    def expand(v):
        hi, mid, _ = _split3(v)
        return jnp.dot(jnp.concatenate([hi, mid], axis=1), e_ref[...], preferred_element_type=F32)

    dt_x, ec_x, dte_x = expand(dt), expand(ec), expand(dte)
    xdt = xs * dt_x
    xdt_b = xdt.astype(BF16)
    xdte_b = (xdt * dte_x).astype(BF16)
    lane = lax.broadcasted_iota(I32, (lc, LANES), 1)
    low_half = lane < SSM_HEAD_DIM

    for g in range(SSM_GROUPS):
        gs = slice(g * GROUP_W, (g + 1) * GROUP_W)
        bm_g = bm[:, g * D_STATE:(g + 1) * D_STATE]
        cm_g = cm[:, g * D_STATE:(g + 1) * D_STATE]
        cb = lax.dot_general(cm_g, bm_g, NT, preferred_element_type=F32)
        cbm = jnp.where(causal, cb, 0.0)
        h_g = h_ref[0, g * SSM_HPG:(g + 1) * SSM_HPG].reshape(GROUP_W, D_STATE)
        y_off = lax.dot_general(cm_g, h_g.astype(BF16), NT, preferred_element_type=F32) * ec_x[:, gs]
        tiles = []
        for j in range(GROUP_W // LANES):
            col = g * GROUP_W + j * LANES
            x_pair = xdt_b[:, col:col + LANES]
            acc = None
            for half in range(2):
                h = col // SSM_HEAD_DIM + half
                seg = cum[:, h:h + 1] - cumT[h:h + 1, :]
                m = (cbm * jnp.exp(jnp.where(causal, seg, 0.0))).astype(BF16)
                x_h = jnp.where(low_half if half == 0 else jnp.logical_not(low_half), x_pair, jnp.zeros_like(x_pair))
                d = jnp.dot(m, x_h, preferred_element_type=F32)
                acc = d if acc is None else acc + d
            tiles.append(acc)
        y_g = jnp.concatenate(tiles, axis=1) + y_off + xs[:, gs] * dsk_ref[:, gs]
        zg = z_ref[:, gs].astype(F32)
        y_g = y_g * (zg * jax.nn.sigmoid(zg))
        y_g = y_g * lax.rsqrt(jnp.mean(y_g * y_g, axis=-1, keepdims=True) + EPS)
        y_ref[:, gs] = (y_g * nw_ref[:, gs]).astype(y_ref.dtype)
        st = lax.dot_general(xdte_b[:, gs], bm_g, TN, preferred_element_type=F32)
        for hh in range(SSM_HPG):
            h = g * SSM_HPG + hh
            rows = slice(hh * SSM_HEAD_DIM, (hh + 1) * SSM_HEAD_DIM)
            h_ref[0, h] = h_g[rows, :] * cd[h:h + 1, 0:1] + st[rows, :]


def _ssd(xbc, z, dt, dtT, h0, consts, nb, nc, lc, n_valid, conv_w, conv_b, hist=None):
    t = xbc.shape[0]
    row = lambda n: pl.BlockSpec((lc, n), lambda b, c: (b * nc + c, 0))
    if dtT.ndim == 2:
        dtT_spec = pl.BlockSpec((SSM_HEADS, lc), lambda b, c: (0, b * nc + c))
    else:
        dtT_spec = pl.BlockSpec((1, SSM_HEADS, lc), lambda b, c: (b * nc + c, 0, 0))
    in_specs = [
        row(CONV_DIM), row(D_INNER), row(SSM_HEADS), dtT_spec,
        pl.BlockSpec((1, SSM_HEADS, SSM_HEAD_DIM, D_STATE), lambda b, c: (b, 0, 0, 0)),
    ] + [_const_spec(a.shape) for a in consts]
    args = [xbc, z, dt, dtT, h0, *consts]
    if hist is not None:
        in_specs.append(pl.BlockSpec((SUBLANES, CONV_DIM), lambda b, c: (b, 0)))
        args.append(hist)
        scratch = [pltpu.VMEM((SUBLANES + lc, CONV_DIM), F32)]
    else:
        scratch = [pltpu.VMEM((PIECE, CONV_DIM), BF16)]
    in_specs += [_const_spec(conv_w.shape), _const_spec(conv_b.shape)]
    args += [conv_w, conv_b]
    return pl.pallas_call(
        functools.partial(_ssd_kernel, lc=lc, n_valid=n_valid, conv="history" if hist is not None else "fresh"),
        grid=(nb, nc),
        in_specs=in_specs,
        out_specs=[row(D_INNER), pl.BlockSpec((1, SSM_HEADS, SSM_HEAD_DIM, D_STATE), lambda b, c: (b, 0, 0, 0))],
        out_shape=[jax.ShapeDtypeStruct((t, D_INNER), BF16),
                   jax.ShapeDtypeStruct((nb, SSM_HEADS, SSM_HEAD_DIM, D_STATE), F32)],
        scratch_shapes=scratch,
        compiler_params=pltpu.CompilerParams(dimension_semantics=("arbitrary", "arbitrary"),
                                             vmem_limit_bytes=VMEM_LIMIT),
        name="ssd",
    )(*args)


def _swa_kernel(slope_ref, sink_ref, q_ref, kc_ref, vc_ref, kp_ref, vp_ref, y_ref, bias_ref, *, tq, prev_always):
    nk = WINDOW + tq

    @pl.when((pl.program_id(0) == 0) & (pl.program_id(1) == 0))
    def _():
        j = lax.broadcasted_iota(I32, (nk, tq), 0)
        r = lax.broadcasted_iota(I32, (nk, tq), 1)
        dist = r + WINDOW - j
        valid = (dist >= 0) & (dist < WINDOW)
        distf = dist.astype(F32)
        for h in range(N_Q_HEADS):
            penalty = -slope_ref[h] * distf
            bias_ref[h] = jnp.where(valid, penalty, NEG_BIG)
            if not prev_always:
                bias_ref[N_Q_HEADS + h] = jnp.where(valid & (j >= WINDOW), penalty, NEG_BIG)

    first = 0 if prev_always else jnp.where(pl.program_id(1) == 0, N_Q_HEADS, 0)
    lane = lax.broadcasted_iota(I32, (nk, LANES), 1)
    transposed_out = tq == WINDOW
    zeros_half = jnp.zeros((ATTN_HEAD_DIM, nk), BF16)
    sink_row = lax.broadcasted_iota(I32, (SUBLANES, tq), 0) == 0
    ones_keys = jnp.ones((nk + SUBLANES, LANES), BF16)
    heads = []
    for t in range(D_KV // LANES):
        cols = slice(t * LANES, (t + 1) * LANES)
        kt = jnp.concatenate([kp_ref[:, cols].astype(F32), kc_ref[:, cols].astype(F32)], axis=0)
        vt = jnp.concatenate([vp_ref[:, cols].astype(F32), vc_ref[:, cols].astype(F32)], axis=0)
        if transposed_out:
            vt_t = vt.T.astype(BF16)
        for b in range(2):
            mine = (lane >= ATTN_HEAD_DIM) if b else (lane < ATTN_HEAD_DIM)
            k_same = jnp.where(mine, kt, 0.0)
            k_half = {b: k_same.astype(BF16), 1 - b: pltpu.roll(k_same, ATTN_HEAD_DIM, 1).astype(BF16)}
            if transposed_out:
                v_g = vt_t[b * ATTN_HEAD_DIM:(b + 1) * ATTN_HEAD_DIM, :]
                v_half = {0: jnp.concatenate([v_g, zeros_half], axis=0),
                          1: jnp.concatenate([zeros_half, v_g], axis=0)}
            else:
                v_same = jnp.concatenate([jnp.where(mine, vt, 0.0), jnp.zeros((SUBLANES, LANES), F32)], axis=0)
                v_half = {b: v_same.astype(BF16), 1 - b: pltpu.roll(v_same, ATTN_HEAD_DIM, 1).astype(BF16)}
            for qi in range(Q_PER_KV):
                a = qi % 2
                heads.append((k_half[a], v_half[a]))

    q_tiles = [(q_ref[:, jq * LANES:(jq + 1) * LANES] * (ATTN_HEAD_DIM ** -0.5)).astype(BF16)
               for jq in range(D_ATTN // LANES)]

    def scores(h):
        return lax.dot_general(heads[h][0], q_tiles[h // 2], NT, preferred_element_type=F32) + bias_ref[first + h]

    def attend(h, s):
        sink = sink_ref[h]
        m = jnp.maximum(jnp.max(s, axis=0, keepdims=True), sink)
        e = jnp.exp(s - m)
        e_sink = jnp.exp(sink - m)
        if transposed_out:
            rden = 1.0 / (jnp.sum(e, axis=0, keepdims=True) + e_sink)
            return jnp.dot(heads[h][1], e.astype(BF16), preferred_element_type=F32) * rden
        p = jnp.concatenate([e, jnp.where(sink_row, e_sink, 0.0)], axis=0).astype(BF16)
        num = lax.dot_general(p, heads[h][1], TN, preferred_element_type=F32)
        den = lax.dot_general(p, ones_keys, TN, preferred_element_type=F32)
        return num / den

    out_tiles = [None] * (D_ATTN // LANES)
    s_next = scores(0)
    for h in range(N_Q_HEADS):
        s_cur = s_next
        if h + 1 < N_Q_HEADS:
            s_next = scores(h + 1)
        o = attend(h, s_cur)
        out_tiles[h // 2] = o if out_tiles[h // 2] is None else out_tiles[h // 2] + o
    for jq, o in enumerate(out_tiles):
        y_ref[:, jq * LANES:(jq + 1) * LANES] = (o.T if transposed_out else o).astype(y_ref.dtype)


def _swa(slopes, sinks, q, k, v, k_prev, v_prev, nb, nblk, tq, prev_always):
    t = q.shape[0]
    cur = lambda n: pl.BlockSpec((tq, n), lambda b, i: (b * nblk + i, 0))
    if prev_always:
        prev = pl.BlockSpec((WINDOW, D_KV), lambda b, i: (b, 0))
    else:
        prev = pl.BlockSpec((WINDOW, D_KV), lambda b, i: (b * nblk + jnp.maximum(i - 1, 0), 0))
    smem = pl.BlockSpec(memory_space=pltpu.SMEM)
    return pl.pallas_call(
        functools.partial(_swa_kernel, tq=tq, prev_always=prev_always),
        grid=(nb, nblk),
        in_specs=[smem, smem, cur(D_ATTN), cur(D_KV), cur(D_KV), prev, prev],
        out_specs=cur(D_ATTN),
        out_shape=jax.ShapeDtypeStruct((t, D_ATTN), BF16),
        scratch_shapes=[pltpu.VMEM(((1 if prev_always else 2) * N_Q_HEADS, WINDOW + tq, tq), F32)],
        compiler_params=pltpu.CompilerParams(dimension_semantics=("arbitrary", "arbitrary"),
                                             vmem_limit_bytes=VMEM_LIMIT),
        name="swa",
    )(slopes, sinks, q, k, v, k_prev, v_prev)


def _post_kernel(x_ref, ys_ref, ya_ref, g_ref, wsp_ref, wap_ref, wo_ref, nw_ref, wr_ref, br_ref, *rest,
                 n_exp, lr, nt, aliased):
    if aliased:
        rest = rest[1:]
    h_ref, xs_ref, lpos_ref, pr_ref, pc_ref = rest
    tp = x_ref.shape[0]

    @pl.when(pl.program_id(0) >= nt)
    def _():
        xs_ref[...] = jnp.zeros_like(xs_ref)

    @pl.when(pl.program_id(0) < nt)
    def _():
        a = jnp.dot(ys_ref[...].astype(BF16), wsp_ref[...], preferred_element_type=F32)
        b = jnp.dot(ya_ref[...].astype(BF16), wap_ref[...], preferred_element_type=F32)
        g = g_ref[...].astype(F32)
        merged = jax.nn.sigmoid(g[:, :D_MODEL]) * a + jax.nn.sigmoid(g[:, D_MODEL:]) * b
        h = x_ref[...] + jnp.dot(merged.astype(BF16), wo_ref[...], preferred_element_type=F32)
        h_ref[...] = h
        hn = h * lax.rsqrt(jnp.mean(h * h, axis=-1, keepdims=True) + EPS) * nw_ref[...]
        w_hi, w_mid, _ = _split3(wr_ref[...])
        x_hi, x_mid, _ = _split3(hn)
        logits = (lax.dot_general(w_hi, x_hi, NT, preferred_element_type=F32)
                  + lax.dot_general(w_hi, x_mid, NT, preferred_element_type=F32)
                  + lax.dot_general(w_mid, x_hi, NT, preferred_element_type=F32)) + br_ref[...]
        eidx = lax.broadcasted_iota(I32, logits.shape, 0).astype(F32)
        work = logits
        vals, ids = [], []
        for _ in range(TOP_K):
            m = jnp.max(work, axis=0, keepdims=True)
            first = jnp.min(jnp.where(work == m, eidx, float(n_exp)), axis=0, keepdims=True)
            vals.append(m)
            ids.append(first)
            work = jnp.where(eidx == first, -jnp.inf, work)
        es = [jnp.exp(v - vals[0]) for v in vals]
        den = es[0] + es[1] + es[2] + es[3]
        eye = jnp.where(lax.broadcasted_iota(I32, (TOP_K, TOP_K), 0) == lax.broadcasted_iota(I32, (TOP_K, TOP_K), 1),
                        1.0, 0.0).astype(BF16)

        def to_columns(rows):
            return sum(lax.dot_general(p, eye, TN, preferred_element_type=F32) for p in _split3(rows))

        pr_ref[...] = to_columns(jnp.concatenate([e / den for e in es], axis=0))

        onehot = [jnp.where(eidx == i, 1.0, 0.0) for i in ids]
        counts = [jnp.sum(o, axis=1, keepdims=True) for o in onehot]
        total = counts[0] + counts[1] + counts[2] + counts[3]
        padded = jnp.floor((total + (PIECE - 1)) * (1.0 / PIECE)) * PIECE
        ei = lax.broadcasted_iota(I32, (n_exp, n_exp), 0)
        ej = lax.broadcasted_iota(I32, (n_exp, n_exp), 1)
        below = jnp.where(ej < ei, 1.0, 0.0).astype(BF16)
        padded_b = jnp.broadcast_to(padded, (n_exp, LANES))
        seg_off = sum(jnp.dot(below, p, preferred_element_type=F32) for p in _split3(padded_b))[:, 0:1]
        ti = lax.broadcasted_iota(I32, (tp, tp), 0)
        tj = lax.broadcasted_iota(I32, (tp, tp), 1)
        before = jnp.where(ti < tj, 1.0, 0.0).astype(BF16)
        base = seg_off
        lpos = []
        for k in range(TOP_K):
            prefix = jnp.dot(onehot[k].astype(BF16), before, preferred_element_type=F32)
            lpos.append(jnp.sum(onehot[k] * (base + prefix), axis=0, keepdims=True))
            base = base + counts[k]
        lpos_ref[...] = to_columns(jnp.concatenate(lpos, axis=0)).astype(I32)
        pc_ref[0] = padded_b.astype(I32)

        hn_b = hn.astype(BF16)
        rc = lr // 4
        for c in range(4):
            ri = (lax.broadcasted_iota(I32, (rc, tp), 0) + c * rc).astype(F32)
            sel = None
            for k in range(TOP_K):
                hit = jnp.where(ri == lpos[k], 1.0, 0.0)
                sel = hit if sel is None else sel + hit
            xs_ref[c * rc:(c + 1) * rc, :] = jnp.dot(sel.astype(BF16), hn_b, preferred_element_type=F32).astype(BF16)


def _post(x, y_ssm, y_attn, gates, wsp, wap, wo, ffn_nw, w_rT, b_r, tm, lr, xs_rows, xs_block0, pad_steps,
          xs_prev=None):
    t = x.shape[0]
    n_exp = w_rT.shape[0]
    nt = t // tm
    last = nt - 1
    row = lambda n: pl.BlockSpec((tm, n), lambda i: (jnp.minimum(i, last), 0))
    col = row(TOP_K)
    in_specs = [row(D_MODEL), row(D_INNER), row(D_ATTN), row(2 * D_MODEL),
                _resident_spec(wsp.shape), _resident_spec(wap.shape), _resident_spec(wo.shape),
                _const_spec(ffn_nw.shape), _const_spec(w_rT.shape), _const_spec(b_r.shape)]
    args = [x, y_ssm, y_attn, gates, wsp, wap, wo, ffn_nw, w_rT, b_r]
    aliases = {}
    if xs_prev is not None:
        in_specs.append(pl.BlockSpec(memory_space=pl.ANY))
        args.append(xs_prev)
        aliases = {len(args) - 1: 1}
    return pl.pallas_call(
        functools.partial(_post_kernel, n_exp=n_exp, lr=lr, nt=nt, aliased=xs_prev is not None),
        grid=(nt + pad_steps,),
        in_specs=in_specs,
        out_specs=[row(D_MODEL), pl.BlockSpec((lr, D_MODEL), lambda i: (xs_block0 + i, 0)), col, col,
                   pl.BlockSpec((1, n_exp, LANES), lambda i: (jnp.minimum(i, last), 0, 0))],
        out_shape=[jax.ShapeDtypeStruct((t, D_MODEL), F32), jax.ShapeDtypeStruct((xs_rows, D_MODEL), BF16),
                   jax.ShapeDtypeStruct((t, TOP_K), I32), jax.ShapeDtypeStruct((t, TOP_K), F32),
                   jax.ShapeDtypeStruct((nt, n_exp, LANES), I32)],
        input_output_aliases=aliases,
        compiler_params=pltpu.CompilerParams(dimension_semantics=("arbitrary",), vmem_limit_bytes=VMEM_LIMIT),
        name="post",
    )(*args)


def _piece(ref, p):
    return ref.at[pl.ds(pl.multiple_of(p * PIECE, PIECE), PIECE)]


def _moe_kernel(te_ref, nu_ref, src0_ref, src1_ref, srcc_ref, xs_hbm, wg_ref, wu_ref, wd_ref,
                bg_ref, bu_ref, bd_ref, ys_hbm, xbuf, obuf, wbf, gsem, ssem):
    i = pl.program_id(0)
    n_used = nu_ref[0]
    slot = i % 2
    tme = PIECES_PER_TILE * PIECE

    def gather(src_ref, s):
        for r in range(PIECES_PER_TILE):
            pltpu.make_async_copy(_piece(xs_hbm, src_ref[0, 0, r]), xbuf.at[s, pl.ds(r * PIECE, PIECE)],
                                  gsem.at[s]).start()

    def wait_gather(s):
        pltpu.make_async_copy(xs_hbm.at[pl.ds(0, tme)], xbuf.at[s], gsem.at[s]).wait()

    def wait_put(s):
        pltpu.make_async_copy(obuf.at[s], ys_hbm.at[pl.ds(0, tme)], ssem.at[s]).wait()

    @pl.when(i == 0)
    def _():
        gather(src0_ref, 0)

    @pl.when(i < n_used)
    def _():
        gather(src1_ref, 1 - slot)
        wait_gather(slot)

        @pl.when(i >= 2)
        def _():
            wait_put(slot)

        @pl.when((i == 0) | (te_ref[i] != te_ref[jnp.maximum(i - 1, 0)]))
        def _():
            wbf[0] = wg_ref[0].astype(BF16)
            wbf[1] = wu_ref[0].astype(BF16)
            wbf[2] = wd_ref[0].astype(BF16)

        x = xbuf[slot]
        g = jnp.minimum(jnp.dot(x, wbf[0], preferred_element_type=F32) + bg_ref[0], SWIGLU_LIMIT)
        u = jnp.clip(jnp.dot(x, wbf[1], preferred_element_type=F32) + bu_ref[0], -SWIGLU_LIMIT, SWIGLU_LIMIT)
        act = ((u + 1.0) * g * jax.nn.sigmoid(SWIGLU_ALPHA * g)).astype(BF16)
        obuf[slot] = (jnp.dot(act, wbf[2], preferred_element_type=F32) + bd_ref[0]).astype(BF16)
        for r in range(PIECES_PER_TILE):
            pltpu.make_async_copy(obuf.at[slot, pl.ds(r * PIECE, PIECE)], _piece(ys_hbm, srcc_ref[0, 0, r]),
                                  ssem.at[slot]).start()

        @pl.when(i == n_used - 1)
        def _():
            wait_put(slot)
            wait_gather(1 - slot)

            @pl.when(i >= 1)
            def _():
                wait_put(1 - slot)


def _moe(tile_expert, n_used, src, xs, wg, wu, wd, bg, bu, bd):
    n_tiles = tile_expert.shape[0]
    idx = src.reshape(n_tiles, 1, PIECES_PER_TILE)
    tme = PIECES_PER_TILE * PIECE
    wspec = pl.BlockSpec((1, D_MODEL, D_FF), lambda i, te, nu: (te[i], 0, 0))
    bspec = pl.BlockSpec((1, 1, D_FF), lambda i, te, nu: (te[i], 0, 0))
    ispec = lambda f: pl.BlockSpec((1, 1, PIECES_PER_TILE), lambda i, te, nu: (f(i), 0, 0),
                                   memory_space=pltpu.SMEM)
    grid_spec = pltpu.PrefetchScalarGridSpec(
        num_scalar_prefetch=2,
        grid=(n_tiles,),
        in_specs=[ispec(lambda i: 0), ispec(lambda i: jnp.minimum(i + 1, n_tiles - 1)), ispec(lambda i: i),
                  pl.BlockSpec(memory_space=pl.ANY), wspec, wspec, wspec, bspec, bspec, bspec],
        out_specs=pl.BlockSpec(memory_space=pl.ANY),
        scratch_shapes=[pltpu.VMEM((2, tme, D_MODEL), BF16), pltpu.VMEM((2, tme, D_MODEL), BF16),
                        pltpu.VMEM((3, D_MODEL, D_FF), BF16),
                        pltpu.SemaphoreType.DMA((2,)), pltpu.SemaphoreType.DMA((2,))],
    )
    return pl.pallas_call(
        _moe_kernel,
        grid_spec=grid_spec,
        out_shape=jax.ShapeDtypeStruct(xs.shape, xs.dtype),
        input_output_aliases={5: 0},
        compiler_params=pltpu.CompilerParams(dimension_semantics=("arbitrary",), vmem_limit_bytes=VMEM_LIMIT),
        name="moe",
    )(tile_expert, n_used, idx, idx, idx, xs, wg, wu, wd, bg, bu, bd)


def _combine_kernel(h_ref, ys_ref, lpos_ref, pr_ref, nw_ref, o_ref):
    tp = h_ref.shape[0]
    lr = ys_ref.shape[0]
    ri = lax.broadcasted_iota(I32, (tp, lr), 1)
    lp = lpos_ref[...]
    pr = pr_ref[...]
    pw = jnp.zeros((tp, lr), F32)
    for k in range(TOP_K):
        pw = jnp.where(ri == lp[:, k:k + 1], pr[:, k:k + 1], pw)
    moe = jnp.dot(pw.astype(BF16), ys_ref[...], preferred_element_type=F32)
    h = h_ref[...] + moe
    o_ref[...] = h * lax.rsqrt(jnp.mean(h * h, axis=-1, keepdims=True) + EPS) * nw_ref[...]


def _combine(h, ys, lpos_t, probs_t, final_nw, tm, lr, ys_block0):
    t = h.shape[0]
    return pl.pallas_call(
        _combine_kernel,
        grid=(t // tm,),
        in_specs=[pl.BlockSpec((tm, D_MODEL), lambda i: (i, 0)),
                  pl.BlockSpec((lr, D_MODEL), lambda i: (ys_block0 + i, 0)),
                  pl.BlockSpec((tm, TOP_K), lambda i: (i, 0)),
                  pl.BlockSpec((tm, TOP_K), lambda i: (i, 0)),
                  _const_spec(final_nw.shape)],
        out_specs=pl.BlockSpec((tm, D_MODEL), lambda i: (i, 0)),
        out_shape=jax.ShapeDtypeStruct((t, D_MODEL), F32),
        compiler_params=pltpu.CompilerParams(dimension_semantics=("arbitrary",), vmem_limit_bytes=VMEM_LIMIT),
        name="combine",
    )(h, ys, lpos_t, probs_t, final_nw)


def _piece_table(padded_counts, tile_row0, n_tiles, spare_piece0):
    n_pieces = (padded_counts // PIECE).T
    seg_row = tile_row0[:, None] + jnp.cumsum(padded_counts, axis=1) - padded_counts
    seg_piece = (seg_row // PIECE).T
    per_expert = n_pieces.sum(axis=1)
    tiles_per = (per_expert + PIECES_PER_TILE - 1) // PIECES_PER_TILE
    tile_end = jnp.cumsum(tiles_per)
    n_used = tile_end[-1]
    slot0 = (tile_end - tiles_per) * PIECES_PER_TILE
    seg_slot = (slot0[:, None] + jnp.cumsum(n_pieces, axis=1) - n_pieces).reshape(-1)
    seg_n = n_pieces.reshape(-1)
    seg_src = seg_piece.reshape(-1)
    slots = jnp.arange(n_tiles * PIECES_PER_TILE, dtype=I32)

    def at_segment_of_slot(f):
        df = f - jnp.concatenate([jnp.zeros((1,), I32), f[:-1]])
        return jnp.sum(jnp.where(seg_slot[None, :] <= slots[:, None], df[None, :], 0), axis=1)

    real = slots < at_segment_of_slot(seg_slot + seg_n)
    padding = jnp.logical_not(real) & (slots < n_used * PIECES_PER_TILE)
    spare = spare_piece0 + jnp.where(padding, jnp.cumsum(padding.astype(I32)), 0)
    src = jnp.where(real, slots + at_segment_of_slot(seg_src - seg_slot), spare).astype(I32)
    tile_ids = jnp.arange(n_tiles, dtype=I32)
    tile_expert = jnp.sum((tile_end[None, :] <= jnp.minimum(tile_ids, n_used - 1)[:, None]).astype(I32), axis=1)
    return tile_expert.astype(I32), n_used.reshape(1).astype(I32), src


def _pick_tile(n, pref):
    while n % pref:
        pref //= 2
    return pref


def _local_rows(tm, n_exp):
    need = TOP_K * tm + n_exp * (PIECE - 1)
    return -(-need // 64) * 64


def kernel(x_prompt, x_sample, state_conv, state_ssm, cache_swa_k, cache_swa_v, attn_norm_w, w_in, conv_w, conv_b, dt_bias, a_log, d_skip, ssm_norm_w, attn_sinks, w_ssm_proj, w_attn_proj, w_o, ffn_norm_w, w_router, b_router, w_gate, b_gate, w_up, b_up, w_down, b_down, final_norm_w):
    assert w_in.shape[0] == 1, "single-layer step"
    nb, seq, _ = x_prompt.shape
    nbs = x_sample.shape[0]
    n_exp = w_router.shape[-1]
    tp, ts = nb * seq, nbs
    pad = SUBLANES

    cuts = [0]
    for n in (D_INNER, CONV_DIM, SSM_HEADS, D_ATTN, D_KV, D_KV, 2 * D_MODEL):
        cuts.append(cuts[-1] + n)
    w_in_b = w_in[0].astype(BF16)
    ws = [w_in_b[:, cuts[i]:cuts[i + 1]] for i in range(7)]
    attn_nw = attn_norm_w[0].reshape(1, D_MODEL)
    a_neg = -jnp.exp(a_log[0].astype(F32))
    head_of = jnp.arange(D_INNER, dtype=I32) // SSM_HEAD_DIM
    expand = jnp.tile((jnp.arange(SSM_HEADS, dtype=I32)[:, None] == head_of[None, :]).astype(BF16), (2, 1))
    conv_consts = (conv_w[0], conv_b[0].reshape(1, CONV_DIM))
    ssd_consts = (dt_bias[0].reshape(1, SSM_HEADS), dt_bias[0].reshape(SSM_HEADS, 1),
                  a_neg.reshape(1, SSM_HEADS), a_neg.reshape(SSM_HEADS, 1),
                  d_skip[0][head_of].reshape(1, D_INNER), ssm_norm_w[0].reshape(1, D_INNER), expand)
    slopes = jnp.exp2(-8.0 * jnp.arange(1, N_Q_HEADS + 1, dtype=F32) / N_Q_HEADS)
    sinks = attn_sinks[0].astype(F32)
    wsp, wap, wo = w_ssm_proj[0].astype(BF16), w_attn_proj[0].astype(BF16), w_o[0].astype(BF16)
    ffn_nw = ffn_norm_w[0].reshape(1, D_MODEL)
    w_rT = w_router[0].T
    b_r = b_router[0].reshape(n_exp, 1)
    bg, bu, bd = (b[0].reshape(n_exp, 1, -1) for b in (b_gate, b_up, b_down))
    final_nw = final_norm_w.reshape(1, D_MODEL)

    xp = x_prompt.reshape(tp, D_MODEL)
    tm_in = _pick_tile(seq, 512)
    z, xbc, dt, q, k, v, gates, conv_tail, dtT, k_tail, v_tail = _in_proj(
        xp, attn_nw, ws, tm_in, BF16, tiles_per_seq=seq // tm_in)
    nc = seq // CHUNK
    y_ssm, ssm_p = _ssd(xbc, z, dt, dtT, jnp.zeros((nb, SSM_HEADS, SSM_HEAD_DIM, D_STATE), F32), ssd_consts,
                        nb, nc, CHUNK, CHUNK, *conv_consts)
    nblk = seq // WINDOW
    y_attn = _swa(slopes, sinks, q, k, v, k, v, nb, nblk, WINDOW, False)

    xs_pad = jnp.pad(x_sample.reshape(ts, 1, D_MODEL), ((0, 0), (0, pad - 1), (0, 0))).reshape(ts * pad, D_MODEL)
    z_s, xbc_s, dt_s, q_s, k_s, v_s, gates_s = _in_proj(xs_pad, attn_nw, ws, _pick_tile(ts * pad, 256), F32)
    dtT_s = dt_s.reshape(ts, pad, SSM_HEADS).transpose(0, 2, 1)
    hist_s = jnp.pad(state_conv[0], ((0, 0), (pad - (CONV_W - 1), 0), (0, 0))).reshape(ts * pad, CONV_DIM)
    y_ssm_s, ssm_s = _ssd(xbc_s, z_s, dt_s, dtT_s, state_ssm[0], ssd_consts, ts, 1, pad, 1, *conv_consts,
                          hist=hist_s)
    kc = cache_swa_k[0].reshape(ts * WINDOW, D_KV)
    vc = cache_swa_v[0].reshape(ts * WINDOW, D_KV)
    y_attn_s = _swa(slopes, sinks, q_s, k_s, v_s, kc, vc, ts, 1, pad, True)
    real = lambda a: a.reshape(ts, pad, -1)[:, 0]

    tm_p, tm_s = _pick_tile(tp, 512), ts
    nt_p = tp // tm_p
    lr_p, lr_s = _local_rows(tm_p, n_exp), _local_rows(tm_s, n_exp)
    if (nt_p * lr_p) % lr_s or lr_s > lr_p:
        lr_s = lr_p
    spare_rows = lr_s + (1 + n_exp * (PIECES_PER_TILE - 1)) * PIECE
    pad_steps = -(-spare_rows // lr_p)
    xs_rows = (nt_p + pad_steps) * lr_p
    block_s = nt_p * lr_p // lr_s
    post_w = (wsp, wap, wo, ffn_nw, w_rT, b_r)
    h_p, xs, lpos_p, pr_p, pc_p = _post(xp, y_ssm, y_attn, gates, *post_w, tm_p, lr_p, xs_rows, 0, pad_steps)
    h_s, xs, lpos_s, pr_s, pc_s = _post(x_sample.reshape(ts, D_MODEL), real(y_ssm_s), real(y_attn_s), real(gates_s),
                                        *post_w, tm_s, lr_s, xs_rows, block_s, 0, xs_prev=xs)

    padded_counts = jnp.concatenate([pc_p[:, :, 0], pc_s[:, :, 0]], axis=0)
    tile_row0 = jnp.concatenate([jnp.arange(nt_p, dtype=I32) * lr_p, jnp.full((1,), nt_p * lr_p, I32)])
    max_pieces = (TOP_K * (tp + ts) + (PIECE - 1) * n_exp * (nt_p + 1)) // PIECE + n_exp * (PIECES_PER_TILE - 1)
    n_tiles = -(-max_pieces // PIECES_PER_TILE) + 1
    tile_expert, n_used, src = _piece_table(padded_counts, tile_row0, n_tiles, (nt_p * lr_p + lr_s) // PIECE)
    ys = _moe(tile_expert, n_used, src, xs, w_gate[0], w_up[0], w_down[0], bg, bu, bd)
    out_p = _combine(h_p, ys, lpos_p, pr_p, final_nw, tm_p, lr_p, 0)
    out_s = _combine(h_s, ys, lpos_s, pr_s, final_nw, tm_s, lr_s, block_s)

    y_prompt = out_p.reshape(nb, seq, D_MODEL)
    y_sample = out_s.reshape(nbs, 1, D_MODEL)
    conv_p = conv_tail[:, SUBLANES - (CONV_W - 1):][None]
    k_p = k_tail.reshape(1, nb, WINDOW, N_KV_HEADS, ATTN_HEAD_DIM)
    v_p = v_tail.reshape(1, nb, WINDOW, N_KV_HEADS, ATTN_HEAD_DIM)
    conv_s = jnp.concatenate([state_conv[0][:, 1:], real(xbc_s)[:, None]], axis=1)[None]
    k_new = real(k_s).reshape(ts, 1, N_KV_HEADS, ATTN_HEAD_DIM)
    v_new = real(v_s).reshape(ts, 1, N_KV_HEADS, ATTN_HEAD_DIM)
    ks_out = jnp.concatenate([cache_swa_k[0][:, 1:], k_new], axis=1)[None]
    vs_out = jnp.concatenate([cache_swa_v[0][:, 1:], v_new], axis=1)[None]
    return (y_prompt, y_sample, conv_p, ssm_p[None], k_p, v_p, conv_s, ssm_s[None], ks_out, vs_out)
```

```python
import functools

import jax
import jax.numpy as jnp
from jax import lax
from jax.experimental import pallas as pl
from jax.experimental.pallas import tpu as pltpu

F32, BF16, I32 = jnp.float32, jnp.bfloat16, jnp.int32

D_MODEL = 1024
D_INNER = 2 * D_MODEL
SSM_HEAD_DIM = 64
SSM_HEADS = D_INNER // SSM_HEAD_DIM
SSM_GROUPS = 4
SSM_HPG = SSM_HEADS // SSM_GROUPS
D_STATE = 128
CONV_W = 4
CONV_DIM = D_INNER + 2 * SSM_GROUPS * D_STATE
CHUNK = 128
ATTN_HEAD_DIM = 64
N_Q_HEADS = D_MODEL // ATTN_HEAD_DIM
N_KV_HEADS = 4
Q_PER_KV = N_Q_HEADS // N_KV_HEADS
D_ATTN = N_Q_HEADS * ATTN_HEAD_DIM
D_KV = N_KV_HEADS * ATTN_HEAD_DIM
WINDOW = 128
TOP_K = 4
D_FF = D_MODEL
SWIGLU_LIMIT = 7.0
SWIGLU_ALPHA = 1.702
EPS = 1e-5
NEG_BIG = -1e30

LANES = 128
SUBLANES = 8
GROUP_W = D_INNER // SSM_GROUPS
PIECE = 2 * SUBLANES
PIECES_PER_TILE = 32
SAMPLE_SEQS_PER_STEP = 4
VMEM_LIMIT = 56 * 1024 * 1024

NT = (((1,), (1,)), ((), ()))
TN = (((0,), (0,)), ((), ()))


def _const_spec(shape):
    return pl.BlockSpec(shape, lambda *_: (0,) * len(shape))


def _resident_spec(shape):
    return pl.BlockSpec(shape, lambda *_: (0,) * len(shape), pipeline_mode=pl.Buffered(1))


def _split3(x):
    hi = x.astype(BF16)
    r1 = x - hi.astype(F32)
    mid = r1.astype(BF16)
    lo = (r1 - mid.astype(F32)).astype(BF16)
    return hi, mid, lo


def _softplus(x):
    return jnp.maximum(x, 0.0) + jnp.log(1.0 + jnp.exp(-jnp.abs(x)))


def _causal_conv_silu(buf_ref, n, cw_ref, cb_ref):
    full = buf_ref[...]
    conv = cb_ref[...]
    for j in range(CONV_W):
        shifted = full if j == CONV_W - 1 else pltpu.roll(full, CONV_W - 1 - j, 0)
        conv = conv + shifted[SUBLANES:SUBLANES + n, :] * cw_ref[j:j + 1, :]
    buf_ref[0:SUBLANES, :] = full[n:n + SUBLANES, :]
    return conv * jax.nn.sigmoid(conv)


def _inproj_kernel(x_ref, nw_ref, wz_ref, wxbc_ref, wdt_ref, wq_ref, wk_ref, wv_ref, wg_ref, *rest, tiles_per_seq):
    if tiles_per_seq:
        wdtT_ref, z_ref, xbc_ref, dt_ref, q_ref, k_ref, v_ref, g_ref, tail_ref, dtT_ref, ktail_ref, vtail_ref = rest
    else:
        z_ref, xbc_ref, dt_ref, q_ref, k_ref, v_ref, g_ref = rest
    tm = x_ref.shape[0]
    x = x_ref[...]
    xn = x * lax.rsqrt(jnp.mean(x * x, axis=-1, keepdims=True) + EPS)
    xn = (xn * nw_ref[...]).astype(BF16)
    for w_ref, o_ref in ((wz_ref, z_ref), (wdt_ref, dt_ref), (wq_ref, q_ref), (wg_ref, g_ref)):
        o_ref[...] = jnp.dot(xn, w_ref[...], preferred_element_type=F32).astype(o_ref.dtype)
    xbc = jnp.dot(xn, wxbc_ref[...], preferred_element_type=F32)
    k = jnp.dot(xn, wk_ref[...], preferred_element_type=F32)
    v = jnp.dot(xn, wv_ref[...], preferred_element_type=F32)
    xbc_ref[...] = xbc.astype(xbc_ref.dtype)
    k_ref[...] = k.astype(k_ref.dtype)
    v_ref[...] = v.astype(v_ref.dtype)
    if tiles_per_seq:
        dtT_ref[...] = lax.dot_general(wdtT_ref[...], xn, NT, preferred_element_type=F32)

        @pl.when(pl.program_id(0) % tiles_per_seq == tiles_per_seq - 1)
        def _():
            tail_ref[0] = xbc[tm - SUBLANES:, :]
            ktail_ref[0] = k[tm - WINDOW:, :]
            vtail_ref[0] = v[tm - WINDOW:, :]


def _in_proj(x, norm_w, ws, tm, act_dtype, tiles_per_seq=None):
    t = x.shape[0]
    widths = (D_INNER, CONV_DIM, SSM_HEADS, D_ATTN, D_KV, D_KV, 2 * D_MODEL)
    dtypes = (act_dtype, act_dtype, F32, act_dtype, act_dtype, act_dtype, act_dtype)
    row = lambda n: pl.BlockSpec((tm, n), lambda i: (i, 0))
    in_specs = [row(D_MODEL), _const_spec((1, D_MODEL))] + [_resident_spec((D_MODEL, n)) for n in widths]
    out_specs = [row(n) for n in widths]
    out_shape = [jax.ShapeDtypeStruct((t, n), d) for n, d in zip(widths, dtypes)]
    args = [x, norm_w, *ws]
    if tiles_per_seq:
        n_seq = t // (tm * tiles_per_seq)
        in_specs.append(_const_spec((SSM_HEADS, D_MODEL)))
        args.append(ws[2].T)
        per_seq = lambda r, c: pl.BlockSpec((1, r, c), lambda i: (i // tiles_per_seq, 0, 0))
        out_specs += [per_seq(SUBLANES, CONV_DIM), pl.BlockSpec((SSM_HEADS, tm), lambda i: (0, i)),
                      per_seq(WINDOW, D_KV), per_seq(WINDOW, D_KV)]
        out_shape += [jax.ShapeDtypeStruct((n_seq, SUBLANES, CONV_DIM), F32),
                      jax.ShapeDtypeStruct((SSM_HEADS, t), F32),
                      jax.ShapeDtypeStruct((n_seq, WINDOW, D_KV), F32),
                      jax.ShapeDtypeStruct((n_seq, WINDOW, D_KV), F32)]
    return pl.pallas_call(
        functools.partial(_inproj_kernel, tiles_per_seq=tiles_per_seq),
        grid=(t // tm,),
        in_specs=in_specs,
        out_specs=out_specs,
        out_shape=out_shape,
        compiler_params=pltpu.CompilerParams(dimension_semantics=("arbitrary",), vmem_limit_bytes=VMEM_LIMIT),
        name="in_proj",
    )(*args)


def _ssd_kernel(xbc_ref, z_ref, dt_ref, dtT_ref, h0_ref, dtb_ref, dtbT_ref, a_ref, aT_ref, dsk_ref, nw_ref, e_ref,
                *rest, lc, n_valid, conv, single_chunk):
    if conv == "history":
        hist_ref, cw_ref, cb_ref, y_ref, h_ref, buf_ref = rest
    else:
        cw_ref, cb_ref, y_ref, h_ref, tail_ref = rest

    def at_first_chunk(fn):
        if single_chunk:
            fn()
        else:
            pl.when(pl.program_id(1) == 0)(fn)

    @at_first_chunk
    def _():
        h_ref[...] = h0_ref[...]

    if conv == "history":
        @at_first_chunk
        def _():
            buf_ref[0:SUBLANES, :] = hist_ref[...]

        buf_ref[SUBLANES:SUBLANES + lc, :] = xbc_ref[...].astype(F32)
        act = _causal_conv_silu(buf_ref, lc, cw_ref, cb_ref)
    else:
        @at_first_chunk
        def _():
            tail_ref[...] = jnp.zeros_like(tail_ref)

        raw = xbc_ref[...]
        nt = tail_ref.shape[0]
        ext = jnp.concatenate([tail_ref[...], raw], axis=0)
        to = lax.broadcasted_iota(I32, (lc, nt + lc), 0)
        frm = lax.broadcasted_iota(I32, (lc, nt + lc), 1)
        conv_acc = cb_ref[...]
        for j in range(CONV_W - 1):
            shift = jnp.where(frm == to + (nt - (CONV_W - 1) + j), 1.0, 0.0).astype(BF16)
            conv_acc = conv_acc + jnp.dot(shift, ext, preferred_element_type=F32) * cw_ref[j:j + 1, :]
        conv_acc = conv_acc + raw.astype(F32) * cw_ref[CONV_W - 1:CONV_W, :]
        tail_ref[...] = raw[lc - nt:, :]
        act = conv_acc * jax.nn.sigmoid(conv_acc)
    xs = act[:, :D_INNER]
    bm = act[:, D_INNER:D_INNER + SSM_GROUPS * D_STATE].astype(BF16)
    cm = act[:, D_INNER + SSM_GROUPS * D_STATE:].astype(BF16)

    dt = _softplus(dt_ref[...] + dtb_ref[...])
    dtT_raw = dtT_ref[0] if len(dtT_ref.shape) == 3 else dtT_ref[...]
    dtT = _softplus(dtT_raw + dtbT_ref[...])
    if n_valid < lc:
        dt = jnp.where(lax.broadcasted_iota(I32, dt.shape, 0) < n_valid, dt, 0.0)
        dtT = jnp.where(lax.broadcasted_iota(I32, dtT.shape, 1) < n_valid, dtT, 0.0)
    la = dt * a_ref[...]
    laT = dtT * aT_ref[...]
    li = lax.broadcasted_iota(I32, (lc, lc), 0)
    si = lax.broadcasted_iota(I32, (lc, lc), 1)
    causal = li >= si
    tril = jnp.where(causal, 1.0, 0.0).astype(BF16)
    triu = jnp.where(li <= si, 1.0, 0.0).astype(BF16)
    cum = sum(jnp.dot(tril, p, preferred_element_type=F32) for p in _split3(la))
    cumT = sum(jnp.dot(p, triu, preferred_element_type=F32) for p in _split3(laT))
    ec = jnp.exp(cum)
    dte = jnp.exp(cum[lc - 1:lc, :] - cum)
    cd = jnp.exp(cumT[:, lc - 1:lc])

    def expand(v):
        hi, mid, _ = _split3(v)
        return jnp.dot(jnp.concatenate([hi, mid], axis=1), e_ref[...], preferred_element_type=F32)

    dt_x, ec_x, dte_x = expand(dt), expand(ec), expand(dte)
    xdt = xs * dt_x
    xdt_b = xdt.astype(BF16)
    xdte_b = (xdt * dte_x).astype(BF16)
    lane = lax.broadcasted_iota(I32, (lc, LANES), 1)
    low_half = lane < SSM_HEAD_DIM

    for g in range(SSM_GROUPS):
        gs = slice(g * GROUP_W, (g + 1) * GROUP_W)
        bm_g = bm[:, g * D_STATE:(g + 1) * D_STATE]
        cm_g = cm[:, g * D_STATE:(g + 1) * D_STATE]
        cb = lax.dot_general(cm_g, bm_g, NT, preferred_element_type=F32)
        cbm = jnp.where(causal, cb, 0.0)
        h_g = h_ref[0, g * SSM_HPG:(g + 1) * SSM_HPG].reshape(GROUP_W, D_STATE)
        y_off = lax.dot_general(cm_g, h_g.astype(BF16), NT, preferred_element_type=F32) * ec_x[:, gs]
        tiles = []
        for j in range(GROUP_W // LANES):
            col = g * GROUP_W + j * LANES
            x_pair = xdt_b[:, col:col + LANES]
            acc = None
            for half in range(2):
                h = col // SSM_HEAD_DIM + half
                seg = cum[:, h:h + 1] - cumT[h:h + 1, :]
                m = (cbm * jnp.exp(jnp.where(causal, seg, 0.0))).astype(BF16)
                x_h = jnp.where(low_half if half == 0 else jnp.logical_not(low_half), x_pair, jnp.zeros_like(x_pair))
                d = jnp.dot(m, x_h, preferred_element_type=F32)
                acc = d if acc is None else acc + d
            tiles.append(acc)
        y_g = jnp.concatenate(tiles, axis=1) + y_off + xs[:, gs] * dsk_ref[:, gs]
        zg = z_ref[:, gs].astype(F32)
        y_g = y_g * (zg * jax.nn.sigmoid(zg))
        y_g = y_g * lax.rsqrt(jnp.mean(y_g * y_g, axis=-1, keepdims=True) + EPS)
        y_ref[:, gs] = (y_g * nw_ref[:, gs]).astype(y_ref.dtype)
        st = lax.dot_general(xdte_b[:, gs], bm_g, TN, preferred_element_type=F32)
        for hh in range(SSM_HPG):
            h = g * SSM_HPG + hh
            rows = slice(hh * SSM_HEAD_DIM, (hh + 1) * SSM_HEAD_DIM)
            h_ref[0, h] = h_g[rows, :] * cd[h:h + 1, 0:1] + st[rows, :]


def _ssd_block_kernel(xbc_ref, z_ref, dt_ref, dtT_ref, h0_ref, *rest, spb, lc, n_consts, n_valid, conv):
    consts, rest = rest[:n_consts], rest[n_consts:]
    if conv == "history":
        hist_ref, cw_ref, cb_ref, y_ref, h_ref, scratch = rest
    else:
        cw_ref, cb_ref, y_ref, h_ref, scratch = rest
    for s in range(spb):
        rows, one = pl.ds(s * lc, lc), pl.ds(s, 1)
        conv_refs = (cw_ref, cb_ref)
        if conv == "history":
            conv_refs = (hist_ref.at[pl.ds(s * SUBLANES, SUBLANES)],) + conv_refs
        _ssd_kernel(xbc_ref.at[rows], z_ref.at[rows], dt_ref.at[rows], dtT_ref.at[one], h0_ref.at[one], *consts,
                    *conv_refs, y_ref.at[rows], h_ref.at[one], scratch,
                    lc=lc, n_valid=n_valid, conv=conv, single_chunk=True)


def _ssd(xbc, z, dt, dtT, h0, consts, nb, nc, lc, n_valid, conv_w, conv_b, hist=None, spb=1):
    assert spb == 1 or (nc == 1 and dtT.ndim == 3 and nb % spb == 0)
    t = xbc.shape[0]
    row = lambda n: pl.BlockSpec((spb * lc, n), lambda b, c: (b * nc + c, 0))
    if dtT.ndim == 2:
        dtT_spec = pl.BlockSpec((SSM_HEADS, lc), lambda b, c: (0, b * nc + c))
    else:
        dtT_spec = pl.BlockSpec((spb, SSM_HEADS, lc), lambda b, c: (b * nc + c, 0, 0))
    state_spec = pl.BlockSpec((spb, SSM_HEADS, SSM_HEAD_DIM, D_STATE), lambda b, c: (b, 0, 0, 0))
    in_specs = [row(CONV_DIM), row(D_INNER), row(SSM_HEADS), dtT_spec, state_spec] + [_const_spec(a.shape) for a in consts]
    args = [xbc, z, dt, dtT, h0, *consts]
    if hist is not None:
        in_specs.append(pl.BlockSpec((spb * SUBLANES, CONV_DIM), lambda b, c: (b, 0)))
        args.append(hist)
        scratch = [pltpu.VMEM((SUBLANES + lc, CONV_DIM), F32)]
    else:
        scratch = [pltpu.VMEM((PIECE, CONV_DIM), BF16)]
    in_specs += [_const_spec(conv_w.shape), _const_spec(conv_b.shape)]
    args += [conv_w, conv_b]
    conv = "history" if hist is not None else "fresh"
    if spb == 1:
        body = functools.partial(_ssd_kernel, lc=lc, n_valid=n_valid, conv=conv, single_chunk=nc == 1)
    else:
        body = functools.partial(_ssd_block_kernel, spb=spb, lc=lc, n_consts=len(consts), n_valid=n_valid, conv=conv)
    return pl.pallas_call(
        body,
        grid=(nb // spb, nc),
        in_specs=in_specs,
        out_specs=[row(D_INNER), state_spec],
        out_shape=[jax.ShapeDtypeStruct((t, D_INNER), BF16),
                   jax.ShapeDtypeStruct((nb, SSM_HEADS, SSM_HEAD_DIM, D_STATE), F32)],
        scratch_shapes=scratch,
        compiler_params=pltpu.CompilerParams(dimension_semantics=("arbitrary", "arbitrary"),
                                             vmem_limit_bytes=VMEM_LIMIT),
        name="ssd",
    )(*args)


def _swa_kernel(slope_ref, sink_ref, q_ref, kc_ref, vc_ref, kp_ref, vp_ref, y_ref, bias_ref, *, tq, prev_always):
    nk = WINDOW + tq

    @pl.when((pl.program_id(0) == 0) & (pl.program_id(1) == 0))
    def _():
        j = lax.broadcasted_iota(I32, (nk, tq), 0)
        r = lax.broadcasted_iota(I32, (nk, tq), 1)
        dist = r + WINDOW - j
        valid = (dist >= 0) & (dist < WINDOW)
        distf = dist.astype(F32)
        for h in range(N_Q_HEADS):
            penalty = -slope_ref[h] * distf
            bias_ref[h] = jnp.where(valid, penalty, NEG_BIG)
            if not prev_always:
                bias_ref[N_Q_HEADS + h] = jnp.where(valid & (j >= WINDOW), penalty, NEG_BIG)

    first = 0 if prev_always else jnp.where(pl.program_id(1) == 0, N_Q_HEADS, 0)
    lane = lax.broadcasted_iota(I32, (nk, LANES), 1)
    transposed_out = tq == WINDOW
    zeros_half = jnp.zeros((ATTN_HEAD_DIM, nk), BF16)
    sink_row = lax.broadcasted_iota(I32, (SUBLANES, tq), 0) == 0
    ones_keys = jnp.ones((nk + SUBLANES, LANES), BF16)
    heads = []
    for t in range(D_KV // LANES):
        cols = slice(t * LANES, (t + 1) * LANES)
        kt = jnp.concatenate([kp_ref[:, cols].astype(F32), kc_ref[:, cols].astype(F32)], axis=0)
        vt = jnp.concatenate([vp_ref[:, cols].astype(F32), vc_ref[:, cols].astype(F32)], axis=0)
        if transposed_out:
            vt_t = vt.T.astype(BF16)
        for b in range(2):
            mine = (lane >= ATTN_HEAD_DIM) if b else (lane < ATTN_HEAD_DIM)
            k_same = jnp.where(mine, kt, 0.0)
            k_half = {b: k_same.astype(BF16), 1 - b: pltpu.roll(k_same, ATTN_HEAD_DIM, 1).astype(BF16)}
            if transposed_out:
                v_g = vt_t[b * ATTN_HEAD_DIM:(b + 1) * ATTN_HEAD_DIM, :]
                v_half = {0: jnp.concatenate([v_g, zeros_half], axis=0),
                          1: jnp.concatenate([zeros_half, v_g], axis=0)}
            else:
                v_same = jnp.concatenate([jnp.where(mine, vt, 0.0), jnp.zeros((SUBLANES, LANES), F32)], axis=0)
                v_half = {b: v_same.astype(BF16), 1 - b: pltpu.roll(v_same, ATTN_HEAD_DIM, 1).astype(BF16)}
            for qi in range(Q_PER_KV):
                a = qi % 2
                heads.append((k_half[a], v_half[a]))

    q_tiles = [(q_ref[:, jq * LANES:(jq + 1) * LANES] * (ATTN_HEAD_DIM ** -0.5)).astype(BF16)
               for jq in range(D_ATTN // LANES)]

    def scores(h):
        return lax.dot_general(heads[h][0], q_tiles[h // 2], NT, preferred_element_type=F32) + bias_ref[first + h]

    def attend(h, s):
        sink = sink_ref[h]
        m = jnp.maximum(jnp.max(s, axis=0, keepdims=True), sink)
        e = jnp.exp(s - m)
        e_sink = jnp.exp(sink - m)
        if transposed_out:
            rden = 1.0 / (jnp.sum(e, axis=0, keepdims=True) + e_sink)
            return jnp.dot(heads[h][1], e.astype(BF16), preferred_element_type=F32) * rden
        p = jnp.concatenate([e, jnp.where(sink_row, e_sink, 0.0)], axis=0).astype(BF16)
        num = lax.dot_general(p, heads[h][1], TN, preferred_element_type=F32)
        den = lax.dot_general(p, ones_keys, TN, preferred_element_type=F32)
        return num / den

    out_tiles = [None] * (D_ATTN // LANES)
    s_next = scores(0)
    for h in range(N_Q_HEADS):
        s_cur = s_next
        if h + 1 < N_Q_HEADS:
            s_next = scores(h + 1)
        o = attend(h, s_cur)
        out_tiles[h // 2] = o if out_tiles[h // 2] is None else out_tiles[h // 2] + o
    for jq, o in enumerate(out_tiles):
        y_ref[:, jq * LANES:(jq + 1) * LANES] = (o.T if transposed_out else o).astype(y_ref.dtype)


def _swa(slopes, sinks, q, k, v, k_prev, v_prev, nb, nblk, tq, prev_always):
    t = q.shape[0]
    cur = lambda n: pl.BlockSpec((tq, n), lambda b, i: (b * nblk + i, 0))
    if prev_always:
        prev = pl.BlockSpec((WINDOW, D_KV), lambda b, i: (b, 0))
    else:
        prev = pl.BlockSpec((WINDOW, D_KV), lambda b, i: (b * nblk + jnp.maximum(i - 1, 0), 0))
    smem = pl.BlockSpec(memory_space=pltpu.SMEM)
    return pl.pallas_call(
        functools.partial(_swa_kernel, tq=tq, prev_always=prev_always),
        grid=(nb, nblk),
        in_specs=[smem, smem, cur(D_ATTN), cur(D_KV), cur(D_KV), prev, prev],
        out_specs=cur(D_ATTN),
        out_shape=jax.ShapeDtypeStruct((t, D_ATTN), BF16),
        scratch_shapes=[pltpu.VMEM(((1 if prev_always else 2) * N_Q_HEADS, WINDOW + tq, tq), F32)],
        compiler_params=pltpu.CompilerParams(dimension_semantics=("arbitrary", "arbitrary"),
                                             vmem_limit_bytes=VMEM_LIMIT),
        name="swa",
    )(slopes, sinks, q, k, v, k_prev, v_prev)


def _post_kernel(x_ref, ys_ref, ya_ref, g_ref, wsp_ref, wap_ref, wo_ref, nw_ref, wr_ref, br_ref, *rest,
                 n_exp, lr, nt, aliased):
    if aliased:
        rest = rest[1:]
    h_ref, xs_ref, lpos_ref, pr_ref, pc_ref = rest
    tp = x_ref.shape[0]

    @pl.when(pl.program_id(0) >= nt)
    def _():
        xs_ref[...] = jnp.zeros_like(xs_ref)

    @pl.when(pl.program_id(0) < nt)
    def _():
        a = jnp.dot(ys_ref[...].astype(BF16), wsp_ref[...], preferred_element_type=F32)
        b = jnp.dot(ya_ref[...].astype(BF16), wap_ref[...], preferred_element_type=F32)
        g = g_ref[...].astype(F32)
        merged = jax.nn.sigmoid(g[:, :D_MODEL]) * a + jax.nn.sigmoid(g[:, D_MODEL:]) * b
        h = x_ref[...] + jnp.dot(merged.astype(BF16), wo_ref[...], preferred_element_type=F32)
        h_ref[...] = h
        hn = h * lax.rsqrt(jnp.mean(h * h, axis=-1, keepdims=True) + EPS) * nw_ref[...]
        w_hi, w_mid, _ = _split3(wr_ref[...])
        x_hi, x_mid, _ = _split3(hn)
        logits = (lax.dot_general(w_hi, x_hi, NT, preferred_element_type=F32)
                  + lax.dot_general(w_hi, x_mid, NT, preferred_element_type=F32)
                  + lax.dot_general(w_mid, x_hi, NT, preferred_element_type=F32)) + br_ref[...]
        eidx = lax.broadcasted_iota(I32, logits.shape, 0).astype(F32)
        work = logits
        vals, ids = [], []
        for _ in range(TOP_K):
            m = jnp.max(work, axis=0, keepdims=True)
            first = jnp.min(jnp.where(work == m, eidx, float(n_exp)), axis=0, keepdims=True)
            vals.append(m)
            ids.append(first)
            work = jnp.where(eidx == first, -jnp.inf, work)
        es = [jnp.exp(v - vals[0]) for v in vals]
        den = es[0] + es[1] + es[2] + es[3]
        eye = jnp.where(lax.broadcasted_iota(I32, (TOP_K, TOP_K), 0) == lax.broadcasted_iota(I32, (TOP_K, TOP_K), 1),
                        1.0, 0.0).astype(BF16)

        def to_columns(rows):
            return sum(lax.dot_general(p, eye, TN, preferred_element_type=F32) for p in _split3(rows))

        onehot = [jnp.where(eidx == i, 1.0, 0.0) for i in ids]
        counts = [jnp.sum(o, axis=1, keepdims=True) for o in onehot]
        total = counts[0] + counts[1] + counts[2] + counts[3]
        padded = jnp.floor((total + (PIECE - 1)) * (1.0 / PIECE)) * PIECE
        ei = lax.broadcasted_iota(I32, (n_exp, n_exp), 0)
        ej = lax.broadcasted_iota(I32, (n_exp, n_exp), 1)
        below = jnp.where(ej < ei, 1.0, 0.0).astype(BF16)
        padded_b = jnp.broadcast_to(padded, (n_exp, LANES))
        seg_off = sum(jnp.dot(below, p, preferred_element_type=F32) for p in _split3(padded_b))[:, 0:1]
        ti = lax.broadcasted_iota(I32, (tp, tp), 0)
        tj = lax.broadcasted_iota(I32, (tp, tp), 1)
        before = jnp.where(ti < tj, 1.0, 0.0).astype(BF16)
        base = seg_off
        lpos = []
        for k in range(TOP_K):
            prefix = jnp.dot(onehot[k].astype(BF16), before, preferred_element_type=F32)
            lpos.append(jnp.sum(onehot[k] * (base + prefix), axis=0, keepdims=True))
            base = base + counts[k]
        pc_ref[0] = padded_b.astype(I32)

        hn_b = hn.astype(BF16)
        rc = lr // 4
        for c in range(4):
            ri = (lax.broadcasted_iota(I32, (rc, tp), 0) + c * rc).astype(F32)
            sel = jnp.zeros((rc, tp), F32)
            for k in range(TOP_K):
                sel = jnp.where(ri == lpos[k], 1.0, sel)
            xs_ref[c * rc:(c + 1) * rc, :] = jnp.dot(sel.astype(BF16), hn_b, preferred_element_type=F32).astype(BF16)

        pr_ref[...] = to_columns(jnp.concatenate([e / den for e in es], axis=0))
        lpos_ref[...] = to_columns(jnp.concatenate(lpos, axis=0)).astype(I32)


def _post(x, y_ssm, y_attn, gates, wsp, wap, wo, ffn_nw, w_rT, b_r, tm, lr, xs_rows, xs_block0, pad_steps,
          xs_prev=None):
    t = x.shape[0]
    n_exp = w_rT.shape[0]
    nt = t // tm
    last = nt - 1
    row = lambda n: pl.BlockSpec((tm, n), lambda i: (jnp.minimum(i, last), 0))
    col = row(TOP_K)
    in_specs = [row(D_MODEL), row(D_INNER), row(D_ATTN), row(2 * D_MODEL),
                _resident_spec(wsp.shape), _resident_spec(wap.shape), _resident_spec(wo.shape),
                _const_spec(ffn_nw.shape), _const_spec(w_rT.shape), _const_spec(b_r.shape)]
    args = [x, y_ssm, y_attn, gates, wsp, wap, wo, ffn_nw, w_rT, b_r]
    aliases = {}
    if xs_prev is not None:
        in_specs.append(pl.BlockSpec(memory_space=pl.ANY))
        args.append(xs_prev)
        aliases = {len(args) - 1: 1}
    return pl.pallas_call(
        functools.partial(_post_kernel, n_exp=n_exp, lr=lr, nt=nt, aliased=xs_prev is not None),
        grid=(nt + pad_steps,),
        in_specs=in_specs,
        out_specs=[row(D_MODEL), pl.BlockSpec((lr, D_MODEL), lambda i: (xs_block0 + i, 0)), col, col,
                   pl.BlockSpec((1, n_exp, LANES), lambda i: (jnp.minimum(i, last), 0, 0))],
        out_shape=[jax.ShapeDtypeStruct((t, D_MODEL), F32), jax.ShapeDtypeStruct((xs_rows, D_MODEL), BF16),
                   jax.ShapeDtypeStruct((t, TOP_K), I32), jax.ShapeDtypeStruct((t, TOP_K), F32),
                   jax.ShapeDtypeStruct((nt, n_exp, LANES), I32)],
        input_output_aliases=aliases,
        compiler_params=pltpu.CompilerParams(dimension_semantics=("arbitrary",), vmem_limit_bytes=VMEM_LIMIT),
        name="post",
    )(*args)


def _piece(ref, p):
    return ref.at[pl.ds(pl.multiple_of(p * PIECE, PIECE), PIECE)]


def _moe_kernel(te_ref, nu_ref, src0_ref, src1_ref, srcc_ref, xs_hbm, wg_ref, wu_ref, wd_ref,
                bg_ref, bu_ref, bd_ref, ys_hbm, xbuf, obuf, wbf, gsem, ssem):
    i = pl.program_id(0)
    n_used = nu_ref[0]
    slot = i % 2
    tme = PIECES_PER_TILE * PIECE

    def gather(src_ref, s):
        for r in range(PIECES_PER_TILE):
            pltpu.make_async_copy(_piece(xs_hbm, src_ref[0, 0, r]), xbuf.at[s, pl.ds(r * PIECE, PIECE)],
                                  gsem.at[s]).start()

    def wait_gather(s):
        pltpu.make_async_copy(xs_hbm.at[pl.ds(0, tme)], xbuf.at[s], gsem.at[s]).wait()

    def wait_put(s):
        pltpu.make_async_copy(obuf.at[s], ys_hbm.at[pl.ds(0, tme)], ssem.at[s]).wait()

    @pl.when(i == 0)
    def _():
        gather(src0_ref, 0)

    @pl.when(i < n_used)
    def _():
        gather(src1_ref, 1 - slot)
        wait_gather(slot)

        @pl.when(i >= 2)
        def _():
            wait_put(slot)

        @pl.when((i == 0) | (te_ref[i] != te_ref[jnp.maximum(i - 1, 0)]))
        def _():
            wbf[0] = wg_ref[0].astype(BF16)
            wbf[1] = wu_ref[0].astype(BF16)
            wbf[2] = wd_ref[0].astype(BF16)

        x = xbuf[slot]
        g = jnp.minimum(jnp.dot(x, wbf[0], preferred_element_type=F32) + bg_ref[0], SWIGLU_LIMIT)
        u = jnp.clip(jnp.dot(x, wbf[1], preferred_element_type=F32) + bu_ref[0], -SWIGLU_LIMIT, SWIGLU_LIMIT)
        act = ((u + 1.0) * g * jax.nn.sigmoid(SWIGLU_ALPHA * g)).astype(BF16)
        obuf[slot] = (jnp.dot(act, wbf[2], preferred_element_type=F32) + bd_ref[0]).astype(BF16)
        for r in range(PIECES_PER_TILE):
            pltpu.make_async_copy(obuf.at[slot, pl.ds(r * PIECE, PIECE)], _piece(ys_hbm, srcc_ref[0, 0, r]),
                                  ssem.at[slot]).start()

        @pl.when(i == n_used - 1)
        def _():
            wait_put(slot)
            wait_gather(1 - slot)

            @pl.when(i >= 1)
            def _():
                wait_put(1 - slot)


def _moe(tile_expert, n_used, src, xs, wg, wu, wd, bg, bu, bd):
    n_tiles = tile_expert.shape[0]
    idx = src.reshape(n_tiles, 1, PIECES_PER_TILE)
    tme = PIECES_PER_TILE * PIECE
    wspec = pl.BlockSpec((1, D_MODEL, D_FF), lambda i, te, nu: (te[i], 0, 0))
    bspec = pl.BlockSpec((1, 1, D_FF), lambda i, te, nu: (te[i], 0, 0))
    ispec = lambda f: pl.BlockSpec((1, 1, PIECES_PER_TILE), lambda i, te, nu: (f(i), 0, 0),
                                   memory_space=pltpu.SMEM)
    grid_spec = pltpu.PrefetchScalarGridSpec(
        num_scalar_prefetch=2,
        grid=(n_tiles,),
        in_specs=[ispec(lambda i: 0), ispec(lambda i: jnp.minimum(i + 1, n_tiles - 1)), ispec(lambda i: i),
                  pl.BlockSpec(memory_space=pl.ANY), wspec, wspec, wspec, bspec, bspec, bspec],
        out_specs=pl.BlockSpec(memory_space=pl.ANY),
        scratch_shapes=[pltpu.VMEM((2, tme, D_MODEL), BF16), pltpu.VMEM((2, tme, D_MODEL), BF16),
                        pltpu.VMEM((3, D_MODEL, D_FF), BF16),
                        pltpu.SemaphoreType.DMA((2,)), pltpu.SemaphoreType.DMA((2,))],
    )
    return pl.pallas_call(
        _moe_kernel,
        grid_spec=grid_spec,
        out_shape=jax.ShapeDtypeStruct(xs.shape, xs.dtype),
        input_output_aliases={5: 0},
        compiler_params=pltpu.CompilerParams(dimension_semantics=("arbitrary",), vmem_limit_bytes=VMEM_LIMIT),
        name="moe",
    )(tile_expert, n_used, idx, idx, idx, xs, wg, wu, wd, bg, bu, bd)


def _combine_kernel(h_ref, ys_ref, lpos_ref, pr_ref, nw_ref, o_ref):
    tp = h_ref.shape[0]
    lr = ys_ref.shape[0]
    ri = lax.broadcasted_iota(I32, (tp, lr), 1)
    lp = lpos_ref[...]
    pr = pr_ref[...]
    pw = jnp.zeros((tp, lr), F32)
    for k in range(TOP_K):
        pw = jnp.where(ri == lp[:, k:k + 1], pr[:, k:k + 1], pw)
    moe = jnp.dot(pw.astype(BF16), ys_ref[...], preferred_element_type=F32)
    h = h_ref[...] + moe
    o_ref[...] = h * lax.rsqrt(jnp.mean(h * h, axis=-1, keepdims=True) + EPS) * nw_ref[...]


def _combine(h, ys, lpos_t, probs_t, final_nw, tm, lr, ys_block0):
    t = h.shape[0]
    return pl.pallas_call(
        _combine_kernel,
        grid=(t // tm,),
        in_specs=[pl.BlockSpec((tm, D_MODEL), lambda i: (i, 0)),
                  pl.BlockSpec((lr, D_MODEL), lambda i: (ys_block0 + i, 0)),
                  pl.BlockSpec((tm, TOP_K), lambda i: (i, 0)),
                  pl.BlockSpec((tm, TOP_K), lambda i: (i, 0)),
                  _const_spec(final_nw.shape)],
        out_specs=pl.BlockSpec((tm, D_MODEL), lambda i: (i, 0)),
        out_shape=jax.ShapeDtypeStruct((t, D_MODEL), F32),
        compiler_params=pltpu.CompilerParams(dimension_semantics=("arbitrary",), vmem_limit_bytes=VMEM_LIMIT),
        name="combine",
    )(h, ys, lpos_t, probs_t, final_nw)


def _piece_table(padded_counts, tile_row0, n_tiles, spare_piece0):
    n_pieces = (padded_counts // PIECE).T
    seg_row = tile_row0[:, None] + jnp.cumsum(padded_counts, axis=1) - padded_counts
    seg_piece = (seg_row // PIECE).T
    per_expert = n_pieces.sum(axis=1)
    tiles_per = (per_expert + PIECES_PER_TILE - 1) // PIECES_PER_TILE
    tile_end = jnp.cumsum(tiles_per)
    n_used = tile_end[-1]
    slot0 = (tile_end - tiles_per) * PIECES_PER_TILE
    seg_slot = (slot0[:, None] + jnp.cumsum(n_pieces, axis=1) - n_pieces).reshape(-1)
    seg_n = n_pieces.reshape(-1)
    seg_src = seg_piece.reshape(-1)
    slots = jnp.arange(n_tiles * PIECES_PER_TILE, dtype=I32)

    def at_segment_of_slot(f):
        df = f - jnp.concatenate([jnp.zeros((1,), I32), f[:-1]])
        return jnp.sum(jnp.where(seg_slot[None, :] <= slots[:, None], df[None, :], 0), axis=1)

    real = slots < at_segment_of_slot(seg_slot + seg_n)
    padding = jnp.logical_not(real) & (slots < n_used * PIECES_PER_TILE)
    spare = spare_piece0 + jnp.where(padding, jnp.cumsum(padding.astype(I32)), 0)
    src = jnp.where(real, slots + at_segment_of_slot(seg_src - seg_slot), spare).astype(I32)
    tile_ids = jnp.arange(n_tiles, dtype=I32)
    tile_expert = jnp.sum((tile_end[None, :] <= jnp.minimum(tile_ids, n_used - 1)[:, None]).astype(I32), axis=1)
    return tile_expert.astype(I32), n_used.reshape(1).astype(I32), src


def _pick_tile(n, pref):
    while n % pref:
        pref //= 2
    return pref


def _local_rows(tm, n_exp):
    need = TOP_K * tm + n_exp * (PIECE - 1)
    return -(-need // 64) * 64


def kernel(x_prompt, x_sample, state_conv, state_ssm, cache_swa_k, cache_swa_v, attn_norm_w, w_in, conv_w, conv_b, dt_bias, a_log, d_skip, ssm_norm_w, attn_sinks, w_ssm_proj, w_attn_proj, w_o, ffn_norm_w, w_router, b_router, w_gate, b_gate, w_up, b_up, w_down, b_down, final_norm_w):
    assert w_in.shape[0] == 1, "single-layer step"
    nb, seq, _ = x_prompt.shape
    nbs = x_sample.shape[0]
    n_exp = w_router.shape[-1]
    tp, ts = nb * seq, nbs
    pad = SUBLANES

    cuts = [0]
    for n in (D_INNER, CONV_DIM, SSM_HEADS, D_ATTN, D_KV, D_KV, 2 * D_MODEL):
        cuts.append(cuts[-1] + n)
    w_in_b = w_in[0].astype(BF16)
    ws = [w_in_b[:, cuts[i]:cuts[i + 1]] for i in range(7)]
    attn_nw = attn_norm_w[0].reshape(1, D_MODEL)
    a_neg = -jnp.exp(a_log[0].astype(F32))
    head_of = jnp.arange(D_INNER, dtype=I32) // SSM_HEAD_DIM
    expand = jnp.tile((jnp.arange(SSM_HEADS, dtype=I32)[:, None] == head_of[None, :]).astype(BF16), (2, 1))
    conv_consts = (conv_w[0], conv_b[0].reshape(1, CONV_DIM))
    ssd_consts = (dt_bias[0].reshape(1, SSM_HEADS), dt_bias[0].reshape(SSM_HEADS, 1),
                  a_neg.reshape(1, SSM_HEADS), a_neg.reshape(SSM_HEADS, 1),
                  d_skip[0][head_of].reshape(1, D_INNER), ssm_norm_w[0].reshape(1, D_INNER), expand)
    slopes = jnp.exp2(-8.0 * jnp.arange(1, N_Q_HEADS + 1, dtype=F32) / N_Q_HEADS)
    sinks = attn_sinks[0].astype(F32)
    wsp, wap, wo = w_ssm_proj[0].astype(BF16), w_attn_proj[0].astype(BF16), w_o[0].astype(BF16)
    ffn_nw = ffn_norm_w[0].reshape(1, D_MODEL)
    w_rT = w_router[0].T
    b_r = b_router[0].reshape(n_exp, 1)
    bg, bu, bd = (b[0].reshape(n_exp, 1, -1) for b in (b_gate, b_up, b_down))
    final_nw = final_norm_w.reshape(1, D_MODEL)

    xp = x_prompt.reshape(tp, D_MODEL)
    tm_in = _pick_tile(seq, 512)
    z, xbc, dt, q, k, v, gates, conv_tail, dtT, k_tail, v_tail = _in_proj(
        xp, attn_nw, ws, tm_in, BF16, tiles_per_seq=seq // tm_in)
    nc = seq // CHUNK
    y_ssm, ssm_p = _ssd(xbc, z, dt, dtT, jnp.zeros((nb, SSM_HEADS, SSM_HEAD_DIM, D_STATE), F32), ssd_consts,
                        nb, nc, CHUNK, CHUNK, *conv_consts)
    nblk = seq // WINDOW
    y_attn = _swa(slopes, sinks, q, k, v, k, v, nb, nblk, WINDOW, False)

    xs_pad = jnp.pad(x_sample.reshape(ts, 1, D_MODEL), ((0, 0), (0, pad - 1), (0, 0))).reshape(ts * pad, D_MODEL)
    z_s, xbc_s, dt_s, q_s, k_s, v_s, gates_s = _in_proj(xs_pad, attn_nw, ws, _pick_tile(ts * pad, 256), F32)
    dtT_s = dt_s.reshape(ts, pad, SSM_HEADS).transpose(0, 2, 1)
    spb = _pick_tile(ts, SAMPLE_SEQS_PER_STEP)
    hist_s = jnp.pad(state_conv[0], ((0, 0), (pad - (CONV_W - 1), 0), (0, 0))).reshape(ts * pad, CONV_DIM)
    y_ssm_s, ssm_s = _ssd(xbc_s, z_s, dt_s, dtT_s, state_ssm[0], ssd_consts, ts, 1, pad, 1, *conv_consts,
                          hist=hist_s, spb=spb)
    kc = cache_swa_k[0].reshape(ts * WINDOW, D_KV)
    vc = cache_swa_v[0].reshape(ts * WINDOW, D_KV)
    y_attn_s = _swa(slopes, sinks, q_s, k_s, v_s, kc, vc, ts, 1, pad, True)
    real = lambda a: a.reshape(ts, pad, -1)[:, 0]

    tm_p, tm_s = _pick_tile(tp, 512), ts
    nt_p = tp // tm_p
    lr_p, lr_s = _local_rows(tm_p, n_exp), _local_rows(tm_s, n_exp)
    if (nt_p * lr_p) % lr_s or lr_s > lr_p:
        lr_s = lr_p
    spare_rows = lr_s + (1 + n_exp * (PIECES_PER_TILE - 1)) * PIECE
    pad_steps = -(-spare_rows // lr_p)
    xs_rows = (nt_p + pad_steps) * lr_p
    block_s = nt_p * lr_p // lr_s
    post_w = (wsp, wap, wo, ffn_nw, w_rT, b_r)
    h_p, xs, lpos_p, pr_p, pc_p = _post(xp, y_ssm, y_attn, gates, *post_w, tm_p, lr_p, xs_rows, 0, pad_steps)
    h_s, xs, lpos_s, pr_s, pc_s = _post(x_sample.reshape(ts, D_MODEL), real(y_ssm_s), real(y_attn_s), real(gates_s),
                                        *post_w, tm_s, lr_s, xs_rows, block_s, 0, xs_prev=xs)

    padded_counts = jnp.concatenate([pc_p[:, :, 0], pc_s[:, :, 0]], axis=0)
    tile_row0 = jnp.concatenate([jnp.arange(nt_p, dtype=I32) * lr_p, jnp.full((1,), nt_p * lr_p, I32)])
    max_pieces = (TOP_K * (tp + ts) + (PIECE - 1) * n_exp * (nt_p + 1)) // PIECE + n_exp * (PIECES_PER_TILE - 1)
    n_tiles = -(-max_pieces // PIECES_PER_TILE) + 1
    tile_expert, n_used, src = _piece_table(padded_counts, tile_row0, n_tiles, (nt_p * lr_p + lr_s) // PIECE)
    ys = _moe(tile_expert, n_used, src, xs, w_gate[0], w_up[0], w_down[0], bg, bu, bd)
    out_p = _combine(h_p, ys, lpos_p, pr_p, final_nw, tm_p, lr_p, 0)
    out_s = _combine(h_s, ys, lpos_s, pr_s, final_nw, tm_s, lr_s, block_s)

    y_prompt = out_p.reshape(nb, seq, D_MODEL)
    y_sample = out_s.reshape(nbs, 1, D_MODEL)
    conv_p = conv_tail[:, SUBLANES - (CONV_W - 1):][None]
    k_p = k_tail.reshape(1, nb, WINDOW, N_KV_HEADS, ATTN_HEAD_DIM)
    v_p = v_tail.reshape(1, nb, WINDOW, N_KV_HEADS, ATTN_HEAD_DIM)
    conv_s = jnp.concatenate([state_conv[0][:, 1:], real(xbc_s)[:, None]], axis=1)[None]
    k_new = real(k_s).reshape(ts, 1, N_KV_HEADS, ATTN_HEAD_DIM)
    v_new = real(v_s).reshape(ts, 1, N_KV_HEADS, ATTN_HEAD_DIM)
    ks_out = jnp.concatenate([cache_swa_k[0][:, 1:], k_new], axis=1)[None]
    vs_out = jnp.concatenate([cache_swa_v[0][:, 1:], v_new], axis=1)[None]
    return (y_prompt, y_sample, conv_p, ssm_p[None], k_p, v_p, conv_s, ssm_s[None], ks_out, vs_out)
```

```python
import functools

import jax
import jax.numpy as jnp
from jax import lax
from jax.experimental import pallas as pl
from jax.experimental.pallas import tpu as pltpu

F32, BF16, I32 = jnp.float32, jnp.bfloat16, jnp.int32

D_MODEL = 1024
D_INNER = 2 * D_MODEL
SSM_HEAD_DIM = 64
SSM_HEADS = D_INNER // SSM_HEAD_DIM
SSM_GROUPS = 4
SSM_HPG = SSM_HEADS // SSM_GROUPS
D_STATE = 128
CONV_W = 4
CONV_DIM = D_INNER + 2 * SSM_GROUPS * D_STATE
CHUNK = 128
ATTN_HEAD_DIM = 64
N_Q_HEADS = D_MODEL // ATTN_HEAD_DIM
N_KV_HEADS = 4
Q_PER_KV = N_Q_HEADS // N_KV_HEADS
D_ATTN = N_Q_HEADS * ATTN_HEAD_DIM
D_KV = N_KV_HEADS * ATTN_HEAD_DIM
WINDOW = 128
TOP_K = 4
D_FF = D_MODEL
SWIGLU_LIMIT = 7.0
SWIGLU_ALPHA = 1.702
EPS = 1e-5
NEG_BIG = -1e30

LANES = 128
SUBLANES = 8
GROUP_W = D_INNER // SSM_GROUPS
PIECE = 2 * SUBLANES
PIECES_PER_TILE = 32
SAMPLE_SEQS_PER_STEP = 4
VMEM_LIMIT = 56 * 1024 * 1024

NT = (((1,), (1,)), ((), ()))
TN = (((0,), (0,)), ((), ()))


def _const_spec(shape):
    return pl.BlockSpec(shape, lambda *_: (0,) * len(shape))


def _resident_spec(shape):
    return pl.BlockSpec(shape, lambda *_: (0,) * len(shape), pipeline_mode=pl.Buffered(1))


def _split3(x):
    hi = x.astype(BF16)
    r1 = x - hi.astype(F32)
    mid = r1.astype(BF16)
    lo = (r1 - mid.astype(F32)).astype(BF16)
    return hi, mid, lo


def _softplus(x):
    return jnp.maximum(x, 0.0) + jnp.log(1.0 + jnp.exp(-jnp.abs(x)))


def _causal_conv_silu(buf_ref, n, cw_ref, cb_ref):
    full = buf_ref[...]
    conv = cb_ref[...]
    for j in range(CONV_W):
        shifted = full if j == CONV_W - 1 else pltpu.roll(full, CONV_W - 1 - j, 0)
        conv = conv + shifted[SUBLANES:SUBLANES + n, :] * cw_ref[j:j + 1, :]
    buf_ref[0:SUBLANES, :] = full[n:n + SUBLANES, :]
    return conv * jax.nn.sigmoid(conv)


def _inproj_kernel(x_ref, nw_ref, wz_ref, wxbc_ref, wdt_ref, wq_ref, wk_ref, wv_ref, wg_ref, *rest, tiles_per_seq):
    if tiles_per_seq:
        wdtT_ref, z_ref, xbc_ref, dt_ref, q_ref, k_ref, v_ref, g_ref, tail_ref, dtT_ref, ktail_ref, vtail_ref = rest
    else:
        z_ref, xbc_ref, dt_ref, q_ref, k_ref, v_ref, g_ref = rest
    tm = x_ref.shape[0]
    x = x_ref[...]
    xn = x * lax.rsqrt(jnp.mean(x * x, axis=-1, keepdims=True) + EPS)
    xn = (xn * nw_ref[...]).astype(BF16)
    for w_ref, o_ref in ((wz_ref, z_ref), (wdt_ref, dt_ref), (wq_ref, q_ref), (wg_ref, g_ref)):
        o_ref[...] = jnp.dot(xn, w_ref[...], preferred_element_type=F32).astype(o_ref.dtype)
    xbc = jnp.dot(xn, wxbc_ref[...], preferred_element_type=F32)
    k = jnp.dot(xn, wk_ref[...], preferred_element_type=F32)
    v = jnp.dot(xn, wv_ref[...], preferred_element_type=F32)
    xbc_ref[...] = xbc.astype(xbc_ref.dtype)
    k_ref[...] = k.astype(k_ref.dtype)
    v_ref[...] = v.astype(v_ref.dtype)
    if tiles_per_seq:
        dtT_ref[...] = lax.dot_general(wdtT_ref[...], xn, NT, preferred_element_type=F32)

        @pl.when(pl.program_id(0) % tiles_per_seq == tiles_per_seq - 1)
        def _():
            tail_ref[0] = xbc[tm - SUBLANES:, :]
            ktail_ref[0] = k[tm - WINDOW:, :]
            vtail_ref[0] = v[tm - WINDOW:, :]


IN_PROJ_WIDTHS = (D_INNER, CONV_DIM, SSM_HEADS, D_ATTN, D_KV, D_KV, 2 * D_MODEL)
IN_PROJ_PACK_ORDER = (1, 3, 0, 6, 4, 5, 2)


def _pack_in_proj_weight(w):
    cuts = [0]
    for n in IN_PROJ_WIDTHS:
        cuts.append(cuts[-1] + n)
    packed = jnp.concatenate([w[:, cuts[i]:cuts[i + 1]] for i in IN_PROJ_PACK_ORDER], axis=1).astype(BF16)
    block, off = [0] * len(IN_PROJ_WIDTHS), 0
    for i in IN_PROJ_PACK_ORDER:
        block[i], rem = divmod(off, IN_PROJ_WIDTHS[i])
        assert rem == 0
        off += IN_PROJ_WIDTHS[i]
    return packed, tuple(block)


def _in_proj(x, norm_w, w_packed, w_blocks, tm, act_dtype, tiles_per_seq=None):
    t = x.shape[0]
    widths = IN_PROJ_WIDTHS
    dtypes = (act_dtype, act_dtype, F32, act_dtype, act_dtype, act_dtype, act_dtype)
    row = lambda n: pl.BlockSpec((tm, n), lambda i: (i, 0))
    w_spec = lambda n, blk: pl.BlockSpec((D_MODEL, n), lambda i: (0, blk), pipeline_mode=pl.Buffered(1))
    dt_col = w_blocks[2] * SSM_HEADS
    w_dt = w_packed[:, dt_col:dt_col + SSM_HEADS]
    in_specs = [row(D_MODEL), _const_spec((1, D_MODEL))]
    args = [x, norm_w]
    for i, (n, blk) in enumerate(zip(widths, w_blocks)):
        in_specs.append(_resident_spec((D_MODEL, n)) if i == 2 else w_spec(n, blk))
        args.append(w_dt if i == 2 else w_packed)
    out_specs = [row(n) for n in widths]
    out_shape = [jax.ShapeDtypeStruct((t, n), d) for n, d in zip(widths, dtypes)]
    if tiles_per_seq:
        n_seq = t // (tm * tiles_per_seq)
        in_specs.append(_const_spec((SSM_HEADS, D_MODEL)))
        args.append(w_dt.T)
        per_seq = lambda r, c: pl.BlockSpec((1, r, c), lambda i: (i // tiles_per_seq, 0, 0))
        out_specs += [per_seq(SUBLANES, CONV_DIM), pl.BlockSpec((SSM_HEADS, tm), lambda i: (0, i)),
                      per_seq(WINDOW, D_KV), per_seq(WINDOW, D_KV)]
        out_shape += [jax.ShapeDtypeStruct((n_seq, SUBLANES, CONV_DIM), F32),
                      jax.ShapeDtypeStruct((SSM_HEADS, t), F32),
                      jax.ShapeDtypeStruct((n_seq, WINDOW, D_KV), F32),
                      jax.ShapeDtypeStruct((n_seq, WINDOW, D_KV), F32)]
    return pl.pallas_call(
        functools.partial(_inproj_kernel, tiles_per_seq=tiles_per_seq),
        grid=(t // tm,),
        in_specs=in_specs,
        out_specs=out_specs,
        out_shape=out_shape,
        compiler_params=pltpu.CompilerParams(dimension_semantics=("arbitrary",), vmem_limit_bytes=VMEM_LIMIT),
        name="in_proj",
    )(*args)


def _ssd_kernel(xbc_ref, z_ref, dt_ref, dtT_ref, h0_ref, dtb_ref, dtbT_ref, a_ref, aT_ref, dsk_ref, nw_ref, e_ref,
                *rest, lc, n_valid, conv, single_chunk):
    if conv == "history":
        hist_ref, cw_ref, cb_ref, y_ref, h_ref, buf_ref = rest
    else:
        cw_ref, cb_ref, y_ref, h_ref, tail_ref = rest

    def at_first_chunk(fn):
        if single_chunk:
            fn()
        else:
            pl.when(pl.program_id(1) == 0)(fn)

    @at_first_chunk
    def _():
        h_ref[...] = h0_ref[...]

    if conv == "history":
        @at_first_chunk
        def _():
            buf_ref[0:SUBLANES, :] = hist_ref[...]

        buf_ref[SUBLANES:SUBLANES + lc, :] = xbc_ref[...].astype(F32)
        act = _causal_conv_silu(buf_ref, lc, cw_ref, cb_ref)
    else:
        @at_first_chunk
        def _():
            tail_ref[...] = jnp.zeros_like(tail_ref)

        raw = xbc_ref[...]
        nt = tail_ref.shape[0]
        ext = jnp.concatenate([tail_ref[...], raw], axis=0)
        to = lax.broadcasted_iota(I32, (lc, nt + lc), 0)
        frm = lax.broadcasted_iota(I32, (lc, nt + lc), 1)
        conv_acc = cb_ref[...]
        for j in range(CONV_W - 1):
            shift = jnp.where(frm == to + (nt - (CONV_W - 1) + j), 1.0, 0.0).astype(BF16)
            conv_acc = conv_acc + jnp.dot(shift, ext, preferred_element_type=F32) * cw_ref[j:j + 1, :]
        conv_acc = conv_acc + raw.astype(F32) * cw_ref[CONV_W - 1:CONV_W, :]
        tail_ref[...] = raw[lc - nt:, :]
        act = conv_acc * jax.nn.sigmoid(conv_acc)
    xs = act[:, :D_INNER]
    bm = act[:, D_INNER:D_INNER + SSM_GROUPS * D_STATE].astype(BF16)
    cm = act[:, D_INNER + SSM_GROUPS * D_STATE:].astype(BF16)

    dt = _softplus(dt_ref[...] + dtb_ref[...])
    dtT_raw = dtT_ref[0] if len(dtT_ref.shape) == 3 else dtT_ref[...]
    dtT = _softplus(dtT_raw + dtbT_ref[...])
    if n_valid < lc:
        dt = jnp.where(lax.broadcasted_iota(I32, dt.shape, 0) < n_valid, dt, 0.0)
        dtT = jnp.where(lax.broadcasted_iota(I32, dtT.shape, 1) < n_valid, dtT, 0.0)
    la = dt * a_ref[...]
    laT = dtT * aT_ref[...]
    li = lax.broadcasted_iota(I32, (lc, lc), 0)
    si = lax.broadcasted_iota(I32, (lc, lc), 1)
    causal = li >= si
    tril = jnp.where(causal, 1.0, 0.0).astype(BF16)
    triu = jnp.where(li <= si, 1.0, 0.0).astype(BF16)
    cum = sum(jnp.dot(tril, p, preferred_element_type=F32) for p in _split3(la))
    cumT = sum(jnp.dot(p, triu, preferred_element_type=F32) for p in _split3(laT))
    ec = jnp.exp(cum)
    dte = jnp.exp(cum[lc - 1:lc, :] - cum)
    cd = jnp.exp(cumT[:, lc - 1:lc])

    def expand(v):
        hi, mid, _ = _split3(v)
        return jnp.dot(jnp.concatenate([hi, mid], axis=1), e_ref[...], preferred_element_type=F32)

    dt_x, ec_x, dte_x = expand(dt), expand(ec), expand(dte)
    xdt = xs * dt_x
    xdt_b = xdt.astype(BF16)
    xdte_b = (xdt * dte_x).astype(BF16)
    lane = lax.broadcasted_iota(I32, (lc, LANES), 1)
    low_half = lane < SSM_HEAD_DIM

    for g in range(SSM_GROUPS):
        gs = slice(g * GROUP_W, (g + 1) * GROUP_W)
        bm_g = bm[:, g * D_STATE:(g + 1) * D_STATE]
        cm_g = cm[:, g * D_STATE:(g + 1) * D_STATE]
        cb = lax.dot_general(cm_g, bm_g, NT, preferred_element_type=F32)
        cbm = jnp.where(causal, cb, 0.0)
        h_g = h_ref[0, g * SSM_HPG:(g + 1) * SSM_HPG].reshape(GROUP_W, D_STATE)
        y_off = lax.dot_general(cm_g, h_g.astype(BF16), NT, preferred_element_type=F32) * ec_x[:, gs]
        tiles = []
        for j in range(GROUP_W // LANES):
            col = g * GROUP_W + j * LANES
            x_pair = xdt_b[:, col:col + LANES]
            acc = None
            for half in range(2):
                h = col // SSM_HEAD_DIM + half
                seg = cum[:, h:h + 1] - cumT[h:h + 1, :]
                m = (cbm * jnp.exp(jnp.where(causal, seg, 0.0))).astype(BF16)
                x_h = jnp.where(low_half if half == 0 else jnp.logical_not(low_half), x_pair, jnp.zeros_like(x_pair))
                d = jnp.dot(m, x_h, preferred_element_type=F32)
                acc = d if acc is None else acc + d
            tiles.append(acc)
        y_g = jnp.concatenate(tiles, axis=1) + y_off + xs[:, gs] * dsk_ref[:, gs]
        zg = z_ref[:, gs].astype(F32)
        y_g = y_g * (zg * jax.nn.sigmoid(zg))
        y_g = y_g * lax.rsqrt(jnp.mean(y_g * y_g, axis=-1, keepdims=True) + EPS)
        y_ref[:, gs] = (y_g * nw_ref[:, gs]).astype(y_ref.dtype)
        st = lax.dot_general(xdte_b[:, gs], bm_g, TN, preferred_element_type=F32)
        for hh in range(SSM_HPG):
            h = g * SSM_HPG + hh
            rows = slice(hh * SSM_HEAD_DIM, (hh + 1) * SSM_HEAD_DIM)
            h_ref[0, h] = h_g[rows, :] * cd[h:h + 1, 0:1] + st[rows, :]


def _ssd_block_kernel(xbc_ref, z_ref, dt_ref, dtT_ref, h0_ref, *rest, spb, lc, n_consts, n_valid, conv):
    consts, rest = rest[:n_consts], rest[n_consts:]
    if conv == "history":
        hist_ref, cw_ref, cb_ref, y_ref, h_ref, scratch = rest
    else:
        cw_ref, cb_ref, y_ref, h_ref, scratch = rest
    for s in range(spb):
        rows, one = pl.ds(s * lc, lc), pl.ds(s, 1)
        conv_refs = (cw_ref, cb_ref)
        if conv == "history":
            conv_refs = (hist_ref.at[pl.ds(s * SUBLANES, SUBLANES)],) + conv_refs
        _ssd_kernel(xbc_ref.at[rows], z_ref.at[rows], dt_ref.at[rows], dtT_ref.at[one], h0_ref.at[one], *consts,
                    *conv_refs, y_ref.at[rows], h_ref.at[one], scratch,
                    lc=lc, n_valid=n_valid, conv=conv, single_chunk=True)


def _ssd(xbc, z, dt, dtT, h0, consts, nb, nc, lc, n_valid, conv_w, conv_b, hist=None, spb=1):
    assert spb == 1 or (nc == 1 and dtT.ndim == 3 and nb % spb == 0)
    t = xbc.shape[0]
    row = lambda n: pl.BlockSpec((spb * lc, n), lambda b, c: (b * nc + c, 0))
    if dtT.ndim == 2:
        dtT_spec = pl.BlockSpec((SSM_HEADS, lc), lambda b, c: (0, b * nc + c))
    else:
        dtT_spec = pl.BlockSpec((spb, SSM_HEADS, lc), lambda b, c: (b * nc + c, 0, 0))
    state_spec = pl.BlockSpec((spb, SSM_HEADS, SSM_HEAD_DIM, D_STATE), lambda b, c: (b, 0, 0, 0))
    in_specs = [row(CONV_DIM), row(D_INNER), row(SSM_HEADS), dtT_spec, state_spec] + [_const_spec(a.shape) for a in consts]
    args = [xbc, z, dt, dtT, h0, *consts]
    if hist is not None:
        in_specs.append(pl.BlockSpec((spb * SUBLANES, CONV_DIM), lambda b, c: (b, 0)))
        args.append(hist)
        scratch = [pltpu.VMEM((SUBLANES + lc, CONV_DIM), F32)]
    else:
        scratch = [pltpu.VMEM((PIECE, CONV_DIM), BF16)]
    in_specs += [_const_spec(conv_w.shape), _const_spec(conv_b.shape)]
    args += [conv_w, conv_b]
    conv = "history" if hist is not None else "fresh"
    if spb == 1:
        body = functools.partial(_ssd_kernel, lc=lc, n_valid=n_valid, conv=conv, single_chunk=nc == 1)
    else:
        body = functools.partial(_ssd_block_kernel, spb=spb, lc=lc, n_consts=len(consts), n_valid=n_valid, conv=conv)
    return pl.pallas_call(
        body,
        grid=(nb // spb, nc),
        in_specs=in_specs,
        out_specs=[row(D_INNER), state_spec],
        out_shape=[jax.ShapeDtypeStruct((t, D_INNER), BF16),
                   jax.ShapeDtypeStruct((nb, SSM_HEADS, SSM_HEAD_DIM, D_STATE), F32)],
        scratch_shapes=scratch,
        compiler_params=pltpu.CompilerParams(dimension_semantics=("arbitrary", "arbitrary"),
                                             vmem_limit_bytes=VMEM_LIMIT),
        name="ssd",
    )(*args)


def _swa_kernel(slope_ref, sink_ref, q_ref, kc_ref, vc_ref, kp_ref, vp_ref, y_ref, bias_ref, *, tq, prev_always):
    nk = WINDOW + tq

    @pl.when((pl.program_id(0) == 0) & (pl.program_id(1) == 0))
    def _():
        j = lax.broadcasted_iota(I32, (nk, tq), 0)
        r = lax.broadcasted_iota(I32, (nk, tq), 1)
        dist = r + WINDOW - j
        valid = (dist >= 0) & (dist < WINDOW)
        distf = dist.astype(F32)
        for h in range(N_Q_HEADS):
            penalty = -slope_ref[h] * distf
            bias_ref[h] = jnp.where(valid, penalty, NEG_BIG)
            if not prev_always:
                bias_ref[N_Q_HEADS + h] = jnp.where(valid & (j >= WINDOW), penalty, NEG_BIG)

    first = 0 if prev_always else jnp.where(pl.program_id(1) == 0, N_Q_HEADS, 0)
    lane = lax.broadcasted_iota(I32, (nk, LANES), 1)
    transposed_out = tq == WINDOW
    zeros_half = jnp.zeros((ATTN_HEAD_DIM, nk), BF16)
    sink_row = lax.broadcasted_iota(I32, (SUBLANES, tq), 0) == 0
    ones_keys = jnp.ones((nk + SUBLANES, LANES), BF16)
    heads = []
    for t in range(D_KV // LANES):
        cols = slice(t * LANES, (t + 1) * LANES)
        kt = jnp.concatenate([kp_ref[:, cols].astype(F32), kc_ref[:, cols].astype(F32)], axis=0)
        vt = jnp.concatenate([vp_ref[:, cols].astype(F32), vc_ref[:, cols].astype(F32)], axis=0)
        if transposed_out:
            vt_t = vt.T.astype(BF16)
        for b in range(2):
            mine = (lane >= ATTN_HEAD_DIM) if b else (lane < ATTN_HEAD_DIM)
            k_same = jnp.where(mine, kt, 0.0)
            k_half = {b: k_same.astype(BF16), 1 - b: pltpu.roll(k_same, ATTN_HEAD_DIM, 1).astype(BF16)}
            if transposed_out:
                v_g = vt_t[b * ATTN_HEAD_DIM:(b + 1) * ATTN_HEAD_DIM, :]
                v_half = {0: jnp.concatenate([v_g, zeros_half], axis=0),
                          1: jnp.concatenate([zeros_half, v_g], axis=0)}
            else:
                v_same = jnp.concatenate([jnp.where(mine, vt, 0.0), jnp.zeros((SUBLANES, LANES), F32)], axis=0)
                v_half = {b: v_same.astype(BF16), 1 - b: pltpu.roll(v_same, ATTN_HEAD_DIM, 1).astype(BF16)}
            for qi in range(Q_PER_KV):
                a = qi % 2
                heads.append((k_half[a], v_half[a]))

    q_tiles = [(q_ref[:, jq * LANES:(jq + 1) * LANES] * (ATTN_HEAD_DIM ** -0.5)).astype(BF16)
               for jq in range(D_ATTN // LANES)]

    def scores(h):
        return lax.dot_general(heads[h][0], q_tiles[h // 2], NT, preferred_element_type=F32) + bias_ref[first + h]

    def attend(h, s):
        sink = sink_ref[h]
        m = jnp.maximum(jnp.max(s, axis=0, keepdims=True), sink)
        e = jnp.exp(s - m)
        e_sink = jnp.exp(sink - m)
        if transposed_out:
            rden = 1.0 / (jnp.sum(e, axis=0, keepdims=True) + e_sink)
            return jnp.dot(heads[h][1], e.astype(BF16), preferred_element_type=F32) * rden
        p = jnp.concatenate([e, jnp.where(sink_row, e_sink, 0.0)], axis=0).astype(BF16)
        num = lax.dot_general(p, heads[h][1], TN, preferred_element_type=F32)
        den = lax.dot_general(p, ones_keys, TN, preferred_element_type=F32)
        return num / den

    out_tiles = [None] * (D_ATTN // LANES)
    s_next = scores(0)
    for h in range(N_Q_HEADS):
        s_cur = s_next
        if h + 1 < N_Q_HEADS:
            s_next = scores(h + 1)
        o = attend(h, s_cur)
        out_tiles[h // 2] = o if out_tiles[h // 2] is None else out_tiles[h // 2] + o
    for jq, o in enumerate(out_tiles):
        y_ref[:, jq * LANES:(jq + 1) * LANES] = (o.T if transposed_out else o).astype(y_ref.dtype)


def _swa(slopes, sinks, q, k, v, k_prev, v_prev, nb, nblk, tq, prev_always):
    t = q.shape[0]
    cur = lambda n: pl.BlockSpec((tq, n), lambda b, i: (b * nblk + i, 0))
    if prev_always:
        prev = pl.BlockSpec((WINDOW, D_KV), lambda b, i: (b, 0))
    else:
        prev = pl.BlockSpec((WINDOW, D_KV), lambda b, i: (b * nblk + jnp.maximum(i - 1, 0), 0))
    smem = pl.BlockSpec(memory_space=pltpu.SMEM)
    return pl.pallas_call(
        functools.partial(_swa_kernel, tq=tq, prev_always=prev_always),
        grid=(nb, nblk),
        in_specs=[smem, smem, cur(D_ATTN), cur(D_KV), cur(D_KV), prev, prev],
        out_specs=cur(D_ATTN),
        out_shape=jax.ShapeDtypeStruct((t, D_ATTN), BF16),
        scratch_shapes=[pltpu.VMEM(((1 if prev_always else 2) * N_Q_HEADS, WINDOW + tq, tq), F32)],
        compiler_params=pltpu.CompilerParams(dimension_semantics=("arbitrary", "arbitrary"),
                                             vmem_limit_bytes=VMEM_LIMIT),
        name="swa",
    )(slopes, sinks, q, k, v, k_prev, v_prev)


def _post_kernel(x_ref, ys_ref, ya_ref, g_ref, wsp_ref, wap_ref, wo_ref, nw_ref, wr_ref, br_ref, *rest,
                 n_exp, lr, nt, aliased):
    if aliased:
        rest = rest[1:]
    h_ref, xs_ref, lpos_ref, pr_ref, pc_ref = rest
    tp = x_ref.shape[0]

    @pl.when(pl.program_id(0) >= nt)
    def _():
        xs_ref[...] = jnp.zeros_like(xs_ref)

    @pl.when(pl.program_id(0) < nt)
    def _():
        a = jnp.dot(ys_ref[...].astype(BF16), wsp_ref[...], preferred_element_type=F32)
        b = jnp.dot(ya_ref[...].astype(BF16), wap_ref[...], preferred_element_type=F32)
        g = g_ref[...].astype(F32)
        merged = jax.nn.sigmoid(g[:, :D_MODEL]) * a + jax.nn.sigmoid(g[:, D_MODEL:]) * b
        h = x_ref[...] + jnp.dot(merged.astype(BF16), wo_ref[...], preferred_element_type=F32)
        h_ref[...] = h
        hn = h * lax.rsqrt(jnp.mean(h * h, axis=-1, keepdims=True) + EPS) * nw_ref[...]
        w_hi, w_mid, _ = _split3(wr_ref[...])
        x_hi, x_mid, _ = _split3(hn)
        logits = (lax.dot_general(w_hi, x_hi, NT, preferred_element_type=F32)
                  + lax.dot_general(w_hi, x_mid, NT, preferred_element_type=F32)
                  + lax.dot_general(w_mid, x_hi, NT, preferred_element_type=F32)) + br_ref[...]
        eidx = lax.broadcasted_iota(I32, logits.shape, 0).astype(F32)
        work = logits
        vals, ids = [], []
        for _ in range(TOP_K):
            m = jnp.max(work, axis=0, keepdims=True)
            first = jnp.min(jnp.where(work == m, eidx, float(n_exp)), axis=0, keepdims=True)
            vals.append(m)
            ids.append(first)
            work = jnp.where(eidx == first, -jnp.inf, work)
        es = [jnp.exp(v - vals[0]) for v in vals]
        den = es[0] + es[1] + es[2] + es[3]
        eye = jnp.where(lax.broadcasted_iota(I32, (TOP_K, TOP_K), 0) == lax.broadcasted_iota(I32, (TOP_K, TOP_K), 1),
                        1.0, 0.0).astype(BF16)

        def to_columns(rows):
            return sum(lax.dot_general(p, eye, TN, preferred_element_type=F32) for p in _split3(rows))

        onehot = [jnp.where(eidx == i, 1.0, 0.0) for i in ids]
        counts = [jnp.sum(o, axis=1, keepdims=True) for o in onehot]
        total = counts[0] + counts[1] + counts[2] + counts[3]
        padded = jnp.floor((total + (PIECE - 1)) * (1.0 / PIECE)) * PIECE
        ei = lax.broadcasted_iota(I32, (n_exp, n_exp), 0)
        ej = lax.broadcasted_iota(I32, (n_exp, n_exp), 1)
        below = jnp.where(ej < ei, 1.0, 0.0).astype(BF16)
        padded_b = jnp.broadcast_to(padded, (n_exp, LANES))
        seg_off = sum(jnp.dot(below, p, preferred_element_type=F32) for p in _split3(padded_b))[:, 0:1]
        ti = lax.broadcasted_iota(I32, (tp, tp), 0)
        tj = lax.broadcasted_iota(I32, (tp, tp), 1)
        before = jnp.where(ti < tj, 1.0, 0.0).astype(BF16)
        base = seg_off
        lpos = []
        for k in range(TOP_K):
            prefix = jnp.dot(onehot[k].astype(BF16), before, preferred_element_type=F32)
            lpos.append(jnp.sum(onehot[k] * (base + prefix), axis=0, keepdims=True))
            base = base + counts[k]
        pc_ref[0] = padded_b.astype(I32)

        hn_b = hn.astype(BF16)
        rc = lr // 4
        for c in range(4):
            ri = (lax.broadcasted_iota(I32, (rc, tp), 0) + c * rc).astype(F32)
            sel = jnp.zeros((rc, tp), F32)
            for k in range(TOP_K):
                sel = jnp.where(ri == lpos[k], 1.0, sel)
            xs_ref[c * rc:(c + 1) * rc, :] = jnp.dot(sel.astype(BF16), hn_b, preferred_element_type=F32).astype(BF16)

        pr_ref[...] = to_columns(jnp.concatenate([e / den for e in es], axis=0))
        lpos_ref[...] = to_columns(jnp.concatenate(lpos, axis=0)).astype(I32)


def _post(x, y_ssm, y_attn, gates, wsp, wap, wo, ffn_nw, w_rT, b_r, tm, lr, xs_rows, xs_block0, pad_steps,
          xs_prev=None):
    t = x.shape[0]
    n_exp = w_rT.shape[0]
    nt = t // tm
    last = nt - 1
    row = lambda n: pl.BlockSpec((tm, n), lambda i: (jnp.minimum(i, last), 0))
    col = row(TOP_K)
    in_specs = [row(D_MODEL), row(D_INNER), row(D_ATTN), row(2 * D_MODEL),
                _resident_spec(wsp.shape), _resident_spec(wap.shape), _resident_spec(wo.shape),
                _const_spec(ffn_nw.shape), _const_spec(w_rT.shape), _const_spec(b_r.shape)]
    args = [x, y_ssm, y_attn, gates, wsp, wap, wo, ffn_nw, w_rT, b_r]
    aliases = {}
    if xs_prev is not None:
        in_specs.append(pl.BlockSpec(memory_space=pl.ANY))
        args.append(xs_prev)
        aliases = {len(args) - 1: 1}
    return pl.pallas_call(
        functools.partial(_post_kernel, n_exp=n_exp, lr=lr, nt=nt, aliased=xs_prev is not None),
        grid=(nt + pad_steps,),
        in_specs=in_specs,
        out_specs=[row(D_MODEL), pl.BlockSpec((lr, D_MODEL), lambda i: (xs_block0 + i, 0)), col, col,
                   pl.BlockSpec((1, n_exp, LANES), lambda i: (jnp.minimum(i, last), 0, 0))],
        out_shape=[jax.ShapeDtypeStruct((t, D_MODEL), F32), jax.ShapeDtypeStruct((xs_rows, D_MODEL), BF16),
                   jax.ShapeDtypeStruct((t, TOP_K), I32), jax.ShapeDtypeStruct((t, TOP_K), F32),
                   jax.ShapeDtypeStruct((nt, n_exp, LANES), I32)],
        input_output_aliases=aliases,
        compiler_params=pltpu.CompilerParams(dimension_semantics=("arbitrary",), vmem_limit_bytes=VMEM_LIMIT),
        name="post",
    )(*args)


def _piece(ref, p):
    return ref.at[pl.ds(pl.multiple_of(p * PIECE, PIECE), PIECE)]


def _moe_kernel(te_ref, nu_ref, src_ref, xs_hbm, wg_ref, wu_ref, wd_ref, b_ref, ys_hbm, xbuf, obuf, wbf, gsem, ssem):
    i = pl.program_id(0)
    n_used = nu_ref[0]
    slot = i % 2
    tme = PIECES_PER_TILE * PIECE

    def gather(tile, s):
        for r in range(PIECES_PER_TILE):
            pltpu.make_async_copy(_piece(xs_hbm, src_ref[tile * PIECES_PER_TILE + r]),
                                  xbuf.at[s, pl.ds(r * PIECE, PIECE)], gsem.at[s]).start()

    def wait_gather(s):
        pltpu.make_async_copy(xs_hbm.at[pl.ds(0, tme)], xbuf.at[s], gsem.at[s]).wait()

    def wait_put(s):
        pltpu.make_async_copy(obuf.at[s], ys_hbm.at[pl.ds(0, tme)], ssem.at[s]).wait()

    @pl.when(i == 0)
    def _():
        gather(0, 0)

    @pl.when(i < n_used)
    def _():
        gather(i + 1, 1 - slot)
        wait_gather(slot)

        @pl.when(i >= 2)
        def _():
            wait_put(slot)

        @pl.when((i == 0) | (te_ref[i] != te_ref[jnp.maximum(i - 1, 0)]))
        def _():
            wbf[0] = wg_ref[0].astype(BF16)
            wbf[1] = wu_ref[0].astype(BF16)
            wbf[2] = wd_ref[0].astype(BF16)

        x = xbuf[slot]
        g = jnp.minimum(jnp.dot(x, wbf[0], preferred_element_type=F32) + b_ref[0, 0:1, :], SWIGLU_LIMIT)
        u = jnp.clip(jnp.dot(x, wbf[1], preferred_element_type=F32) + b_ref[0, 1:2, :], -SWIGLU_LIMIT, SWIGLU_LIMIT)
        act = ((u + 1.0) * g * jax.nn.sigmoid(SWIGLU_ALPHA * g)).astype(BF16)
        obuf[slot] = (jnp.dot(act, wbf[2], preferred_element_type=F32) + b_ref[0, 2:3, :]).astype(BF16)
        for r in range(PIECES_PER_TILE):
            pltpu.make_async_copy(obuf.at[slot, pl.ds(r * PIECE, PIECE)],
                                  _piece(ys_hbm, src_ref[i * PIECES_PER_TILE + r]), ssem.at[slot]).start()

        @pl.when(i == n_used - 1)
        def _():
            wait_put(slot)
            wait_gather(1 - slot)

            @pl.when(i >= 1)
            def _():
                wait_put(1 - slot)


def _moe(tile_expert, n_used, src, xs, wg, wu, wd, biases):
    n_tiles = tile_expert.shape[0]
    tme = PIECES_PER_TILE * PIECE
    wspec = pl.BlockSpec((1, D_MODEL, D_FF), lambda i, te, nu, sr: (te[i], 0, 0))
    grid_spec = pltpu.PrefetchScalarGridSpec(
        num_scalar_prefetch=3,
        grid=(n_tiles,),
        in_specs=[pl.BlockSpec(memory_space=pl.ANY), wspec, wspec, wspec,
                  pl.BlockSpec((1, 3, D_FF), lambda i, te, nu, sr: (te[i], 0, 0))],
        out_specs=pl.BlockSpec(memory_space=pl.ANY),
        scratch_shapes=[pltpu.VMEM((2, tme, D_MODEL), BF16), pltpu.VMEM((2, tme, D_MODEL), BF16),
                        pltpu.VMEM((3, D_MODEL, D_FF), BF16),
                        pltpu.SemaphoreType.DMA((2,)), pltpu.SemaphoreType.DMA((2,))],
    )
    return pl.pallas_call(
        _moe_kernel,
        grid_spec=grid_spec,
        out_shape=jax.ShapeDtypeStruct(xs.shape, xs.dtype),
        input_output_aliases={3: 0},
        compiler_params=pltpu.CompilerParams(dimension_semantics=("arbitrary",), vmem_limit_bytes=VMEM_LIMIT),
        name="moe",
    )(tile_expert, n_used, src, xs, wg, wu, wd, biases)


def _combine_kernel(h_ref, ys_ref, lpos_ref, pr_ref, nw_ref, o_ref):
    tp = h_ref.shape[0]
    lr = ys_ref.shape[0]
    ri = lax.broadcasted_iota(I32, (tp, lr), 1)
    lp = lpos_ref[...]
    pr = pr_ref[...]
    pw = jnp.zeros((tp, lr), F32)
    for k in range(TOP_K):
        pw = jnp.where(ri == lp[:, k:k + 1], pr[:, k:k + 1], pw)
    moe = jnp.dot(pw.astype(BF16), ys_ref[...], preferred_element_type=F32)
    h = h_ref[...] + moe
    o_ref[...] = h * lax.rsqrt(jnp.mean(h * h, axis=-1, keepdims=True) + EPS) * nw_ref[...]


def _combine(h, ys, lpos_t, probs_t, final_nw, tm, lr, ys_block0):
    t = h.shape[0]
    return pl.pallas_call(
        _combine_kernel,
        grid=(t // tm,),
        in_specs=[pl.BlockSpec((tm, D_MODEL), lambda i: (i, 0)),
                  pl.BlockSpec((lr, D_MODEL), lambda i: (ys_block0 + i, 0)),
                  pl.BlockSpec((tm, TOP_K), lambda i: (i, 0)),
                  pl.BlockSpec((tm, TOP_K), lambda i: (i, 0)),
                  _const_spec(final_nw.shape)],
        out_specs=pl.BlockSpec((tm, D_MODEL), lambda i: (i, 0)),
        out_shape=jax.ShapeDtypeStruct((t, D_MODEL), F32),
        compiler_params=pltpu.CompilerParams(dimension_semantics=("arbitrary",), vmem_limit_bytes=VMEM_LIMIT),
        name="combine",
    )(h, ys, lpos_t, probs_t, final_nw)


def _piece_table(padded_counts, tile_row0, n_tiles, spare_piece0):
    n_pieces = (padded_counts // PIECE).T
    seg_row = tile_row0[:, None] + jnp.cumsum(padded_counts, axis=1) - padded_counts
    seg_piece = (seg_row // PIECE).T
    per_expert = n_pieces.sum(axis=1)
    tiles_per = (per_expert + PIECES_PER_TILE - 1) // PIECES_PER_TILE
    tile_end = jnp.cumsum(tiles_per)
    n_used = tile_end[-1]
    slot0 = (tile_end - tiles_per) * PIECES_PER_TILE
    seg_slot = (slot0[:, None] + jnp.cumsum(n_pieces, axis=1) - n_pieces).reshape(-1)
    seg_n = n_pieces.reshape(-1)
    seg_src = seg_piece.reshape(-1)
    slots = jnp.arange(n_tiles * PIECES_PER_TILE, dtype=I32)

    def at_segment_of_slot(f):
        df = f - jnp.concatenate([jnp.zeros((1,), I32), f[:-1]])
        return jnp.sum(jnp.where(seg_slot[None, :] <= slots[:, None], df[None, :], 0), axis=1)

    real = slots < at_segment_of_slot(seg_slot + seg_n)
    padding = jnp.logical_not(real) & (slots < n_used * PIECES_PER_TILE)
    spare = spare_piece0 + jnp.where(padding, jnp.cumsum(padding.astype(I32)), 0)
    src = jnp.where(real, slots + at_segment_of_slot(seg_src - seg_slot), spare).astype(I32)
    tile_ids = jnp.arange(n_tiles, dtype=I32)
    tile_expert = jnp.sum((tile_end[None, :] <= jnp.minimum(tile_ids, n_used - 1)[:, None]).astype(I32), axis=1)
    return tile_expert.astype(I32), n_used.reshape(1).astype(I32), src


def _pick_tile(n, pref):
    while n % pref:
        pref //= 2
    return pref


def _local_rows(tm, n_exp):
    need = TOP_K * tm + n_exp * (PIECE - 1)
    return -(-need // 64) * 64


def kernel(x_prompt, x_sample, state_conv, state_ssm, cache_swa_k, cache_swa_v, attn_norm_w, w_in, conv_w, conv_b, dt_bias, a_log, d_skip, ssm_norm_w, attn_sinks, w_ssm_proj, w_attn_proj, w_o, ffn_norm_w, w_router, b_router, w_gate, b_gate, w_up, b_up, w_down, b_down, final_norm_w):
    assert w_in.shape[0] == 1, "single-layer step"
    nb, seq, _ = x_prompt.shape
    nbs = x_sample.shape[0]
    n_exp = w_router.shape[-1]
    tp, ts = nb * seq, nbs
    pad = SUBLANES

    w_packed, w_blocks = _pack_in_proj_weight(w_in[0])
    attn_nw = attn_norm_w[0].reshape(1, D_MODEL)
    a_neg = -jnp.exp(a_log[0].astype(F32))
    head_of = jnp.arange(D_INNER, dtype=I32) // SSM_HEAD_DIM
    expand = jnp.tile((jnp.arange(SSM_HEADS, dtype=I32)[:, None] == head_of[None, :]).astype(BF16), (2, 1))
    conv_consts = (conv_w[0], conv_b[0].reshape(1, CONV_DIM))
    ssd_consts = (dt_bias[0].reshape(1, SSM_HEADS), dt_bias[0].reshape(SSM_HEADS, 1),
                  a_neg.reshape(1, SSM_HEADS), a_neg.reshape(SSM_HEADS, 1),
                  d_skip[0][head_of].reshape(1, D_INNER), ssm_norm_w[0].reshape(1, D_INNER), expand)
    slopes = jnp.exp2(-8.0 * jnp.arange(1, N_Q_HEADS + 1, dtype=F32) / N_Q_HEADS)
    sinks = attn_sinks[0].astype(F32)
    wsp, wap, wo = w_ssm_proj[0].astype(BF16), w_attn_proj[0].astype(BF16), w_o[0].astype(BF16)
    ffn_nw = ffn_norm_w[0].reshape(1, D_MODEL)
    w_rT = w_router[0].T
    b_r = b_router[0].reshape(n_exp, 1)
    expert_biases = jnp.stack([b_gate[0], b_up[0], b_down[0]], axis=1)
    final_nw = final_norm_w.reshape(1, D_MODEL)

    xp = x_prompt.reshape(tp, D_MODEL)
    tm_in = _pick_tile(seq, 512)
    z, xbc, dt, q, k, v, gates, conv_tail, dtT, k_tail, v_tail = _in_proj(
        xp, attn_nw, w_packed, w_blocks, tm_in, BF16, tiles_per_seq=seq // tm_in)
    nc = seq // CHUNK
    y_ssm, ssm_p = _ssd(xbc, z, dt, dtT, jnp.zeros((nb, SSM_HEADS, SSM_HEAD_DIM, D_STATE), F32), ssd_consts,
                        nb, nc, CHUNK, CHUNK, *conv_consts)
    nblk = seq // WINDOW
    y_attn = _swa(slopes, sinks, q, k, v, k, v, nb, nblk, WINDOW, False)

    xs_pad = jnp.pad(x_sample.reshape(ts, 1, D_MODEL), ((0, 0), (0, pad - 1), (0, 0))).reshape(ts * pad, D_MODEL)
    z_s, xbc_s, dt_s, q_s, k_s, v_s, gates_s = _in_proj(xs_pad, attn_nw, w_packed, w_blocks,
                                                        _pick_tile(ts * pad, 256), F32)
    dtT_s = dt_s.reshape(ts, pad, SSM_HEADS).transpose(0, 2, 1)
    spb = _pick_tile(ts, SAMPLE_SEQS_PER_STEP)
    hist_s = jnp.pad(state_conv[0], ((0, 0), (pad - (CONV_W - 1), 0), (0, 0))).reshape(ts * pad, CONV_DIM)
    y_ssm_s, ssm_s = _ssd(xbc_s, z_s, dt_s, dtT_s, state_ssm[0], ssd_consts, ts, 1, pad, 1, *conv_consts,
                          hist=hist_s, spb=spb)
    kc = cache_swa_k[0].reshape(ts * WINDOW, D_KV)
    vc = cache_swa_v[0].reshape(ts * WINDOW, D_KV)
    y_attn_s = _swa(slopes, sinks, q_s, k_s, v_s, kc, vc, ts, 1, pad, True)
    real = lambda a: a.reshape(ts, pad, -1)[:, 0]

    tm_p, tm_s = _pick_tile(tp, 512), ts
    nt_p = tp // tm_p
    lr_p, lr_s = _local_rows(tm_p, n_exp), _local_rows(tm_s, n_exp)
    if (nt_p * lr_p) % lr_s or lr_s > lr_p:
        lr_s = lr_p
    spare_rows = lr_s + (1 + n_exp * (PIECES_PER_TILE - 1)) * PIECE
    pad_steps = -(-spare_rows // lr_p)
    xs_rows = (nt_p + pad_steps) * lr_p
    block_s = nt_p * lr_p // lr_s
    post_w = (wsp, wap, wo, ffn_nw, w_rT, b_r)
    h_p, xs, lpos_p, pr_p, pc_p = _post(xp, y_ssm, y_attn, gates, *post_w, tm_p, lr_p, xs_rows, 0, pad_steps)
    h_s, xs, lpos_s, pr_s, pc_s = _post(x_sample.reshape(ts, D_MODEL), real(y_ssm_s), real(y_attn_s), real(gates_s),
                                        *post_w, tm_s, lr_s, xs_rows, block_s, 0, xs_prev=xs)

    padded_counts = jnp.concatenate([pc_p[:, :, 0], pc_s[:, :, 0]], axis=0)
    tile_row0 = jnp.concatenate([jnp.arange(nt_p, dtype=I32) * lr_p, jnp.full((1,), nt_p * lr_p, I32)])
    max_pieces = (TOP_K * (tp + ts) + (PIECE - 1) * n_exp * (nt_p + 1)) // PIECE + n_exp * (PIECES_PER_TILE - 1)
    n_tiles = -(-max_pieces // PIECES_PER_TILE) + 1
    tile_expert, n_used, src = _piece_table(padded_counts, tile_row0, n_tiles, (nt_p * lr_p + lr_s) // PIECE)
    ys = _moe(tile_expert, n_used, src, xs, w_gate[0], w_up[0], w_down[0], expert_biases)
    out_p = _combine(h_p, ys, lpos_p, pr_p, final_nw, tm_p, lr_p, 0)
    out_s = _combine(h_s, ys, lpos_s, pr_s, final_nw, tm_s, lr_s, block_s)

    y_prompt = out_p.reshape(nb, seq, D_MODEL)
    y_sample = out_s.reshape(nbs, 1, D_MODEL)
    conv_p = conv_tail[:, SUBLANES - (CONV_W - 1):][None]
    k_p = k_tail.reshape(1, nb, WINDOW, N_KV_HEADS, ATTN_HEAD_DIM)
    v_p = v_tail.reshape(1, nb, WINDOW, N_KV_HEADS, ATTN_HEAD_DIM)
    conv_s = jnp.concatenate([state_conv[0][:, 1:], real(xbc_s)[:, None]], axis=1)[None]
    k_new = real(k_s).reshape(ts, 1, N_KV_HEADS, ATTN_HEAD_DIM)
    v_new = real(v_s).reshape(ts, 1, N_KV_HEADS, ATTN_HEAD_DIM)
    ks_out = jnp.concatenate([cache_swa_k[0][:, 1:], k_new], axis=1)[None]
    vs_out = jnp.concatenate([cache_swa_v[0][:, 1:], v_new], axis=1)[None]
    return (y_prompt, y_sample, conv_p, ssm_p[None], k_p, v_p, conv_s, ssm_s[None], ks_out, vs_out)
```

```python
import functools

import jax
import jax.numpy as jnp
from jax import lax
from jax.experimental import pallas as pl
from jax.experimental.pallas import tpu as pltpu

F32, BF16, I32 = jnp.float32, jnp.bfloat16, jnp.int32

D_MODEL = 1024
D_INNER = 2 * D_MODEL
SSM_HEAD_DIM = 64
SSM_HEADS = D_INNER // SSM_HEAD_DIM
SSM_GROUPS = 4
SSM_HPG = SSM_HEADS // SSM_GROUPS
D_STATE = 128
CONV_W = 4
CONV_DIM = D_INNER + 2 * SSM_GROUPS * D_STATE
CHUNK = 128
ATTN_HEAD_DIM = 64
N_Q_HEADS = D_MODEL // ATTN_HEAD_DIM
N_KV_HEADS = 4
Q_PER_KV = N_Q_HEADS // N_KV_HEADS
D_ATTN = N_Q_HEADS * ATTN_HEAD_DIM
D_KV = N_KV_HEADS * ATTN_HEAD_DIM
WINDOW = 128
TOP_K = 4
D_FF = D_MODEL
SWIGLU_LIMIT = 7.0
SWIGLU_ALPHA = 1.702
EPS = 1e-5
NEG_BIG = -1e30

LANES = 128
SUBLANES = 8
GROUP_W = D_INNER // SSM_GROUPS
PIECE = 2 * SUBLANES
PIECES_PER_TILE = 32
SAMPLE_SEQS_PER_STEP = 4
VMEM_LIMIT = 56 * 1024 * 1024

NT = (((1,), (1,)), ((), ()))
TN = (((0,), (0,)), ((), ()))


def _const_spec(shape):
    return pl.BlockSpec(shape, lambda *_: (0,) * len(shape))


def _resident_spec(shape):
    return pl.BlockSpec(shape, lambda *_: (0,) * len(shape), pipeline_mode=pl.Buffered(1))


def _split3(x):
    hi = x.astype(BF16)
    r1 = x - hi.astype(F32)
    mid = r1.astype(BF16)
    lo = (r1 - mid.astype(F32)).astype(BF16)
    return hi, mid, lo


def _softplus(x):
    return jnp.maximum(x, 0.0) + jnp.log(1.0 + jnp.exp(-jnp.abs(x)))


def _causal_conv_silu(buf_ref, n, cw_ref, cb_ref):
    full = buf_ref[...]
    conv = cb_ref[...]
    for j in range(CONV_W):
        shifted = full if j == CONV_W - 1 else pltpu.roll(full, CONV_W - 1 - j, 0)
        conv = conv + shifted[SUBLANES:SUBLANES + n, :] * cw_ref[j:j + 1, :]
    buf_ref[0:SUBLANES, :] = full[n:n + SUBLANES, :]
    return conv * jax.nn.sigmoid(conv)


def _inproj_kernel(x_ref, nw_ref, wz_ref, wxbc_ref, wdt_ref, wq_ref, wk_ref, wv_ref, wg_ref, *rest, tiles_per_seq):
    if tiles_per_seq:
        wdtT_ref, z_ref, xbc_ref, dt_ref, q_ref, k_ref, v_ref, g_ref, tail_ref, dtT_ref, ktail_ref, vtail_ref = rest
    else:
        z_ref, xbc_ref, dt_ref, q_ref, k_ref, v_ref, g_ref = rest
    tm = x_ref.shape[0]
    x = x_ref[...]
    xn = x * lax.rsqrt(jnp.mean(x * x, axis=-1, keepdims=True) + EPS)
    xn = (xn * nw_ref[...]).astype(BF16)
    for w_ref, o_ref in ((wz_ref, z_ref), (wdt_ref, dt_ref), (wq_ref, q_ref), (wg_ref, g_ref)):
        o_ref[...] = jnp.dot(xn, w_ref[...], preferred_element_type=F32).astype(o_ref.dtype)
    xbc = jnp.dot(xn, wxbc_ref[...], preferred_element_type=F32)
    k = jnp.dot(xn, wk_ref[...], preferred_element_type=F32)
    v = jnp.dot(xn, wv_ref[...], preferred_element_type=F32)
    xbc_ref[...] = xbc.astype(xbc_ref.dtype)
    k_ref[...] = k.astype(k_ref.dtype)
    v_ref[...] = v.astype(v_ref.dtype)
    if tiles_per_seq:
        dtT_ref[...] = lax.dot_general(wdtT_ref[...], xn, NT, preferred_element_type=F32)

        @pl.when(pl.program_id(0) % tiles_per_seq == tiles_per_seq - 1)
        def _():
            tail_ref[0] = xbc[tm - SUBLANES:, :]
            ktail_ref[0] = k[tm - WINDOW:, :]
            vtail_ref[0] = v[tm - WINDOW:, :]


IN_PROJ_WIDTHS = (D_INNER, CONV_DIM, SSM_HEADS, D_ATTN, D_KV, D_KV, 2 * D_MODEL)
IN_PROJ_PACK_ORDER = (1, 3, 0, 6, 4, 5, 2)


def _pack_in_proj_weight(w):
    cuts = [0]
    for n in IN_PROJ_WIDTHS:
        cuts.append(cuts[-1] + n)
    packed = jnp.concatenate([w[:, cuts[i]:cuts[i + 1]] for i in IN_PROJ_PACK_ORDER], axis=1).astype(BF16)
    block, off = [0] * len(IN_PROJ_WIDTHS), 0
    for i in IN_PROJ_PACK_ORDER:
        block[i], rem = divmod(off, IN_PROJ_WIDTHS[i])
        assert rem == 0
        off += IN_PROJ_WIDTHS[i]
    return packed, tuple(block)


def _in_proj(x, norm_w, w_packed, w_blocks, tm, act_dtype, tiles_per_seq=None):
    t = x.shape[0]
    widths = IN_PROJ_WIDTHS
    dtypes = (act_dtype, act_dtype, F32, act_dtype, act_dtype, act_dtype, act_dtype)
    row = lambda n: pl.BlockSpec((tm, n), lambda i: (i, 0))
    w_spec = lambda n, blk: pl.BlockSpec((D_MODEL, n), lambda i: (0, blk), pipeline_mode=pl.Buffered(1))
    dt_col = w_blocks[2] * SSM_HEADS
    w_dt = w_packed[:, dt_col:dt_col + SSM_HEADS]
    in_specs = [row(D_MODEL), _const_spec((1, D_MODEL))]
    args = [x, norm_w]
    for i, (n, blk) in enumerate(zip(widths, w_blocks)):
        in_specs.append(_resident_spec((D_MODEL, n)) if i == 2 else w_spec(n, blk))
        args.append(w_dt if i == 2 else w_packed)
    out_specs = [row(n) for n in widths]
    out_shape = [jax.ShapeDtypeStruct((t, n), d) for n, d in zip(widths, dtypes)]
    if tiles_per_seq:
        n_seq = t // (tm * tiles_per_seq)
        in_specs.append(_const_spec((SSM_HEADS, D_MODEL)))
        args.append(w_dt.T)
        per_seq = lambda r, c: pl.BlockSpec((1, r, c), lambda i: (i // tiles_per_seq, 0, 0))
        out_specs += [per_seq(SUBLANES, CONV_DIM), pl.BlockSpec((SSM_HEADS, tm), lambda i: (0, i)),
                      per_seq(WINDOW, D_KV), per_seq(WINDOW, D_KV)]
        out_shape += [jax.ShapeDtypeStruct((n_seq, SUBLANES, CONV_DIM), F32),
                      jax.ShapeDtypeStruct((SSM_HEADS, t), F32),
                      jax.ShapeDtypeStruct((n_seq, WINDOW, D_KV), F32),
                      jax.ShapeDtypeStruct((n_seq, WINDOW, D_KV), F32)]
    return pl.pallas_call(
        functools.partial(_inproj_kernel, tiles_per_seq=tiles_per_seq),
        grid=(t // tm,),
        in_specs=in_specs,
        out_specs=out_specs,
        out_shape=out_shape,
        compiler_params=pltpu.CompilerParams(dimension_semantics=("arbitrary",), vmem_limit_bytes=VMEM_LIMIT),
        name="in_proj",
    )(*args)


def _ssd_kernel(xbc_ref, z_ref, dt_ref, dtT_ref, h0_ref, dtb_ref, dtbT_ref, a_ref, aT_ref, dsk_ref, nw_ref, e_ref,
                *rest, lc, n_valid, conv, single_chunk):
    if conv == "history":
        hist_ref, cw_ref, cb_ref, y_ref, h_ref, buf_ref = rest
    else:
        cw_ref, cb_ref, y_ref, h_ref, tail_ref = rest

    def at_first_chunk(fn):
        if single_chunk:
            fn()
        else:
            pl.when(pl.program_id(1) == 0)(fn)

    @at_first_chunk
    def _():
        h_ref[...] = h0_ref[...]

    if conv == "history":
        @at_first_chunk
        def _():
            buf_ref[0:SUBLANES, :] = hist_ref[...]

        buf_ref[SUBLANES:SUBLANES + lc, :] = xbc_ref[...].astype(F32)
        act = _causal_conv_silu(buf_ref, lc, cw_ref, cb_ref)
    else:
        @at_first_chunk
        def _():
            tail_ref[...] = jnp.zeros_like(tail_ref)

        raw = xbc_ref[...]
        nt = tail_ref.shape[0]
        ext = jnp.concatenate([tail_ref[...], raw], axis=0)
        to = lax.broadcasted_iota(I32, (lc, nt + lc), 0)
        frm = lax.broadcasted_iota(I32, (lc, nt + lc), 1)
        conv_acc = cb_ref[...]
        for j in range(CONV_W - 1):
            shift = jnp.where(frm == to + (nt - (CONV_W - 1) + j), 1.0, 0.0).astype(BF16)
            conv_acc = conv_acc + jnp.dot(shift, ext, preferred_element_type=F32) * cw_ref[j:j + 1, :]
        conv_acc = conv_acc + raw.astype(F32) * cw_ref[CONV_W - 1:CONV_W, :]
        tail_ref[...] = raw[lc - nt:, :]
        act = conv_acc * jax.nn.sigmoid(conv_acc)
    xs = act[:, :D_INNER]
    bm = act[:, D_INNER:D_INNER + SSM_GROUPS * D_STATE].astype(BF16)
    cm = act[:, D_INNER + SSM_GROUPS * D_STATE:].astype(BF16)

    dt = _softplus(dt_ref[...] + dtb_ref[...])
    dtT_raw = dtT_ref[0] if len(dtT_ref.shape) == 3 else dtT_ref[...]
    dtT = _softplus(dtT_raw + dtbT_ref[...])
    if n_valid < lc:
        dt = jnp.where(lax.broadcasted_iota(I32, dt.shape, 0) < n_valid, dt, 0.0)
        dtT = jnp.where(lax.broadcasted_iota(I32, dtT.shape, 1) < n_valid, dtT, 0.0)
    la = dt * a_ref[...]
    laT = dtT * aT_ref[...]
    li = lax.broadcasted_iota(I32, (lc, lc), 0)
    si = lax.broadcasted_iota(I32, (lc, lc), 1)
    causal = li >= si
    tril = jnp.where(causal, 1.0, 0.0).astype(BF16)
    triu = jnp.where(li <= si, 1.0, 0.0).astype(BF16)
    cum = sum(jnp.dot(tril, p, preferred_element_type=F32) for p in _split3(la))
    cumT = sum(jnp.dot(p, triu, preferred_element_type=F32) for p in _split3(laT))
    ec = jnp.exp(cum)
    dte = jnp.exp(cum[lc - 1:lc, :] - cum)
    cd = jnp.exp(cumT[:, lc - 1:lc])

    def expand(v):
        hi, mid, _ = _split3(v)
        return jnp.dot(jnp.concatenate([hi, mid], axis=1), e_ref[...], preferred_element_type=F32)

    dt_x, ec_x, dte_x = expand(dt), expand(ec), expand(dte)
    xdt = xs * dt_x
    xdt_b = xdt.astype(BF16)
    xdte_b = (xdt * dte_x).astype(BF16)
    lane = lax.broadcasted_iota(I32, (lc, LANES), 1)
    low_half = lane < SSM_HEAD_DIM

    for g in range(SSM_GROUPS):
        gs = slice(g * GROUP_W, (g + 1) * GROUP_W)
        bm_g = bm[:, g * D_STATE:(g + 1) * D_STATE]
        cm_g = cm[:, g * D_STATE:(g + 1) * D_STATE]
        cb = lax.dot_general(cm_g, bm_g, NT, preferred_element_type=F32)
        cbm = jnp.where(causal, cb, 0.0)
        h_g = h_ref[0, g * SSM_HPG:(g + 1) * SSM_HPG].reshape(GROUP_W, D_STATE)
        y_off = lax.dot_general(cm_g, h_g.astype(BF16), NT, preferred_element_type=F32) * ec_x[:, gs]
        tiles = []
        for j in range(GROUP_W // LANES):
            col = g * GROUP_W + j * LANES
            x_pair = xdt_b[:, col:col + LANES]
            acc = None
            for half in range(2):
                h = col // SSM_HEAD_DIM + half
                seg = cum[:, h:h + 1] - cumT[h:h + 1, :]
                m = (cbm * jnp.exp(jnp.where(causal, seg, 0.0))).astype(BF16)
                x_h = jnp.where(low_half if half == 0 else jnp.logical_not(low_half), x_pair, jnp.zeros_like(x_pair))
                d = jnp.dot(m, x_h, preferred_element_type=F32)
                acc = d if acc is None else acc + d
            tiles.append(acc)
        y_g = jnp.concatenate(tiles, axis=1) + y_off + xs[:, gs] * dsk_ref[:, gs]
        zg = z_ref[:, gs].astype(F32)
        y_g = y_g * (zg * jax.nn.sigmoid(zg))
        y_g = y_g * lax.rsqrt(jnp.mean(y_g * y_g, axis=-1, keepdims=True) + EPS)
        y_ref[:, gs] = (y_g * nw_ref[:, gs]).astype(y_ref.dtype)
        st = lax.dot_general(xdte_b[:, gs], bm_g, TN, preferred_element_type=F32)
        for hh in range(SSM_HPG):
            h = g * SSM_HPG + hh
            rows = slice(hh * SSM_HEAD_DIM, (hh + 1) * SSM_HEAD_DIM)
            h_ref[0, h] = h_g[rows, :] * cd[h:h + 1, 0:1] + st[rows, :]


def _ssd_block_kernel(xbc_ref, z_ref, dt_ref, dtT_ref, h0_ref, *rest, spb, lc, n_consts, n_valid, conv):
    consts, rest = rest[:n_consts], rest[n_consts:]
    if conv == "history":
        hist_ref, cw_ref, cb_ref, y_ref, h_ref, scratch = rest
    else:
        cw_ref, cb_ref, y_ref, h_ref, scratch = rest
    for s in range(spb):
        rows, one = pl.ds(s * lc, lc), pl.ds(s, 1)
        conv_refs = (cw_ref, cb_ref)
        if conv == "history":
            conv_refs = (hist_ref.at[pl.ds(s * SUBLANES, SUBLANES)],) + conv_refs
        _ssd_kernel(xbc_ref.at[rows], z_ref.at[rows], dt_ref.at[rows], dtT_ref.at[one], h0_ref.at[one], *consts,
                    *conv_refs, y_ref.at[rows], h_ref.at[one], scratch,
                    lc=lc, n_valid=n_valid, conv=conv, single_chunk=True)


def _ssd(xbc, z, dt, dtT, h0, consts, nb, nc, lc, n_valid, conv_w, conv_b, hist=None, spb=1):
    assert spb == 1 or (nc == 1 and dtT.ndim == 3 and nb % spb == 0)
    t = xbc.shape[0]
    row = lambda n: pl.BlockSpec((spb * lc, n), lambda b, c: (b * nc + c, 0))
    if dtT.ndim == 2:
        dtT_spec = pl.BlockSpec((SSM_HEADS, lc), lambda b, c: (0, b * nc + c))
    else:
        dtT_spec = pl.BlockSpec((spb, SSM_HEADS, lc), lambda b, c: (b * nc + c, 0, 0))
    state_spec = pl.BlockSpec((spb, SSM_HEADS, SSM_HEAD_DIM, D_STATE), lambda b, c: (b, 0, 0, 0))
    in_specs = [row(CONV_DIM), row(D_INNER), row(SSM_HEADS), dtT_spec, state_spec] + [_const_spec(a.shape) for a in consts]
    args = [xbc, z, dt, dtT, h0, *consts]
    if hist is not None:
        in_specs.append(pl.BlockSpec((spb * SUBLANES, CONV_DIM), lambda b, c: (b, 0)))
        args.append(hist)
        scratch = [pltpu.VMEM((SUBLANES + lc, CONV_DIM), F32)]
    else:
        scratch = [pltpu.VMEM((PIECE, CONV_DIM), BF16)]
    in_specs += [_const_spec(conv_w.shape), _const_spec(conv_b.shape)]
    args += [conv_w, conv_b]
    conv = "history" if hist is not None else "fresh"
    if spb == 1:
        body = functools.partial(_ssd_kernel, lc=lc, n_valid=n_valid, conv=conv, single_chunk=nc == 1)
    else:
        body = functools.partial(_ssd_block_kernel, spb=spb, lc=lc, n_consts=len(consts), n_valid=n_valid, conv=conv)
    return pl.pallas_call(
        body,
        grid=(nb // spb, nc),
        in_specs=in_specs,
        out_specs=[row(D_INNER), state_spec],
        out_shape=[jax.ShapeDtypeStruct((t, D_INNER), BF16),
                   jax.ShapeDtypeStruct((nb, SSM_HEADS, SSM_HEAD_DIM, D_STATE), F32)],
        scratch_shapes=scratch,
        compiler_params=pltpu.CompilerParams(dimension_semantics=("arbitrary", "arbitrary"),
                                             vmem_limit_bytes=VMEM_LIMIT),
        name="ssd",
    )(*args)


def _swa_kernel(slope_ref, sink_ref, q_ref, kc_ref, vc_ref, kp_ref, vp_ref, y_ref, bias_ref, *, tq, prev_always):
    fold = tq == WINDOW
    nk = WINDOW + tq

    @pl.when((pl.program_id(0) == 0) & (pl.program_id(1) == 0))
    def _():
        rows = WINDOW if fold else nk
        j = lax.broadcasted_iota(I32, (rows, tq), 0)
        r = lax.broadcasted_iota(I32, (rows, tq), 1)
        if fold:
            dist = jnp.where(j > r, r + WINDOW - j, r - j)
            valid, from_previous = dist >= 0, j > r
        else:
            dist = r + WINDOW - j
            valid, from_previous = (dist >= 0) & (dist < WINDOW), j < WINDOW
        distf = dist.astype(F32)
        for h in range(N_Q_HEADS):
            penalty = -slope_ref[h] * distf
            bias_ref[h] = jnp.where(valid, penalty, NEG_BIG)
            if not prev_always:
                bias_ref[N_Q_HEADS + h] = jnp.where(valid & jnp.logical_not(from_previous), penalty, NEG_BIG)

    first = 0 if prev_always else jnp.where(pl.program_id(1) == 0, N_Q_HEADS, 0)
    lane = lax.broadcasted_iota(I32, (nk, LANES), 1)
    zeros_half = jnp.zeros((ATTN_HEAD_DIM, nk), BF16)
    sink_row = lax.broadcasted_iota(I32, (SUBLANES, tq), 0) == 0
    ones_keys = jnp.ones((nk + SUBLANES, LANES), BF16)
    if fold:
        ji = lax.broadcasted_iota(I32, (WINDOW, tq), 0)
        ri = lax.broadcasted_iota(I32, (WINDOW, tq), 1)
        from_prev = ji > ri
        from_prev_b = ji.astype(BF16) > ri.astype(BF16)
    heads = []
    for t in range(D_KV // LANES):
        cols = slice(t * LANES, (t + 1) * LANES)
        kt = jnp.concatenate([kp_ref[:, cols].astype(F32), kc_ref[:, cols].astype(F32)], axis=0)
        vt = jnp.concatenate([vp_ref[:, cols].astype(F32), vc_ref[:, cols].astype(F32)], axis=0)
        if fold:
            vt_t = vt.T.astype(BF16)
        for b in range(2):
            mine = (lane >= ATTN_HEAD_DIM) if b else (lane < ATTN_HEAD_DIM)
            k_same = jnp.where(mine, kt, 0.0)
            k_half = {b: k_same.astype(BF16), 1 - b: pltpu.roll(k_same, ATTN_HEAD_DIM, 1).astype(BF16)}
            if fold:
                v_g = vt_t[b * ATTN_HEAD_DIM:(b + 1) * ATTN_HEAD_DIM, :]
                v_half = {0: jnp.concatenate([v_g, zeros_half], axis=0),
                          1: jnp.concatenate([zeros_half, v_g], axis=0)}
            else:
                v_same = jnp.concatenate([jnp.where(mine, vt, 0.0), jnp.zeros((SUBLANES, LANES), F32)], axis=0)
                v_half = {b: v_same.astype(BF16), 1 - b: pltpu.roll(v_same, ATTN_HEAD_DIM, 1).astype(BF16)}
            for qi in range(Q_PER_KV):
                a = qi % 2
                heads.append((k_half[a], v_half[a]))

    q_tiles = [(q_ref[:, jq * LANES:(jq + 1) * LANES] * (ATTN_HEAD_DIM ** -0.5)).astype(BF16)
               for jq in range(D_ATTN // LANES)]

    def scores(h):
        k, q = heads[h][0], q_tiles[h // 2]
        s = lax.dot_general(k, q, NT, preferred_element_type=F32)
        if fold:
            s = jnp.where(from_prev, s[:WINDOW], s[WINDOW:])
        return s + bias_ref[first + h]

    def attend(h, s):
        sink = sink_ref[h]
        v = heads[h][1]
        m = jnp.maximum(jnp.max(s, axis=0, keepdims=True), sink)
        e = jnp.exp(s - m)
        e_sink = jnp.exp(sink - m)
        if fold:
            rden = 1.0 / (jnp.sum(e, axis=0, keepdims=True) + e_sink)
            e_b = e.astype(BF16)
            zero = jnp.zeros_like(e_b)
            p = jnp.concatenate([jnp.where(from_prev_b, e_b, zero), jnp.where(from_prev_b, zero, e_b)], axis=0)
            return jnp.dot(v, p, preferred_element_type=F32) * rden
        p = jnp.concatenate([e, jnp.where(sink_row, e_sink, 0.0)], axis=0).astype(BF16)
        num = lax.dot_general(p, v, TN, preferred_element_type=F32)
        den = lax.dot_general(p, ones_keys, TN, preferred_element_type=F32)
        return num / den

    out_tiles = [None] * (D_ATTN // LANES)
    s_next = scores(0)
    for h in range(N_Q_HEADS):
        s_cur = s_next
        if h + 1 < N_Q_HEADS:
            s_next = scores(h + 1)
        o = attend(h, s_cur)
        out_tiles[h // 2] = o if out_tiles[h // 2] is None else out_tiles[h // 2] + o
    for jq, o in enumerate(out_tiles):
        y_ref[:, jq * LANES:(jq + 1) * LANES] = (o.T if fold else o).astype(y_ref.dtype)


def _swa(slopes, sinks, q, k, v, k_prev, v_prev, nb, nblk, tq, prev_always):
    t = q.shape[0]
    cur = lambda n: pl.BlockSpec((tq, n), lambda b, i: (b * nblk + i, 0))
    if prev_always:
        prev = pl.BlockSpec((WINDOW, D_KV), lambda b, i: (b, 0))
    else:
        prev = pl.BlockSpec((WINDOW, D_KV), lambda b, i: (b * nblk + jnp.maximum(i - 1, 0), 0))
    smem = pl.BlockSpec(memory_space=pltpu.SMEM)
    return pl.pallas_call(
        functools.partial(_swa_kernel, tq=tq, prev_always=prev_always),
        grid=(nb, nblk),
        in_specs=[smem, smem, cur(D_ATTN), cur(D_KV), cur(D_KV), prev, prev],
        out_specs=cur(D_ATTN),
        out_shape=jax.ShapeDtypeStruct((t, D_ATTN), BF16),
        scratch_shapes=[pltpu.VMEM(((1 if prev_always else 2) * N_Q_HEADS, WINDOW if tq == WINDOW else WINDOW + tq,
                                    tq), F32)],
        compiler_params=pltpu.CompilerParams(dimension_semantics=("arbitrary", "arbitrary"),
                                             vmem_limit_bytes=VMEM_LIMIT),
        name="swa",
    )(slopes, sinks, q, k, v, k_prev, v_prev)


def _post_kernel(x_ref, ys_ref, ya_ref, g_ref, wsp_ref, wap_ref, wo_ref, nw_ref, wr_ref, br_ref, *rest,
                 n_exp, lr, nt, aliased):
    if aliased:
        rest = rest[1:]
    h_ref, xs_ref, lpos_ref, pr_ref, pc_ref = rest
    tp = x_ref.shape[0]

    @pl.when(pl.program_id(0) >= nt)
    def _():
        xs_ref[...] = jnp.zeros_like(xs_ref)

    @pl.when(pl.program_id(0) < nt)
    def _():
        a = jnp.dot(ys_ref[...].astype(BF16), wsp_ref[...], preferred_element_type=F32)
        b = jnp.dot(ya_ref[...].astype(BF16), wap_ref[...], preferred_element_type=F32)
        g = g_ref[...].astype(F32)
        merged = jax.nn.sigmoid(g[:, :D_MODEL]) * a + jax.nn.sigmoid(g[:, D_MODEL:]) * b
        h = x_ref[...] + jnp.dot(merged.astype(BF16), wo_ref[...], preferred_element_type=F32)
        h_ref[...] = h
        hn = h * lax.rsqrt(jnp.mean(h * h, axis=-1, keepdims=True) + EPS) * nw_ref[...]
        w_hi, w_mid, _ = _split3(wr_ref[...])
        x_hi, x_mid, _ = _split3(hn)
        logits = (lax.dot_general(w_hi, x_hi, NT, preferred_element_type=F32)
                  + lax.dot_general(w_hi, x_mid, NT, preferred_element_type=F32)
                  + lax.dot_general(w_mid, x_hi, NT, preferred_element_type=F32)) + br_ref[...]
        eidx = lax.broadcasted_iota(I32, logits.shape, 0).astype(F32)
        work = logits
        vals, ids = [], []
        for _ in range(TOP_K):
            m = jnp.max(work, axis=0, keepdims=True)
            first = jnp.min(jnp.where(work == m, eidx, float(n_exp)), axis=0, keepdims=True)
            vals.append(m)
            ids.append(first)
            work = jnp.where(eidx == first, -jnp.inf, work)
        es = [jnp.exp(v - vals[0]) for v in vals]
        den = es[0] + es[1] + es[2] + es[3]
        eye = jnp.where(lax.broadcasted_iota(I32, (TOP_K, TOP_K), 0) == lax.broadcasted_iota(I32, (TOP_K, TOP_K), 1),
                        1.0, 0.0).astype(BF16)

        def to_columns(rows):
            return sum(lax.dot_general(p, eye, TN, preferred_element_type=F32) for p in _split3(rows))

        onehot = [jnp.where(eidx == i, 1.0, 0.0) for i in ids]
        counts = [jnp.sum(o, axis=1, keepdims=True) for o in onehot]
        total = counts[0] + counts[1] + counts[2] + counts[3]
        padded = jnp.floor((total + (PIECE - 1)) * (1.0 / PIECE)) * PIECE
        ei = lax.broadcasted_iota(I32, (n_exp, n_exp), 0)
        ej = lax.broadcasted_iota(I32, (n_exp, n_exp), 1)
        below = jnp.where(ej < ei, 1.0, 0.0).astype(BF16)
        padded_b = jnp.broadcast_to(padded, (n_exp, LANES))
        seg_off = sum(jnp.dot(below, p, preferred_element_type=F32) for p in _split3(padded_b))[:, 0:1]
        ti = lax.broadcasted_iota(I32, (tp, tp), 0)
        tj = lax.broadcasted_iota(I32, (tp, tp), 1)
        before = jnp.where(ti < tj, 1.0, 0.0).astype(BF16)
        base = seg_off
        lpos = []
        for k in range(TOP_K):
            prefix = jnp.dot(onehot[k].astype(BF16), before, preferred_element_type=F32)
            lpos.append(jnp.sum(onehot[k] * (base + prefix), axis=0, keepdims=True))
            base = base + counts[k]
        pc_ref[0] = padded_b.astype(I32)

        hn_b = hn.astype(BF16)
        rc = lr // 4
        for c in range(4):
            ri = (lax.broadcasted_iota(I32, (rc, tp), 0) + c * rc).astype(F32)
            sel = jnp.zeros((rc, tp), F32)
            for k in range(TOP_K):
                sel = jnp.where(ri == lpos[k], 1.0, sel)
            xs_ref[c * rc:(c + 1) * rc, :] = jnp.dot(sel.astype(BF16), hn_b, preferred_element_type=F32).astype(BF16)

        pr_ref[...] = to_columns(jnp.concatenate([e / den for e in es], axis=0))
        lpos_ref[...] = to_columns(jnp.concatenate(lpos, axis=0)).astype(I32)


def _post(x, y_ssm, y_attn, gates, wsp, wap, wo, ffn_nw, w_rT, b_r, tm, lr, xs_rows, xs_block0, pad_steps,
          xs_prev=None):
    t = x.shape[0]
    n_exp = w_rT.shape[0]
    nt = t // tm
    last = nt - 1
    row = lambda n: pl.BlockSpec((tm, n), lambda i: (jnp.minimum(i, last), 0))
    col = row(TOP_K)
    in_specs = [row(D_MODEL), row(D_INNER), row(D_ATTN), row(2 * D_MODEL),
                _resident_spec(wsp.shape), _resident_spec(wap.shape), _resident_spec(wo.shape),
                _const_spec(ffn_nw.shape), _const_spec(w_rT.shape), _const_spec(b_r.shape)]
    args = [x, y_ssm, y_attn, gates, wsp, wap, wo, ffn_nw, w_rT, b_r]
    aliases = {}
    if xs_prev is not None:
        in_specs.append(pl.BlockSpec(memory_space=pl.ANY))
        args.append(xs_prev)
        aliases = {len(args) - 1: 1}
    return pl.pallas_call(
        functools.partial(_post_kernel, n_exp=n_exp, lr=lr, nt=nt, aliased=xs_prev is not None),
        grid=(nt + pad_steps,),
        in_specs=in_specs,
        out_specs=[row(D_MODEL), pl.BlockSpec((lr, D_MODEL), lambda i: (xs_block0 + i, 0)), col, col,
                   pl.BlockSpec((1, n_exp, LANES), lambda i: (jnp.minimum(i, last), 0, 0))],
        out_shape=[jax.ShapeDtypeStruct((t, D_MODEL), F32), jax.ShapeDtypeStruct((xs_rows, D_MODEL), BF16),
                   jax.ShapeDtypeStruct((t, TOP_K), I32), jax.ShapeDtypeStruct((t, TOP_K), F32),
                   jax.ShapeDtypeStruct((nt, n_exp, LANES), I32)],
        input_output_aliases=aliases,
        compiler_params=pltpu.CompilerParams(dimension_semantics=("arbitrary",), vmem_limit_bytes=VMEM_LIMIT),
        name="post",
    )(*args)


def _piece(ref, p):
    return ref.at[pl.ds(pl.multiple_of(p * PIECE, PIECE), PIECE)]


def _moe_kernel(te_ref, nu_ref, src_ref, xs_hbm, wg_ref, wu_ref, wd_ref, b_ref, ys_hbm, xbuf, obuf, wbf, gsem, ssem):
    i = pl.program_id(0)
    n_used = nu_ref[0]
    slot = i % 2
    tme = PIECES_PER_TILE * PIECE

    def gather(tile, s):
        for r in range(PIECES_PER_TILE):
            pltpu.make_async_copy(_piece(xs_hbm, src_ref[tile * PIECES_PER_TILE + r]),
                                  xbuf.at[s, pl.ds(r * PIECE, PIECE)], gsem.at[s]).start()

    def wait_gather(s):
        pltpu.make_async_copy(xs_hbm.at[pl.ds(0, tme)], xbuf.at[s], gsem.at[s]).wait()

    def wait_put(s):
        pltpu.make_async_copy(obuf.at[s], ys_hbm.at[pl.ds(0, tme)], ssem.at[s]).wait()

    @pl.when(i == 0)
    def _():
        gather(0, 0)

    @pl.when(i < n_used)
    def _():
        gather(i + 1, 1 - slot)
        wait_gather(slot)

        @pl.when(i >= 2)
        def _():
            wait_put(slot)

        @pl.when((i == 0) | (te_ref[i] != te_ref[jnp.maximum(i - 1, 0)]))
        def _():
            wbf[0] = wg_ref[0].astype(BF16)
            wbf[1] = wu_ref[0].astype(BF16)
            wbf[2] = wd_ref[0].astype(BF16)

        x = xbuf[slot]
        g = jnp.minimum(jnp.dot(x, wbf[0], preferred_element_type=F32) + b_ref[0, 0:1, :], SWIGLU_LIMIT)
        u = jnp.clip(jnp.dot(x, wbf[1], preferred_element_type=F32) + b_ref[0, 1:2, :], -SWIGLU_LIMIT, SWIGLU_LIMIT)
        act = ((u + 1.0) * g * jax.nn.sigmoid(SWIGLU_ALPHA * g)).astype(BF16)
        obuf[slot] = (jnp.dot(act, wbf[2], preferred_element_type=F32) + b_ref[0, 2:3, :]).astype(BF16)
        for r in range(PIECES_PER_TILE):
            pltpu.make_async_copy(obuf.at[slot, pl.ds(r * PIECE, PIECE)],
                                  _piece(ys_hbm, src_ref[i * PIECES_PER_TILE + r]), ssem.at[slot]).start()

        @pl.when(i == n_used - 1)
        def _():
            wait_put(slot)
            wait_gather(1 - slot)

            @pl.when(i >= 1)
            def _():
                wait_put(1 - slot)


def _moe(tile_expert, n_used, src, xs, wg, wu, wd, biases):
    n_tiles = tile_expert.shape[0]
    tme = PIECES_PER_TILE * PIECE
    wspec = pl.BlockSpec((1, D_MODEL, D_FF), lambda i, te, nu, sr: (te[i], 0, 0))
    grid_spec = pltpu.PrefetchScalarGridSpec(
        num_scalar_prefetch=3,
        grid=(n_tiles,),
        in_specs=[pl.BlockSpec(memory_space=pl.ANY), wspec, wspec, wspec,
                  pl.BlockSpec((1, 3, D_FF), lambda i, te, nu, sr: (te[i], 0, 0))],
        out_specs=pl.BlockSpec(memory_space=pl.ANY),
        scratch_shapes=[pltpu.VMEM((2, tme, D_MODEL), BF16), pltpu.VMEM((2, tme, D_MODEL), BF16),
                        pltpu.VMEM((3, D_MODEL, D_FF), BF16),
                        pltpu.SemaphoreType.DMA((2,)), pltpu.SemaphoreType.DMA((2,))],
    )
    return pl.pallas_call(
        _moe_kernel,
        grid_spec=grid_spec,
        out_shape=jax.ShapeDtypeStruct(xs.shape, xs.dtype),
        input_output_aliases={3: 0},
        compiler_params=pltpu.CompilerParams(dimension_semantics=("arbitrary",), vmem_limit_bytes=VMEM_LIMIT),
        name="moe",
    )(tile_expert, n_used, src, xs, wg, wu, wd, biases)


def _combine_kernel(h_ref, ys_ref, lpos_ref, pr_ref, nw_ref, o_ref):
    tp = h_ref.shape[0]
    lr = ys_ref.shape[0]
    ri = lax.broadcasted_iota(I32, (tp, lr), 1)
    lp = lpos_ref[...]
    pr = pr_ref[...]
    pw = jnp.zeros((tp, lr), F32)
    for k in range(TOP_K):
        pw = jnp.where(ri == lp[:, k:k + 1], pr[:, k:k + 1], pw)
    moe = jnp.dot(pw.astype(BF16), ys_ref[...], preferred_element_type=F32)
    h = h_ref[...] + moe
    o_ref[...] = h * lax.rsqrt(jnp.mean(h * h, axis=-1, keepdims=True) + EPS) * nw_ref[...]


def _combine(h, ys, lpos_t, probs_t, final_nw, tm, lr, ys_block0):
    t = h.shape[0]
    return pl.pallas_call(
        _combine_kernel,
        grid=(t // tm,),
        in_specs=[pl.BlockSpec((tm, D_MODEL), lambda i: (i, 0)),
                  pl.BlockSpec((lr, D_MODEL), lambda i: (ys_block0 + i, 0)),
                  pl.BlockSpec((tm, TOP_K), lambda i: (i, 0)),
                  pl.BlockSpec((tm, TOP_K), lambda i: (i, 0)),
                  _const_spec(final_nw.shape)],
        out_specs=pl.BlockSpec((tm, D_MODEL), lambda i: (i, 0)),
        out_shape=jax.ShapeDtypeStruct((t, D_MODEL), F32),
        compiler_params=pltpu.CompilerParams(dimension_semantics=("arbitrary",), vmem_limit_bytes=VMEM_LIMIT),
        name="combine",
    )(h, ys, lpos_t, probs_t, final_nw)


def _piece_table(padded_counts, tile_row0, n_tiles, spare_piece0):
    n_pieces = (padded_counts // PIECE).T
    seg_row = tile_row0[:, None] + jnp.cumsum(padded_counts, axis=1) - padded_counts
    seg_piece = (seg_row // PIECE).T
    per_expert = n_pieces.sum(axis=1)
    tiles_per = (per_expert + PIECES_PER_TILE - 1) // PIECES_PER_TILE
    tile_end = jnp.cumsum(tiles_per)
    n_used = tile_end[-1]
    slot0 = (tile_end - tiles_per) * PIECES_PER_TILE
    seg_slot = (slot0[:, None] + jnp.cumsum(n_pieces, axis=1) - n_pieces).reshape(-1)
    seg_n = n_pieces.reshape(-1)
    seg_src = seg_piece.reshape(-1)
    slots = jnp.arange(n_tiles * PIECES_PER_TILE, dtype=I32)

    def at_segment_of_slot(f):
        df = f - jnp.concatenate([jnp.zeros((1,), I32), f[:-1]])
        return jnp.sum(jnp.where(seg_slot[None, :] <= slots[:, None], df[None, :], 0), axis=1)

    real = slots < at_segment_of_slot(seg_slot + seg_n)
    padding = jnp.logical_not(real) & (slots < n_used * PIECES_PER_TILE)
    spare = spare_piece0 + jnp.where(padding, jnp.cumsum(padding.astype(I32)), 0)
    src = jnp.where(real, slots + at_segment_of_slot(seg_src - seg_slot), spare).astype(I32)
    tile_ids = jnp.arange(n_tiles, dtype=I32)
    tile_expert = jnp.sum((tile_end[None, :] <= jnp.minimum(tile_ids, n_used - 1)[:, None]).astype(I32), axis=1)
    return tile_expert.astype(I32), n_used.reshape(1).astype(I32), src


def _pick_tile(n, pref):
    while n % pref:
        pref //= 2
    return pref


def _local_rows(tm, n_exp):
    need = TOP_K * tm + n_exp * (PIECE - 1)
    return -(-need // 64) * 64


def kernel(x_prompt, x_sample, state_conv, state_ssm, cache_swa_k, cache_swa_v, attn_norm_w, w_in, conv_w, conv_b, dt_bias, a_log, d_skip, ssm_norm_w, attn_sinks, w_ssm_proj, w_attn_proj, w_o, ffn_norm_w, w_router, b_router, w_gate, b_gate, w_up, b_up, w_down, b_down, final_norm_w):
    assert w_in.shape[0] == 1, "single-layer step"
    nb, seq, _ = x_prompt.shape
    nbs = x_sample.shape[0]
    n_exp = w_router.shape[-1]
    tp, ts = nb * seq, nbs
    pad = SUBLANES

    w_packed, w_blocks = _pack_in_proj_weight(w_in[0])
    attn_nw = attn_norm_w[0].reshape(1, D_MODEL)
    a_neg = -jnp.exp(a_log[0].astype(F32))
    head_of = jnp.arange(D_INNER, dtype=I32) // SSM_HEAD_DIM
    expand = jnp.tile((jnp.arange(SSM_HEADS, dtype=I32)[:, None] == head_of[None, :]).astype(BF16), (2, 1))
    conv_consts = (conv_w[0], conv_b[0].reshape(1, CONV_DIM))
    ssd_consts = (dt_bias[0].reshape(1, SSM_HEADS), dt_bias[0].reshape(SSM_HEADS, 1),
                  a_neg.reshape(1, SSM_HEADS), a_neg.reshape(SSM_HEADS, 1),
                  d_skip[0][head_of].reshape(1, D_INNER), ssm_norm_w[0].reshape(1, D_INNER), expand)
    slopes = jnp.exp2(-8.0 * jnp.arange(1, N_Q_HEADS + 1, dtype=F32) / N_Q_HEADS)
    sinks = attn_sinks[0].astype(F32)
    wsp, wap, wo = w_ssm_proj[0].astype(BF16), w_attn_proj[0].astype(BF16), w_o[0].astype(BF16)
    ffn_nw = ffn_norm_w[0].reshape(1, D_MODEL)
    w_rT = w_router[0].T
    b_r = b_router[0].reshape(n_exp, 1)
    expert_biases = jnp.stack([b_gate[0], b_up[0], b_down[0]], axis=1)
    final_nw = final_norm_w.reshape(1, D_MODEL)

    xp = x_prompt.reshape(tp, D_MODEL)
    tm_in = _pick_tile(seq, 512)
    z, xbc, dt, q, k, v, gates, conv_tail, dtT, k_tail, v_tail = _in_proj(
        xp, attn_nw, w_packed, w_blocks, tm_in, BF16, tiles_per_seq=seq // tm_in)
    nc = seq // CHUNK
    y_ssm, ssm_p = _ssd(xbc, z, dt, dtT, jnp.zeros((nb, SSM_HEADS, SSM_HEAD_DIM, D_STATE), F32), ssd_consts,
                        nb, nc, CHUNK, CHUNK, *conv_consts)
    nblk = seq // WINDOW
    y_attn = _swa(slopes, sinks, q, k, v, k, v, nb, nblk, WINDOW, False)

    xs_pad = jnp.pad(x_sample.reshape(ts, 1, D_MODEL), ((0, 0), (0, pad - 1), (0, 0))).reshape(ts * pad, D_MODEL)
    z_s, xbc_s, dt_s, q_s, k_s, v_s, gates_s = _in_proj(xs_pad, attn_nw, w_packed, w_blocks,
                                                        _pick_tile(ts * pad, 256), F32)
    dtT_s = dt_s.reshape(ts, pad, SSM_HEADS).transpose(0, 2, 1)
    spb = _pick_tile(ts, SAMPLE_SEQS_PER_STEP)
    hist_s = jnp.pad(state_conv[0], ((0, 0), (pad - (CONV_W - 1), 0), (0, 0))).reshape(ts * pad, CONV_DIM)
    y_ssm_s, ssm_s = _ssd(xbc_s, z_s, dt_s, dtT_s, state_ssm[0], ssd_consts, ts, 1, pad, 1, *conv_consts,
                          hist=hist_s, spb=spb)
    kc = cache_swa_k[0].reshape(ts * WINDOW, D_KV)
    vc = cache_swa_v[0].reshape(ts * WINDOW, D_KV)
    y_attn_s = _swa(slopes, sinks, q_s, k_s, v_s, kc, vc, ts, 1, pad, True)
    real = lambda a: a.reshape(ts, pad, -1)[:, 0]

    tm_p, tm_s = _pick_tile(tp, 512), ts
    nt_p = tp // tm_p
    lr_p, lr_s = _local_rows(tm_p, n_exp), _local_rows(tm_s, n_exp)
    if (nt_p * lr_p) % lr_s or lr_s > lr_p:
        lr_s = lr_p
    spare_rows = lr_s + (1 + n_exp * (PIECES_PER_TILE - 1)) * PIECE
    pad_steps = -(-spare_rows // lr_p)
    xs_rows = (nt_p + pad_steps) * lr_p
    block_s = nt_p * lr_p // lr_s
    post_w = (wsp, wap, wo, ffn_nw, w_rT, b_r)
    h_p, xs, lpos_p, pr_p, pc_p = _post(xp, y_ssm, y_attn, gates, *post_w, tm_p, lr_p, xs_rows, 0, pad_steps)
    h_s, xs, lpos_s, pr_s, pc_s = _post(x_sample.reshape(ts, D_MODEL), real(y_ssm_s), real(y_attn_s), real(gates_s),
                                        *post_w, tm_s, lr_s, xs_rows, block_s, 0, xs_prev=xs)

    padded_counts = jnp.concatenate([pc_p[:, :, 0], pc_s[:, :, 0]], axis=0)
    tile_row0 = jnp.concatenate([jnp.arange(nt_p, dtype=I32) * lr_p, jnp.full((1,), nt_p * lr_p, I32)])
    max_pieces = (TOP_K * (tp + ts) + (PIECE - 1) * n_exp * (nt_p + 1)) // PIECE + n_exp * (PIECES_PER_TILE - 1)
    n_tiles = -(-max_pieces // PIECES_PER_TILE) + 1
    tile_expert, n_used, src = _piece_table(padded_counts, tile_row0, n_tiles, (nt_p * lr_p + lr_s) // PIECE)
    ys = _moe(tile_expert, n_used, src, xs, w_gate[0], w_up[0], w_down[0], expert_biases)
    out_p = _combine(h_p, ys, lpos_p, pr_p, final_nw, tm_p, lr_p, 0)
    out_s = _combine(h_s, ys, lpos_s, pr_s, final_nw, tm_s, lr_s, block_s)

    y_prompt = out_p.reshape(nb, seq, D_MODEL)
    y_sample = out_s.reshape(nbs, 1, D_MODEL)
    conv_p = conv_tail[:, SUBLANES - (CONV_W - 1):][None]
    k_p = k_tail.reshape(1, nb, WINDOW, N_KV_HEADS, ATTN_HEAD_DIM)
    v_p = v_tail.reshape(1, nb, WINDOW, N_KV_HEADS, ATTN_HEAD_DIM)
    conv_s = jnp.concatenate([state_conv[0][:, 1:], real(xbc_s)[:, None]], axis=1)[None]
    k_new = real(k_s).reshape(ts, 1, N_KV_HEADS, ATTN_HEAD_DIM)
    v_new = real(v_s).reshape(ts, 1, N_KV_HEADS, ATTN_HEAD_DIM)
    ks_out = jnp.concatenate([cache_swa_k[0][:, 1:], k_new], axis=1)[None]
    vs_out = jnp.concatenate([cache_swa_v[0][:, 1:], v_new], axis=1)[None]
    return (y_prompt, y_sample, conv_p, ssm_p[None], k_p, v_p, conv_s, ssm_s[None], ks_out, vs_out)
```

```python
import functools

import jax
import jax.numpy as jnp
from jax import lax
from jax.experimental import pallas as pl
from jax.experimental.pallas import tpu as pltpu

F32, BF16, I32 = jnp.float32, jnp.bfloat16, jnp.int32

D_MODEL = 1024
D_INNER = 2 * D_MODEL
SSM_HEAD_DIM = 64
SSM_HEADS = D_INNER // SSM_HEAD_DIM
SSM_GROUPS = 4
SSM_HPG = SSM_HEADS // SSM_GROUPS
D_STATE = 128
CONV_W = 4
CONV_DIM = D_INNER + 2 * SSM_GROUPS * D_STATE
CHUNK = 128
ATTN_HEAD_DIM = 64
N_Q_HEADS = D_MODEL // ATTN_HEAD_DIM
N_KV_HEADS = 4
Q_PER_KV = N_Q_HEADS // N_KV_HEADS
D_ATTN = N_Q_HEADS * ATTN_HEAD_DIM
D_KV = N_KV_HEADS * ATTN_HEAD_DIM
WINDOW = 128
TOP_K = 4
D_FF = D_MODEL
SWIGLU_LIMIT = 7.0
SWIGLU_ALPHA = 1.702
EPS = 1e-5
NEG_BIG = -1e30

LANES = 128
SUBLANES = 8
GROUP_W = D_INNER // SSM_GROUPS
PIECE = 2 * SUBLANES
PIECES_PER_TILE = 32
SAMPLE_SEQS_PER_STEP = 4
VMEM_LIMIT = 56 * 1024 * 1024

NT = (((1,), (1,)), ((), ()))
TN = (((0,), (0,)), ((), ()))


def _const_spec(shape):
    return pl.BlockSpec(shape, lambda *_: (0,) * len(shape))


def _resident_spec(shape):
    return pl.BlockSpec(shape, lambda *_: (0,) * len(shape), pipeline_mode=pl.Buffered(1))


def _split3(x):
    hi = x.astype(BF16)
    r1 = x - hi.astype(F32)
    mid = r1.astype(BF16)
    lo = (r1 - mid.astype(F32)).astype(BF16)
    return hi, mid, lo


def _softplus(x):
    return jnp.maximum(x, 0.0) + jnp.log(1.0 + jnp.exp(-jnp.abs(x)))


def _causal_conv_silu(buf_ref, n, cw_ref, cb_ref):
    full = buf_ref[...]
    conv = cb_ref[...]
    for j in range(CONV_W):
        shifted = full if j == CONV_W - 1 else pltpu.roll(full, CONV_W - 1 - j, 0)
        conv = conv + shifted[SUBLANES:SUBLANES + n, :] * cw_ref[j:j + 1, :]
    buf_ref[0:SUBLANES, :] = full[n:n + SUBLANES, :]
    return conv * jax.nn.sigmoid(conv)


def _inproj_kernel(x_ref, nw_ref, wz_ref, wxbc_ref, wdt_ref, wq_ref, wk_ref, wv_ref, wg_ref, *rest, tiles_per_seq):
    if tiles_per_seq:
        wdtT_ref, z_ref, xbc_ref, dt_ref, q_ref, k_ref, v_ref, g_ref, tail_ref, dtT_ref, ktail_ref, vtail_ref = rest
    else:
        z_ref, xbc_ref, dt_ref, q_ref, k_ref, v_ref, g_ref = rest
    tm = x_ref.shape[0]
    x = x_ref[...]
    xn = x * lax.rsqrt(jnp.mean(x * x, axis=-1, keepdims=True) + EPS)
    xn = (xn * nw_ref[...]).astype(BF16)
    for w_ref, o_ref in ((wz_ref, z_ref), (wdt_ref, dt_ref), (wq_ref, q_ref), (wg_ref, g_ref)):
        o_ref[...] = jnp.dot(xn, w_ref[...], preferred_element_type=F32).astype(o_ref.dtype)
    xbc = jnp.dot(xn, wxbc_ref[...], preferred_element_type=F32)
    k = jnp.dot(xn, wk_ref[...], preferred_element_type=F32)
    v = jnp.dot(xn, wv_ref[...], preferred_element_type=F32)
    xbc_ref[...] = xbc.astype(xbc_ref.dtype)
    k_ref[...] = k.astype(k_ref.dtype)
    v_ref[...] = v.astype(v_ref.dtype)
    if tiles_per_seq:
        dtT_ref[...] = lax.dot_general(wdtT_ref[...], xn, NT, preferred_element_type=F32)

        @pl.when(pl.program_id(0) % tiles_per_seq == tiles_per_seq - 1)
        def _():
            tail_ref[0] = xbc[tm - SUBLANES:, :]
            ktail_ref[0] = k[tm - WINDOW:, :]
            vtail_ref[0] = v[tm - WINDOW:, :]


IN_PROJ_WIDTHS = (D_INNER, CONV_DIM, SSM_HEADS, D_ATTN, D_KV, D_KV, 2 * D_MODEL)
IN_PROJ_PACK_ORDER = (1, 3, 0, 6, 4, 5, 2)


def _pack_in_proj_weight(w):
    cuts = [0]
    for n in IN_PROJ_WIDTHS:
        cuts.append(cuts[-1] + n)
    packed = jnp.concatenate([w[:, cuts[i]:cuts[i + 1]] for i in IN_PROJ_PACK_ORDER], axis=1).astype(BF16)
    block, off = [0] * len(IN_PROJ_WIDTHS), 0
    for i in IN_PROJ_PACK_ORDER:
        block[i], rem = divmod(off, IN_PROJ_WIDTHS[i])
        assert rem == 0
        off += IN_PROJ_WIDTHS[i]
    return packed, tuple(block)


def _in_proj(x, norm_w, w_packed, w_blocks, tm, act_dtype, tiles_per_seq=None):
    t = x.shape[0]
    widths = IN_PROJ_WIDTHS
    dtypes = (act_dtype, act_dtype, F32, act_dtype, act_dtype, act_dtype, act_dtype)
    row = lambda n: pl.BlockSpec((tm, n), lambda i: (i, 0))
    w_spec = lambda n, blk: pl.BlockSpec((D_MODEL, n), lambda i: (0, blk), pipeline_mode=pl.Buffered(1))
    dt_col = w_blocks[2] * SSM_HEADS
    w_dt = w_packed[:, dt_col:dt_col + SSM_HEADS]
    in_specs = [row(D_MODEL), _const_spec((1, D_MODEL))]
    args = [x, norm_w]
    for i, (n, blk) in enumerate(zip(widths, w_blocks)):
        in_specs.append(_resident_spec((D_MODEL, n)) if i == 2 else w_spec(n, blk))
        args.append(w_dt if i == 2 else w_packed)
    out_specs = [row(n) for n in widths]
    out_shape = [jax.ShapeDtypeStruct((t, n), d) for n, d in zip(widths, dtypes)]
    if tiles_per_seq:
        n_seq = t // (tm * tiles_per_seq)
        in_specs.append(_const_spec((SSM_HEADS, D_MODEL)))
        args.append(w_dt.T)
        per_seq = lambda r, c: pl.BlockSpec((1, r, c), lambda i: (i // tiles_per_seq, 0, 0))
        out_specs += [per_seq(SUBLANES, CONV_DIM), pl.BlockSpec((SSM_HEADS, tm), lambda i: (0, i)),
                      per_seq(WINDOW, D_KV), per_seq(WINDOW, D_KV)]
        out_shape += [jax.ShapeDtypeStruct((n_seq, SUBLANES, CONV_DIM), F32),
                      jax.ShapeDtypeStruct((SSM_HEADS, t), F32),
                      jax.ShapeDtypeStruct((n_seq, WINDOW, D_KV), F32),
                      jax.ShapeDtypeStruct((n_seq, WINDOW, D_KV), F32)]
    return pl.pallas_call(
        functools.partial(_inproj_kernel, tiles_per_seq=tiles_per_seq),
        grid=(t // tm,),
        in_specs=in_specs,
        out_specs=out_specs,
        out_shape=out_shape,
        compiler_params=pltpu.CompilerParams(dimension_semantics=("arbitrary",), vmem_limit_bytes=VMEM_LIMIT),
        name="in_proj",
    )(*args)


def _ssd_kernel(xbc_ref, z_ref, dt_ref, dtT_ref, h0_ref, dtb_ref, dtbT_ref, a_ref, aT_ref, dsk_ref, nw_ref, e_ref,
                *rest, lc, n_valid, conv, single_chunk):
    if conv == "history":
        hist_ref, cw_ref, cb_ref, y_ref, h_ref, buf_ref = rest
    else:
        cw_ref, cb_ref, y_ref, h_ref, tail_ref = rest

    def at_first_chunk(fn):
        if single_chunk:
            fn()
        else:
            pl.when(pl.program_id(1) == 0)(fn)

    @at_first_chunk
    def _():
        h_ref[...] = h0_ref[...]

    if conv == "history":
        @at_first_chunk
        def _():
            buf_ref[0:SUBLANES, :] = hist_ref[...]

        buf_ref[SUBLANES:SUBLANES + lc, :] = xbc_ref[...].astype(F32)
        act = _causal_conv_silu(buf_ref, lc, cw_ref, cb_ref)
    else:
        @at_first_chunk
        def _():
            tail_ref[...] = jnp.zeros_like(tail_ref)

        raw = xbc_ref[...]
        nt = tail_ref.shape[0]
        ext = jnp.concatenate([tail_ref[...], raw], axis=0)
        to = lax.broadcasted_iota(I32, (lc, nt + lc), 0)
        frm = lax.broadcasted_iota(I32, (lc, nt + lc), 1)
        conv_acc = cb_ref[...]
        for j in range(CONV_W - 1):
            shift = jnp.where(frm == to + (nt - (CONV_W - 1) + j), 1.0, 0.0).astype(BF16)
            conv_acc = conv_acc + jnp.dot(shift, ext, preferred_element_type=F32) * cw_ref[j:j + 1, :]
        conv_acc = conv_acc + raw.astype(F32) * cw_ref[CONV_W - 1:CONV_W, :]
        tail_ref[...] = raw[lc - nt:, :]
        act = conv_acc * jax.nn.sigmoid(conv_acc)
    xs = act[:, :D_INNER]
    bm = act[:, D_INNER:D_INNER + SSM_GROUPS * D_STATE].astype(BF16)
    cm = act[:, D_INNER + SSM_GROUPS * D_STATE:].astype(BF16)

    dt = _softplus(dt_ref[...] + dtb_ref[...])
    dtT_raw = dtT_ref[0] if len(dtT_ref.shape) == 3 else dtT_ref[...]
    dtT = _softplus(dtT_raw + dtbT_ref[...])
    if n_valid < lc:
        dt = jnp.where(lax.broadcasted_iota(I32, dt.shape, 0) < n_valid, dt, 0.0)
        dtT = jnp.where(lax.broadcasted_iota(I32, dtT.shape, 1) < n_valid, dtT, 0.0)
    la = dt * a_ref[...]
    laT = dtT * aT_ref[...]
    li = lax.broadcasted_iota(I32, (lc, lc), 0)
    si = lax.broadcasted_iota(I32, (lc, lc), 1)
    causal = li >= si
    tril = jnp.where(causal, 1.0, 0.0).astype(BF16)
    triu = jnp.where(li <= si, 1.0, 0.0).astype(BF16)
    cum = sum(jnp.dot(tril, p, preferred_element_type=F32) for p in _split3(la))
    cumT = sum(jnp.dot(p, triu, preferred_element_type=F32) for p in _split3(laT))
    ec = jnp.exp(cum)
    dte = jnp.exp(cum[lc - 1:lc, :] - cum)
    cd = jnp.exp(cumT[:, lc - 1:lc])

    def expand(v):
        hi, mid, _ = _split3(v)
        return jnp.dot(jnp.concatenate([hi, mid], axis=1), e_ref[...], preferred_element_type=F32)

    dt_x, ec_x, dte_x = expand(dt), expand(ec), expand(dte)
    xdt = xs * dt_x
    xdt_b = xdt.astype(BF16)
    xdte_b = (xdt * dte_x).astype(BF16)
    lane = lax.broadcasted_iota(I32, (lc, LANES), 1)
    low_half = lane < SSM_HEAD_DIM

    for g in range(SSM_GROUPS):
        gs = slice(g * GROUP_W, (g + 1) * GROUP_W)
        bm_g = bm[:, g * D_STATE:(g + 1) * D_STATE]
        cm_g = cm[:, g * D_STATE:(g + 1) * D_STATE]
        cb = lax.dot_general(cm_g, bm_g, NT, preferred_element_type=F32)
        cbm = jnp.where(causal, cb, 0.0)
        h_g = h_ref[0, g * SSM_HPG:(g + 1) * SSM_HPG].reshape(GROUP_W, D_STATE)
        y_off = lax.dot_general(cm_g, h_g.astype(BF16), NT, preferred_element_type=F32) * ec_x[:, gs]
        tiles = []
        for j in range(GROUP_W // LANES):
            col = g * GROUP_W + j * LANES
            x_pair = xdt_b[:, col:col + LANES]
            acc = None
            for half in range(2):
                h = col // SSM_HEAD_DIM + half
                seg = cum[:, h:h + 1] - cumT[h:h + 1, :]
                m = (cbm * jnp.exp(jnp.where(causal, seg, 0.0))).astype(BF16)
                x_h = jnp.where(low_half if half == 0 else jnp.logical_not(low_half), x_pair, jnp.zeros_like(x_pair))
                d = jnp.dot(m, x_h, preferred_element_type=F32)
                acc = d if acc is None else acc + d
            tiles.append(acc)
        y_g = jnp.concatenate(tiles, axis=1) + y_off + xs[:, gs] * dsk_ref[:, gs]
        zg = z_ref[:, gs].astype(F32)
        y_g = y_g * (zg * jax.nn.sigmoid(zg))
        y_g = y_g * lax.rsqrt(jnp.mean(y_g * y_g, axis=-1, keepdims=True) + EPS)
        y_ref[:, gs] = (y_g * nw_ref[:, gs]).astype(y_ref.dtype)
        st = lax.dot_general(xdte_b[:, gs], bm_g, TN, preferred_element_type=F32)
        for hh in range(SSM_HPG):
            h = g * SSM_HPG + hh
            rows = slice(hh * SSM_HEAD_DIM, (hh + 1) * SSM_HEAD_DIM)
            h_ref[0, h] = h_g[rows, :] * cd[h:h + 1, 0:1] + st[rows, :]


def _ssd_block_kernel(xbc_ref, z_ref, dt_ref, dtT_ref, h0_ref, *rest, spb, lc, n_consts, n_valid, conv):
    consts, rest = rest[:n_consts], rest[n_consts:]
    if conv == "history":
        hist_ref, cw_ref, cb_ref, y_ref, h_ref, scratch = rest
    else:
        cw_ref, cb_ref, y_ref, h_ref, scratch = rest
    for s in range(spb):
        rows, one = pl.ds(s * lc, lc), pl.ds(s, 1)
        conv_refs = (cw_ref, cb_ref)
        if conv == "history":
            conv_refs = (hist_ref.at[pl.ds(s * SUBLANES, SUBLANES)],) + conv_refs
        _ssd_kernel(xbc_ref.at[rows], z_ref.at[rows], dt_ref.at[rows], dtT_ref.at[one], h0_ref.at[one], *consts,
                    *conv_refs, y_ref.at[rows], h_ref.at[one], scratch,
                    lc=lc, n_valid=n_valid, conv=conv, single_chunk=True)


def _ssd(xbc, z, dt, dtT, h0, consts, nb, nc, lc, n_valid, conv_w, conv_b, hist=None, spb=1):
    assert spb == 1 or (nc == 1 and dtT.ndim == 3 and nb % spb == 0)
    t = xbc.shape[0]
    row = lambda n: pl.BlockSpec((spb * lc, n), lambda b, c: (b * nc + c, 0))
    if dtT.ndim == 2:
        dtT_spec = pl.BlockSpec((SSM_HEADS, lc), lambda b, c: (0, b * nc + c))
    else:
        dtT_spec = pl.BlockSpec((spb, SSM_HEADS, lc), lambda b, c: (b * nc + c, 0, 0))
    state_spec = pl.BlockSpec((spb, SSM_HEADS, SSM_HEAD_DIM, D_STATE), lambda b, c: (b, 0, 0, 0))
    in_specs = [row(CONV_DIM), row(D_INNER), row(SSM_HEADS), dtT_spec, state_spec] + [_const_spec(a.shape) for a in consts]
    args = [xbc, z, dt, dtT, h0, *consts]
    if hist is not None:
        in_specs.append(pl.BlockSpec((spb * SUBLANES, CONV_DIM), lambda b, c: (b, 0)))
        args.append(hist)
        scratch = [pltpu.VMEM((SUBLANES + lc, CONV_DIM), F32)]
    else:
        scratch = [pltpu.VMEM((PIECE, CONV_DIM), BF16)]
    in_specs += [_const_spec(conv_w.shape), _const_spec(conv_b.shape)]
    args += [conv_w, conv_b]
    conv = "history" if hist is not None else "fresh"
    if spb == 1:
        body = functools.partial(_ssd_kernel, lc=lc, n_valid=n_valid, conv=conv, single_chunk=nc == 1)
    else:
        body = functools.partial(_ssd_block_kernel, spb=spb, lc=lc, n_consts=len(consts), n_valid=n_valid, conv=conv)
    return pl.pallas_call(
        body,
        grid=(nb // spb, nc),
        in_specs=in_specs,
        out_specs=[row(D_INNER), state_spec],
        out_shape=[jax.ShapeDtypeStruct((t, D_INNER), BF16),
                   jax.ShapeDtypeStruct((nb, SSM_HEADS, SSM_HEAD_DIM, D_STATE), F32)],
        scratch_shapes=scratch,
        compiler_params=pltpu.CompilerParams(dimension_semantics=("arbitrary", "arbitrary"),
                                             vmem_limit_bytes=VMEM_LIMIT),
        name="ssd",
    )(*args)


def _swa_kernel(slope_ref, sink_ref, q_ref, kc_ref, vc_ref, kp_ref, vp_ref, y_ref, bias_ref, *, tq, prev_always,
                prev_by_head):
    fold = tq == WINDOW
    nk = WINDOW + tq

    @pl.when((pl.program_id(0) == 0) & (pl.program_id(1) == 0))
    def _():
        rows = WINDOW if fold else nk
        j = lax.broadcasted_iota(I32, (rows, tq), 0)
        r = lax.broadcasted_iota(I32, (rows, tq), 1)
        if fold:
            dist = jnp.where(j > r, r + WINDOW - j, r - j)
            valid, from_previous = dist >= 0, j > r
        else:
            dist = r + WINDOW - j
            valid, from_previous = (dist >= 0) & (dist < WINDOW), j < WINDOW
        distf = dist.astype(F32)
        for h in range(N_Q_HEADS):
            penalty = -slope_ref[h] * distf
            bias_ref[h] = jnp.where(valid, penalty, NEG_BIG)
            if not prev_always:
                bias_ref[N_Q_HEADS + h] = jnp.where(valid & jnp.logical_not(from_previous), penalty, NEG_BIG)

    first = 0 if prev_always else jnp.where(pl.program_id(1) == 0, N_Q_HEADS, 0)
    lane = lax.broadcasted_iota(I32, (nk, LANES), 1)
    zeros_half = jnp.zeros((ATTN_HEAD_DIM, nk), BF16)
    sink_row = lax.broadcasted_iota(I32, (SUBLANES, tq), 0) == 0
    ones_keys = jnp.ones((nk + SUBLANES, LANES), BF16)
    if fold:
        ji = lax.broadcasted_iota(I32, (WINDOW, tq), 0)
        ri = lax.broadcasted_iota(I32, (WINDOW, tq), 1)
        from_prev = ji > ri
        from_prev_b = ji.astype(BF16) > ri.astype(BF16)
    heads = []
    for t in range(D_KV // LANES):
        cols = slice(t * LANES, (t + 1) * LANES)
        def prev_tile(ref):
            if not prev_by_head:
                return ref[:, cols].astype(F32)
            return jnp.concatenate([ref[pl.ds(2 * t + b, WINDOW, stride=N_KV_HEADS), :] for b in range(2)],
                                   axis=1).astype(F32)

        kt = jnp.concatenate([prev_tile(kp_ref), kc_ref[:, cols].astype(F32)], axis=0)
        vt = jnp.concatenate([prev_tile(vp_ref), vc_ref[:, cols].astype(F32)], axis=0)
        if fold:
            vt_t = vt.T.astype(BF16)
        for b in range(2):
            mine = (lane >= ATTN_HEAD_DIM) if b else (lane < ATTN_HEAD_DIM)
            k_same = jnp.where(mine, kt, 0.0)
            k_half = {b: k_same.astype(BF16), 1 - b: pltpu.roll(k_same, ATTN_HEAD_DIM, 1).astype(BF16)}
            if fold:
                v_g = vt_t[b * ATTN_HEAD_DIM:(b + 1) * ATTN_HEAD_DIM, :]
                v_half = {0: jnp.concatenate([v_g, zeros_half], axis=0),
                          1: jnp.concatenate([zeros_half, v_g], axis=0)}
            else:
                v_same = jnp.concatenate([jnp.where(mine, vt, 0.0), jnp.zeros((SUBLANES, LANES), F32)], axis=0)
                v_half = {b: v_same.astype(BF16), 1 - b: pltpu.roll(v_same, ATTN_HEAD_DIM, 1).astype(BF16)}
            for qi in range(Q_PER_KV):
                a = qi % 2
                heads.append((k_half[a], v_half[a]))

    q_tiles = [(q_ref[:, jq * LANES:(jq + 1) * LANES] * (ATTN_HEAD_DIM ** -0.5)).astype(BF16)
               for jq in range(D_ATTN // LANES)]

    def scores(h):
        k, q = heads[h][0], q_tiles[h // 2]
        s = lax.dot_general(k, q, NT, preferred_element_type=F32)
        if fold:
            s = jnp.where(from_prev, s[:WINDOW], s[WINDOW:])
        return s + bias_ref[first + h]

    def attend(h, s):
        sink = sink_ref[h]
        v = heads[h][1]
        m = jnp.maximum(jnp.max(s, axis=0, keepdims=True), sink)
        e = jnp.exp(s - m)
        e_sink = jnp.exp(sink - m)
        if fold:
            rden = 1.0 / (jnp.sum(e, axis=0, keepdims=True) + e_sink)
            e_b = e.astype(BF16)
            zero = jnp.zeros_like(e_b)
            p = jnp.concatenate([jnp.where(from_prev_b, e_b, zero), jnp.where(from_prev_b, zero, e_b)], axis=0)
            return jnp.dot(v, p, preferred_element_type=F32) * rden
        p = jnp.concatenate([e, jnp.where(sink_row, e_sink, 0.0)], axis=0).astype(BF16)
        num = lax.dot_general(p, v, TN, preferred_element_type=F32)
        den = lax.dot_general(p, ones_keys, TN, preferred_element_type=F32)
        return num / den

    out_tiles = [None] * (D_ATTN // LANES)
    s_next = scores(0)
    for h in range(N_Q_HEADS):
        s_cur = s_next
        if h + 1 < N_Q_HEADS:
            s_next = scores(h + 1)
        o = attend(h, s_cur)
        out_tiles[h // 2] = o if out_tiles[h // 2] is None else out_tiles[h // 2] + o
    for jq, o in enumerate(out_tiles):
        y_ref[:, jq * LANES:(jq + 1) * LANES] = (o.T if fold else o).astype(y_ref.dtype)


def _swa(slopes, sinks, q, k, v, k_prev, v_prev, nb, nblk, tq, prev_always):
    t = q.shape[0]
    cur = lambda n: pl.BlockSpec((tq, n), lambda b, i: (b * nblk + i, 0))
    prev_by_head = k_prev.shape[-1] == ATTN_HEAD_DIM
    if prev_by_head:
        prev = pl.BlockSpec((WINDOW * N_KV_HEADS, ATTN_HEAD_DIM), lambda b, i: (b, 0))
    elif prev_always:
        prev = pl.BlockSpec((WINDOW, D_KV), lambda b, i: (b, 0))
    else:
        prev = pl.BlockSpec((WINDOW, D_KV), lambda b, i: (b * nblk + jnp.maximum(i - 1, 0), 0))
    smem = pl.BlockSpec(memory_space=pltpu.SMEM)
    return pl.pallas_call(
        functools.partial(_swa_kernel, tq=tq, prev_always=prev_always, prev_by_head=prev_by_head),
        grid=(nb, nblk),
        in_specs=[smem, smem, cur(D_ATTN), cur(D_KV), cur(D_KV), prev, prev],
        out_specs=cur(D_ATTN),
        out_shape=jax.ShapeDtypeStruct((t, D_ATTN), BF16),
        scratch_shapes=[pltpu.VMEM(((1 if prev_always else 2) * N_Q_HEADS, WINDOW if tq == WINDOW else WINDOW + tq,
                                    tq), F32)],
        compiler_params=pltpu.CompilerParams(dimension_semantics=("arbitrary", "arbitrary"),
                                             vmem_limit_bytes=VMEM_LIMIT),
        name="swa",
    )(slopes, sinks, q, k, v, k_prev, v_prev)


def _post_kernel(x_ref, ys_ref, ya_ref, g_ref, wsp_ref, wap_ref, wo_ref, nw_ref, wr_ref, br_ref, *rest,
                 n_exp, lr, nt, aliased):
    if aliased:
        rest = rest[1:]
    h_ref, xs_ref, lpos_ref, pr_ref, pc_ref = rest
    tp = x_ref.shape[0]

    @pl.when(pl.program_id(0) >= nt)
    def _():
        xs_ref[...] = jnp.zeros_like(xs_ref)

    @pl.when(pl.program_id(0) < nt)
    def _():
        a = jnp.dot(ys_ref[...].astype(BF16), wsp_ref[...], preferred_element_type=F32)
        b = jnp.dot(ya_ref[...].astype(BF16), wap_ref[...], preferred_element_type=F32)
        g = g_ref[...].astype(F32)
        merged = jax.nn.sigmoid(g[:, :D_MODEL]) * a + jax.nn.sigmoid(g[:, D_MODEL:]) * b
        h = x_ref[...] + jnp.dot(merged.astype(BF16), wo_ref[...], preferred_element_type=F32)
        h_ref[...] = h
        hn = h * lax.rsqrt(jnp.mean(h * h, axis=-1, keepdims=True) + EPS) * nw_ref[...]
        w_hi, w_mid, _ = _split3(wr_ref[...])
        x_hi, x_mid, _ = _split3(hn)
        both = lax.dot_general(jnp.concatenate([w_hi, w_mid], axis=0), x_hi, NT, preferred_element_type=F32)
        logits = (both[:n_exp] + lax.dot_general(w_hi, x_mid, NT, preferred_element_type=F32)
                  + both[n_exp:]) + br_ref[...]
        eidx = lax.broadcasted_iota(I32, logits.shape, 0).astype(F32)
        work = logits
        vals, ids = [], []
        for _ in range(TOP_K):
            m = jnp.max(work, axis=0, keepdims=True)
            first = jnp.min(jnp.where(work == m, eidx, float(n_exp)), axis=0, keepdims=True)
            vals.append(m)
            ids.append(first)
            work = jnp.where(eidx == first, -jnp.inf, work)
        es = [jnp.exp(v - vals[0]) for v in vals]
        den = es[0] + es[1] + es[2] + es[3]
        eye = jnp.where(lax.broadcasted_iota(I32, (TOP_K, TOP_K), 0) == lax.broadcasted_iota(I32, (TOP_K, TOP_K), 1),
                        1.0, 0.0).astype(BF16)

        def to_columns(rows):
            return sum(lax.dot_general(p, eye, TN, preferred_element_type=F32) for p in _split3(rows))

        onehot = [jnp.where(eidx == i, 1.0, 0.0) for i in ids]
        counts = [jnp.sum(o, axis=1, keepdims=True) for o in onehot]
        total = counts[0] + counts[1] + counts[2] + counts[3]
        padded = jnp.floor((total + (PIECE - 1)) * (1.0 / PIECE)) * PIECE
        ei = lax.broadcasted_iota(I32, (n_exp, n_exp), 0)
        ej = lax.broadcasted_iota(I32, (n_exp, n_exp), 1)
        below = jnp.where(ej < ei, 1.0, 0.0).astype(BF16)
        padded_b = jnp.broadcast_to(padded, (n_exp, LANES))
        seg_off = sum(jnp.dot(below, p, preferred_element_type=F32) for p in _split3(padded_b))[:, 0:1]
        ti = lax.broadcasted_iota(I32, (tp, tp), 0)
        tj = lax.broadcasted_iota(I32, (tp, tp), 1)
        before = jnp.where(ti < tj, 1.0, 0.0).astype(BF16)
        base = seg_off
        lpos = []
        prefixes = jnp.dot(jnp.concatenate(onehot, axis=0).astype(BF16), before,
                           preferred_element_type=F32)
        for k in range(TOP_K):
            prefix = prefixes[k * n_exp:(k + 1) * n_exp]
            lpos.append(jnp.sum(onehot[k] * (base + prefix), axis=0, keepdims=True))
            base = base + counts[k]
        pc_ref[0] = padded_b.astype(I32)

        hn_b = hn.astype(BF16)
        rc = lr // 4
        for c in range(4):
            ri = (lax.broadcasted_iota(I32, (rc, tp), 0) + c * rc).astype(F32)
            sel = jnp.zeros((rc, tp), F32)
            for k in range(TOP_K):
                sel = jnp.where(ri == lpos[k], 1.0, sel)
            xs_ref[c * rc:(c + 1) * rc, :] = jnp.dot(sel.astype(BF16), hn_b, preferred_element_type=F32).astype(BF16)

        pr_ref[...] = to_columns(jnp.concatenate([e / den for e in es], axis=0))
        lpos_ref[...] = to_columns(jnp.concatenate(lpos, axis=0)).astype(I32)


def _post(x, y_ssm, y_attn, gates, wsp, wap, wo, ffn_nw, w_rT, b_r, tm, lr, xs_rows, xs_block0, pad_steps,
          xs_prev=None):
    t = x.shape[0]
    n_exp = w_rT.shape[0]
    nt = t // tm
    last = nt - 1
    row = lambda n: pl.BlockSpec((tm, n), lambda i: (jnp.minimum(i, last), 0))
    col = row(TOP_K)
    in_specs = [row(D_MODEL), row(D_INNER), row(D_ATTN), row(2 * D_MODEL),
                _resident_spec(wsp.shape), _resident_spec(wap.shape), _resident_spec(wo.shape),
                _const_spec(ffn_nw.shape), _const_spec(w_rT.shape), _const_spec(b_r.shape)]
    args = [x, y_ssm, y_attn, gates, wsp, wap, wo, ffn_nw, w_rT, b_r]
    aliases = {}
    if xs_prev is not None:
        in_specs.append(pl.BlockSpec(memory_space=pl.ANY))
        args.append(xs_prev)
        aliases = {len(args) - 1: 1}
    return pl.pallas_call(
        functools.partial(_post_kernel, n_exp=n_exp, lr=lr, nt=nt, aliased=xs_prev is not None),
        grid=(nt + pad_steps,),
        in_specs=in_specs,
        out_specs=[row(D_MODEL), pl.BlockSpec((lr, D_MODEL), lambda i: (xs_block0 + i, 0)), col, col,
                   pl.BlockSpec((1, n_exp, LANES), lambda i: (jnp.minimum(i, last), 0, 0))],
        out_shape=[jax.ShapeDtypeStruct((t, D_MODEL), F32), jax.ShapeDtypeStruct((xs_rows, D_MODEL), BF16),
                   jax.ShapeDtypeStruct((t, TOP_K), I32), jax.ShapeDtypeStruct((t, TOP_K), F32),
                   jax.ShapeDtypeStruct((nt, n_exp, LANES), I32)],
        input_output_aliases=aliases,
        compiler_params=pltpu.CompilerParams(dimension_semantics=("arbitrary",), vmem_limit_bytes=VMEM_LIMIT),
        name="post",
    )(*args)


def _piece(ref, p):
    return ref.at[pl.ds(pl.multiple_of(p * PIECE, PIECE), PIECE)]


def _moe_kernel(te_ref, nu_ref, src_ref, xs_hbm, wg_ref, wu_ref, wd_ref, b_ref, ys_hbm, xbuf, obuf, wbf, gsem, ssem):
    i = pl.program_id(0)
    n_used = nu_ref[0]
    slot = i % 2
    tme = PIECES_PER_TILE * PIECE

    def gather(tile, s):
        for r in range(PIECES_PER_TILE):
            pltpu.make_async_copy(_piece(xs_hbm, src_ref[tile * PIECES_PER_TILE + r]),
                                  xbuf.at[s, pl.ds(r * PIECE, PIECE)], gsem.at[s]).start()

    def wait_gather(s):
        pltpu.make_async_copy(xs_hbm.at[pl.ds(0, tme)], xbuf.at[s], gsem.at[s]).wait()

    def wait_put(s):
        pltpu.make_async_copy(obuf.at[s], ys_hbm.at[pl.ds(0, tme)], ssem.at[s]).wait()

    @pl.when(i == 0)
    def _():
        gather(0, 0)

    @pl.when(i < n_used)
    def _():
        gather(i + 1, 1 - slot)
        wait_gather(slot)

        @pl.when(i >= 2)
        def _():
            wait_put(slot)

        @pl.when((i == 0) | (te_ref[i] != te_ref[jnp.maximum(i - 1, 0)]))
        def _():
            wbf[0] = wg_ref[0].astype(BF16)
            wbf[1] = wu_ref[0].astype(BF16)
            wbf[2] = wd_ref[0].astype(BF16)

        x = xbuf[slot]
        g = jnp.minimum(jnp.dot(x, wbf[0], preferred_element_type=F32) + b_ref[0, 0:1, :], SWIGLU_LIMIT)
        u = jnp.clip(jnp.dot(x, wbf[1], preferred_element_type=F32) + b_ref[0, 1:2, :], -SWIGLU_LIMIT, SWIGLU_LIMIT)
        act = ((u + 1.0) * g * jax.nn.sigmoid(SWIGLU_ALPHA * g)).astype(BF16)
        obuf[slot] = (jnp.dot(act, wbf[2], preferred_element_type=F32) + b_ref[0, 2:3, :]).astype(BF16)
        for r in range(PIECES_PER_TILE):
            pltpu.make_async_copy(obuf.at[slot, pl.ds(r * PIECE, PIECE)],
                                  _piece(ys_hbm, src_ref[i * PIECES_PER_TILE + r]), ssem.at[slot]).start()

        @pl.when(i == n_used - 1)
        def _():
            wait_put(slot)
            wait_gather(1 - slot)

            @pl.when(i >= 1)
            def _():
                wait_put(1 - slot)


def _moe(tile_expert, n_used, src, xs, wg, wu, wd, biases):
    n_tiles = tile_expert.shape[0]
    tme = PIECES_PER_TILE * PIECE
    wspec = pl.BlockSpec((1, D_MODEL, D_FF), lambda i, te, nu, sr: (te[i], 0, 0))
    grid_spec = pltpu.PrefetchScalarGridSpec(
        num_scalar_prefetch=3,
        grid=(n_tiles,),
        in_specs=[pl.BlockSpec(memory_space=pl.ANY), wspec, wspec, wspec,
                  pl.BlockSpec((1, 3, D_FF), lambda i, te, nu, sr: (te[i], 0, 0))],
        out_specs=pl.BlockSpec(memory_space=pl.ANY),
        scratch_shapes=[pltpu.VMEM((2, tme, D_MODEL), BF16), pltpu.VMEM((2, tme, D_MODEL), BF16),
                        pltpu.VMEM((3, D_MODEL, D_FF), BF16),
                        pltpu.SemaphoreType.DMA((2,)), pltpu.SemaphoreType.DMA((2,))],
    )
    return pl.pallas_call(
        _moe_kernel,
        grid_spec=grid_spec,
        out_shape=jax.ShapeDtypeStruct(xs.shape, xs.dtype),
        input_output_aliases={3: 0},
        compiler_params=pltpu.CompilerParams(dimension_semantics=("arbitrary",), vmem_limit_bytes=VMEM_LIMIT),
        name="moe",
    )(tile_expert, n_used, src, xs, wg, wu, wd, biases)


def _combine_kernel(h_ref, ys_ref, lpos_ref, pr_ref, nw_ref, o_ref):
    tp = h_ref.shape[0]
    lr = ys_ref.shape[0]
    ri = lax.broadcasted_iota(I32, (tp, lr), 1)
    lp = lpos_ref[...]
    pr = pr_ref[...]
    pw = jnp.zeros((tp, lr), F32)
    for k in range(TOP_K):
        pw = jnp.where(ri == lp[:, k:k + 1], pr[:, k:k + 1], pw)
    moe = jnp.dot(pw.astype(BF16), ys_ref[...], preferred_element_type=F32)
    h = h_ref[...] + moe
    o_ref[...] = h * lax.rsqrt(jnp.mean(h * h, axis=-1, keepdims=True) + EPS) * nw_ref[...]


def _combine(h, ys, lpos_t, probs_t, final_nw, tm, lr, ys_block0):
    t = h.shape[0]
    return pl.pallas_call(
        _combine_kernel,
        grid=(t // tm,),
        in_specs=[pl.BlockSpec((tm, D_MODEL), lambda i: (i, 0)),
                  pl.BlockSpec((lr, D_MODEL), lambda i: (ys_block0 + i, 0)),
                  pl.BlockSpec((tm, TOP_K), lambda i: (i, 0)),
                  pl.BlockSpec((tm, TOP_K), lambda i: (i, 0)),
                  _const_spec(final_nw.shape)],
        out_specs=pl.BlockSpec((tm, D_MODEL), lambda i: (i, 0)),
        out_shape=jax.ShapeDtypeStruct((t, D_MODEL), F32),
        compiler_params=pltpu.CompilerParams(dimension_semantics=("arbitrary",), vmem_limit_bytes=VMEM_LIMIT),
        name="combine",
    )(h, ys, lpos_t, probs_t, final_nw)


def _piece_table(padded_counts, tile_row0, n_tiles, spare_piece0):
    n_pieces = (padded_counts // PIECE).T
    seg_row = tile_row0[:, None] + jnp.cumsum(padded_counts, axis=1) - padded_counts
    seg_piece = (seg_row // PIECE).T
    per_expert = n_pieces.sum(axis=1)
    tiles_per = (per_expert + PIECES_PER_TILE - 1) // PIECES_PER_TILE
    tile_end = jnp.cumsum(tiles_per)
    n_used = tile_end[-1]
    slot0 = (tile_end - tiles_per) * PIECES_PER_TILE
    seg_slot = (slot0[:, None] + jnp.cumsum(n_pieces, axis=1) - n_pieces).reshape(-1)
    seg_n = n_pieces.reshape(-1)
    seg_src = seg_piece.reshape(-1)
    slots = jnp.arange(n_tiles * PIECES_PER_TILE, dtype=I32)

    def at_segment_of_slot(f):
        df = f - jnp.concatenate([jnp.zeros((1,), I32), f[:-1]])
        return jnp.sum(jnp.where(seg_slot[None, :] <= slots[:, None], df[None, :], 0), axis=1)

    real = slots < at_segment_of_slot(seg_slot + seg_n)
    padding = jnp.logical_not(real) & (slots < n_used * PIECES_PER_TILE)
    spare = spare_piece0 + jnp.where(padding, jnp.cumsum(padding.astype(I32)), 0)
    src = jnp.where(real, slots + at_segment_of_slot(seg_src - seg_slot), spare).astype(I32)
    tile_ids = jnp.arange(n_tiles, dtype=I32)
    tile_expert = jnp.sum((tile_end[None, :] <= jnp.minimum(tile_ids, n_used - 1)[:, None]).astype(I32), axis=1)
    return tile_expert.astype(I32), n_used.reshape(1).astype(I32), src


def _pick_tile(n, pref):
    while n % pref:
        pref //= 2
    return pref


def _local_rows(tm, n_exp):
    need = TOP_K * tm + n_exp * (PIECE - 1)
    return -(-need // 64) * 64


def kernel(x_prompt, x_sample, state_conv, state_ssm, cache_swa_k, cache_swa_v, attn_norm_w, w_in, conv_w, conv_b, dt_bias, a_log, d_skip, ssm_norm_w, attn_sinks, w_ssm_proj, w_attn_proj, w_o, ffn_norm_w, w_router, b_router, w_gate, b_gate, w_up, b_up, w_down, b_down, final_norm_w):
    assert w_in.shape[0] == 1, "single-layer step"
    nb, seq, _ = x_prompt.shape
    nbs = x_sample.shape[0]
    n_exp = w_router.shape[-1]
    tp, ts = nb * seq, nbs
    pad = SUBLANES

    w_packed, w_blocks = _pack_in_proj_weight(w_in[0])
    attn_nw = attn_norm_w[0].reshape(1, D_MODEL)
    a_neg = -jnp.exp(a_log[0].astype(F32))
    head_of = jnp.arange(D_INNER, dtype=I32) // SSM_HEAD_DIM
    expand = jnp.tile((jnp.arange(SSM_HEADS, dtype=I32)[:, None] == head_of[None, :]).astype(BF16), (2, 1))
    conv_consts = (conv_w[0], conv_b[0].reshape(1, CONV_DIM))
    ssd_consts = (dt_bias[0].reshape(1, SSM_HEADS), dt_bias[0].reshape(SSM_HEADS, 1),
                  a_neg.reshape(1, SSM_HEADS), a_neg.reshape(SSM_HEADS, 1),
                  d_skip[0][head_of].reshape(1, D_INNER), ssm_norm_w[0].reshape(1, D_INNER), expand)
    slopes = jnp.exp2(-8.0 * jnp.arange(1, N_Q_HEADS + 1, dtype=F32) / N_Q_HEADS)
    sinks = attn_sinks[0].astype(F32)
    wsp, wap, wo = w_ssm_proj[0].astype(BF16), w_attn_proj[0].astype(BF16), w_o[0].astype(BF16)
    ffn_nw = ffn_norm_w[0].reshape(1, D_MODEL)
    w_rT = w_router[0].T
    b_r = b_router[0].reshape(n_exp, 1)
    expert_biases = jnp.stack([b_gate[0], b_up[0], b_down[0]], axis=1)
    final_nw = final_norm_w.reshape(1, D_MODEL)

    xp = x_prompt.reshape(tp, D_MODEL)
    tm_in = _pick_tile(seq, 512)
    z, xbc, dt, q, k, v, gates, conv_tail, dtT, k_tail, v_tail = _in_proj(
        xp, attn_nw, w_packed, w_blocks, tm_in, BF16, tiles_per_seq=seq // tm_in)
    nc = seq // CHUNK
    y_ssm, ssm_p = _ssd(xbc, z, dt, dtT, jnp.zeros((nb, SSM_HEADS, SSM_HEAD_DIM, D_STATE), F32), ssd_consts,
                        nb, nc, CHUNK, CHUNK, *conv_consts)
    nblk = seq // WINDOW
    y_attn = _swa(slopes, sinks, q, k, v, k, v, nb, nblk, WINDOW, False)

    xs_pad = jnp.pad(x_sample.reshape(ts, 1, D_MODEL), ((0, 0), (0, pad - 1), (0, 0))).reshape(ts * pad, D_MODEL)
    z_s, xbc_s, dt_s, q_s, k_s, v_s, gates_s = _in_proj(xs_pad, attn_nw, w_packed, w_blocks,
                                                        _pick_tile(ts * pad, 256), F32)
    dtT_s = dt_s.reshape(ts, pad, SSM_HEADS).transpose(0, 2, 1)
    spb = _pick_tile(ts, SAMPLE_SEQS_PER_STEP)
    hist_s = jnp.pad(state_conv[0], ((0, 0), (pad - (CONV_W - 1), 0), (0, 0))).reshape(ts * pad, CONV_DIM)
    y_ssm_s, ssm_s = _ssd(xbc_s, z_s, dt_s, dtT_s, state_ssm[0], ssd_consts, ts, 1, pad, 1, *conv_consts,
                          hist=hist_s, spb=spb)
    kc = cache_swa_k[0].reshape(ts * WINDOW * N_KV_HEADS, ATTN_HEAD_DIM)
    vc = cache_swa_v[0].reshape(ts * WINDOW * N_KV_HEADS, ATTN_HEAD_DIM)
    y_attn_s = _swa(slopes, sinks, q_s, k_s, v_s, kc, vc, ts, 1, pad, True)
    real = lambda a: a.reshape(ts, pad, -1)[:, 0]

    tm_p, tm_s = _pick_tile(tp, 512), ts
    nt_p = tp // tm_p
    lr_p, lr_s = _local_rows(tm_p, n_exp), _local_rows(tm_s, n_exp)
    if (nt_p * lr_p) % lr_s or lr_s > lr_p:
        lr_s = lr_p
    spare_rows = lr_s + (1 + n_exp * (PIECES_PER_TILE - 1)) * PIECE
    pad_steps = -(-spare_rows // lr_p)
    xs_rows = (nt_p + pad_steps) * lr_p
    block_s = nt_p * lr_p // lr_s
    post_w = (wsp, wap, wo, ffn_nw, w_rT, b_r)
    h_p, xs, lpos_p, pr_p, pc_p = _post(xp, y_ssm, y_attn, gates, *post_w, tm_p, lr_p, xs_rows, 0, pad_steps)
    h_s, xs, lpos_s, pr_s, pc_s = _post(x_sample.reshape(ts, D_MODEL), real(y_ssm_s), real(y_attn_s), real(gates_s),
                                        *post_w, tm_s, lr_s, xs_rows, block_s, 0, xs_prev=xs)

    padded_counts = jnp.concatenate([pc_p[:, :, 0], pc_s[:, :, 0]], axis=0)
    tile_row0 = jnp.concatenate([jnp.arange(nt_p, dtype=I32) * lr_p, jnp.full((1,), nt_p * lr_p, I32)])
    max_pieces = (TOP_K * (tp + ts) + (PIECE - 1) * n_exp * (nt_p + 1)) // PIECE + n_exp * (PIECES_PER_TILE - 1)
    n_tiles = -(-max_pieces // PIECES_PER_TILE) + 1
    tile_expert, n_used, src = _piece_table(padded_counts, tile_row0, n_tiles, (nt_p * lr_p + lr_s) // PIECE)
    ys = _moe(tile_expert, n_used, src, xs, w_gate[0], w_up[0], w_down[0], expert_biases)
    out_p = _combine(h_p, ys, lpos_p, pr_p, final_nw, tm_p, lr_p, 0)
    out_s = _combine(h_s, ys, lpos_s, pr_s, final_nw, tm_s, lr_s, block_s)

    y_prompt = out_p.reshape(nb, seq, D_MODEL)
    y_sample = out_s.reshape(nbs, 1, D_MODEL)
    conv_p = conv_tail[:, SUBLANES - (CONV_W - 1):][None]
    k_p = k_tail.reshape(1, nb, WINDOW, N_KV_HEADS, ATTN_HEAD_DIM)
    v_p = v_tail.reshape(1, nb, WINDOW, N_KV_HEADS, ATTN_HEAD_DIM)
    conv_s = jnp.concatenate([state_conv[0][:, 1:], real(xbc_s)[:, None]], axis=1)[None]
    k_new = real(k_s).reshape(ts, 1, N_KV_HEADS, ATTN_HEAD_DIM)
    v_new = real(v_s).reshape(ts, 1, N_KV_HEADS, ATTN_HEAD_DIM)
    ks_out = jnp.concatenate([cache_swa_k[0][:, 1:], k_new], axis=1)[None]
    vs_out = jnp.concatenate([cache_swa_v[0][:, 1:], v_new], axis=1)[None]
    return (y_prompt, y_sample, conv_p, ssm_p[None], k_p, v_p, conv_s, ssm_s[None], ks_out, vs_out)
```

```python
import functools

import jax
import jax.numpy as jnp
from jax import lax
from jax.experimental import pallas as pl
from jax.experimental.pallas import tpu as pltpu

F32, BF16, I32 = jnp.float32, jnp.bfloat16, jnp.int32

D_MODEL = 1024
D_INNER = 2 * D_MODEL
SSM_HEAD_DIM = 64
SSM_HEADS = D_INNER // SSM_HEAD_DIM
SSM_GROUPS = 4
SSM_HPG = SSM_HEADS // SSM_GROUPS
D_STATE = 128
CONV_W = 4
CONV_DIM = D_INNER + 2 * SSM_GROUPS * D_STATE
CHUNK = 128
ATTN_HEAD_DIM = 64
N_Q_HEADS = D_MODEL // ATTN_HEAD_DIM
N_KV_HEADS = 4
Q_PER_KV = N_Q_HEADS // N_KV_HEADS
D_ATTN = N_Q_HEADS * ATTN_HEAD_DIM
D_KV = N_KV_HEADS * ATTN_HEAD_DIM
WINDOW = 128
TOP_K = 4
D_FF = D_MODEL
SWIGLU_LIMIT = 7.0
SWIGLU_ALPHA = 1.702
EPS = 1e-5
NEG_BIG = -1e30

LANES = 128
SUBLANES = 8
GROUP_W = D_INNER // SSM_GROUPS
PIECE = 2 * SUBLANES
PIECES_PER_TILE = 32
SAMPLE_SEQS_PER_STEP = 4
VMEM_LIMIT = 56 * 1024 * 1024

NT = (((1,), (1,)), ((), ()))
TN = (((0,), (0,)), ((), ()))


def _const_spec(shape):
    return pl.BlockSpec(shape, lambda *_: (0,) * len(shape))


def _resident_spec(shape):
    return pl.BlockSpec(shape, lambda *_: (0,) * len(shape), pipeline_mode=pl.Buffered(1))


def _split3(x):
    hi = x.astype(BF16)
    r1 = x - hi.astype(F32)
    mid = r1.astype(BF16)
    lo = (r1 - mid.astype(F32)).astype(BF16)
    return hi, mid, lo


def _softplus(x):
    return jnp.maximum(x, 0.0) + jnp.log(1.0 + jnp.exp(-jnp.abs(x)))


def _causal_conv_silu(buf_ref, n, cw_ref, cb_ref):
    full = buf_ref[...]
    conv = cb_ref[...]
    for j in range(CONV_W):
        shifted = full if j == CONV_W - 1 else pltpu.roll(full, CONV_W - 1 - j, 0)
        conv = conv + shifted[SUBLANES:SUBLANES + n, :] * cw_ref[j:j + 1, :]
    buf_ref[0:SUBLANES, :] = full[n:n + SUBLANES, :]
    return conv * jax.nn.sigmoid(conv)


def _inproj_kernel(x_ref, nw_ref, wz_ref, wxbc_ref, wdt_ref, wq_ref, wk_ref, wv_ref, wg_ref, *rest, tiles_per_seq):
    if tiles_per_seq:
        wdtT_ref, z_ref, xbc_ref, dt_ref, q_ref, k_ref, v_ref, g_ref, tail_ref, dtT_ref, ktail_ref, vtail_ref = rest
    else:
        z_ref, xbc_ref, dt_ref, q_ref, k_ref, v_ref, g_ref = rest
    tm = x_ref.shape[0]
    x = x_ref[...]
    xn = x * lax.rsqrt(jnp.mean(x * x, axis=-1, keepdims=True) + EPS)
    xn = (xn * nw_ref[...]).astype(BF16)
    for w_ref, o_ref in ((wz_ref, z_ref), (wdt_ref, dt_ref), (wq_ref, q_ref), (wg_ref, g_ref)):
        o_ref[...] = jnp.dot(xn, w_ref[...], preferred_element_type=F32).astype(o_ref.dtype)
    xbc = jnp.dot(xn, wxbc_ref[...], preferred_element_type=F32)
    k = jnp.dot(xn, wk_ref[...], preferred_element_type=F32)
    v = jnp.dot(xn, wv_ref[...], preferred_element_type=F32)
    xbc_ref[...] = xbc.astype(xbc_ref.dtype)
    k_ref[...] = k.astype(k_ref.dtype)
    v_ref[...] = v.astype(v_ref.dtype)
    if tiles_per_seq:
        dtT_ref[...] = lax.dot_general(wdtT_ref[...], xn, NT, preferred_element_type=F32)

        @pl.when(pl.program_id(0) % tiles_per_seq == tiles_per_seq - 1)
        def _():
            tail_ref[0] = xbc[tm - SUBLANES:, :]
            ktail_ref[0] = k[tm - WINDOW:, :]
            vtail_ref[0] = v[tm - WINDOW:, :]


IN_PROJ_WIDTHS = (D_INNER, CONV_DIM, SSM_HEADS, D_ATTN, D_KV, D_KV, 2 * D_MODEL)
IN_PROJ_PACK_ORDER = (1, 3, 0, 6, 4, 5, 2)


def _pack_in_proj_weight(w):
    cuts = [0]
    for n in IN_PROJ_WIDTHS:
        cuts.append(cuts[-1] + n)
    packed = jnp.concatenate([w[:, cuts[i]:cuts[i + 1]] for i in IN_PROJ_PACK_ORDER], axis=1).astype(BF16)
    block, off = [0] * len(IN_PROJ_WIDTHS), 0
    for i in IN_PROJ_PACK_ORDER:
        block[i], rem = divmod(off, IN_PROJ_WIDTHS[i])
        assert rem == 0
        off += IN_PROJ_WIDTHS[i]
    return packed, tuple(block)


def _in_proj(x, norm_w, w_packed, w_blocks, tm, act_dtype, tiles_per_seq=None):
    t = x.shape[0]
    widths = IN_PROJ_WIDTHS
    dtypes = (act_dtype, act_dtype, F32, act_dtype, act_dtype, act_dtype, act_dtype)
    row = lambda n: pl.BlockSpec((tm, n), lambda i: (i, 0))
    w_spec = lambda n, blk: pl.BlockSpec((D_MODEL, n), lambda i: (0, blk), pipeline_mode=pl.Buffered(1))
    dt_col = w_blocks[2] * SSM_HEADS
    w_dt = w_packed[:, dt_col:dt_col + SSM_HEADS]
    in_specs = [row(D_MODEL), _const_spec((1, D_MODEL))]
    args = [x, norm_w]
    for i, (n, blk) in enumerate(zip(widths, w_blocks)):
        in_specs.append(_resident_spec((D_MODEL, n)) if i == 2 else w_spec(n, blk))
        args.append(w_dt if i == 2 else w_packed)
    out_specs = [row(n) for n in widths]
    out_shape = [jax.ShapeDtypeStruct((t, n), d) for n, d in zip(widths, dtypes)]
    if tiles_per_seq:
        n_seq = t // (tm * tiles_per_seq)
        in_specs.append(_const_spec((SSM_HEADS, D_MODEL)))
        args.append(w_dt.T)
        per_seq = lambda r, c: pl.BlockSpec((1, r, c), lambda i: (i // tiles_per_seq, 0, 0))
        out_specs += [per_seq(SUBLANES, CONV_DIM), pl.BlockSpec((SSM_HEADS, tm), lambda i: (0, i)),
                      per_seq(WINDOW, D_KV), per_seq(WINDOW, D_KV)]
        out_shape += [jax.ShapeDtypeStruct((n_seq, SUBLANES, CONV_DIM), F32),
                      jax.ShapeDtypeStruct((SSM_HEADS, t), F32),
                      jax.ShapeDtypeStruct((n_seq, WINDOW, D_KV), F32),
                      jax.ShapeDtypeStruct((n_seq, WINDOW, D_KV), F32)]
    return pl.pallas_call(
        functools.partial(_inproj_kernel, tiles_per_seq=tiles_per_seq),
        grid=(t // tm,),
        in_specs=in_specs,
        out_specs=out_specs,
        out_shape=out_shape,
        compiler_params=pltpu.CompilerParams(dimension_semantics=("arbitrary",), vmem_limit_bytes=VMEM_LIMIT),
        name="in_proj",
    )(*args)


def _ssd_kernel(xbc_ref, z_ref, dt_ref, dtT_ref, h0_ref, dtb_ref, dtbT_ref, a_ref, aT_ref, dsk_ref, nw_ref, e_ref,
                *rest, lc, n_valid, conv, single_chunk):
    if conv == "history":
        hist_ref, cw_ref, cb_ref, y_ref, h_ref, buf_ref = rest
    else:
        cw_ref, cb_ref, y_ref, h_ref, tail_ref = rest

    def at_first_chunk(fn):
        if single_chunk:
            fn()
        else:
            pl.when(pl.program_id(1) == 0)(fn)

    @at_first_chunk
    def _():
        h_ref[...] = h0_ref[...]

    if conv == "history":
        @at_first_chunk
        def _():
            buf_ref[0:SUBLANES, :] = hist_ref[...]

        buf_ref[SUBLANES:SUBLANES + lc, :] = xbc_ref[...].astype(F32)
        act = _causal_conv_silu(buf_ref, lc, cw_ref, cb_ref)
    else:
        @at_first_chunk
        def _():
            tail_ref[...] = jnp.zeros_like(tail_ref)

        raw = xbc_ref[...]
        nt = tail_ref.shape[0]
        ext = jnp.concatenate([tail_ref[...], raw], axis=0)
        ne = nt + lc
        cw_b = cw_ref[...].astype(BF16)
        scaled = jnp.concatenate([ext * cw_b[j:j + 1, :] for j in range(CONV_W)], axis=0)
        to = lax.broadcasted_iota(I32, (lc, CONV_W * ne), 0)
        frm = lax.broadcasted_iota(I32, (lc, CONV_W * ne), 1)
        picks = jnp.zeros((lc, CONV_W * ne), F32)
        for j in range(CONV_W):
            picks = jnp.where(frm == to + (j * ne + nt - (CONV_W - 1) + j), 1.0, picks)
        conv_acc = cb_ref[...] + jnp.dot(picks.astype(BF16), scaled, preferred_element_type=F32)
        tail_ref[...] = raw[lc - nt:, :]
        act = conv_acc * jax.nn.sigmoid(conv_acc)
    xs = act[:, :D_INNER]
    bm = act[:, D_INNER:D_INNER + SSM_GROUPS * D_STATE].astype(BF16)
    cm = act[:, D_INNER + SSM_GROUPS * D_STATE:].astype(BF16)

    dt = _softplus(dt_ref[...] + dtb_ref[...])
    dtT_raw = dtT_ref[0] if len(dtT_ref.shape) == 3 else dtT_ref[...]
    dtT = _softplus(dtT_raw + dtbT_ref[...])
    if n_valid < lc:
        dt = jnp.where(lax.broadcasted_iota(I32, dt.shape, 0) < n_valid, dt, 0.0)
        dtT = jnp.where(lax.broadcasted_iota(I32, dtT.shape, 1) < n_valid, dtT, 0.0)
    la = dt * a_ref[...]
    laT = dtT * aT_ref[...]
    li = lax.broadcasted_iota(I32, (lc, lc), 0)
    si = lax.broadcasted_iota(I32, (lc, lc), 1)
    causal = li >= si
    tril = jnp.where(causal, 1.0, 0.0).astype(BF16)
    triu = jnp.where(li <= si, 1.0, 0.0).astype(BF16)
    cum = sum(jnp.dot(tril, p, preferred_element_type=F32) for p in _split3(la))
    cumT = sum(jnp.dot(p, triu, preferred_element_type=F32) for p in _split3(laT))
    ec = jnp.exp(cum)
    dte = jnp.exp(cum[lc - 1:lc, :] - cum)
    cd = jnp.exp(cumT[:, lc - 1:lc])

    def expand(v):
        hi, mid, _ = _split3(v)
        return jnp.dot(jnp.concatenate([hi, mid], axis=1), e_ref[...], preferred_element_type=F32)

    dt_x, ec_x, dte_x = expand(dt), expand(ec), expand(dte)
    xdt = xs * dt_x
    xdt_b = xdt.astype(BF16)
    xdte_b = (xdt * dte_x).astype(BF16)
    lane = lax.broadcasted_iota(I32, (lc, LANES), 1)
    low_half = lane < SSM_HEAD_DIM

    for g in range(SSM_GROUPS):
        gs = slice(g * GROUP_W, (g + 1) * GROUP_W)
        bm_g = bm[:, g * D_STATE:(g + 1) * D_STATE]
        cm_g = cm[:, g * D_STATE:(g + 1) * D_STATE]
        cb = lax.dot_general(cm_g, bm_g, NT, preferred_element_type=F32)
        cbm = jnp.where(causal, cb, 0.0)
        h_g = h_ref[0, g * SSM_HPG:(g + 1) * SSM_HPG].reshape(GROUP_W, D_STATE)
        y_off = lax.dot_general(cm_g, h_g.astype(BF16), NT, preferred_element_type=F32) * ec_x[:, gs]
        tiles = []
        for j in range(GROUP_W // LANES):
            col = g * GROUP_W + j * LANES
            x_pair = xdt_b[:, col:col + LANES]
            acc = None
            for half in range(2):
                h = col // SSM_HEAD_DIM + half
                seg = cum[:, h:h + 1] - cumT[h:h + 1, :]
                m = (cbm * jnp.exp(jnp.where(causal, seg, 0.0))).astype(BF16)
                x_h = jnp.where(low_half if half == 0 else jnp.logical_not(low_half), x_pair, jnp.zeros_like(x_pair))
                d = jnp.dot(m, x_h, preferred_element_type=F32)
                acc = d if acc is None else acc + d
            tiles.append(acc)
        y_g = jnp.concatenate(tiles, axis=1) + y_off + xs[:, gs] * dsk_ref[:, gs]
        zg = z_ref[:, gs].astype(F32)
        y_g = y_g * (zg * jax.nn.sigmoid(zg))
        y_g = y_g * lax.rsqrt(jnp.mean(y_g * y_g, axis=-1, keepdims=True) + EPS)
        y_ref[:, gs] = (y_g * nw_ref[:, gs]).astype(y_ref.dtype)
        st = lax.dot_general(xdte_b[:, gs], bm_g, TN, preferred_element_type=F32)
        for hh in range(SSM_HPG):
            h = g * SSM_HPG + hh
            rows = slice(hh * SSM_HEAD_DIM, (hh + 1) * SSM_HEAD_DIM)
            h_ref[0, h] = h_g[rows, :] * cd[h:h + 1, 0:1] + st[rows, :]


def _ssd_block_kernel(xbc_ref, z_ref, dt_ref, dtT_ref, h0_ref, *rest, spb, lc, n_consts, n_valid, conv):
    consts, rest = rest[:n_consts], rest[n_consts:]
    if conv == "history":
        hist_ref, cw_ref, cb_ref, y_ref, h_ref, scratch = rest
    else:
        cw_ref, cb_ref, y_ref, h_ref, scratch = rest
    for s in range(spb):
        rows, one = pl.ds(s * lc, lc), pl.ds(s, 1)
        conv_refs = (cw_ref, cb_ref)
        if conv == "history":
            conv_refs = (hist_ref.at[pl.ds(s * SUBLANES, SUBLANES)],) + conv_refs
        _ssd_kernel(xbc_ref.at[rows], z_ref.at[rows], dt_ref.at[rows], dtT_ref.at[one], h0_ref.at[one], *consts,
                    *conv_refs, y_ref.at[rows], h_ref.at[one], scratch,
                    lc=lc, n_valid=n_valid, conv=conv, single_chunk=True)


def _ssd(xbc, z, dt, dtT, h0, consts, nb, nc, lc, n_valid, conv_w, conv_b, hist=None, spb=1):
    assert spb == 1 or (nc == 1 and dtT.ndim == 3 and nb % spb == 0)
    t = xbc.shape[0]
    row = lambda n: pl.BlockSpec((spb * lc, n), lambda b, c: (b * nc + c, 0))
    if dtT.ndim == 2:
        dtT_spec = pl.BlockSpec((SSM_HEADS, lc), lambda b, c: (0, b * nc + c))
    else:
        dtT_spec = pl.BlockSpec((spb, SSM_HEADS, lc), lambda b, c: (b * nc + c, 0, 0))
    state_spec = pl.BlockSpec((spb, SSM_HEADS, SSM_HEAD_DIM, D_STATE), lambda b, c: (b, 0, 0, 0))
    in_specs = [row(CONV_DIM), row(D_INNER), row(SSM_HEADS), dtT_spec, state_spec] + [_const_spec(a.shape) for a in consts]
    args = [xbc, z, dt, dtT, h0, *consts]
    if hist is not None:
        in_specs.append(pl.BlockSpec((spb * SUBLANES, CONV_DIM), lambda b, c: (b, 0)))
        args.append(hist)
        scratch = [pltpu.VMEM((SUBLANES + lc, CONV_DIM), F32)]
    else:
        scratch = [pltpu.VMEM((PIECE, CONV_DIM), BF16)]
    in_specs += [_const_spec(conv_w.shape), _const_spec(conv_b.shape)]
    args += [conv_w, conv_b]
    conv = "history" if hist is not None else "fresh"
    if spb == 1:
        body = functools.partial(_ssd_kernel, lc=lc, n_valid=n_valid, conv=conv, single_chunk=nc == 1)
    else:
        body = functools.partial(_ssd_block_kernel, spb=spb, lc=lc, n_consts=len(consts), n_valid=n_valid, conv=conv)
    return pl.pallas_call(
        body,
        grid=(nb // spb, nc),
        in_specs=in_specs,
        out_specs=[row(D_INNER), state_spec],
        out_shape=[jax.ShapeDtypeStruct((t, D_INNER), BF16),
                   jax.ShapeDtypeStruct((nb, SSM_HEADS, SSM_HEAD_DIM, D_STATE), F32)],
        scratch_shapes=scratch,
        compiler_params=pltpu.CompilerParams(dimension_semantics=("arbitrary", "arbitrary"),
                                             vmem_limit_bytes=VMEM_LIMIT),
        name="ssd",
    )(*args)


def _swa_kernel(slope_ref, sink_ref, q_ref, kc_ref, vc_ref, kp_ref, vp_ref, y_ref, bias_ref, *, tq, prev_always):
    fold = tq == WINDOW
    nk = WINDOW + tq

    @pl.when((pl.program_id(0) == 0) & (pl.program_id(1) == 0))
    def _():
        rows = WINDOW if fold else nk
        j = lax.broadcasted_iota(I32, (rows, tq), 0)
        r = lax.broadcasted_iota(I32, (rows, tq), 1)
        if fold:
            dist = jnp.where(j > r, r + WINDOW - j, r - j)
            valid, from_previous = dist >= 0, j > r
        else:
            dist = r + WINDOW - j
            valid, from_previous = (dist >= 0) & (dist < WINDOW), j < WINDOW
        distf = dist.astype(F32)
        for h in range(N_Q_HEADS):
            penalty = -slope_ref[h] * distf
            bias_ref[h] = jnp.where(valid, penalty, NEG_BIG)
            if not prev_always:
                bias_ref[N_Q_HEADS + h] = jnp.where(valid & jnp.logical_not(from_previous), penalty, NEG_BIG)

    first = 0 if prev_always else jnp.where(pl.program_id(1) == 0, N_Q_HEADS, 0)
    lane = lax.broadcasted_iota(I32, (nk, LANES), 1)
    zeros_half = jnp.zeros((ATTN_HEAD_DIM, nk), BF16)
    sink_row = lax.broadcasted_iota(I32, (SUBLANES, tq), 0) == 0
    ones_keys = jnp.ones((nk + SUBLANES, LANES), BF16)
    if fold:
        ji = lax.broadcasted_iota(I32, (WINDOW, tq), 0)
        ri = lax.broadcasted_iota(I32, (WINDOW, tq), 1)
        from_prev = ji > ri
        from_prev_b = ji.astype(BF16) > ri.astype(BF16)
    heads = []
    for t in range(D_KV // LANES):
        cols = slice(t * LANES, (t + 1) * LANES)
        kt = jnp.concatenate([kp_ref[:, cols].astype(F32), kc_ref[:, cols].astype(F32)], axis=0)
        vt = jnp.concatenate([vp_ref[:, cols].astype(F32), vc_ref[:, cols].astype(F32)], axis=0)
        if fold:
            vt_t = vt.T.astype(BF16)
        for b in range(2):
            mine = (lane >= ATTN_HEAD_DIM) if b else (lane < ATTN_HEAD_DIM)
            k_same = jnp.where(mine, kt, 0.0)
            k_half = {b: k_same.astype(BF16), 1 - b: pltpu.roll(k_same, ATTN_HEAD_DIM, 1).astype(BF16)}
            if fold:
                v_g = vt_t[b * ATTN_HEAD_DIM:(b + 1) * ATTN_HEAD_DIM, :]
                v_half = {0: jnp.concatenate([v_g, zeros_half], axis=0),
                          1: jnp.concatenate([zeros_half, v_g], axis=0)}
            else:
                v_same = jnp.concatenate([jnp.where(mine, vt, 0.0), jnp.zeros((SUBLANES, LANES), F32)], axis=0)
                v_half = {b: v_same.astype(BF16), 1 - b: pltpu.roll(v_same, ATTN_HEAD_DIM, 1).astype(BF16)}
            for qi in range(Q_PER_KV):
                a = qi % 2
                heads.append((k_half[a], v_half[a]))

    q_tiles = [(q_ref[:, jq * LANES:(jq + 1) * LANES] * (ATTN_HEAD_DIM ** -0.5)).astype(BF16)
               for jq in range(D_ATTN // LANES)]

    def scores(h):
        k, q = heads[h][0], q_tiles[h // 2]
        s = lax.dot_general(k, q, NT, preferred_element_type=F32)
        if fold:
            s = jnp.where(from_prev, s[:WINDOW], s[WINDOW:])
        return s + bias_ref[first + h]

    def attend(h, s):
        sink = sink_ref[h]
        v = heads[h][1]
        m = jnp.maximum(jnp.max(s, axis=0, keepdims=True), sink)
        e = jnp.exp(s - m)
        e_sink = jnp.exp(sink - m)
        if fold:
            rden = 1.0 / (jnp.sum(e, axis=0, keepdims=True) + e_sink)
            e_b = e.astype(BF16)
            zero = jnp.zeros_like(e_b)
            p = jnp.concatenate([jnp.where(from_prev_b, e_b, zero), jnp.where(from_prev_b, zero, e_b)], axis=0)
            return jnp.dot(v, p, preferred_element_type=F32) * rden
        p = jnp.concatenate([e, jnp.where(sink_row, e_sink, 0.0)], axis=0).astype(BF16)
        num = lax.dot_general(p, v, TN, preferred_element_type=F32)
        den = lax.dot_general(p, ones_keys, TN, preferred_element_type=F32)
        return num / den

    out_tiles = [None] * (D_ATTN // LANES)
    s_next = scores(0)
    for h in range(N_Q_HEADS):
        s_cur = s_next
        if h + 1 < N_Q_HEADS:
            s_next = scores(h + 1)
        o = attend(h, s_cur)
        out_tiles[h // 2] = o if out_tiles[h // 2] is None else out_tiles[h // 2] + o
    for jq, o in enumerate(out_tiles):
        y_ref[:, jq * LANES:(jq + 1) * LANES] = (o.T if fold else o).astype(y_ref.dtype)


def _swa(slopes, sinks, q, k, v, k_prev, v_prev, nb, nblk, tq, prev_always):
    t = q.shape[0]
    cur = lambda n: pl.BlockSpec((tq, n), lambda b, i: (b * nblk + i, 0))
    if prev_always:
        prev = pl.BlockSpec((WINDOW, D_KV), lambda b, i: (b, 0))
    else:
        prev = pl.BlockSpec((WINDOW, D_KV), lambda b, i: (b * nblk + jnp.maximum(i - 1, 0), 0))
    smem = pl.BlockSpec(memory_space=pltpu.SMEM)
    return pl.pallas_call(
        functools.partial(_swa_kernel, tq=tq, prev_always=prev_always),
        grid=(nb, nblk),
        in_specs=[smem, smem, cur(D_ATTN), cur(D_KV), cur(D_KV), prev, prev],
        out_specs=cur(D_ATTN),
        out_shape=jax.ShapeDtypeStruct((t, D_ATTN), BF16),
        scratch_shapes=[pltpu.VMEM(((1 if prev_always else 2) * N_Q_HEADS, WINDOW if tq == WINDOW else WINDOW + tq,
                                    tq), F32)],
        compiler_params=pltpu.CompilerParams(dimension_semantics=("arbitrary", "arbitrary"),
                                             vmem_limit_bytes=VMEM_LIMIT),
        name="swa",
    )(slopes, sinks, q, k, v, k_prev, v_prev)


def _post_kernel(x_ref, ys_ref, ya_ref, g_ref, wsp_ref, wap_ref, wo_ref, nw_ref, wr_ref, br_ref, *rest,
                 n_exp, lr, nt, aliased):
    if aliased:
        rest = rest[1:]
    h_ref, xs_ref, lpos_ref, pr_ref, pc_ref = rest
    tp = x_ref.shape[0]

    @pl.when(pl.program_id(0) >= nt)
    def _():
        xs_ref[...] = jnp.zeros_like(xs_ref)

    @pl.when(pl.program_id(0) < nt)
    def _():
        a = jnp.dot(ys_ref[...].astype(BF16), wsp_ref[...], preferred_element_type=F32)
        b = jnp.dot(ya_ref[...].astype(BF16), wap_ref[...], preferred_element_type=F32)
        g = g_ref[...].astype(F32)
        merged = jax.nn.sigmoid(g[:, :D_MODEL]) * a + jax.nn.sigmoid(g[:, D_MODEL:]) * b
        h = x_ref[...] + jnp.dot(merged.astype(BF16), wo_ref[...], preferred_element_type=F32)
        h_ref[...] = h
        hn = h * lax.rsqrt(jnp.mean(h * h, axis=-1, keepdims=True) + EPS) * nw_ref[...]
        w_hi, w_mid, _ = _split3(wr_ref[...])
        x_hi, x_mid, _ = _split3(hn)
        both = lax.dot_general(jnp.concatenate([w_hi, w_mid], axis=0), x_hi, NT, preferred_element_type=F32)
        logits = (both[:n_exp] + lax.dot_general(w_hi, x_mid, NT, preferred_element_type=F32)
                  + both[n_exp:]) + br_ref[...]
        eidx = lax.broadcasted_iota(I32, logits.shape, 0).astype(F32)
        work = logits
        vals, ids = [], []
        for _ in range(TOP_K):
            m = jnp.max(work, axis=0, keepdims=True)
            first = jnp.min(jnp.where(work == m, eidx, float(n_exp)), axis=0, keepdims=True)
            vals.append(m)
            ids.append(first)
            work = jnp.where(eidx == first, -jnp.inf, work)
        es = [jnp.exp(v - vals[0]) for v in vals]
        den = es[0] + es[1] + es[2] + es[3]
        eye = jnp.where(lax.broadcasted_iota(I32, (TOP_K, TOP_K), 0) == lax.broadcasted_iota(I32, (TOP_K, TOP_K), 1),
                        1.0, 0.0).astype(BF16)

        def to_columns(rows):
            return sum(lax.dot_general(p, eye, TN, preferred_element_type=F32) for p in _split3(rows))

        onehot = [jnp.where(eidx == i, 1.0, 0.0) for i in ids]
        counts = [jnp.sum(o, axis=1, keepdims=True) for o in onehot]
        total = counts[0] + counts[1] + counts[2] + counts[3]
        padded = jnp.floor((total + (PIECE - 1)) * (1.0 / PIECE)) * PIECE
        ei = lax.broadcasted_iota(I32, (n_exp, n_exp), 0)
        ej = lax.broadcasted_iota(I32, (n_exp, n_exp), 1)
        below = jnp.where(ej < ei, 1.0, 0.0).astype(BF16)
        padded_b = jnp.broadcast_to(padded, (n_exp, LANES))
        seg_off = sum(jnp.dot(below, p, preferred_element_type=F32) for p in _split3(padded_b))[:, 0:1]
        ti = lax.broadcasted_iota(I32, (tp, tp), 0)
        tj = lax.broadcasted_iota(I32, (tp, tp), 1)
        before = jnp.where(ti < tj, 1.0, 0.0).astype(BF16)
        base = seg_off
        lpos = []
        prefixes = jnp.dot(jnp.concatenate(onehot, axis=0).astype(BF16), before,
                           preferred_element_type=F32)
        for k in range(TOP_K):
            prefix = prefixes[k * n_exp:(k + 1) * n_exp]
            lpos.append(jnp.sum(onehot[k] * (base + prefix), axis=0, keepdims=True))
            base = base + counts[k]
        pc_ref[0] = padded_b.astype(I32)

        hn_b = hn.astype(BF16)
        rc = lr // 4
        for c in range(4):
            ri = (lax.broadcasted_iota(I32, (rc, tp), 0) + c * rc).astype(F32)
            sel = jnp.zeros((rc, tp), F32)
            for k in range(TOP_K):
                sel = jnp.where(ri == lpos[k], 1.0, sel)
            xs_ref[c * rc:(c + 1) * rc, :] = jnp.dot(sel.astype(BF16), hn_b, preferred_element_type=F32).astype(BF16)

        pr_ref[...] = to_columns(jnp.concatenate([e / den for e in es], axis=0))
        lpos_ref[...] = to_columns(jnp.concatenate(lpos, axis=0)).astype(I32)


def _post(x, y_ssm, y_attn, gates, wsp, wap, wo, ffn_nw, w_rT, b_r, tm, lr, xs_rows, xs_block0, pad_steps,
          xs_prev=None):
    t = x.shape[0]
    n_exp = w_rT.shape[0]
    nt = t // tm
    last = nt - 1
    row = lambda n: pl.BlockSpec((tm, n), lambda i: (jnp.minimum(i, last), 0))
    col = row(TOP_K)
    in_specs = [row(D_MODEL), row(D_INNER), row(D_ATTN), row(2 * D_MODEL),
                _resident_spec(wsp.shape), _resident_spec(wap.shape), _resident_spec(wo.shape),
                _const_spec(ffn_nw.shape), _const_spec(w_rT.shape), _const_spec(b_r.shape)]
    args = [x, y_ssm, y_attn, gates, wsp, wap, wo, ffn_nw, w_rT, b_r]
    aliases = {}
    if xs_prev is not None:
        in_specs.append(pl.BlockSpec(memory_space=pl.ANY))
        args.append(xs_prev)
        aliases = {len(args) - 1: 1}
    return pl.pallas_call(
        functools.partial(_post_kernel, n_exp=n_exp, lr=lr, nt=nt, aliased=xs_prev is not None),
        grid=(nt + pad_steps,),
        in_specs=in_specs,
        out_specs=[row(D_MODEL), pl.BlockSpec((lr, D_MODEL), lambda i: (xs_block0 + i, 0)), col, col,
                   pl.BlockSpec((1, n_exp, LANES), lambda i: (jnp.minimum(i, last), 0, 0))],
        out_shape=[jax.ShapeDtypeStruct((t, D_MODEL), F32), jax.ShapeDtypeStruct((xs_rows, D_MODEL), BF16),
                   jax.ShapeDtypeStruct((t, TOP_K), I32), jax.ShapeDtypeStruct((t, TOP_K), F32),
                   jax.ShapeDtypeStruct((nt, n_exp, LANES), I32)],
        input_output_aliases=aliases,
        compiler_params=pltpu.CompilerParams(dimension_semantics=("arbitrary",), vmem_limit_bytes=VMEM_LIMIT),
        name="post",
    )(*args)


def _piece(ref, p):
    return ref.at[pl.ds(pl.multiple_of(p * PIECE, PIECE), PIECE)]


def _moe_kernel(te_ref, nu_ref, src_ref, xs_hbm, wg_ref, wu_ref, wd_ref, b_ref, ys_hbm, xbuf, obuf, wbf, gsem, ssem):
    i = pl.program_id(0)
    n_used = nu_ref[0]
    slot = i % 2
    tme = PIECES_PER_TILE * PIECE

    def gather(tile, s):
        for r in range(PIECES_PER_TILE):
            pltpu.make_async_copy(_piece(xs_hbm, src_ref[tile * PIECES_PER_TILE + r]),
                                  xbuf.at[s, pl.ds(r * PIECE, PIECE)], gsem.at[s]).start()

    def wait_gather(s):
        pltpu.make_async_copy(xs_hbm.at[pl.ds(0, tme)], xbuf.at[s], gsem.at[s]).wait()

    def wait_put(s):
        pltpu.make_async_copy(obuf.at[s], ys_hbm.at[pl.ds(0, tme)], ssem.at[s]).wait()

    @pl.when(i == 0)
    def _():
        gather(0, 0)

    @pl.when(i < n_used)
    def _():
        gather(i + 1, 1 - slot)
        wait_gather(slot)

        @pl.when(i >= 2)
        def _():
            wait_put(slot)

        @pl.when((i == 0) | (te_ref[i] != te_ref[jnp.maximum(i - 1, 0)]))
        def _():
            wbf[0] = wg_ref[0].astype(BF16)
            wbf[1] = wu_ref[0].astype(BF16)
            wbf[2] = wd_ref[0].astype(BF16)

        x = xbuf[slot]
        g = jnp.minimum(jnp.dot(x, wbf[0], preferred_element_type=F32) + b_ref[0, 0:1, :], SWIGLU_LIMIT)
        u = jnp.clip(jnp.dot(x, wbf[1], preferred_element_type=F32) + b_ref[0, 1:2, :], -SWIGLU_LIMIT, SWIGLU_LIMIT)
        act = ((u + 1.0) * g * jax.nn.sigmoid(SWIGLU_ALPHA * g)).astype(BF16)
        obuf[slot] = (jnp.dot(act, wbf[2], preferred_element_type=F32) + b_ref[0, 2:3, :]).astype(BF16)
        for r in range(PIECES_PER_TILE):
            pltpu.make_async_copy(obuf.at[slot, pl.ds(r * PIECE, PIECE)],
                                  _piece(ys_hbm, src_ref[i * PIECES_PER_TILE + r]), ssem.at[slot]).start()

        @pl.when(i == n_used - 1)
        def _():
            wait_put(slot)
            wait_gather(1 - slot)

            @pl.when(i >= 1)
            def _():
                wait_put(1 - slot)


def _moe(tile_expert, n_used, src, xs, wg, wu, wd, biases):
    n_tiles = tile_expert.shape[0]
    tme = PIECES_PER_TILE * PIECE
    wspec = pl.BlockSpec((1, D_MODEL, D_FF), lambda i, te, nu, sr: (te[i], 0, 0))
    grid_spec = pltpu.PrefetchScalarGridSpec(
        num_scalar_prefetch=3,
        grid=(n_tiles,),
        in_specs=[pl.BlockSpec(memory_space=pl.ANY), wspec, wspec, wspec,
                  pl.BlockSpec((1, 3, D_FF), lambda i, te, nu, sr: (te[i], 0, 0))],
        out_specs=pl.BlockSpec(memory_space=pl.ANY),
        scratch_shapes=[pltpu.VMEM((2, tme, D_MODEL), BF16), pltpu.VMEM((2, tme, D_MODEL), BF16),
                        pltpu.VMEM((3, D_MODEL, D_FF), BF16),
                        pltpu.SemaphoreType.DMA((2,)), pltpu.SemaphoreType.DMA((2,))],
    )
    return pl.pallas_call(
        _moe_kernel,
        grid_spec=grid_spec,
        out_shape=jax.ShapeDtypeStruct(xs.shape, xs.dtype),
        input_output_aliases={3: 0},
        compiler_params=pltpu.CompilerParams(dimension_semantics=("arbitrary",), vmem_limit_bytes=VMEM_LIMIT),
        name="moe",
    )(tile_expert, n_used, src, xs, wg, wu, wd, biases)


def _combine_kernel(h_ref, ys_ref, lpos_ref, pr_ref, nw_ref, o_ref):
    tp = h_ref.shape[0]
    lr = ys_ref.shape[0]
    ri = lax.broadcasted_iota(I32, (tp, lr), 1)
    lp = lpos_ref[...]
    pr = pr_ref[...]
    pw = jnp.zeros((tp, lr), F32)
    for k in range(TOP_K):
        pw = jnp.where(ri == lp[:, k:k + 1], pr[:, k:k + 1], pw)
    moe = jnp.dot(pw.astype(BF16), ys_ref[...], preferred_element_type=F32)
    h = h_ref[...] + moe
    o_ref[...] = h * lax.rsqrt(jnp.mean(h * h, axis=-1, keepdims=True) + EPS) * nw_ref[...]


def _combine(h, ys, lpos_t, probs_t, final_nw, tm, lr, ys_block0):
    t = h.shape[0]
    return pl.pallas_call(
        _combine_kernel,
        grid=(t // tm,),
        in_specs=[pl.BlockSpec((tm, D_MODEL), lambda i: (i, 0)),
                  pl.BlockSpec((lr, D_MODEL), lambda i: (ys_block0 + i, 0)),
                  pl.BlockSpec((tm, TOP_K), lambda i: (i, 0)),
                  pl.BlockSpec((tm, TOP_K), lambda i: (i, 0)),
                  _const_spec(final_nw.shape)],
        out_specs=pl.BlockSpec((tm, D_MODEL), lambda i: (i, 0)),
        out_shape=jax.ShapeDtypeStruct((t, D_MODEL), F32),
        compiler_params=pltpu.CompilerParams(dimension_semantics=("arbitrary",), vmem_limit_bytes=VMEM_LIMIT),
        name="combine",
    )(h, ys, lpos_t, probs_t, final_nw)


def _piece_table(padded_counts, tile_row0, n_tiles, spare_piece0):
    n_pieces = (padded_counts // PIECE).T
    seg_row = tile_row0[:, None] + jnp.cumsum(padded_counts, axis=1) - padded_counts
    seg_piece = (seg_row // PIECE).T
    per_expert = n_pieces.sum(axis=1)
    tiles_per = (per_expert + PIECES_PER_TILE - 1) // PIECES_PER_TILE
    tile_end = jnp.cumsum(tiles_per)
    n_used = tile_end[-1]
    slot0 = (tile_end - tiles_per) * PIECES_PER_TILE
    seg_slot = (slot0[:, None] + jnp.cumsum(n_pieces, axis=1) - n_pieces).reshape(-1)
    seg_n = n_pieces.reshape(-1)
    seg_src = seg_piece.reshape(-1)
    slots = jnp.arange(n_tiles * PIECES_PER_TILE, dtype=I32)

    def at_segment_of_slot(f):
        df = f - jnp.concatenate([jnp.zeros((1,), I32), f[:-1]])
        return jnp.sum(jnp.where(seg_slot[None, :] <= slots[:, None], df[None, :], 0), axis=1)

    real = slots < at_segment_of_slot(seg_slot + seg_n)
    padding = jnp.logical_not(real) & (slots < n_used * PIECES_PER_TILE)
    spare = spare_piece0 + jnp.where(padding, jnp.cumsum(padding.astype(I32)), 0)
    src = jnp.where(real, slots + at_segment_of_slot(seg_src - seg_slot), spare).astype(I32)
    tile_ids = jnp.arange(n_tiles, dtype=I32)
    tile_expert = jnp.sum((tile_end[None, :] <= jnp.minimum(tile_ids, n_used - 1)[:, None]).astype(I32), axis=1)
    return tile_expert.astype(I32), n_used.reshape(1).astype(I32), src


def _pick_tile(n, pref):
    while n % pref:
        pref //= 2
    return pref


def _local_rows(tm, n_exp):
    need = TOP_K * tm + n_exp * (PIECE - 1)
    return -(-need // 64) * 64


def kernel(x_prompt, x_sample, state_conv, state_ssm, cache_swa_k, cache_swa_v, attn_norm_w, w_in, conv_w, conv_b, dt_bias, a_log, d_skip, ssm_norm_w, attn_sinks, w_ssm_proj, w_attn_proj, w_o, ffn_norm_w, w_router, b_router, w_gate, b_gate, w_up, b_up, w_down, b_down, final_norm_w):
    assert w_in.shape[0] == 1, "single-layer step"
    nb, seq, _ = x_prompt.shape
    nbs = x_sample.shape[0]
    n_exp = w_router.shape[-1]
    tp, ts = nb * seq, nbs
    pad = SUBLANES

    w_packed, w_blocks = _pack_in_proj_weight(w_in[0])
    attn_nw = attn_norm_w[0].reshape(1, D_MODEL)
    a_neg = -jnp.exp(a_log[0].astype(F32))
    head_of = jnp.arange(D_INNER, dtype=I32) // SSM_HEAD_DIM
    expand = jnp.tile((jnp.arange(SSM_HEADS, dtype=I32)[:, None] == head_of[None, :]).astype(BF16), (2, 1))
    conv_consts = (conv_w[0], conv_b[0].reshape(1, CONV_DIM))
    ssd_consts = (dt_bias[0].reshape(1, SSM_HEADS), dt_bias[0].reshape(SSM_HEADS, 1),
                  a_neg.reshape(1, SSM_HEADS), a_neg.reshape(SSM_HEADS, 1),
                  d_skip[0][head_of].reshape(1, D_INNER), ssm_norm_w[0].reshape(1, D_INNER), expand)
    slopes = jnp.exp2(-8.0 * jnp.arange(1, N_Q_HEADS + 1, dtype=F32) / N_Q_HEADS)
    sinks = attn_sinks[0].astype(F32)
    wsp, wap, wo = w_ssm_proj[0].astype(BF16), w_attn_proj[0].astype(BF16), w_o[0].astype(BF16)
    ffn_nw = ffn_norm_w[0].reshape(1, D_MODEL)
    w_rT = w_router[0].T
    b_r = b_router[0].reshape(n_exp, 1)
    expert_biases = jnp.stack([b_gate[0], b_up[0], b_down[0]], axis=1)
    final_nw = final_norm_w.reshape(1, D_MODEL)

    xp = x_prompt.reshape(tp, D_MODEL)
    tm_in = _pick_tile(seq, 512)
    z, xbc, dt, q, k, v, gates, conv_tail, dtT, k_tail, v_tail = _in_proj(
        xp, attn_nw, w_packed, w_blocks, tm_in, BF16, tiles_per_seq=seq // tm_in)
    nc = seq // CHUNK
    y_ssm, ssm_p = _ssd(xbc, z, dt, dtT, jnp.zeros((nb, SSM_HEADS, SSM_HEAD_DIM, D_STATE), F32), ssd_consts,
                        nb, nc, CHUNK, CHUNK, *conv_consts)
    nblk = seq // WINDOW
    y_attn = _swa(slopes, sinks, q, k, v, k, v, nb, nblk, WINDOW, False)

    xs_pad = jnp.pad(x_sample.reshape(ts, 1, D_MODEL), ((0, 0), (0, pad - 1), (0, 0))).reshape(ts * pad, D_MODEL)
    z_s, xbc_s, dt_s, q_s, k_s, v_s, gates_s = _in_proj(xs_pad, attn_nw, w_packed, w_blocks,
                                                        _pick_tile(ts * pad, 256), F32)
    dtT_s = dt_s.reshape(ts, pad, SSM_HEADS).transpose(0, 2, 1)
    spb = _pick_tile(ts, SAMPLE_SEQS_PER_STEP)
    hist_s = jnp.pad(state_conv[0], ((0, 0), (pad - (CONV_W - 1), 0), (0, 0))).reshape(ts * pad, CONV_DIM)
    y_ssm_s, ssm_s = _ssd(xbc_s, z_s, dt_s, dtT_s, state_ssm[0], ssd_consts, ts, 1, pad, 1, *conv_consts,
                          hist=hist_s, spb=spb)
    kc = cache_swa_k[0].reshape(ts * WINDOW, D_KV)
    vc = cache_swa_v[0].reshape(ts * WINDOW, D_KV)
    y_attn_s = _swa(slopes, sinks, q_s, k_s, v_s, kc, vc, ts, 1, pad, True)
    real = lambda a: a.reshape(ts, pad, -1)[:, 0]

    tm_p, tm_s = _pick_tile(tp, 512), ts
    nt_p = tp // tm_p
    lr_p, lr_s = _local_rows(tm_p, n_exp), _local_rows(tm_s, n_exp)
    if (nt_p * lr_p) % lr_s or lr_s > lr_p:
        lr_s = lr_p
    spare_rows = lr_s + (1 + n_exp * (PIECES_PER_TILE - 1)) * PIECE
    pad_steps = -(-spare_rows // lr_p)
    xs_rows = (nt_p + pad_steps) * lr_p
    block_s = nt_p * lr_p // lr_s
    post_w = (wsp, wap, wo, ffn_nw, w_rT, b_r)
    h_p, xs, lpos_p, pr_p, pc_p = _post(xp, y_ssm, y_attn, gates, *post_w, tm_p, lr_p, xs_rows, 0, pad_steps)
    h_s, xs, lpos_s, pr_s, pc_s = _post(x_sample.reshape(ts, D_MODEL), real(y_ssm_s), real(y_attn_s), real(gates_s),
                                        *post_w, tm_s, lr_s, xs_rows, block_s, 0, xs_prev=xs)

    padded_counts = jnp.concatenate([pc_p[:, :, 0], pc_s[:, :, 0]], axis=0)
    tile_row0 = jnp.concatenate([jnp.arange(nt_p, dtype=I32) * lr_p, jnp.full((1,), nt_p * lr_p, I32)])
    max_pieces = (TOP_K * (tp + ts) + (PIECE - 1) * n_exp * (nt_p + 1)) // PIECE + n_exp * (PIECES_PER_TILE - 1)
    n_tiles = -(-max_pieces // PIECES_PER_TILE) + 1
    tile_expert, n_used, src = _piece_table(padded_counts, tile_row0, n_tiles, (nt_p * lr_p + lr_s) // PIECE)
    ys = _moe(tile_expert, n_used, src, xs, w_gate[0], w_up[0], w_down[0], expert_biases)
    out_p = _combine(h_p, ys, lpos_p, pr_p, final_nw, tm_p, lr_p, 0)
    out_s = _combine(h_s, ys, lpos_s, pr_s, final_nw, tm_s, lr_s, block_s)

    y_prompt = out_p.reshape(nb, seq, D_MODEL)
    y_sample = out_s.reshape(nbs, 1, D_MODEL)
    conv_p = conv_tail[:, SUBLANES - (CONV_W - 1):][None]
    k_p = k_tail.reshape(1, nb, WINDOW, N_KV_HEADS, ATTN_HEAD_DIM)
    v_p = v_tail.reshape(1, nb, WINDOW, N_KV_HEADS, ATTN_HEAD_DIM)
    conv_s = jnp.concatenate([state_conv[0][:, 1:], real(xbc_s)[:, None]], axis=1)[None]
    k_new = real(k_s).reshape(ts, 1, N_KV_HEADS, ATTN_HEAD_DIM)
    v_new = real(v_s).reshape(ts, 1, N_KV_HEADS, ATTN_HEAD_DIM)
    ks_out = jnp.concatenate([cache_swa_k[0][:, 1:], k_new], axis=1)[None]
    vs_out = jnp.concatenate([cache_swa_v[0][:, 1:], v_new], axis=1)[None]
    return (y_prompt, y_sample, conv_p, ssm_p[None], k_p, v_p, conv_s, ssm_s[None], ks_out, vs_out)
```

```python
import functools

import jax
import jax.numpy as jnp
from jax import lax
from jax.experimental import pallas as pl
from jax.experimental.pallas import tpu as pltpu

F32, BF16, I32 = jnp.float32, jnp.bfloat16, jnp.int32

D_MODEL = 1024
D_INNER = 2 * D_MODEL
SSM_HEAD_DIM = 64
SSM_HEADS = D_INNER // SSM_HEAD_DIM
SSM_GROUPS = 4
SSM_HPG = SSM_HEADS // SSM_GROUPS
D_STATE = 128
CONV_W = 4
CONV_DIM = D_INNER + 2 * SSM_GROUPS * D_STATE
CHUNK = 128
ATTN_HEAD_DIM = 64
N_Q_HEADS = D_MODEL // ATTN_HEAD_DIM
N_KV_HEADS = 4
Q_PER_KV = N_Q_HEADS // N_KV_HEADS
D_ATTN = N_Q_HEADS * ATTN_HEAD_DIM
D_KV = N_KV_HEADS * ATTN_HEAD_DIM
WINDOW = 128
TOP_K = 4
D_FF = D_MODEL
SWIGLU_LIMIT = 7.0
SWIGLU_ALPHA = 1.702
EPS = 1e-5
NEG_BIG = -1e30

LANES = 128
SUBLANES = 8
GROUP_W = D_INNER // SSM_GROUPS
PIECE = 2 * SUBLANES
PIECES_PER_TILE = 32
SAMPLE_SEQS_PER_STEP = 4
VMEM_LIMIT = 56 * 1024 * 1024

NT = (((1,), (1,)), ((), ()))
TN = (((0,), (0,)), ((), ()))


def _const_spec(shape):
    return pl.BlockSpec(shape, lambda *_: (0,) * len(shape))


def _resident_spec(shape):
    return pl.BlockSpec(shape, lambda *_: (0,) * len(shape), pipeline_mode=pl.Buffered(1))


def _split3(x):
    hi = x.astype(BF16)
    r1 = x - hi.astype(F32)
    mid = r1.astype(BF16)
    lo = (r1 - mid.astype(F32)).astype(BF16)
    return hi, mid, lo


def _softplus(x):
    return jnp.maximum(x, 0.0) + jnp.log(1.0 + jnp.exp(-jnp.abs(x)))


def _causal_conv_silu(buf_ref, n, cw_ref, cb_ref):
    full = buf_ref[...]
    conv = cb_ref[...]
    for j in range(CONV_W):
        shifted = full if j == CONV_W - 1 else pltpu.roll(full, CONV_W - 1 - j, 0)
        conv = conv + shifted[SUBLANES:SUBLANES + n, :] * cw_ref[j:j + 1, :]
    buf_ref[0:SUBLANES, :] = full[n:n + SUBLANES, :]
    return conv * jax.nn.sigmoid(conv)


def _inproj_kernel(x_ref, nw_ref, wz_ref, wxbc_ref, wdt_ref, wq_ref, wk_ref, wv_ref, wg_ref, *rest, tiles_per_seq):
    if tiles_per_seq:
        wdtT_ref, z_ref, xbc_ref, dt_ref, q_ref, k_ref, v_ref, g_ref, tail_ref, dtT_ref, ktail_ref, vtail_ref = rest
    else:
        z_ref, xbc_ref, dt_ref, q_ref, k_ref, v_ref, g_ref = rest
    tm = x_ref.shape[0]
    x = x_ref[...]
    xn = x * lax.rsqrt(jnp.mean(x * x, axis=-1, keepdims=True) + EPS)
    xn = (xn * nw_ref[...]).astype(BF16)
    for w_ref, o_ref in ((wz_ref, z_ref), (wdt_ref, dt_ref), (wq_ref, q_ref), (wg_ref, g_ref)):
        o_ref[...] = jnp.dot(xn, w_ref[...], preferred_element_type=F32).astype(o_ref.dtype)
    xbc = jnp.dot(xn, wxbc_ref[...], preferred_element_type=F32)
    k = jnp.dot(xn, wk_ref[...], preferred_element_type=F32)
    v = jnp.dot(xn, wv_ref[...], preferred_element_type=F32)
    xbc_ref[...] = xbc.astype(xbc_ref.dtype)
    k_ref[...] = k.astype(k_ref.dtype)
    v_ref[...] = v.astype(v_ref.dtype)
    if tiles_per_seq:
        dtT_ref[...] = lax.dot_general(wdtT_ref[...], xn, NT, preferred_element_type=F32)

        @pl.when(pl.program_id(0) % tiles_per_seq == tiles_per_seq - 1)
        def _():
            tail_ref[0] = xbc[tm - SUBLANES:, :]
            ktail_ref[0] = k[tm - WINDOW:, :]
            vtail_ref[0] = v[tm - WINDOW:, :]


IN_PROJ_WIDTHS = (D_INNER, CONV_DIM, SSM_HEADS, D_ATTN, D_KV, D_KV, 2 * D_MODEL)
IN_PROJ_PACK_ORDER = (1, 3, 0, 6, 4, 5, 2)


def _pack_in_proj_weight(w):
    cuts = [0]
    for n in IN_PROJ_WIDTHS:
        cuts.append(cuts[-1] + n)
    packed = jnp.concatenate([w[:, cuts[i]:cuts[i + 1]] for i in IN_PROJ_PACK_ORDER], axis=1).astype(BF16)
    block, off = [0] * len(IN_PROJ_WIDTHS), 0
    for i in IN_PROJ_PACK_ORDER:
        block[i], rem = divmod(off, IN_PROJ_WIDTHS[i])
        assert rem == 0
        off += IN_PROJ_WIDTHS[i]
    return packed, tuple(block)


def _in_proj(x, norm_w, w_packed, w_blocks, tm, act_dtype, tiles_per_seq=None):
    t = x.shape[0]
    widths = IN_PROJ_WIDTHS
    dtypes = (act_dtype, act_dtype, F32, act_dtype, act_dtype, act_dtype, act_dtype)
    row = lambda n: pl.BlockSpec((tm, n), lambda i: (i, 0))
    w_spec = lambda n, blk: pl.BlockSpec((D_MODEL, n), lambda i: (0, blk), pipeline_mode=pl.Buffered(1))
    dt_col = w_blocks[2] * SSM_HEADS
    w_dt = w_packed[:, dt_col:dt_col + SSM_HEADS]
    in_specs = [row(D_MODEL), _const_spec((1, D_MODEL))]
    args = [x, norm_w]
    for i, (n, blk) in enumerate(zip(widths, w_blocks)):
        in_specs.append(_resident_spec((D_MODEL, n)) if i == 2 else w_spec(n, blk))
        args.append(w_dt if i == 2 else w_packed)
    out_specs = [row(n) for n in widths]
    out_shape = [jax.ShapeDtypeStruct((t, n), d) for n, d in zip(widths, dtypes)]
    if tiles_per_seq:
        n_seq = t // (tm * tiles_per_seq)
        in_specs.append(_const_spec((SSM_HEADS, D_MODEL)))
        args.append(w_dt.T)
        per_seq = lambda r, c: pl.BlockSpec((1, r, c), lambda i: (i // tiles_per_seq, 0, 0))
        out_specs += [per_seq(SUBLANES, CONV_DIM), pl.BlockSpec((SSM_HEADS, tm), lambda i: (0, i)),
                      per_seq(WINDOW, D_KV), per_seq(WINDOW, D_KV)]
        out_shape += [jax.ShapeDtypeStruct((n_seq, SUBLANES, CONV_DIM), F32),
                      jax.ShapeDtypeStruct((SSM_HEADS, t), F32),
                      jax.ShapeDtypeStruct((n_seq, WINDOW, D_KV), F32),
                      jax.ShapeDtypeStruct((n_seq, WINDOW, D_KV), F32)]
    return pl.pallas_call(
        functools.partial(_inproj_kernel, tiles_per_seq=tiles_per_seq),
        grid=(t // tm,),
        in_specs=in_specs,
        out_specs=out_specs,
        out_shape=out_shape,
        compiler_params=pltpu.CompilerParams(dimension_semantics=("arbitrary",), vmem_limit_bytes=VMEM_LIMIT),
        name="in_proj",
    )(*args)


def _ssd_kernel(xbc_ref, z_ref, dt_ref, dtT_ref, h0_ref, dtb_ref, dtbT_ref, a_ref, aT_ref, dsk_ref, nw_ref, e_ref,
                *rest, lc, n_valid, conv, single_chunk):
    if conv == "history":
        hist_ref, cw_ref, cb_ref, y_ref, h_ref, buf_ref = rest
    else:
        cw_ref, cb_ref, y_ref, h_ref, tail_ref = rest

    def at_first_chunk(fn):
        if single_chunk:
            fn()
        else:
            pl.when(pl.program_id(1) == 0)(fn)

    @at_first_chunk
    def _():
        h_ref[...] = h0_ref[...]

    if conv == "history":
        @at_first_chunk
        def _():
            buf_ref[0:SUBLANES, :] = hist_ref[...]

        buf_ref[SUBLANES:SUBLANES + lc, :] = xbc_ref[...].astype(F32)
        act = _causal_conv_silu(buf_ref, lc, cw_ref, cb_ref)
    else:
        @at_first_chunk
        def _():
            tail_ref[...] = jnp.zeros_like(tail_ref)

        raw = xbc_ref[...]
        nt = tail_ref.shape[0]
        ext = jnp.concatenate([tail_ref[...], raw], axis=0)
        ne = nt + lc
        cw_b = cw_ref[...].astype(BF16)
        scaled = jnp.concatenate([ext * cw_b[j:j + 1, :] for j in range(CONV_W)], axis=0)
        to = lax.broadcasted_iota(I32, (lc, CONV_W * ne), 0)
        frm = lax.broadcasted_iota(I32, (lc, CONV_W * ne), 1)
        picks = jnp.zeros((lc, CONV_W * ne), F32)
        for j in range(CONV_W):
            picks = jnp.where(frm == to + (j * ne + nt - (CONV_W - 1) + j), 1.0, picks)
        conv_acc = cb_ref[...] + jnp.dot(picks.astype(BF16), scaled, preferred_element_type=F32)
        tail_ref[...] = raw[lc - nt:, :]
        act = conv_acc * jax.nn.sigmoid(conv_acc)
    xs = act[:, :D_INNER]
    bm = act[:, D_INNER:D_INNER + SSM_GROUPS * D_STATE].astype(BF16)
    cm = act[:, D_INNER + SSM_GROUPS * D_STATE:].astype(BF16)

    dt = _softplus(dt_ref[...] + dtb_ref[...])
    dtT_raw = dtT_ref[0] if len(dtT_ref.shape) == 3 else dtT_ref[...]
    dtT = _softplus(dtT_raw + dtbT_ref[...])
    if n_valid < lc:
        dt = jnp.where(lax.broadcasted_iota(I32, dt.shape, 0) < n_valid, dt, 0.0)
        dtT = jnp.where(lax.broadcasted_iota(I32, dtT.shape, 1) < n_valid, dtT, 0.0)
    la = dt * a_ref[...]
    laT = dtT * aT_ref[...]
    li = lax.broadcasted_iota(I32, (lc, lc), 0)
    si = lax.broadcasted_iota(I32, (lc, lc), 1)
    causal = li >= si
    tril = jnp.where(causal, 1.0, 0.0).astype(BF16)
    triu = jnp.where(li <= si, 1.0, 0.0).astype(BF16)
    cum = sum(jnp.dot(tril, p, preferred_element_type=F32) for p in _split3(la))
    cumT = sum(jnp.dot(p, triu, preferred_element_type=F32) for p in _split3(laT))
    ec = jnp.exp(cum)
    dte = jnp.exp(cum[lc - 1:lc, :] - cum)
    cd = jnp.exp(cumT[:, lc - 1:lc])

    def expand(v):
        hi, mid, _ = _split3(v)
        return jnp.dot(jnp.concatenate([hi, mid], axis=1), e_ref[...], preferred_element_type=F32)

    dt_x, ec_x, dte_x = expand(dt), expand(ec), expand(dte)
    xdt = xs * dt_x
    xdt_b = xdt.astype(BF16)
    xdte_b = (xdt * dte_x).astype(BF16)
    lane = lax.broadcasted_iota(I32, (lc, LANES), 1)
    low_half = lane < SSM_HEAD_DIM

    for g in range(SSM_GROUPS):
        gs = slice(g * GROUP_W, (g + 1) * GROUP_W)
        bm_g = bm[:, g * D_STATE:(g + 1) * D_STATE]
        cm_g = cm[:, g * D_STATE:(g + 1) * D_STATE]
        cb = lax.dot_general(cm_g, bm_g, NT, preferred_element_type=F32)
        cbm = jnp.where(causal, cb, 0.0)
        h_g = h_ref[0, g * SSM_HPG:(g + 1) * SSM_HPG].reshape(GROUP_W, D_STATE)
        y_off = lax.dot_general(cm_g, h_g.astype(BF16), NT, preferred_element_type=F32) * ec_x[:, gs]
        tiles = []
        for j in range(GROUP_W // LANES):
            col = g * GROUP_W + j * LANES
            x_pair = xdt_b[:, col:col + LANES]
            acc = None
            for half in range(2):
                h = col // SSM_HEAD_DIM + half
                seg = cum[:, h:h + 1] - cumT[h:h + 1, :]
                m = (cbm * jnp.exp(jnp.where(causal, seg, 0.0))).astype(BF16)
                x_h = jnp.where(low_half if half == 0 else jnp.logical_not(low_half), x_pair, jnp.zeros_like(x_pair))
                d = jnp.dot(m, x_h, preferred_element_type=F32)
                acc = d if acc is None else acc + d
            tiles.append(acc)
        y_g = jnp.concatenate(tiles, axis=1) + y_off + xs[:, gs] * dsk_ref[:, gs]
        zg = z_ref[:, gs].astype(F32)
        y_g = y_g * (zg * jax.nn.sigmoid(zg))
        y_g = y_g * lax.rsqrt(jnp.mean(y_g * y_g, axis=-1, keepdims=True) + EPS)
        y_ref[:, gs] = (y_g * nw_ref[:, gs]).astype(y_ref.dtype)
        st = lax.dot_general(xdte_b[:, gs], bm_g, TN, preferred_element_type=F32)
        for hh in range(SSM_HPG):
            h = g * SSM_HPG + hh
            rows = slice(hh * SSM_HEAD_DIM, (hh + 1) * SSM_HEAD_DIM)
            h_ref[0, h] = h_g[rows, :] * cd[h:h + 1, 0:1] + st[rows, :]


def _ssd_block_kernel(xbc_ref, z_ref, dt_ref, dtT_ref, h0_ref, *rest, spb, lc, n_consts, n_valid, conv):
    consts, rest = rest[:n_consts], rest[n_consts:]
    if conv == "history":
        hist_ref, cw_ref, cb_ref, y_ref, h_ref, scratch = rest
    else:
        cw_ref, cb_ref, y_ref, h_ref, scratch = rest
    for s in range(spb):
        rows, one = pl.ds(s * lc, lc), pl.ds(s, 1)
        conv_refs = (cw_ref, cb_ref)
        if conv == "history":
            conv_refs = (hist_ref.at[pl.ds(s * SUBLANES, SUBLANES)],) + conv_refs
        _ssd_kernel(xbc_ref.at[rows], z_ref.at[rows], dt_ref.at[rows], dtT_ref.at[one], h0_ref.at[one], *consts,
                    *conv_refs, y_ref.at[rows], h_ref.at[one], scratch,
                    lc=lc, n_valid=n_valid, conv=conv, single_chunk=True)


def _ssd(xbc, z, dt, dtT, h0, consts, nb, nc, lc, n_valid, conv_w, conv_b, hist=None, spb=1):
    assert spb == 1 or (nc == 1 and dtT.ndim == 3 and nb % spb == 0)
    t = xbc.shape[0]
    row = lambda n: pl.BlockSpec((spb * lc, n), lambda b, c: (b * nc + c, 0))
    if dtT.ndim == 2:
        dtT_spec = pl.BlockSpec((SSM_HEADS, lc), lambda b, c: (0, b * nc + c))
    else:
        dtT_spec = pl.BlockSpec((spb, SSM_HEADS, lc), lambda b, c: (b * nc + c, 0, 0))
    state_spec = pl.BlockSpec((spb, SSM_HEADS, SSM_HEAD_DIM, D_STATE), lambda b, c: (b, 0, 0, 0))
    in_specs = [row(CONV_DIM), row(D_INNER), row(SSM_HEADS), dtT_spec, state_spec] + [_const_spec(a.shape) for a in consts]
    args = [xbc, z, dt, dtT, h0, *consts]
    if hist is not None:
        in_specs.append(pl.BlockSpec((spb * SUBLANES, CONV_DIM), lambda b, c: (b, 0)))
        args.append(hist)
        scratch = [pltpu.VMEM((SUBLANES + lc, CONV_DIM), F32)]
    else:
        scratch = [pltpu.VMEM((PIECE, CONV_DIM), BF16)]
    in_specs += [_const_spec(conv_w.shape), _const_spec(conv_b.shape)]
    args += [conv_w, conv_b]
    conv = "history" if hist is not None else "fresh"
    if spb == 1:
        body = functools.partial(_ssd_kernel, lc=lc, n_valid=n_valid, conv=conv, single_chunk=nc == 1)
    else:
        body = functools.partial(_ssd_block_kernel, spb=spb, lc=lc, n_consts=len(consts), n_valid=n_valid, conv=conv)
    return pl.pallas_call(
        body,
        grid=(nb // spb, nc),
        in_specs=in_specs,
        out_specs=[row(D_INNER), state_spec],
        out_shape=[jax.ShapeDtypeStruct((t, D_INNER), BF16),
                   jax.ShapeDtypeStruct((nb, SSM_HEADS, SSM_HEAD_DIM, D_STATE), F32)],
        scratch_shapes=scratch,
        compiler_params=pltpu.CompilerParams(dimension_semantics=("arbitrary", "arbitrary"),
                                             vmem_limit_bytes=VMEM_LIMIT),
        name="ssd",
    )(*args)


def _swa_kernel(slope_ref, sink_ref, q_ref, kc_ref, vc_ref, kp_ref, vp_ref, y_ref, bias_ref, *, tq, prev_always):
    fold = tq == WINDOW
    nk = WINDOW + tq

    @pl.when((pl.program_id(0) == 0) & (pl.program_id(1) == 0))
    def _():
        rows = WINDOW if fold else nk
        j = lax.broadcasted_iota(I32, (rows, tq), 0)
        r = lax.broadcasted_iota(I32, (rows, tq), 1)
        if fold:
            dist = jnp.where(j > r, r + WINDOW - j, r - j)
            valid, from_previous = dist >= 0, j > r
        else:
            dist = r + WINDOW - j
            valid, from_previous = (dist >= 0) & (dist < WINDOW), j < WINDOW
        distf = dist.astype(F32)
        for h in range(N_Q_HEADS):
            penalty = -slope_ref[h] * distf
            bias_ref[h] = jnp.where(valid, penalty, NEG_BIG)
            if not prev_always:
                bias_ref[N_Q_HEADS + h] = jnp.where(valid & jnp.logical_not(from_previous), penalty, NEG_BIG)

    first = 0 if prev_always else jnp.where(pl.program_id(1) == 0, N_Q_HEADS, 0)
    lane = lax.broadcasted_iota(I32, (nk, LANES), 1)
    zeros_half = jnp.zeros((ATTN_HEAD_DIM, nk), BF16)
    sink_row = lax.broadcasted_iota(I32, (SUBLANES, tq), 0) == 0
    ones_keys = jnp.ones((nk + SUBLANES, LANES), BF16)
    if fold:
        ji = lax.broadcasted_iota(I32, (WINDOW, tq), 0)
        ri = lax.broadcasted_iota(I32, (WINDOW, tq), 1)
        from_prev = ji > ri
        from_prev_b = ji.astype(BF16) > ri.astype(BF16)
    heads = []
    for t in range(D_KV // LANES):
        cols = slice(t * LANES, (t + 1) * LANES)
        kt = jnp.concatenate([kp_ref[:, cols].astype(F32), kc_ref[:, cols].astype(F32)], axis=0)
        vt = jnp.concatenate([vp_ref[:, cols].astype(F32), vc_ref[:, cols].astype(F32)], axis=0)
        if fold:
            vt_t = vt.T.astype(BF16)
        for b in range(2):
            mine = (lane >= ATTN_HEAD_DIM) if b else (lane < ATTN_HEAD_DIM)
            k_same = jnp.where(mine, kt, 0.0)
            k_half = {b: k_same.astype(BF16), 1 - b: pltpu.roll(k_same, ATTN_HEAD_DIM, 1).astype(BF16)}
            if fold:
                v_g = vt_t[b * ATTN_HEAD_DIM:(b + 1) * ATTN_HEAD_DIM, :]
                v_half = {0: jnp.concatenate([v_g, zeros_half], axis=0),
                          1: jnp.concatenate([zeros_half, v_g], axis=0)}
            else:
                v_same = jnp.concatenate([jnp.where(mine, vt, 0.0), jnp.zeros((SUBLANES, LANES), F32)], axis=0)
                v_half = {b: v_same.astype(BF16), 1 - b: pltpu.roll(v_same, ATTN_HEAD_DIM, 1).astype(BF16)}
            for qi in range(Q_PER_KV):
                a = qi % 2
                heads.append((k_half[a], v_half[a]))

    q_tiles = [(q_ref[:, jq * LANES:(jq + 1) * LANES] * (ATTN_HEAD_DIM ** -0.5)).astype(BF16)
               for jq in range(D_ATTN // LANES)]

    def scores(h):
        k, q = heads[h][0], q_tiles[h // 2]
        s = lax.dot_general(k, q, NT, preferred_element_type=F32)
        if fold:
            s = jnp.where(from_prev, s[:WINDOW], s[WINDOW:])
        return s + bias_ref[first + h]

    def attend(h, s):
        sink = sink_ref[h]
        v = heads[h][1]
        m = jnp.maximum(jnp.max(s, axis=0, keepdims=True), sink)
        e = jnp.exp(s - m)
        e_sink = jnp.exp(sink - m)
        if fold:
            rden = 1.0 / (jnp.sum(e, axis=0, keepdims=True) + e_sink)
            e_b = e.astype(BF16)
            zero = jnp.zeros_like(e_b)
            p = jnp.concatenate([jnp.where(from_prev_b, e_b, zero), jnp.where(from_prev_b, zero, e_b)], axis=0)
            return jnp.dot(v, p, preferred_element_type=F32) * rden
        p = jnp.concatenate([e, jnp.where(sink_row, e_sink, 0.0)], axis=0).astype(BF16)
        num = lax.dot_general(p, v, TN, preferred_element_type=F32)
        den = lax.dot_general(p, ones_keys, TN, preferred_element_type=F32)
        return num / den

    out_tiles = [None] * (D_ATTN // LANES)
    s_next = scores(0)
    for h in range(N_Q_HEADS):
        s_cur = s_next
        if h + 1 < N_Q_HEADS:
            s_next = scores(h + 1)
        o = attend(h, s_cur)
        out_tiles[h // 2] = o if out_tiles[h // 2] is None else out_tiles[h // 2] + o
    for jq, o in enumerate(out_tiles):
        y_ref[:, jq * LANES:(jq + 1) * LANES] = (o.T if fold else o).astype(y_ref.dtype)


def _swa(slopes, sinks, q, k, v, k_prev, v_prev, nb, nblk, tq, prev_always):
    t = q.shape[0]
    cur = lambda n: pl.BlockSpec((tq, n), lambda b, i: (b * nblk + i, 0))
    if prev_always:
        prev = pl.BlockSpec((WINDOW, D_KV), lambda b, i: (b, 0))
    else:
        prev = pl.BlockSpec((WINDOW, D_KV), lambda b, i: (b * nblk + jnp.maximum(i - 1, 0), 0))
    smem = pl.BlockSpec(memory_space=pltpu.SMEM)
    return pl.pallas_call(
        functools.partial(_swa_kernel, tq=tq, prev_always=prev_always),
        grid=(nb, nblk),
        in_specs=[smem, smem, cur(D_ATTN), cur(D_KV), cur(D_KV), prev, prev],
        out_specs=cur(D_ATTN),
        out_shape=jax.ShapeDtypeStruct((t, D_ATTN), BF16),
        scratch_shapes=[pltpu.VMEM(((1 if prev_always else 2) * N_Q_HEADS, WINDOW if tq == WINDOW else WINDOW + tq,
                                    tq), F32)],
        compiler_params=pltpu.CompilerParams(dimension_semantics=("arbitrary", "arbitrary"),
                                             vmem_limit_bytes=VMEM_LIMIT),
        name="swa",
    )(slopes, sinks, q, k, v, k_prev, v_prev)


def _post_kernel(x_ref, ys_ref, ya_ref, g_ref, wsp_ref, wap_ref, wo_ref, nw_ref, wr_ref, br_ref, *rest,
                 n_exp, lr, nt, aliased):
    if aliased:
        rest = rest[1:]
    h_ref, xs_ref, lpos_ref, pr_ref, pc_ref = rest
    tp = x_ref.shape[0]

    @pl.when(pl.program_id(0) >= nt)
    def _():
        xs_ref[...] = jnp.zeros_like(xs_ref)

    @pl.when(pl.program_id(0) < nt)
    def _():
        a = jnp.dot(ys_ref[...].astype(BF16), wsp_ref[...], preferred_element_type=F32)
        b = jnp.dot(ya_ref[...].astype(BF16), wap_ref[...], preferred_element_type=F32)
        g = g_ref[...].astype(F32)
        merged = jax.nn.sigmoid(g[:, :D_MODEL]) * a + jax.nn.sigmoid(g[:, D_MODEL:]) * b
        h = x_ref[...] + jnp.dot(merged.astype(BF16), wo_ref[...], preferred_element_type=F32)
        h_ref[...] = h
        hn = h * lax.rsqrt(jnp.mean(h * h, axis=-1, keepdims=True) + EPS) * nw_ref[...]
        w_hi, w_mid, _ = _split3(wr_ref[...])
        x_hi, x_mid, _ = _split3(hn)
        both = lax.dot_general(jnp.concatenate([w_hi, w_mid], axis=0), x_hi, NT, preferred_element_type=F32)
        logits = (both[:n_exp] + lax.dot_general(w_hi, x_mid, NT, preferred_element_type=F32)
                  + both[n_exp:]) + br_ref[...]
        eidx = lax.broadcasted_iota(I32, logits.shape, 0).astype(F32)
        work = logits
        vals, ids = [], []
        for _ in range(TOP_K):
            m = jnp.max(work, axis=0, keepdims=True)
            first = jnp.min(jnp.where(work == m, eidx, float(n_exp)), axis=0, keepdims=True)
            vals.append(m)
            ids.append(first)
            work = jnp.where(eidx == first, -jnp.inf, work)
        es = [jnp.exp(v - vals[0]) for v in vals]
        den = es[0] + es[1] + es[2] + es[3]
        eye = jnp.where(lax.broadcasted_iota(I32, (TOP_K, TOP_K), 0) == lax.broadcasted_iota(I32, (TOP_K, TOP_K), 1),
                        1.0, 0.0).astype(BF16)

        def to_columns(rows):
            return sum(lax.dot_general(p, eye, TN, preferred_element_type=F32) for p in _split3(rows))

        onehot = [jnp.where(eidx == i, 1.0, 0.0) for i in ids]
        counts = [jnp.sum(o, axis=1, keepdims=True) for o in onehot]
        total = counts[0] + counts[1] + counts[2] + counts[3]
        padded = jnp.floor((total + (PIECE - 1)) * (1.0 / PIECE)) * PIECE
        ei = lax.broadcasted_iota(I32, (n_exp, n_exp), 0)
        ej = lax.broadcasted_iota(I32, (n_exp, n_exp), 1)
        below = jnp.where(ej < ei, 1.0, 0.0).astype(BF16)
        padded_b = jnp.broadcast_to(padded, (n_exp, LANES))
        seg_off = sum(jnp.dot(below, p, preferred_element_type=F32) for p in _split3(padded_b))[:, 0:1]
        ti = lax.broadcasted_iota(I32, (tp, tp), 0)
        tj = lax.broadcasted_iota(I32, (tp, tp), 1)
        before = jnp.where(ti < tj, 1.0, 0.0).astype(BF16)
        base = seg_off
        lpos = []
        prefixes = jnp.dot(jnp.concatenate(onehot, axis=0).astype(BF16), before,
                           preferred_element_type=F32)
        for k in range(TOP_K):
            prefix = prefixes[k * n_exp:(k + 1) * n_exp]
            lpos.append(jnp.sum(onehot[k] * (base + prefix), axis=0, keepdims=True))
            base = base + counts[k]
        pc_ref[0] = padded_b.astype(I32)

        hn_b = hn.astype(BF16)
        rc = lr // 4
        for c in range(4):
            ri = (lax.broadcasted_iota(I32, (rc, tp), 0) + c * rc).astype(F32)
            sel = jnp.zeros((rc, tp), F32)
            for k in range(TOP_K):
                sel = jnp.where(ri == lpos[k], 1.0, sel)
            xs_ref[c * rc:(c + 1) * rc, :] = jnp.dot(sel.astype(BF16), hn_b, preferred_element_type=F32).astype(BF16)

        pr_ref[...] = to_columns(jnp.concatenate([e / den for e in es], axis=0))
        lpos_ref[...] = to_columns(jnp.concatenate(lpos, axis=0)).astype(I32)


def _post(x, y_ssm, y_attn, gates, wsp, wap, wo, ffn_nw, w_rT, b_r, tm, lr, xs_rows, xs_block0, pad_steps,
          xs_prev=None):
    t = x.shape[0]
    n_exp = w_rT.shape[0]
    nt = t // tm
    last = nt - 1
    row = lambda n: pl.BlockSpec((tm, n), lambda i: (jnp.minimum(i, last), 0))
    col = row(TOP_K)
    in_specs = [row(D_MODEL), row(D_INNER), row(D_ATTN), row(2 * D_MODEL),
                _resident_spec(wsp.shape), _resident_spec(wap.shape), _resident_spec(wo.shape),
                _const_spec(ffn_nw.shape), _const_spec(w_rT.shape), _const_spec(b_r.shape)]
    args = [x, y_ssm, y_attn, gates, wsp, wap, wo, ffn_nw, w_rT, b_r]
    aliases = {}
    if xs_prev is not None:
        in_specs.append(pl.BlockSpec(memory_space=pl.ANY))
        args.append(xs_prev)
        aliases = {len(args) - 1: 1}
    return pl.pallas_call(
        functools.partial(_post_kernel, n_exp=n_exp, lr=lr, nt=nt, aliased=xs_prev is not None),
        grid=(nt + pad_steps,),
        in_specs=in_specs,
        out_specs=[row(D_MODEL), pl.BlockSpec((lr, D_MODEL), lambda i: (xs_block0 + i, 0)), col, col,
                   pl.BlockSpec((1, n_exp, LANES), lambda i: (jnp.minimum(i, last), 0, 0))],
        out_shape=[jax.ShapeDtypeStruct((t, D_MODEL), F32), jax.ShapeDtypeStruct((xs_rows, D_MODEL), BF16),
                   jax.ShapeDtypeStruct((t, TOP_K), I32), jax.ShapeDtypeStruct((t, TOP_K), F32),
                   jax.ShapeDtypeStruct((nt, n_exp, LANES), I32)],
        input_output_aliases=aliases,
        compiler_params=pltpu.CompilerParams(dimension_semantics=("arbitrary",), vmem_limit_bytes=VMEM_LIMIT),
        name="post",
    )(*args)


def _piece(ref, p):
    return ref.at[pl.ds(pl.multiple_of(p * PIECE, PIECE), PIECE)]


def _moe_kernel(te_ref, nu_ref, src_ref, xs_hbm, wg_ref, wu_ref, wd_ref, b_ref, ys_hbm, xbuf, obuf, wbf, gsem, ssem):
    i = pl.program_id(0)
    n_used = nu_ref[0]
    slot = i % 2
    tme = PIECES_PER_TILE * PIECE

    def gather(tile, s):
        for r in range(PIECES_PER_TILE):
            pltpu.make_async_copy(_piece(xs_hbm, src_ref[tile * PIECES_PER_TILE + r]),
                                  xbuf.at[s, pl.ds(r * PIECE, PIECE)], gsem.at[s]).start()

    def wait_gather(s):
        pltpu.make_async_copy(xs_hbm.at[pl.ds(0, tme)], xbuf.at[s], gsem.at[s]).wait()

    def wait_put(s):
        pltpu.make_async_copy(obuf.at[s], ys_hbm.at[pl.ds(0, tme)], ssem.at[s]).wait()

    @pl.when(i == 0)
    def _():
        gather(0, 0)

    @pl.when(i < n_used)
    def _():
        gather(i + 1, 1 - slot)
        wait_gather(slot)

        @pl.when(i >= 2)
        def _():
            wait_put(slot)

        @pl.when((i == 0) | (te_ref[i] != te_ref[jnp.maximum(i - 1, 0)]))
        def _():
            wbf[0] = wg_ref[0].astype(BF16)
            wbf[1] = wu_ref[0].astype(BF16)
            wbf[2] = wd_ref[0].astype(BF16)

        x = xbuf[slot]
        g = jnp.minimum(jnp.dot(x, wbf[0], preferred_element_type=F32) + b_ref[0, 0:1, :], SWIGLU_LIMIT)
        u = jnp.clip(jnp.dot(x, wbf[1], preferred_element_type=F32) + b_ref[0, 1:2, :], -SWIGLU_LIMIT, SWIGLU_LIMIT)
        act = ((u + 1.0) * g * jax.nn.sigmoid(SWIGLU_ALPHA * g)).astype(BF16)
        obuf[slot] = (jnp.dot(act, wbf[2], preferred_element_type=F32) + b_ref[0, 2:3, :]).astype(BF16)
        for r in range(PIECES_PER_TILE):
            pltpu.make_async_copy(obuf.at[slot, pl.ds(r * PIECE, PIECE)],
                                  _piece(ys_hbm, src_ref[i * PIECES_PER_TILE + r]), ssem.at[slot]).start()

        @pl.when(i == n_used - 1)
        def _():
            wait_put(slot)
            wait_gather(1 - slot)

            @pl.when(i >= 1)
            def _():
                wait_put(1 - slot)


def _moe(tile_expert, n_used, src, xs, wg, wu, wd, biases):
    n_tiles = tile_expert.shape[0]
    tme = PIECES_PER_TILE * PIECE
    wspec = pl.BlockSpec((1, D_MODEL, D_FF), lambda i, te, nu, sr: (te[i], 0, 0))
    grid_spec = pltpu.PrefetchScalarGridSpec(
        num_scalar_prefetch=3,
        grid=(n_tiles,),
        in_specs=[pl.BlockSpec(memory_space=pl.ANY), wspec, wspec, wspec,
                  pl.BlockSpec((1, 3, D_FF), lambda i, te, nu, sr: (te[i], 0, 0))],
        out_specs=pl.BlockSpec(memory_space=pl.ANY),
        scratch_shapes=[pltpu.VMEM((2, tme, D_MODEL), BF16), pltpu.VMEM((2, tme, D_MODEL), BF16),
                        pltpu.VMEM((3, D_MODEL, D_FF), BF16),
                        pltpu.SemaphoreType.DMA((2,)), pltpu.SemaphoreType.DMA((2,))],
    )
    return pl.pallas_call(
        _moe_kernel,
        grid_spec=grid_spec,
        out_shape=jax.ShapeDtypeStruct(xs.shape, xs.dtype),
        input_output_aliases={3: 0},
        compiler_params=pltpu.CompilerParams(dimension_semantics=("arbitrary",), vmem_limit_bytes=VMEM_LIMIT),
        name="moe",
    )(tile_expert, n_used, src, xs, wg, wu, wd, biases)


def _combine_kernel(h_ref, ys_ref, lpos_ref, pr_ref, nw_ref, o_ref):
    tp = h_ref.shape[0]
    lr = ys_ref.shape[0]
    ri = lax.broadcasted_iota(I32, (tp, lr), 1)
    lp = lpos_ref[...]
    pr = pr_ref[...]
    pw = jnp.zeros((tp, lr), F32)
    for k in range(TOP_K):
        pw = jnp.where(ri == lp[:, k:k + 1], pr[:, k:k + 1], pw)
    moe = jnp.dot(pw.astype(BF16), ys_ref[...], preferred_element_type=F32)
    h = h_ref[...] + moe
    o_ref[...] = h * lax.rsqrt(jnp.mean(h * h, axis=-1, keepdims=True) + EPS) * nw_ref[...]


def _combine(h, ys, lpos_t, probs_t, final_nw, tm, lr, ys_block0):
    t = h.shape[0]
    return pl.pallas_call(
        _combine_kernel,
        grid=(t // tm,),
        in_specs=[pl.BlockSpec((tm, D_MODEL), lambda i: (i, 0)),
                  pl.BlockSpec((lr, D_MODEL), lambda i: (ys_block0 + i, 0)),
                  pl.BlockSpec((tm, TOP_K), lambda i: (i, 0)),
                  pl.BlockSpec((tm, TOP_K), lambda i: (i, 0)),
                  _const_spec(final_nw.shape)],
        out_specs=pl.BlockSpec((tm, D_MODEL), lambda i: (i, 0)),
        out_shape=jax.ShapeDtypeStruct((t, D_MODEL), F32),
        compiler_params=pltpu.CompilerParams(dimension_semantics=("arbitrary",), vmem_limit_bytes=VMEM_LIMIT),
        name="combine",
    )(h, ys, lpos_t, probs_t, final_nw)


def _roll_window_kernel(ck_hbm, cv_hbm, kn_hbm, vn_hbm, ok_hbm, ov_hbm, sem):
    keep = WINDOW - 1
    copies = []
    for i, (old, new, out) in enumerate(((ck_hbm, kn_hbm, ok_hbm), (cv_hbm, vn_hbm, ov_hbm))):
        copies.append(pltpu.make_async_copy(old.at[:, pl.ds(1, keep)], out.at[:, pl.ds(0, keep)], sem.at[2 * i]))
        copies.append(pltpu.make_async_copy(new, out.at[:, pl.ds(keep, 1)], sem.at[2 * i + 1]))
    for c in copies:
        c.start()
    for c in copies:
        c.wait()


def _roll_window(cache_k, cache_v, k_new, v_new):
    any_spec = pl.BlockSpec(memory_space=pl.ANY)
    return pl.pallas_call(
        _roll_window_kernel,
        in_specs=[any_spec] * 4,
        out_specs=[any_spec] * 2,
        out_shape=[jax.ShapeDtypeStruct(cache_k.shape, cache_k.dtype), jax.ShapeDtypeStruct(cache_v.shape, cache_v.dtype)],
        scratch_shapes=[pltpu.SemaphoreType.DMA((4,))],
        name="roll_window",
    )(cache_k, cache_v, k_new.astype(cache_k.dtype), v_new.astype(cache_v.dtype))


def _piece_table(padded_counts, tile_row0, n_tiles, spare_piece0):
    n_pieces = (padded_counts // PIECE).T
    seg_row = tile_row0[:, None] + jnp.cumsum(padded_counts, axis=1) - padded_counts
    seg_piece = (seg_row // PIECE).T
    per_expert = n_pieces.sum(axis=1)
    tiles_per = (per_expert + PIECES_PER_TILE - 1) // PIECES_PER_TILE
    tile_end = jnp.cumsum(tiles_per)
    n_used = tile_end[-1]
    slot0 = (tile_end - tiles_per) * PIECES_PER_TILE
    seg_slot = (slot0[:, None] + jnp.cumsum(n_pieces, axis=1) - n_pieces).reshape(-1)
    seg_n = n_pieces.reshape(-1)
    seg_src = seg_piece.reshape(-1)
    slots = jnp.arange(n_tiles * PIECES_PER_TILE, dtype=I32)

    def at_segment_of_slot(f):
        df = f - jnp.concatenate([jnp.zeros((1,), I32), f[:-1]])
        return jnp.sum(jnp.where(seg_slot[None, :] <= slots[:, None], df[None, :], 0), axis=1)

    real = slots < at_segment_of_slot(seg_slot + seg_n)
    padding = jnp.logical_not(real) & (slots < n_used * PIECES_PER_TILE)
    spare = spare_piece0 + jnp.where(padding, jnp.cumsum(padding.astype(I32)), 0)
    src = jnp.where(real, slots + at_segment_of_slot(seg_src - seg_slot), spare).astype(I32)
    tile_ids = jnp.arange(n_tiles, dtype=I32)
    tile_expert = jnp.sum((tile_end[None, :] <= jnp.minimum(tile_ids, n_used - 1)[:, None]).astype(I32), axis=1)
    return tile_expert.astype(I32), n_used.reshape(1).astype(I32), src


def _pick_tile(n, pref):
    while n % pref:
        pref //= 2
    return pref


def _local_rows(tm, n_exp):
    need = TOP_K * tm + n_exp * (PIECE - 1)
    return -(-need // 64) * 64


def kernel(x_prompt, x_sample, state_conv, state_ssm, cache_swa_k, cache_swa_v, attn_norm_w, w_in, conv_w, conv_b, dt_bias, a_log, d_skip, ssm_norm_w, attn_sinks, w_ssm_proj, w_attn_proj, w_o, ffn_norm_w, w_router, b_router, w_gate, b_gate, w_up, b_up, w_down, b_down, final_norm_w):
    assert w_in.shape[0] == 1, "single-layer step"
    nb, seq, _ = x_prompt.shape
    nbs = x_sample.shape[0]
    n_exp = w_router.shape[-1]
    tp, ts = nb * seq, nbs
    pad = SUBLANES

    w_packed, w_blocks = _pack_in_proj_weight(w_in[0])
    attn_nw = attn_norm_w[0].reshape(1, D_MODEL)
    a_neg = -jnp.exp(a_log[0].astype(F32))
    head_of = jnp.arange(D_INNER, dtype=I32) // SSM_HEAD_DIM
    expand = jnp.tile((jnp.arange(SSM_HEADS, dtype=I32)[:, None] == head_of[None, :]).astype(BF16), (2, 1))
    conv_consts = (conv_w[0], conv_b[0].reshape(1, CONV_DIM))
    ssd_consts = (dt_bias[0].reshape(1, SSM_HEADS), dt_bias[0].reshape(SSM_HEADS, 1),
                  a_neg.reshape(1, SSM_HEADS), a_neg.reshape(SSM_HEADS, 1),
                  d_skip[0][head_of].reshape(1, D_INNER), ssm_norm_w[0].reshape(1, D_INNER), expand)
    slopes = jnp.exp2(-8.0 * jnp.arange(1, N_Q_HEADS + 1, dtype=F32) / N_Q_HEADS)
    sinks = attn_sinks[0].astype(F32)
    wsp, wap, wo = w_ssm_proj[0].astype(BF16), w_attn_proj[0].astype(BF16), w_o[0].astype(BF16)
    ffn_nw = ffn_norm_w[0].reshape(1, D_MODEL)
    w_rT = w_router[0].T
    b_r = b_router[0].reshape(n_exp, 1)
    expert_biases = jnp.stack([b_gate[0], b_up[0], b_down[0]], axis=1)
    final_nw = final_norm_w.reshape(1, D_MODEL)

    xp = x_prompt.reshape(tp, D_MODEL)
    tm_in = _pick_tile(seq, 512)
    z, xbc, dt, q, k, v, gates, conv_tail, dtT, k_tail, v_tail = _in_proj(
        xp, attn_nw, w_packed, w_blocks, tm_in, BF16, tiles_per_seq=seq // tm_in)
    nc = seq // CHUNK
    y_ssm, ssm_p = _ssd(xbc, z, dt, dtT, jnp.zeros((nb, SSM_HEADS, SSM_HEAD_DIM, D_STATE), F32), ssd_consts,
                        nb, nc, CHUNK, CHUNK, *conv_consts)
    nblk = seq // WINDOW
    y_attn = _swa(slopes, sinks, q, k, v, k, v, nb, nblk, WINDOW, False)

    xs_pad = jnp.pad(x_sample.reshape(ts, 1, D_MODEL), ((0, 0), (0, pad - 1), (0, 0))).reshape(ts * pad, D_MODEL)
    z_s, xbc_s, dt_s, q_s, k_s, v_s, gates_s = _in_proj(xs_pad, attn_nw, w_packed, w_blocks,
                                                        _pick_tile(ts * pad, 256), F32)
    dtT_s = dt_s.reshape(ts, pad, SSM_HEADS).transpose(0, 2, 1)
    spb = _pick_tile(ts, SAMPLE_SEQS_PER_STEP)
    hist_s = jnp.pad(state_conv[0], ((0, 0), (pad - (CONV_W - 1), 0), (0, 0))).reshape(ts * pad, CONV_DIM)
    y_ssm_s, ssm_s = _ssd(xbc_s, z_s, dt_s, dtT_s, state_ssm[0], ssd_consts, ts, 1, pad, 1, *conv_consts,
                          hist=hist_s, spb=spb)
    kc = cache_swa_k[0].reshape(ts * WINDOW, D_KV)
    vc = cache_swa_v[0].reshape(ts * WINDOW, D_KV)
    y_attn_s = _swa(slopes, sinks, q_s, k_s, v_s, kc, vc, ts, 1, pad, True)
    real = lambda a: a.reshape(ts, pad, -1)[:, 0]

    tm_p, tm_s = _pick_tile(tp, 512), ts
    nt_p = tp // tm_p
    lr_p, lr_s = _local_rows(tm_p, n_exp), _local_rows(tm_s, n_exp)
    if (nt_p * lr_p) % lr_s or lr_s > lr_p:
        lr_s = lr_p
    spare_rows = lr_s + (1 + n_exp * (PIECES_PER_TILE - 1)) * PIECE
    pad_steps = -(-spare_rows // lr_p)
    xs_rows = (nt_p + pad_steps) * lr_p
    block_s = nt_p * lr_p // lr_s
    post_w = (wsp, wap, wo, ffn_nw, w_rT, b_r)
    h_p, xs, lpos_p, pr_p, pc_p = _post(xp, y_ssm, y_attn, gates, *post_w, tm_p, lr_p, xs_rows, 0, pad_steps)
    h_s, xs, lpos_s, pr_s, pc_s = _post(x_sample.reshape(ts, D_MODEL), real(y_ssm_s), real(y_attn_s), real(gates_s),
                                        *post_w, tm_s, lr_s, xs_rows, block_s, 0, xs_prev=xs)

    padded_counts = jnp.concatenate([pc_p[:, :, 0], pc_s[:, :, 0]], axis=0)
    tile_row0 = jnp.concatenate([jnp.arange(nt_p, dtype=I32) * lr_p, jnp.full((1,), nt_p * lr_p, I32)])
    max_pieces = (TOP_K * (tp + ts) + (PIECE - 1) * n_exp * (nt_p + 1)) // PIECE + n_exp * (PIECES_PER_TILE - 1)
    n_tiles = -(-max_pieces // PIECES_PER_TILE) + 1
    tile_expert, n_used, src = _piece_table(padded_counts, tile_row0, n_tiles, (nt_p * lr_p + lr_s) // PIECE)
    ys = _moe(tile_expert, n_used, src, xs, w_gate[0], w_up[0], w_down[0], expert_biases)
    out_p = _combine(h_p, ys, lpos_p, pr_p, final_nw, tm_p, lr_p, 0)
    out_s = _combine(h_s, ys, lpos_s, pr_s, final_nw, tm_s, lr_s, block_s)

    y_prompt = out_p.reshape(nb, seq, D_MODEL)
    y_sample = out_s.reshape(nbs, 1, D_MODEL)
    conv_p = conv_tail[:, SUBLANES - (CONV_W - 1):][None]
    k_p = k_tail.reshape(1, nb, WINDOW, N_KV_HEADS, ATTN_HEAD_DIM)
    v_p = v_tail.reshape(1, nb, WINDOW, N_KV_HEADS, ATTN_HEAD_DIM)
    conv_s = jnp.concatenate([state_conv[0][:, 1:], real(xbc_s)[:, None]], axis=1)[None]
    k_new = real(k_s).reshape(ts, 1, N_KV_HEADS, ATTN_HEAD_DIM)
    v_new = real(v_s).reshape(ts, 1, N_KV_HEADS, ATTN_HEAD_DIM)
    ks_out, vs_out = _roll_window(cache_swa_k[0], cache_swa_v[0], k_new, v_new)
    ks_out, vs_out = ks_out[None], vs_out[None]
    return (y_prompt, y_sample, conv_p, ssm_p[None], k_p, v_p, conv_s, ssm_s[None], ks_out, vs_out)
```

```python
import functools

import jax
import jax.numpy as jnp
from jax import lax
from jax.experimental import pallas as pl
from jax.experimental.pallas import tpu as pltpu

F32, BF16, I32 = jnp.float32, jnp.bfloat16, jnp.int32

D_MODEL = 1024
D_INNER = 2 * D_MODEL
SSM_HEAD_DIM = 64
SSM_HEADS = D_INNER // SSM_HEAD_DIM
SSM_GROUPS = 4
SSM_HPG = SSM_HEADS // SSM_GROUPS
D_STATE = 128
CONV_W = 4
CONV_DIM = D_INNER + 2 * SSM_GROUPS * D_STATE
CHUNK = 128
ATTN_HEAD_DIM = 64
N_Q_HEADS = D_MODEL // ATTN_HEAD_DIM
N_KV_HEADS = 4
Q_PER_KV = N_Q_HEADS // N_KV_HEADS
D_ATTN = N_Q_HEADS * ATTN_HEAD_DIM
D_KV = N_KV_HEADS * ATTN_HEAD_DIM
WINDOW = 128
TOP_K = 4
D_FF = D_MODEL
SWIGLU_LIMIT = 7.0
SWIGLU_ALPHA = 1.702
EPS = 1e-5
NEG_BIG = -1e30

LANES = 128
SUBLANES = 8
GROUP_W = D_INNER // SSM_GROUPS
PIECE = 2 * SUBLANES
PIECES_PER_TILE = 32
SAMPLE_SEQS_PER_STEP = 8
VMEM_LIMIT = 56 * 1024 * 1024

NT = (((1,), (1,)), ((), ()))
TN = (((0,), (0,)), ((), ()))


def _const_spec(shape):
    return pl.BlockSpec(shape, lambda *_: (0,) * len(shape))


def _resident_spec(shape):
    return pl.BlockSpec(shape, lambda *_: (0,) * len(shape), pipeline_mode=pl.Buffered(1))


def _split3(x):
    hi = x.astype(BF16)
    r1 = x - hi.astype(F32)
    mid = r1.astype(BF16)
    lo = (r1 - mid.astype(F32)).astype(BF16)
    return hi, mid, lo


def _softplus(x):
    return jnp.maximum(x, 0.0) + jnp.log(1.0 + jnp.exp(-jnp.abs(x)))


def _causal_conv_silu(buf_ref, n, cw_ref, cb_ref):
    full = buf_ref[...]
    conv = cb_ref[...]
    for j in range(CONV_W):
        shifted = full if j == CONV_W - 1 else pltpu.roll(full, CONV_W - 1 - j, 0)
        conv = conv + shifted[SUBLANES:SUBLANES + n, :] * cw_ref[j:j + 1, :]
    buf_ref[0:SUBLANES, :] = full[n:n + SUBLANES, :]
    return conv * jax.nn.sigmoid(conv)


def _inproj_kernel(x_ref, nw_ref, wz_ref, wxbc_ref, wdt_ref, wq_ref, wk_ref, wv_ref, wg_ref, *rest, tiles_per_seq):
    if tiles_per_seq:
        wdtT_ref, z_ref, xbc_ref, dt_ref, q_ref, k_ref, v_ref, g_ref, tail_ref, dtT_ref, ktail_ref, vtail_ref = rest
    else:
        z_ref, xbc_ref, dt_ref, q_ref, k_ref, v_ref, g_ref = rest
    tm = x_ref.shape[0]
    x = x_ref[...]
    xn = x * lax.rsqrt(jnp.mean(x * x, axis=-1, keepdims=True) + EPS)
    xn = (xn * nw_ref[...]).astype(BF16)
    for w_ref, o_ref in ((wz_ref, z_ref), (wdt_ref, dt_ref), (wq_ref, q_ref), (wg_ref, g_ref)):
        o_ref[...] = jnp.dot(xn, w_ref[...], preferred_element_type=F32).astype(o_ref.dtype)
    xbc = jnp.dot(xn, wxbc_ref[...], preferred_element_type=F32)
    k = jnp.dot(xn, wk_ref[...], preferred_element_type=F32)
    v = jnp.dot(xn, wv_ref[...], preferred_element_type=F32)
    xbc_ref[...] = xbc.astype(xbc_ref.dtype)
    k_ref[...] = k.astype(k_ref.dtype)
    v_ref[...] = v.astype(v_ref.dtype)
    if tiles_per_seq:
        dtT_ref[...] = lax.dot_general(wdtT_ref[...], xn, NT, preferred_element_type=F32)

        @pl.when(pl.program_id(0) % tiles_per_seq == tiles_per_seq - 1)
        def _():
            tail_ref[0] = xbc[tm - SUBLANES:, :]
            ktail_ref[0] = k[tm - WINDOW:, :]
            vtail_ref[0] = v[tm - WINDOW:, :]


IN_PROJ_WIDTHS = (D_INNER, CONV_DIM, SSM_HEADS, D_ATTN, D_KV, D_KV, 2 * D_MODEL)
IN_PROJ_PACK_ORDER = (1, 3, 0, 6, 4, 5, 2)


def _pack_in_proj_weight(w):
    cuts = [0]
    for n in IN_PROJ_WIDTHS:
        cuts.append(cuts[-1] + n)
    packed = jnp.concatenate([w[:, cuts[i]:cuts[i + 1]] for i in IN_PROJ_PACK_ORDER], axis=1).astype(BF16)
    block, off = [0] * len(IN_PROJ_WIDTHS), 0
    for i in IN_PROJ_PACK_ORDER:
        block[i], rem = divmod(off, IN_PROJ_WIDTHS[i])
        assert rem == 0
        off += IN_PROJ_WIDTHS[i]
    return packed, tuple(block)


def _in_proj(x, norm_w, w_packed, w_blocks, tm, act_dtype, tiles_per_seq=None):
    t = x.shape[0]
    widths = IN_PROJ_WIDTHS
    dtypes = (act_dtype, act_dtype, F32, act_dtype, act_dtype, act_dtype, act_dtype)
    row = lambda n: pl.BlockSpec((tm, n), lambda i: (i, 0))
    w_spec = lambda n, blk: pl.BlockSpec((D_MODEL, n), lambda i: (0, blk), pipeline_mode=pl.Buffered(1))
    dt_col = w_blocks[2] * SSM_HEADS
    w_dt = w_packed[:, dt_col:dt_col + SSM_HEADS]
    in_specs = [row(D_MODEL), _const_spec((1, D_MODEL))]
    args = [x, norm_w]
    for i, (n, blk) in enumerate(zip(widths, w_blocks)):
        in_specs.append(_resident_spec((D_MODEL, n)) if i == 2 else w_spec(n, blk))
        args.append(w_dt if i == 2 else w_packed)
    out_specs = [row(n) for n in widths]
    out_shape = [jax.ShapeDtypeStruct((t, n), d) for n, d in zip(widths, dtypes)]
    if tiles_per_seq:
        n_seq = t // (tm * tiles_per_seq)
        in_specs.append(_const_spec((SSM_HEADS, D_MODEL)))
        args.append(w_dt.T)
        per_seq = lambda r, c: pl.BlockSpec((1, r, c), lambda i: (i // tiles_per_seq, 0, 0))
        out_specs += [per_seq(SUBLANES, CONV_DIM), pl.BlockSpec((SSM_HEADS, tm), lambda i: (0, i)),
                      per_seq(WINDOW, D_KV), per_seq(WINDOW, D_KV)]
        out_shape += [jax.ShapeDtypeStruct((n_seq, SUBLANES, CONV_DIM), F32),
                      jax.ShapeDtypeStruct((SSM_HEADS, t), F32),
                      jax.ShapeDtypeStruct((n_seq, WINDOW, D_KV), F32),
                      jax.ShapeDtypeStruct((n_seq, WINDOW, D_KV), F32)]
    return pl.pallas_call(
        functools.partial(_inproj_kernel, tiles_per_seq=tiles_per_seq),
        grid=(t // tm,),
        in_specs=in_specs,
        out_specs=out_specs,
        out_shape=out_shape,
        compiler_params=pltpu.CompilerParams(dimension_semantics=("arbitrary",), vmem_limit_bytes=VMEM_LIMIT),
        name="in_proj",
    )(*args)


def _ssd_kernel(xbc_ref, z_ref, dt_ref, dtT_ref, h0_ref, dtb_ref, dtbT_ref, a_ref, aT_ref, dsk_ref, nw_ref, e_ref,
                *rest, lc, n_valid, conv, single_chunk):
    if conv == "history":
        hist_ref, cw_ref, cb_ref, y_ref, h_ref, buf_ref = rest
    else:
        cw_ref, cb_ref, y_ref, h_ref, tail_ref = rest

    def at_first_chunk(fn):
        if single_chunk:
            fn()
        else:
            pl.when(pl.program_id(1) == 0)(fn)

    @at_first_chunk
    def _():
        h_ref[...] = h0_ref[...]

    if conv == "history":
        @at_first_chunk
        def _():
            buf_ref[0:SUBLANES, :] = hist_ref[...]

        buf_ref[SUBLANES:SUBLANES + lc, :] = xbc_ref[...].astype(F32)
        act = _causal_conv_silu(buf_ref, lc, cw_ref, cb_ref)
    else:
        @at_first_chunk
        def _():
            tail_ref[...] = jnp.zeros_like(tail_ref)

        raw = xbc_ref[...]
        nt = tail_ref.shape[0]
        ext = jnp.concatenate([tail_ref[...], raw], axis=0)
        ne = nt + lc
        cw_b = cw_ref[...].astype(BF16)
        scaled = jnp.concatenate([ext * cw_b[j:j + 1, :] for j in range(CONV_W)], axis=0)
        to = lax.broadcasted_iota(I32, (lc, CONV_W * ne), 0)
        frm = lax.broadcasted_iota(I32, (lc, CONV_W * ne), 1)
        picks = jnp.zeros((lc, CONV_W * ne), F32)
        for j in range(CONV_W):
            picks = jnp.where(frm == to + (j * ne + nt - (CONV_W - 1) + j), 1.0, picks)
        conv_acc = cb_ref[...] + jnp.dot(picks.astype(BF16), scaled, preferred_element_type=F32)
        tail_ref[...] = raw[lc - nt:, :]
        act = conv_acc * jax.nn.sigmoid(conv_acc)
    xs = act[:, :D_INNER]
    bm = act[:, D_INNER:D_INNER + SSM_GROUPS * D_STATE].astype(BF16)
    cm = act[:, D_INNER + SSM_GROUPS * D_STATE:].astype(BF16)

    dt = _softplus(dt_ref[...] + dtb_ref[...])
    dtT_raw = dtT_ref[0] if len(dtT_ref.shape) == 3 else dtT_ref[...]
    dtT = _softplus(dtT_raw + dtbT_ref[...])
    if n_valid < lc:
        dt = jnp.where(lax.broadcasted_iota(I32, dt.shape, 0) < n_valid, dt, 0.0)
        dtT = jnp.where(lax.broadcasted_iota(I32, dtT.shape, 1) < n_valid, dtT, 0.0)
    la = dt * a_ref[...]
    laT = dtT * aT_ref[...]
    li = lax.broadcasted_iota(I32, (lc, lc), 0)
    si = lax.broadcasted_iota(I32, (lc, lc), 1)
    causal = li >= si
    tril = jnp.where(causal, 1.0, 0.0).astype(BF16)
    triu = jnp.where(li <= si, 1.0, 0.0).astype(BF16)
    cum = sum(jnp.dot(tril, p, preferred_element_type=F32) for p in _split3(la))
    cumT = sum(jnp.dot(p, triu, preferred_element_type=F32) for p in _split3(laT))
    ec = jnp.exp(cum)
    dte = jnp.exp(cum[lc - 1:lc, :] - cum)
    cd = jnp.exp(cumT[:, lc - 1:lc])

    def expand(v):
        hi, mid, _ = _split3(v)
        return jnp.dot(jnp.concatenate([hi, mid], axis=1), e_ref[...], preferred_element_type=F32)

    dt_x, ec_x, dte_x = expand(dt), expand(ec), expand(dte)
    xdt = xs * dt_x
    xdt_b = xdt.astype(BF16)
    xdte_b = (xdt * dte_x).astype(BF16)
    lane = lax.broadcasted_iota(I32, (lc, LANES), 1)
    low_half = lane < SSM_HEAD_DIM

    for g in range(SSM_GROUPS):
        gs = slice(g * GROUP_W, (g + 1) * GROUP_W)
        bm_g = bm[:, g * D_STATE:(g + 1) * D_STATE]
        cm_g = cm[:, g * D_STATE:(g + 1) * D_STATE]
        cb = lax.dot_general(cm_g, bm_g, NT, preferred_element_type=F32)
        cbm = jnp.where(causal, cb, 0.0)
        h_g = h_ref[0, g * SSM_HPG:(g + 1) * SSM_HPG].reshape(GROUP_W, D_STATE)
        y_off = lax.dot_general(cm_g, h_g.astype(BF16), NT, preferred_element_type=F32) * ec_x[:, gs]
        tiles = []
        for j in range(GROUP_W // LANES):
            col = g * GROUP_W + j * LANES
            x_pair = xdt_b[:, col:col + LANES]
            acc = None
            for half in range(2):
                h = col // SSM_HEAD_DIM + half
                seg = cum[:, h:h + 1] - cumT[h:h + 1, :]
                m = (cbm * jnp.exp(jnp.where(causal, seg, 0.0))).astype(BF16)
                x_h = jnp.where(low_half if half == 0 else jnp.logical_not(low_half), x_pair, jnp.zeros_like(x_pair))
                d = jnp.dot(m, x_h, preferred_element_type=F32)
                acc = d if acc is None else acc + d
            tiles.append(acc)
        y_g = jnp.concatenate(tiles, axis=1) + y_off + xs[:, gs] * dsk_ref[:, gs]
        zg = z_ref[:, gs].astype(F32)
        y_g = y_g * (zg * jax.nn.sigmoid(zg))
        y_g = y_g * lax.rsqrt(jnp.mean(y_g * y_g, axis=-1, keepdims=True) + EPS)
        y_ref[:, gs] = (y_g * nw_ref[:, gs]).astype(y_ref.dtype)
        st = lax.dot_general(xdte_b[:, gs], bm_g, TN, preferred_element_type=F32)
        for hh in range(SSM_HPG):
            h = g * SSM_HPG + hh
            rows = slice(hh * SSM_HEAD_DIM, (hh + 1) * SSM_HEAD_DIM)
            h_ref[0, h] = h_g[rows, :] * cd[h:h + 1, 0:1] + st[rows, :]


def _ssd_block_kernel(xbc_ref, z_ref, dt_ref, dtT_ref, h0_ref, *rest, spb, lc, n_consts, n_valid, conv):
    consts, rest = rest[:n_consts], rest[n_consts:]
    if conv == "history":
        hist_ref, cw_ref, cb_ref, y_ref, h_ref, scratch = rest
    else:
        cw_ref, cb_ref, y_ref, h_ref, scratch = rest
    for s in range(spb):
        rows, one = pl.ds(s * lc, lc), pl.ds(s, 1)
        conv_refs = (cw_ref, cb_ref)
        if conv == "history":
            conv_refs = (hist_ref.at[pl.ds(s * SUBLANES, SUBLANES)],) + conv_refs
        _ssd_kernel(xbc_ref.at[rows], z_ref.at[rows], dt_ref.at[rows], dtT_ref.at[one], h0_ref.at[one], *consts,
                    *conv_refs, y_ref.at[rows], h_ref.at[one], scratch,
                    lc=lc, n_valid=n_valid, conv=conv, single_chunk=True)


def _ssd(xbc, z, dt, dtT, h0, consts, nb, nc, lc, n_valid, conv_w, conv_b, hist=None, spb=1):
    assert spb == 1 or (nc == 1 and dtT.ndim == 3 and nb % spb == 0)
    t = xbc.shape[0]
    row = lambda n: pl.BlockSpec((spb * lc, n), lambda b, c: (b * nc + c, 0))
    if dtT.ndim == 2:
        dtT_spec = pl.BlockSpec((SSM_HEADS, lc), lambda b, c: (0, b * nc + c))
    else:
        dtT_spec = pl.BlockSpec((spb, SSM_HEADS, lc), lambda b, c: (b * nc + c, 0, 0))
    state_spec = pl.BlockSpec((spb, SSM_HEADS, SSM_HEAD_DIM, D_STATE), lambda b, c: (b, 0, 0, 0))
    in_specs = [row(CONV_DIM), row(D_INNER), row(SSM_HEADS), dtT_spec, state_spec] + [_const_spec(a.shape) for a in consts]
    args = [xbc, z, dt, dtT, h0, *consts]
    if hist is not None:
        in_specs.append(pl.BlockSpec((spb * SUBLANES, CONV_DIM), lambda b, c: (b, 0)))
        args.append(hist)
        scratch = [pltpu.VMEM((SUBLANES + lc, CONV_DIM), F32)]
    else:
        scratch = [pltpu.VMEM((PIECE, CONV_DIM), BF16)]
    in_specs += [_const_spec(conv_w.shape), _const_spec(conv_b.shape)]
    args += [conv_w, conv_b]
    conv = "history" if hist is not None else "fresh"
    if spb == 1:
        body = functools.partial(_ssd_kernel, lc=lc, n_valid=n_valid, conv=conv, single_chunk=nc == 1)
    else:
        body = functools.partial(_ssd_block_kernel, spb=spb, lc=lc, n_consts=len(consts), n_valid=n_valid, conv=conv)
    return pl.pallas_call(
        body,
        grid=(nb // spb, nc),
        in_specs=in_specs,
        out_specs=[row(D_INNER), state_spec],
        out_shape=[jax.ShapeDtypeStruct((t, D_INNER), BF16),
                   jax.ShapeDtypeStruct((nb, SSM_HEADS, SSM_HEAD_DIM, D_STATE), F32)],
        scratch_shapes=scratch,
        compiler_params=pltpu.CompilerParams(dimension_semantics=("arbitrary", "arbitrary"),
                                             vmem_limit_bytes=VMEM_LIMIT),
        name="ssd",
    )(*args)


def _swa_kernel(slope_ref, sink_ref, q_ref, kc_ref, vc_ref, kp_ref, vp_ref, y_ref, bias_ref, *, tq, prev_always):
    fold = tq == WINDOW
    nk = WINDOW + tq

    @pl.when((pl.program_id(0) == 0) & (pl.program_id(1) == 0))
    def _():
        rows = WINDOW if fold else nk
        j = lax.broadcasted_iota(I32, (rows, tq), 0)
        r = lax.broadcasted_iota(I32, (rows, tq), 1)
        if fold:
            dist = jnp.where(j > r, r + WINDOW - j, r - j)
            valid, from_previous = dist >= 0, j > r
        else:
            dist = r + WINDOW - j
            valid, from_previous = (dist >= 0) & (dist < WINDOW), j < WINDOW
        distf = dist.astype(F32)
        for h in range(N_Q_HEADS):
            penalty = -slope_ref[h] * distf
            bias_ref[h] = jnp.where(valid, penalty, NEG_BIG)
            if not prev_always:
                bias_ref[N_Q_HEADS + h] = jnp.where(valid & jnp.logical_not(from_previous), penalty, NEG_BIG)

    first = 0 if prev_always else jnp.where(pl.program_id(1) == 0, N_Q_HEADS, 0)
    lane = lax.broadcasted_iota(I32, (nk, LANES), 1)
    zeros_half = jnp.zeros((ATTN_HEAD_DIM, nk), BF16)
    sink_row = lax.broadcasted_iota(I32, (SUBLANES, tq), 0) == 0
    ones_keys = jnp.ones((nk + SUBLANES, LANES), BF16)
    if fold:
        ji = lax.broadcasted_iota(I32, (WINDOW, tq), 0)
        ri = lax.broadcasted_iota(I32, (WINDOW, tq), 1)
        from_prev = ji > ri
        from_prev_b = ji.astype(BF16) > ri.astype(BF16)
    heads = []
    for t in range(D_KV // LANES):
        cols = slice(t * LANES, (t + 1) * LANES)
        kt = jnp.concatenate([kp_ref[:, cols].astype(F32), kc_ref[:, cols].astype(F32)], axis=0)
        vt = jnp.concatenate([vp_ref[:, cols].astype(F32), vc_ref[:, cols].astype(F32)], axis=0)
        if fold:
            vt_t = vt.T.astype(BF16)
        for b in range(2):
            mine = (lane >= ATTN_HEAD_DIM) if b else (lane < ATTN_HEAD_DIM)
            k_same = jnp.where(mine, kt, 0.0)
            k_half = {b: k_same.astype(BF16), 1 - b: pltpu.roll(k_same, ATTN_HEAD_DIM, 1).astype(BF16)}
            if fold:
                v_g = vt_t[b * ATTN_HEAD_DIM:(b + 1) * ATTN_HEAD_DIM, :]
                v_half = {0: jnp.concatenate([v_g, zeros_half], axis=0),
                          1: jnp.concatenate([zeros_half, v_g], axis=0)}
            else:
                v_same = jnp.concatenate([jnp.where(mine, vt, 0.0), jnp.zeros((SUBLANES, LANES), F32)], axis=0)
                v_half = {b: v_same.astype(BF16), 1 - b: pltpu.roll(v_same, ATTN_HEAD_DIM, 1).astype(BF16)}
            for qi in range(Q_PER_KV):
                a = qi % 2
                heads.append((k_half[a], v_half[a]))

    q_tiles = [(q_ref[:, jq * LANES:(jq + 1) * LANES] * (ATTN_HEAD_DIM ** -0.5)).astype(BF16)
               for jq in range(D_ATTN // LANES)]

    def scores(h):
        k, q = heads[h][0], q_tiles[h // 2]
        s = lax.dot_general(k, q, NT, preferred_element_type=F32)
        if fold:
            s = jnp.where(from_prev, s[:WINDOW], s[WINDOW:])
        return s + bias_ref[first + h]

    def attend(h, s):
        sink = sink_ref[h]
        v = heads[h][1]
        m = jnp.maximum(jnp.max(s, axis=0, keepdims=True), sink)
        e = jnp.exp(s - m)
        e_sink = jnp.exp(sink - m)
        if fold:
            rden = 1.0 / (jnp.sum(e, axis=0, keepdims=True) + e_sink)
            e_b = e.astype(BF16)
            zero = jnp.zeros_like(e_b)
            p = jnp.concatenate([jnp.where(from_prev_b, e_b, zero), jnp.where(from_prev_b, zero, e_b)], axis=0)
            return jnp.dot(v, p, preferred_element_type=F32) * rden
        p = jnp.concatenate([e, jnp.where(sink_row, e_sink, 0.0)], axis=0).astype(BF16)
        num = lax.dot_general(p, v, TN, preferred_element_type=F32)
        den = lax.dot_general(p, ones_keys, TN, preferred_element_type=F32)
        return num / den

    out_tiles = [None] * (D_ATTN // LANES)
    s_next = scores(0)
    for h in range(N_Q_HEADS):
        s_cur = s_next
        if h + 1 < N_Q_HEADS:
            s_next = scores(h + 1)
        o = attend(h, s_cur)
        out_tiles[h // 2] = o if out_tiles[h // 2] is None else out_tiles[h // 2] + o
    for jq, o in enumerate(out_tiles):
        y_ref[:, jq * LANES:(jq + 1) * LANES] = (o.T if fold else o).astype(y_ref.dtype)


def _swa(slopes, sinks, q, k, v, k_prev, v_prev, nb, nblk, tq, prev_always):
    t = q.shape[0]
    cur = lambda n: pl.BlockSpec((tq, n), lambda b, i: (b * nblk + i, 0))
    if prev_always:
        prev = pl.BlockSpec((WINDOW, D_KV), lambda b, i: (b, 0))
    else:
        prev = pl.BlockSpec((WINDOW, D_KV), lambda b, i: (b * nblk + jnp.maximum(i - 1, 0), 0))
    smem = pl.BlockSpec(memory_space=pltpu.SMEM)
    return pl.pallas_call(
        functools.partial(_swa_kernel, tq=tq, prev_always=prev_always),
        grid=(nb, nblk),
        in_specs=[smem, smem, cur(D_ATTN), cur(D_KV), cur(D_KV), prev, prev],
        out_specs=cur(D_ATTN),
        out_shape=jax.ShapeDtypeStruct((t, D_ATTN), BF16),
        scratch_shapes=[pltpu.VMEM(((1 if prev_always else 2) * N_Q_HEADS, WINDOW if tq == WINDOW else WINDOW + tq,
                                    tq), F32)],
        compiler_params=pltpu.CompilerParams(dimension_semantics=("arbitrary", "arbitrary"),
                                             vmem_limit_bytes=VMEM_LIMIT),
        name="swa",
    )(slopes, sinks, q, k, v, k_prev, v_prev)


def _post_kernel(x_ref, ys_ref, ya_ref, g_ref, wsp_ref, wap_ref, wo_ref, nw_ref, wr_ref, br_ref, *rest,
                 n_exp, lr, nt, aliased):
    if aliased:
        rest = rest[1:]
    h_ref, xs_ref, lpos_ref, pr_ref, pc_ref = rest
    tp = x_ref.shape[0]

    @pl.when(pl.program_id(0) >= nt)
    def _():
        xs_ref[...] = jnp.zeros_like(xs_ref)

    @pl.when(pl.program_id(0) < nt)
    def _():
        a = jnp.dot(ys_ref[...].astype(BF16), wsp_ref[...], preferred_element_type=F32)
        b = jnp.dot(ya_ref[...].astype(BF16), wap_ref[...], preferred_element_type=F32)
        g = g_ref[...].astype(F32)
        merged = jax.nn.sigmoid(g[:, :D_MODEL]) * a + jax.nn.sigmoid(g[:, D_MODEL:]) * b
        h = x_ref[...] + jnp.dot(merged.astype(BF16), wo_ref[...], preferred_element_type=F32)
        h_ref[...] = h
        hn = h * lax.rsqrt(jnp.mean(h * h, axis=-1, keepdims=True) + EPS) * nw_ref[...]
        w_hi, w_mid, _ = _split3(wr_ref[...])
        x_hi, x_mid, _ = _split3(hn)
        both = lax.dot_general(jnp.concatenate([w_hi, w_mid], axis=0), x_hi, NT, preferred_element_type=F32)
        logits = (both[:n_exp] + lax.dot_general(w_hi, x_mid, NT, preferred_element_type=F32)
                  + both[n_exp:]) + br_ref[...]
        eidx = lax.broadcasted_iota(I32, logits.shape, 0).astype(F32)
        work = logits
        vals, ids = [], []
        for _ in range(TOP_K):
            m = jnp.max(work, axis=0, keepdims=True)
            first = jnp.min(jnp.where(work == m, eidx, float(n_exp)), axis=0, keepdims=True)
            vals.append(m)
            ids.append(first)
            work = jnp.where(eidx == first, -jnp.inf, work)
        es = [jnp.exp(v - vals[0]) for v in vals]
        den = es[0] + es[1] + es[2] + es[3]
        eye = jnp.where(lax.broadcasted_iota(I32, (TOP_K, TOP_K), 0) == lax.broadcasted_iota(I32, (TOP_K, TOP_K), 1),
                        1.0, 0.0).astype(BF16)

        def to_columns(rows):
            return sum(lax.dot_general(p, eye, TN, preferred_element_type=F32) for p in _split3(rows))

        onehot = [jnp.where(eidx == i, 1.0, 0.0) for i in ids]
        counts = [jnp.sum(o, axis=1, keepdims=True) for o in onehot]
        total = counts[0] + counts[1] + counts[2] + counts[3]
        padded = jnp.floor((total + (PIECE - 1)) * (1.0 / PIECE)) * PIECE
        ei = lax.broadcasted_iota(I32, (n_exp, n_exp), 0)
        ej = lax.broadcasted_iota(I32, (n_exp, n_exp), 1)
        below = jnp.where(ej < ei, 1.0, 0.0).astype(BF16)
        padded_b = jnp.broadcast_to(padded, (n_exp, LANES))
        seg_off = sum(jnp.dot(below, p, preferred_element_type=F32) for p in _split3(padded_b))[:, 0:1]
        ti = lax.broadcasted_iota(I32, (tp, tp), 0)
        tj = lax.broadcasted_iota(I32, (tp, tp), 1)
        before = jnp.where(ti < tj, 1.0, 0.0).astype(BF16)
        base = seg_off
        lpos = []
        prefixes = jnp.dot(jnp.concatenate(onehot, axis=0).astype(BF16), before,
                           preferred_element_type=F32)
        for k in range(TOP_K):
            prefix = prefixes[k * n_exp:(k + 1) * n_exp]
            lpos.append(jnp.sum(onehot[k] * (base + prefix), axis=0, keepdims=True))
            base = base + counts[k]
        pc_ref[0] = padded_b.astype(I32)

        hn_b = hn.astype(BF16)
        rc = lr // 4
        for c in range(4):
            ri = (lax.broadcasted_iota(I32, (rc, tp), 0) + c * rc).astype(F32)
            sel = jnp.zeros((rc, tp), F32)
            for k in range(TOP_K):
                sel = jnp.where(ri == lpos[k], 1.0, sel)
            xs_ref[c * rc:(c + 1) * rc, :] = jnp.dot(sel.astype(BF16), hn_b, preferred_element_type=F32).astype(BF16)

        pr_ref[...] = to_columns(jnp.concatenate([e / den for e in es], axis=0))
        lpos_ref[...] = to_columns(jnp.concatenate(lpos, axis=0)).astype(I32)


def _post(x, y_ssm, y_attn, gates, wsp, wap, wo, ffn_nw, w_rT, b_r, tm, lr, xs_rows, xs_block0, pad_steps,
          xs_prev=None):
    t = x.shape[0]
    n_exp = w_rT.shape[0]
    nt = t // tm
    last = nt - 1
    row = lambda n: pl.BlockSpec((tm, n), lambda i: (jnp.minimum(i, last), 0))
    col = row(TOP_K)
    in_specs = [row(D_MODEL), row(D_INNER), row(D_ATTN), row(2 * D_MODEL),
                _resident_spec(wsp.shape), _resident_spec(wap.shape), _resident_spec(wo.shape),
                _const_spec(ffn_nw.shape), _const_spec(w_rT.shape), _const_spec(b_r.shape)]
    args = [x, y_ssm, y_attn, gates, wsp, wap, wo, ffn_nw, w_rT, b_r]
    aliases = {}
    if xs_prev is not None:
        in_specs.append(pl.BlockSpec(memory_space=pl.ANY))
        args.append(xs_prev)
        aliases = {len(args) - 1: 1}
    return pl.pallas_call(
        functools.partial(_post_kernel, n_exp=n_exp, lr=lr, nt=nt, aliased=xs_prev is not None),
        grid=(nt + pad_steps,),
        in_specs=in_specs,
        out_specs=[row(D_MODEL), pl.BlockSpec((lr, D_MODEL), lambda i: (xs_block0 + i, 0)), col, col,
                   pl.BlockSpec((1, n_exp, LANES), lambda i: (jnp.minimum(i, last), 0, 0))],
        out_shape=[jax.ShapeDtypeStruct((t, D_MODEL), F32), jax.ShapeDtypeStruct((xs_rows, D_MODEL), BF16),
                   jax.ShapeDtypeStruct((t, TOP_K), I32), jax.ShapeDtypeStruct((t, TOP_K), F32),
                   jax.ShapeDtypeStruct((nt, n_exp, LANES), I32)],
        input_output_aliases=aliases,
        compiler_params=pltpu.CompilerParams(dimension_semantics=("arbitrary",), vmem_limit_bytes=VMEM_LIMIT),
        name="post",
    )(*args)


def _piece(ref, p):
    return ref.at[pl.ds(pl.multiple_of(p * PIECE, PIECE), PIECE)]


def _moe_kernel(te_ref, nu_ref, src_ref, xs_hbm, wg_ref, wu_ref, wd_ref, b_ref, ys_hbm, xbuf, obuf, wbf, gsem, ssem):
    i = pl.program_id(0)
    n_used = nu_ref[0]
    slot = i % 2
    tme = PIECES_PER_TILE * PIECE

    def gather(tile, s):
        for r in range(PIECES_PER_TILE):
            pltpu.make_async_copy(_piece(xs_hbm, src_ref[tile * PIECES_PER_TILE + r]),
                                  xbuf.at[s, pl.ds(r * PIECE, PIECE)], gsem.at[s]).start()

    def wait_gather(s):
        pltpu.make_async_copy(xs_hbm.at[pl.ds(0, tme)], xbuf.at[s], gsem.at[s]).wait()

    def wait_put(s):
        pltpu.make_async_copy(obuf.at[s], ys_hbm.at[pl.ds(0, tme)], ssem.at[s]).wait()

    @pl.when(i == 0)
    def _():
        gather(0, 0)

    @pl.when(i < n_used)
    def _():
        gather(i + 1, 1 - slot)
        wait_gather(slot)

        @pl.when(i >= 2)
        def _():
            wait_put(slot)

        @pl.when((i == 0) | (te_ref[i] != te_ref[jnp.maximum(i - 1, 0)]))
        def _():
            wbf[0] = wg_ref[0].astype(BF16)
            wbf[1] = wu_ref[0].astype(BF16)
            wbf[2] = wd_ref[0].astype(BF16)

        x = xbuf[slot]
        g = jnp.minimum(jnp.dot(x, wbf[0], preferred_element_type=F32) + b_ref[0, 0:1, :], SWIGLU_LIMIT)
        u = jnp.clip(jnp.dot(x, wbf[1], preferred_element_type=F32) + b_ref[0, 1:2, :], -SWIGLU_LIMIT, SWIGLU_LIMIT)
        act = ((u + 1.0) * g * jax.nn.sigmoid(SWIGLU_ALPHA * g)).astype(BF16)
        obuf[slot] = (jnp.dot(act, wbf[2], preferred_element_type=F32) + b_ref[0, 2:3, :]).astype(BF16)
        for r in range(PIECES_PER_TILE):
            pltpu.make_async_copy(obuf.at[slot, pl.ds(r * PIECE, PIECE)],
                                  _piece(ys_hbm, src_ref[i * PIECES_PER_TILE + r]), ssem.at[slot]).start()

        @pl.when(i == n_used - 1)
        def _():
            wait_put(slot)
            wait_gather(1 - slot)

            @pl.when(i >= 1)
            def _():
                wait_put(1 - slot)


def _moe(tile_expert, n_used, src, xs, wg, wu, wd, biases):
    n_tiles = tile_expert.shape[0]
    tme = PIECES_PER_TILE * PIECE
    wspec = pl.BlockSpec((1, D_MODEL, D_FF), lambda i, te, nu, sr: (te[i], 0, 0))
    grid_spec = pltpu.PrefetchScalarGridSpec(
        num_scalar_prefetch=3,
        grid=(n_tiles,),
        in_specs=[pl.BlockSpec(memory_space=pl.ANY), wspec, wspec, wspec,
                  pl.BlockSpec((1, 3, D_FF), lambda i, te, nu, sr: (te[i], 0, 0))],
        out_specs=pl.BlockSpec(memory_space=pl.ANY),
        scratch_shapes=[pltpu.VMEM((2, tme, D_MODEL), BF16), pltpu.VMEM((2, tme, D_MODEL), BF16),
                        pltpu.VMEM((3, D_MODEL, D_FF), BF16),
                        pltpu.SemaphoreType.DMA((2,)), pltpu.SemaphoreType.DMA((2,))],
    )
    return pl.pallas_call(
        _moe_kernel,
        grid_spec=grid_spec,
        out_shape=jax.ShapeDtypeStruct(xs.shape, xs.dtype),
        input_output_aliases={3: 0},
        compiler_params=pltpu.CompilerParams(dimension_semantics=("arbitrary",), vmem_limit_bytes=VMEM_LIMIT),
        name="moe",
    )(tile_expert, n_used, src, xs, wg, wu, wd, biases)


def _combine_kernel(h_ref, ys_ref, lpos_ref, pr_ref, nw_ref, o_ref):
    tp = h_ref.shape[0]
    lr = ys_ref.shape[0]
    ri = lax.broadcasted_iota(I32, (tp, lr), 1)
    lp = lpos_ref[...]
    pr = pr_ref[...]
    pw = jnp.zeros((tp, lr), F32)
    for k in range(TOP_K):
        pw = jnp.where(ri == lp[:, k:k + 1], pr[:, k:k + 1], pw)
    moe = jnp.dot(pw.astype(BF16), ys_ref[...], preferred_element_type=F32)
    h = h_ref[...] + moe
    o_ref[...] = h * lax.rsqrt(jnp.mean(h * h, axis=-1, keepdims=True) + EPS) * nw_ref[...]


def _combine(h, ys, lpos_t, probs_t, final_nw, tm, lr, ys_block0):
    t = h.shape[0]
    return pl.pallas_call(
        _combine_kernel,
        grid=(t // tm,),
        in_specs=[pl.BlockSpec((tm, D_MODEL), lambda i: (i, 0)),
                  pl.BlockSpec((lr, D_MODEL), lambda i: (ys_block0 + i, 0)),
                  pl.BlockSpec((tm, TOP_K), lambda i: (i, 0)),
                  pl.BlockSpec((tm, TOP_K), lambda i: (i, 0)),
                  _const_spec(final_nw.shape)],
        out_specs=pl.BlockSpec((tm, D_MODEL), lambda i: (i, 0)),
        out_shape=jax.ShapeDtypeStruct((t, D_MODEL), F32),
        compiler_params=pltpu.CompilerParams(dimension_semantics=("arbitrary",), vmem_limit_bytes=VMEM_LIMIT),
        name="combine",
    )(h, ys, lpos_t, probs_t, final_nw)


def _piece_table(padded_counts, tile_row0, n_tiles, spare_piece0):
    n_pieces = (padded_counts // PIECE).T
    seg_row = tile_row0[:, None] + jnp.cumsum(padded_counts, axis=1) - padded_counts
    seg_piece = (seg_row // PIECE).T
    per_expert = n_pieces.sum(axis=1)
    tiles_per = (per_expert + PIECES_PER_TILE - 1) // PIECES_PER_TILE
    tile_end = jnp.cumsum(tiles_per)
    n_used = tile_end[-1]
    slot0 = (tile_end - tiles_per) * PIECES_PER_TILE
    seg_slot = (slot0[:, None] + jnp.cumsum(n_pieces, axis=1) - n_pieces).reshape(-1)
    seg_n = n_pieces.reshape(-1)
    seg_src = seg_piece.reshape(-1)
    slots = jnp.arange(n_tiles * PIECES_PER_TILE, dtype=I32)

    def at_segment_of_slot(f):
        df = f - jnp.concatenate([jnp.zeros((1,), I32), f[:-1]])
        return jnp.cumsum(jnp.zeros(slots.shape, I32).at[seg_slot].add(df))

    real = slots < at_segment_of_slot(seg_slot + seg_n)
    padding = jnp.logical_not(real) & (slots < n_used * PIECES_PER_TILE)
    spare = spare_piece0 + jnp.where(padding, jnp.cumsum(padding.astype(I32)), 0)
    src = jnp.where(real, slots + at_segment_of_slot(seg_src - seg_slot), spare).astype(I32)
    tile_ids = jnp.arange(n_tiles, dtype=I32)
    tile_expert = jnp.sum((tile_end[None, :] <= jnp.minimum(tile_ids, n_used - 1)[:, None]).astype(I32), axis=1)
    return tile_expert.astype(I32), n_used.reshape(1).astype(I32), src


def _pick_tile(n, pref):
    while n % pref:
        pref //= 2
    return pref


def _local_rows(tm, n_exp):
    need = TOP_K * tm + n_exp * (PIECE - 1)
    return -(-need // 64) * 64


def kernel(x_prompt, x_sample, state_conv, state_ssm, cache_swa_k, cache_swa_v, attn_norm_w, w_in, conv_w, conv_b, dt_bias, a_log, d_skip, ssm_norm_w, attn_sinks, w_ssm_proj, w_attn_proj, w_o, ffn_norm_w, w_router, b_router, w_gate, b_gate, w_up, b_up, w_down, b_down, final_norm_w):
    assert w_in.shape[0] == 1, "single-layer step"
    nb, seq, _ = x_prompt.shape
    nbs = x_sample.shape[0]
    n_exp = w_router.shape[-1]
    tp, ts = nb * seq, nbs
    pad = SUBLANES

    w_packed, w_blocks = _pack_in_proj_weight(w_in[0])
    attn_nw = attn_norm_w[0].reshape(1, D_MODEL)
    a_neg = -jnp.exp(a_log[0].astype(F32))
    head_of = jnp.arange(D_INNER, dtype=I32) // SSM_HEAD_DIM
    expand = jnp.tile((jnp.arange(SSM_HEADS, dtype=I32)[:, None] == head_of[None, :]).astype(BF16), (2, 1))
    conv_consts = (conv_w[0], conv_b[0].reshape(1, CONV_DIM))
    ssd_consts = (dt_bias[0].reshape(1, SSM_HEADS), dt_bias[0].reshape(SSM_HEADS, 1),
                  a_neg.reshape(1, SSM_HEADS), a_neg.reshape(SSM_HEADS, 1),
                  d_skip[0][head_of].reshape(1, D_INNER), ssm_norm_w[0].reshape(1, D_INNER), expand)
    slopes = jnp.exp2(-8.0 * jnp.arange(1, N_Q_HEADS + 1, dtype=F32) / N_Q_HEADS)
    sinks = attn_sinks[0].astype(F32)
    wsp, wap, wo = w_ssm_proj[0].astype(BF16), w_attn_proj[0].astype(BF16), w_o[0].astype(BF16)
    ffn_nw = ffn_norm_w[0].reshape(1, D_MODEL)
    w_rT = w_router[0].T
    b_r = b_router[0].reshape(n_exp, 1)
    expert_biases = jnp.stack([b_gate[0], b_up[0], b_down[0]], axis=1)
    final_nw = final_norm_w.reshape(1, D_MODEL)

    xp = x_prompt.reshape(tp, D_MODEL)
    tm_in = _pick_tile(seq, 512)
    z, xbc, dt, q, k, v, gates, conv_tail, dtT, k_tail, v_tail = _in_proj(
        xp, attn_nw, w_packed, w_blocks, tm_in, BF16, tiles_per_seq=seq // tm_in)
    nc = seq // CHUNK
    y_ssm, ssm_p = _ssd(xbc, z, dt, dtT, jnp.zeros((nb, SSM_HEADS, SSM_HEAD_DIM, D_STATE), F32), ssd_consts,
                        nb, nc, CHUNK, CHUNK, *conv_consts)
    nblk = seq // WINDOW
    y_attn = _swa(slopes, sinks, q, k, v, k, v, nb, nblk, WINDOW, False)

    xs_pad = jnp.pad(x_sample.reshape(ts, 1, D_MODEL), ((0, 0), (0, pad - 1), (0, 0))).reshape(ts * pad, D_MODEL)
    z_s, xbc_s, dt_s, q_s, k_s, v_s, gates_s = _in_proj(xs_pad, attn_nw, w_packed, w_blocks,
                                                        _pick_tile(ts * pad, 256), F32)
    dtT_s = dt_s.reshape(ts, pad, SSM_HEADS).transpose(0, 2, 1)
    spb = _pick_tile(ts, SAMPLE_SEQS_PER_STEP)
    hist_s = jnp.pad(state_conv[0], ((0, 0), (pad - (CONV_W - 1), 0), (0, 0))).reshape(ts * pad, CONV_DIM)
    y_ssm_s, ssm_s = _ssd(xbc_s, z_s, dt_s, dtT_s, state_ssm[0], ssd_consts, ts, 1, pad, 1, *conv_consts,
                          hist=hist_s, spb=spb)
    kc = cache_swa_k[0].reshape(ts * WINDOW, D_KV)
    vc = cache_swa_v[0].reshape(ts * WINDOW, D_KV)
    y_attn_s = _swa(slopes, sinks, q_s, k_s, v_s, kc, vc, ts, 1, pad, True)
    real = lambda a: a.reshape(ts, pad, -1)[:, 0]

    tm_p, tm_s = _pick_tile(tp, 512), ts
    nt_p = tp // tm_p
    lr_p, lr_s = _local_rows(tm_p, n_exp), _local_rows(tm_s, n_exp)
    if (nt_p * lr_p) % lr_s or lr_s > lr_p:
        lr_s = lr_p
    spare_rows = lr_s + (1 + n_exp * (PIECES_PER_TILE - 1)) * PIECE
    pad_steps = -(-spare_rows // lr_p)
    xs_rows = (nt_p + pad_steps) * lr_p
    block_s = nt_p * lr_p // lr_s
    post_w = (wsp, wap, wo, ffn_nw, w_rT, b_r)
    h_p, xs, lpos_p, pr_p, pc_p = _post(xp, y_ssm, y_attn, gates, *post_w, tm_p, lr_p, xs_rows, 0, pad_steps)
    h_s, xs, lpos_s, pr_s, pc_s = _post(x_sample.reshape(ts, D_MODEL), real(y_ssm_s), real(y_attn_s), real(gates_s),
                                        *post_w, tm_s, lr_s, xs_rows, block_s, 0, xs_prev=xs)

    padded_counts = jnp.concatenate([pc_p[:, :, 0], pc_s[:, :, 0]], axis=0)
    tile_row0 = jnp.concatenate([jnp.arange(nt_p, dtype=I32) * lr_p, jnp.full((1,), nt_p * lr_p, I32)])
    max_pieces = (TOP_K * (tp + ts) + (PIECE - 1) * n_exp * (nt_p + 1)) // PIECE + n_exp * (PIECES_PER_TILE - 1)
    n_tiles = -(-max_pieces // PIECES_PER_TILE) + 1
    tile_expert, n_used, src = _piece_table(padded_counts, tile_row0, n_tiles, (nt_p * lr_p + lr_s) // PIECE)
    ys = _moe(tile_expert, n_used, src, xs, w_gate[0], w_up[0], w_down[0], expert_biases)
    out_p = _combine(h_p, ys, lpos_p, pr_p, final_nw, tm_p, lr_p, 0)
    out_s = _combine(h_s, ys, lpos_s, pr_s, final_nw, tm_s, lr_s, block_s)

    y_prompt = out_p.reshape(nb, seq, D_MODEL)
    y_sample = out_s.reshape(nbs, 1, D_MODEL)
    conv_p = conv_tail[:, SUBLANES - (CONV_W - 1):][None]
    k_p = k_tail.reshape(1, nb, WINDOW, N_KV_HEADS, ATTN_HEAD_DIM)
    v_p = v_tail.reshape(1, nb, WINDOW, N_KV_HEADS, ATTN_HEAD_DIM)
    conv_s = jnp.concatenate([state_conv[0][:, 1:], real(xbc_s)[:, None]], axis=1)[None]
    k_new = real(k_s).reshape(ts, 1, N_KV_HEADS, ATTN_HEAD_DIM)
    v_new = real(v_s).reshape(ts, 1, N_KV_HEADS, ATTN_HEAD_DIM)
    ks_out = jnp.concatenate([cache_swa_k[0][:, 1:], k_new], axis=1)[None]
    vs_out = jnp.concatenate([cache_swa_v[0][:, 1:], v_new], axis=1)[None]
    return (y_prompt, y_sample, conv_p, ssm_p[None], k_p, v_p, conv_s, ssm_s[None], ks_out, vs_out)
```

```python
import functools

import jax
import jax.numpy as jnp
from jax import lax
from jax.experimental import pallas as pl
from jax.experimental.pallas import tpu as pltpu

F32, BF16, I32 = jnp.float32, jnp.bfloat16, jnp.int32

D_MODEL = 1024
D_INNER = 2 * D_MODEL
SSM_HEAD_DIM = 64
SSM_HEADS = D_INNER // SSM_HEAD_DIM
SSM_GROUPS = 4
SSM_HPG = SSM_HEADS // SSM_GROUPS
D_STATE = 128
CONV_W = 4
CONV_DIM = D_INNER + 2 * SSM_GROUPS * D_STATE
CHUNK = 128
ATTN_HEAD_DIM = 64
N_Q_HEADS = D_MODEL // ATTN_HEAD_DIM
N_KV_HEADS = 4
Q_PER_KV = N_Q_HEADS // N_KV_HEADS
D_ATTN = N_Q_HEADS * ATTN_HEAD_DIM
D_KV = N_KV_HEADS * ATTN_HEAD_DIM
WINDOW = 128
TOP_K = 4
D_FF = D_MODEL
SWIGLU_LIMIT = 7.0
SWIGLU_ALPHA = 1.702
EPS = 1e-5
NEG_BIG = -1e30

LANES = 128
SUBLANES = 8
GROUP_W = D_INNER // SSM_GROUPS
PIECE = 2 * SUBLANES
PIECES_PER_TILE = 32
SSD_CHUNKS_PER_STEP = 2
SAMPLE_SEQS_PER_STEP = 8
VMEM_LIMIT = 56 * 1024 * 1024

NT = (((1,), (1,)), ((), ()))
TN = (((0,), (0,)), ((), ()))


def _const_spec(shape):
    return pl.BlockSpec(shape, lambda *_: (0,) * len(shape))


def _resident_spec(shape):
    return pl.BlockSpec(shape, lambda *_: (0,) * len(shape), pipeline_mode=pl.Buffered(1))


def _split3(x):
    hi = x.astype(BF16)
    r1 = x - hi.astype(F32)
    mid = r1.astype(BF16)
    lo = (r1 - mid.astype(F32)).astype(BF16)
    return hi, mid, lo


def _softplus(x):
    return jnp.maximum(x, 0.0) + jnp.log(1.0 + jnp.exp(-jnp.abs(x)))


def _causal_conv_silu(buf_ref, n, cw_ref, cb_ref):
    full = buf_ref[...]
    conv = cb_ref[...]
    for j in range(CONV_W):
        shifted = full if j == CONV_W - 1 else pltpu.roll(full, CONV_W - 1 - j, 0)
        conv = conv + shifted[SUBLANES:SUBLANES + n, :] * cw_ref[j:j + 1, :]
    buf_ref[0:SUBLANES, :] = full[n:n + SUBLANES, :]
    return conv * jax.nn.sigmoid(conv)


def _inproj_kernel(x_ref, nw_ref, wz_ref, wxbc_ref, wdt_ref, wq_ref, wk_ref, wv_ref, wg_ref, *rest, tiles_per_seq):
    if tiles_per_seq:
        wdtT_ref, z_ref, xbc_ref, dt_ref, q_ref, k_ref, v_ref, g_ref, tail_ref, dtT_ref, ktail_ref, vtail_ref = rest
    else:
        z_ref, xbc_ref, dt_ref, q_ref, k_ref, v_ref, g_ref = rest
    tm = x_ref.shape[0]
    x = x_ref[...]
    xn = x * lax.rsqrt(jnp.mean(x * x, axis=-1, keepdims=True) + EPS)
    xn = (xn * nw_ref[...]).astype(BF16)
    for w_ref, o_ref in ((wz_ref, z_ref), (wdt_ref, dt_ref), (wq_ref, q_ref), (wg_ref, g_ref)):
        o_ref[...] = jnp.dot(xn, w_ref[...], preferred_element_type=F32).astype(o_ref.dtype)
    xbc = jnp.dot(xn, wxbc_ref[...], preferred_element_type=F32)
    k = jnp.dot(xn, wk_ref[...], preferred_element_type=F32)
    v = jnp.dot(xn, wv_ref[...], preferred_element_type=F32)
    xbc_ref[...] = xbc.astype(xbc_ref.dtype)
    k_ref[...] = k.astype(k_ref.dtype)
    v_ref[...] = v.astype(v_ref.dtype)
    if tiles_per_seq:
        dtT_ref[...] = lax.dot_general(wdtT_ref[...], xn, NT, preferred_element_type=F32)

        @pl.when(pl.program_id(0) % tiles_per_seq == tiles_per_seq - 1)
        def _():
            tail_ref[0] = xbc[tm - SUBLANES:, :]
            ktail_ref[0] = k[tm - WINDOW:, :]
            vtail_ref[0] = v[tm - WINDOW:, :]


IN_PROJ_WIDTHS = (D_INNER, CONV_DIM, SSM_HEADS, D_ATTN, D_KV, D_KV, 2 * D_MODEL)
IN_PROJ_PACK_ORDER = (1, 3, 0, 6, 4, 5, 2)


def _pack_in_proj_weight(w):
    cuts = [0]
    for n in IN_PROJ_WIDTHS:
        cuts.append(cuts[-1] + n)
    packed = jnp.concatenate([w[:, cuts[i]:cuts[i + 1]] for i in IN_PROJ_PACK_ORDER], axis=1).astype(BF16)
    block, off = [0] * len(IN_PROJ_WIDTHS), 0
    for i in IN_PROJ_PACK_ORDER:
        block[i], rem = divmod(off, IN_PROJ_WIDTHS[i])
        assert rem == 0
        off += IN_PROJ_WIDTHS[i]
    return packed, tuple(block)


def _in_proj(x, norm_w, w_packed, w_blocks, tm, act_dtype, tiles_per_seq=None):
    t = x.shape[0]
    widths = IN_PROJ_WIDTHS
    dtypes = (act_dtype, act_dtype, F32, act_dtype, act_dtype, act_dtype, act_dtype)
    row = lambda n: pl.BlockSpec((tm, n), lambda i: (i, 0))
    w_spec = lambda n, blk: pl.BlockSpec((D_MODEL, n), lambda i: (0, blk), pipeline_mode=pl.Buffered(1))
    dt_col = w_blocks[2] * SSM_HEADS
    w_dt = w_packed[:, dt_col:dt_col + SSM_HEADS]
    in_specs = [row(D_MODEL), _const_spec((1, D_MODEL))]
    args = [x, norm_w]
    for i, (n, blk) in enumerate(zip(widths, w_blocks)):
        in_specs.append(_resident_spec((D_MODEL, n)) if i == 2 else w_spec(n, blk))
        args.append(w_dt if i == 2 else w_packed)
    out_specs = [row(n) for n in widths]
    out_shape = [jax.ShapeDtypeStruct((t, n), d) for n, d in zip(widths, dtypes)]
    if tiles_per_seq:
        n_seq = t // (tm * tiles_per_seq)
        in_specs.append(_const_spec((SSM_HEADS, D_MODEL)))
        args.append(w_dt.T)
        per_seq = lambda r, c: pl.BlockSpec((1, r, c), lambda i: (i // tiles_per_seq, 0, 0))
        out_specs += [per_seq(SUBLANES, CONV_DIM), pl.BlockSpec((SSM_HEADS, tm), lambda i: (0, i)),
                      per_seq(WINDOW, D_KV), per_seq(WINDOW, D_KV)]
        out_shape += [jax.ShapeDtypeStruct((n_seq, SUBLANES, CONV_DIM), F32),
                      jax.ShapeDtypeStruct((SSM_HEADS, t), F32),
                      jax.ShapeDtypeStruct((n_seq, WINDOW, D_KV), F32),
                      jax.ShapeDtypeStruct((n_seq, WINDOW, D_KV), F32)]
    return pl.pallas_call(
        functools.partial(_inproj_kernel, tiles_per_seq=tiles_per_seq),
        grid=(t // tm,),
        in_specs=in_specs,
        out_specs=out_specs,
        out_shape=out_shape,
        compiler_params=pltpu.CompilerParams(dimension_semantics=("arbitrary",), vmem_limit_bytes=VMEM_LIMIT),
        name="in_proj",
    )(*args)


def _ssd_kernel(xbc_ref, z_ref, dt_ref, dtT_ref, h0_ref, dtb_ref, dtbT_ref, a_ref, aT_ref, dsk_ref, nw_ref, e_ref,
                *rest, lc, n_valid, conv, first_chunk):
    if conv == "history":
        hist_ref, cw_ref, cb_ref, y_ref, h_ref, buf_ref = rest
    else:
        cw_ref, cb_ref, y_ref, h_ref, tail_ref = rest

    def at_first_chunk(fn):
        if first_chunk == "always":
            fn()
        elif first_chunk == "grid":
            pl.when(pl.program_id(1) == 0)(fn)

    @at_first_chunk
    def _():
        h_ref[...] = h0_ref[...]

    if conv == "history":
        @at_first_chunk
        def _():
            buf_ref[0:SUBLANES, :] = hist_ref[...]

        buf_ref[SUBLANES:SUBLANES + lc, :] = xbc_ref[...].astype(F32)
        act = _causal_conv_silu(buf_ref, lc, cw_ref, cb_ref)
    else:
        @at_first_chunk
        def _():
            tail_ref[...] = jnp.zeros_like(tail_ref)

        raw = xbc_ref[...]
        nt = tail_ref.shape[0]
        ext = jnp.concatenate([tail_ref[...], raw], axis=0)
        ne = nt + lc
        cw_b = cw_ref[...].astype(BF16)
        scaled = jnp.concatenate([ext * cw_b[j:j + 1, :] for j in range(CONV_W)], axis=0)
        to = lax.broadcasted_iota(I32, (lc, CONV_W * ne), 0)
        frm = lax.broadcasted_iota(I32, (lc, CONV_W * ne), 1)
        picks = jnp.zeros((lc, CONV_W * ne), F32)
        for j in range(CONV_W):
            picks = jnp.where(frm == to + (j * ne + nt - (CONV_W - 1) + j), 1.0, picks)
        conv_acc = cb_ref[...] + jnp.dot(picks.astype(BF16), scaled, preferred_element_type=F32)
        tail_ref[...] = raw[lc - nt:, :]
        act = conv_acc * jax.nn.sigmoid(conv_acc)
    xs = act[:, :D_INNER]
    bm = act[:, D_INNER:D_INNER + SSM_GROUPS * D_STATE].astype(BF16)
    cm = act[:, D_INNER + SSM_GROUPS * D_STATE:].astype(BF16)

    dt = _softplus(dt_ref[...] + dtb_ref[...])
    dtT_raw = dtT_ref[0] if len(dtT_ref.shape) == 3 else dtT_ref[...]
    dtT = _softplus(dtT_raw + dtbT_ref[...])
    if n_valid < lc:
        dt = jnp.where(lax.broadcasted_iota(I32, dt.shape, 0) < n_valid, dt, 0.0)
        dtT = jnp.where(lax.broadcasted_iota(I32, dtT.shape, 1) < n_valid, dtT, 0.0)
    la = dt * a_ref[...]
    laT = dtT * aT_ref[...]
    li = lax.broadcasted_iota(I32, (lc, lc), 0)
    si = lax.broadcasted_iota(I32, (lc, lc), 1)
    causal = li >= si
    tril = jnp.where(causal, 1.0, 0.0).astype(BF16)
    triu = jnp.where(li <= si, 1.0, 0.0).astype(BF16)
    cum = sum(jnp.dot(tril, p, preferred_element_type=F32) for p in _split3(la))
    cumT = sum(jnp.dot(p, triu, preferred_element_type=F32) for p in _split3(laT))
    ec = jnp.exp(cum)
    dte = jnp.exp(cum[lc - 1:lc, :] - cum)
    cd = jnp.exp(cumT[:, lc - 1:lc])

    def expand(v):
        hi, mid, _ = _split3(v)
        return jnp.dot(jnp.concatenate([hi, mid], axis=1), e_ref[...], preferred_element_type=F32)

    dt_x, ec_x, dte_x = expand(dt), expand(ec), expand(dte)
    xdt = xs * dt_x
    xdt_b = xdt.astype(BF16)
    xdte_b = (xdt * dte_x).astype(BF16)
    lane = lax.broadcasted_iota(I32, (lc, LANES), 1)
    low_half = lane < SSM_HEAD_DIM

    for g in range(SSM_GROUPS):
        gs = slice(g * GROUP_W, (g + 1) * GROUP_W)
        bm_g = bm[:, g * D_STATE:(g + 1) * D_STATE]
        cm_g = cm[:, g * D_STATE:(g + 1) * D_STATE]
        cb = lax.dot_general(cm_g, bm_g, NT, preferred_element_type=F32)
        cbm = jnp.where(causal, cb, 0.0)
        h_g = h_ref[0, g * SSM_HPG:(g + 1) * SSM_HPG].reshape(GROUP_W, D_STATE)
        y_off = lax.dot_general(cm_g, h_g.astype(BF16), NT, preferred_element_type=F32) * ec_x[:, gs]
        tiles = []
        for j in range(GROUP_W // LANES):
            col = g * GROUP_W + j * LANES
            x_pair = xdt_b[:, col:col + LANES]
            acc = None
            for half in range(2):
                h = col // SSM_HEAD_DIM + half
                seg = cum[:, h:h + 1] - cumT[h:h + 1, :]
                m = (cbm * jnp.exp(jnp.where(causal, seg, 0.0))).astype(BF16)
                x_h = jnp.where(low_half if half == 0 else jnp.logical_not(low_half), x_pair, jnp.zeros_like(x_pair))
                d = jnp.dot(m, x_h, preferred_element_type=F32)
                acc = d if acc is None else acc + d
            tiles.append(acc)
        y_g = jnp.concatenate(tiles, axis=1) + y_off + xs[:, gs] * dsk_ref[:, gs]
        zg = z_ref[:, gs].astype(F32)
        y_g = y_g * (zg * jax.nn.sigmoid(zg))
        y_g = y_g * lax.rsqrt(jnp.mean(y_g * y_g, axis=-1, keepdims=True) + EPS)
        y_ref[:, gs] = (y_g * nw_ref[:, gs]).astype(y_ref.dtype)
        st = lax.dot_general(xdte_b[:, gs], bm_g, TN, preferred_element_type=F32)
        for hh in range(SSM_HPG):
            h = g * SSM_HPG + hh
            rows = slice(hh * SSM_HEAD_DIM, (hh + 1) * SSM_HEAD_DIM)
            h_ref[0, h] = h_g[rows, :] * cd[h:h + 1, 0:1] + st[rows, :]


def _ssd_block_kernel(xbc_ref, z_ref, dt_ref, dtT_ref, h0_ref, *rest, spb, lc, n_consts, n_valid, conv):
    consts, rest = rest[:n_consts], rest[n_consts:]
    if conv == "history":
        hist_ref, cw_ref, cb_ref, y_ref, h_ref, scratch = rest
    else:
        cw_ref, cb_ref, y_ref, h_ref, scratch = rest
    for s in range(spb):
        rows, one = pl.ds(s * lc, lc), pl.ds(s, 1)
        conv_refs = (cw_ref, cb_ref)
        if conv == "history":
            conv_refs = (hist_ref.at[pl.ds(s * SUBLANES, SUBLANES)],) + conv_refs
        _ssd_kernel(xbc_ref.at[rows], z_ref.at[rows], dt_ref.at[rows], dtT_ref.at[one], h0_ref.at[one], *consts,
                    *conv_refs, y_ref.at[rows], h_ref.at[one], scratch,
                    lc=lc, n_valid=n_valid, conv=conv, first_chunk="always")


def _ssd_chain_kernel(xbc_ref, z_ref, dt_ref, dtT_ref, h0_ref, *rest, cps, lc, n_consts, n_valid):
    consts, (cw_ref, cb_ref, y_ref, h_ref, tail_ref) = rest[:n_consts], rest[n_consts:]
    for k in range(cps):
        rows = pl.ds(k * lc, lc)
        _ssd_kernel(xbc_ref.at[rows], z_ref.at[rows], dt_ref.at[rows], dtT_ref.at[:, rows], h0_ref, *consts,
                    cw_ref, cb_ref, y_ref.at[rows], h_ref, tail_ref,
                    lc=lc, n_valid=n_valid, conv="fresh", first_chunk="grid" if k == 0 else "never")


def _ssd(xbc, z, dt, dtT, h0, consts, nb, nc, lc, n_valid, conv_w, conv_b, hist=None, spb=1, cps=1):
    assert spb == 1 or (nc == 1 and dtT.ndim == 3 and nb % spb == 0 and cps == 1)
    assert cps == 1 or (hist is None and dtT.ndim == 2 and nc % cps == 0)
    t = xbc.shape[0]
    nc = nc // cps
    row = lambda n: pl.BlockSpec((spb * cps * lc, n), lambda b, c: (b * nc + c, 0))
    if dtT.ndim == 2:
        dtT_spec = pl.BlockSpec((SSM_HEADS, cps * lc), lambda b, c: (0, b * nc + c))
    else:
        dtT_spec = pl.BlockSpec((spb, SSM_HEADS, lc), lambda b, c: (b * nc + c, 0, 0))
    state_spec = pl.BlockSpec((spb, SSM_HEADS, SSM_HEAD_DIM, D_STATE), lambda b, c: (b, 0, 0, 0))
    in_specs = [row(CONV_DIM), row(D_INNER), row(SSM_HEADS), dtT_spec, state_spec] + [_const_spec(a.shape) for a in consts]
    args = [xbc, z, dt, dtT, h0, *consts]
    if hist is not None:
        in_specs.append(pl.BlockSpec((spb * SUBLANES, CONV_DIM), lambda b, c: (b, 0)))
        args.append(hist)
        scratch = [pltpu.VMEM((SUBLANES + lc, CONV_DIM), F32)]
    else:
        scratch = [pltpu.VMEM((PIECE, CONV_DIM), BF16)]
    in_specs += [_const_spec(conv_w.shape), _const_spec(conv_b.shape)]
    args += [conv_w, conv_b]
    conv = "history" if hist is not None else "fresh"
    if cps > 1:
        body = functools.partial(_ssd_chain_kernel, cps=cps, lc=lc, n_consts=len(consts), n_valid=n_valid)
    elif spb == 1:
        body = functools.partial(_ssd_kernel, lc=lc, n_valid=n_valid, conv=conv,
                                 first_chunk="always" if nc == 1 else "grid")
    else:
        body = functools.partial(_ssd_block_kernel, spb=spb, lc=lc, n_consts=len(consts), n_valid=n_valid, conv=conv)
    return pl.pallas_call(
        body,
        grid=(nb // spb, nc),
        in_specs=in_specs,
        out_specs=[row(D_INNER), state_spec],
        out_shape=[jax.ShapeDtypeStruct((t, D_INNER), BF16),
                   jax.ShapeDtypeStruct((nb, SSM_HEADS, SSM_HEAD_DIM, D_STATE), F32)],
        scratch_shapes=scratch,
        compiler_params=pltpu.CompilerParams(dimension_semantics=("arbitrary", "arbitrary"),
                                             vmem_limit_bytes=VMEM_LIMIT),
        name="ssd",
    )(*args)


def _swa_kernel(slope_ref, sink_ref, q_ref, kc_ref, vc_ref, kp_ref, vp_ref, y_ref, bias_ref, *, tq, prev_always):
    fold = tq == WINDOW
    nk = WINDOW + tq

    @pl.when((pl.program_id(0) == 0) & (pl.program_id(1) == 0))
    def _():
        rows = WINDOW if fold else nk
        j = lax.broadcasted_iota(I32, (rows, tq), 0)
        r = lax.broadcasted_iota(I32, (rows, tq), 1)
        if fold:
            dist = jnp.where(j > r, r + WINDOW - j, r - j)
            valid, from_previous = dist >= 0, j > r
        else:
            dist = r + WINDOW - j
            valid, from_previous = (dist >= 0) & (dist < WINDOW), j < WINDOW
        distf = dist.astype(F32)
        for h in range(N_Q_HEADS):
            penalty = -slope_ref[h] * distf
            bias_ref[h] = jnp.where(valid, penalty, NEG_BIG)
            if not prev_always:
                bias_ref[N_Q_HEADS + h] = jnp.where(valid & jnp.logical_not(from_previous), penalty, NEG_BIG)

    first = 0 if prev_always else jnp.where(pl.program_id(1) == 0, N_Q_HEADS, 0)
    lane = lax.broadcasted_iota(I32, (nk, LANES), 1)
    zeros_half = jnp.zeros((ATTN_HEAD_DIM, nk), BF16)
    sink_row = lax.broadcasted_iota(I32, (SUBLANES, tq), 0) == 0
    ones_keys = jnp.ones((nk + SUBLANES, LANES), BF16)
    if fold:
        ji = lax.broadcasted_iota(I32, (WINDOW, tq), 0)
        ri = lax.broadcasted_iota(I32, (WINDOW, tq), 1)
        from_prev = ji > ri
        from_prev_b = ji.astype(BF16) > ri.astype(BF16)
    heads = []
    for t in range(D_KV // LANES):
        cols = slice(t * LANES, (t + 1) * LANES)
        kt = jnp.concatenate([kp_ref[:, cols].astype(F32), kc_ref[:, cols].astype(F32)], axis=0)
        vt = jnp.concatenate([vp_ref[:, cols].astype(F32), vc_ref[:, cols].astype(F32)], axis=0)
        if fold:
            vt_t = vt.T.astype(BF16)
        for b in range(2):
            mine = (lane >= ATTN_HEAD_DIM) if b else (lane < ATTN_HEAD_DIM)
            k_same = jnp.where(mine, kt, 0.0)
            k_half = {b: k_same.astype(BF16), 1 - b: pltpu.roll(k_same, ATTN_HEAD_DIM, 1).astype(BF16)}
            if fold:
                v_g = vt_t[b * ATTN_HEAD_DIM:(b + 1) * ATTN_HEAD_DIM, :]
                v_half = {0: jnp.concatenate([v_g, zeros_half], axis=0),
                          1: jnp.concatenate([zeros_half, v_g], axis=0)}
            else:
                v_same = jnp.concatenate([jnp.where(mine, vt, 0.0), jnp.zeros((SUBLANES, LANES), F32)], axis=0)
                v_half = {b: v_same.astype(BF16), 1 - b: pltpu.roll(v_same, ATTN_HEAD_DIM, 1).astype(BF16)}
            for qi in range(Q_PER_KV):
                a = qi % 2
                heads.append((k_half[a], v_half[a]))

    q_tiles = [(q_ref[:, jq * LANES:(jq + 1) * LANES] * (ATTN_HEAD_DIM ** -0.5)).astype(BF16)
               for jq in range(D_ATTN // LANES)]

    def scores(h):
        k, q = heads[h][0], q_tiles[h // 2]
        s = lax.dot_general(k, q, NT, preferred_element_type=F32)
        if fold:
            s = jnp.where(from_prev, s[:WINDOW], s[WINDOW:])
        return s + bias_ref[first + h]

    def attend(h, s):
        sink = sink_ref[h]
        v = heads[h][1]
        m = jnp.maximum(jnp.max(s, axis=0, keepdims=True), sink)
        e = jnp.exp(s - m)
        e_sink = jnp.exp(sink - m)
        if fold:
            rden = 1.0 / (jnp.sum(e, axis=0, keepdims=True) + e_sink)
            e_b = e.astype(BF16)
            zero = jnp.zeros_like(e_b)
            p = jnp.concatenate([jnp.where(from_prev_b, e_b, zero), jnp.where(from_prev_b, zero, e_b)], axis=0)
            return jnp.dot(v, p, preferred_element_type=F32) * rden
        p = jnp.concatenate([e, jnp.where(sink_row, e_sink, 0.0)], axis=0).astype(BF16)
        num = lax.dot_general(p, v, TN, preferred_element_type=F32)
        den = lax.dot_general(p, ones_keys, TN, preferred_element_type=F32)
        return num / den

    out_tiles = [None] * (D_ATTN // LANES)
    s_next = scores(0)
    for h in range(N_Q_HEADS):
        s_cur = s_next
        if h + 1 < N_Q_HEADS:
            s_next = scores(h + 1)
        o = attend(h, s_cur)
        out_tiles[h // 2] = o if out_tiles[h // 2] is None else out_tiles[h // 2] + o
    for jq, o in enumerate(out_tiles):
        y_ref[:, jq * LANES:(jq + 1) * LANES] = (o.T if fold else o).astype(y_ref.dtype)


def _swa(slopes, sinks, q, k, v, k_prev, v_prev, nb, nblk, tq, prev_always):
    t = q.shape[0]
    cur = lambda n: pl.BlockSpec((tq, n), lambda b, i: (b * nblk + i, 0))
    if prev_always:
        prev = pl.BlockSpec((WINDOW, D_KV), lambda b, i: (b, 0))
    else:
        prev = pl.BlockSpec((WINDOW, D_KV), lambda b, i: (b * nblk + jnp.maximum(i - 1, 0), 0))
    smem = pl.BlockSpec(memory_space=pltpu.SMEM)
    return pl.pallas_call(
        functools.partial(_swa_kernel, tq=tq, prev_always=prev_always),
        grid=(nb, nblk),
        in_specs=[smem, smem, cur(D_ATTN), cur(D_KV), cur(D_KV), prev, prev],
        out_specs=cur(D_ATTN),
        out_shape=jax.ShapeDtypeStruct((t, D_ATTN), BF16),
        scratch_shapes=[pltpu.VMEM(((1 if prev_always else 2) * N_Q_HEADS, WINDOW if tq == WINDOW else WINDOW + tq,
                                    tq), F32)],
        compiler_params=pltpu.CompilerParams(dimension_semantics=("arbitrary", "arbitrary"),
                                             vmem_limit_bytes=VMEM_LIMIT),
        name="swa",
    )(slopes, sinks, q, k, v, k_prev, v_prev)


def _post_kernel(x_ref, ys_ref, ya_ref, g_ref, wsp_ref, wap_ref, wo_ref, nw_ref, wr_ref, br_ref, *rest,
                 n_exp, lr, nt, aliased):
    if aliased:
        rest = rest[1:]
    h_ref, xs_ref, lpos_ref, pr_ref, pc_ref = rest
    tp = x_ref.shape[0]

    @pl.when(pl.program_id(0) >= nt)
    def _():
        xs_ref[...] = jnp.zeros_like(xs_ref)

    @pl.when(pl.program_id(0) < nt)
    def _():
        a = jnp.dot(ys_ref[...].astype(BF16), wsp_ref[...], preferred_element_type=F32)
        b = jnp.dot(ya_ref[...].astype(BF16), wap_ref[...], preferred_element_type=F32)
        g = g_ref[...].astype(F32)
        merged = jax.nn.sigmoid(g[:, :D_MODEL]) * a + jax.nn.sigmoid(g[:, D_MODEL:]) * b
        h = x_ref[...] + jnp.dot(merged.astype(BF16), wo_ref[...], preferred_element_type=F32)
        h_ref[...] = h
        hn = h * lax.rsqrt(jnp.mean(h * h, axis=-1, keepdims=True) + EPS) * nw_ref[...]
        w_hi, w_mid, _ = _split3(wr_ref[...])
        x_hi, x_mid, _ = _split3(hn)
        both = lax.dot_general(jnp.concatenate([w_hi, w_mid], axis=0), x_hi, NT, preferred_element_type=F32)
        logits = (both[:n_exp] + lax.dot_general(w_hi, x_mid, NT, preferred_element_type=F32)
                  + both[n_exp:]) + br_ref[...]
        eidx = lax.broadcasted_iota(I32, logits.shape, 0).astype(F32)
        work = logits
        vals, ids = [], []
        for _ in range(TOP_K):
            m = jnp.max(work, axis=0, keepdims=True)
            first = jnp.min(jnp.where(work == m, eidx, float(n_exp)), axis=0, keepdims=True)
            vals.append(m)
            ids.append(first)
            work = jnp.where(eidx == first, -jnp.inf, work)
        es = [jnp.exp(v - vals[0]) for v in vals]
        den = es[0] + es[1] + es[2] + es[3]
        eye = jnp.where(lax.broadcasted_iota(I32, (TOP_K, TOP_K), 0) == lax.broadcasted_iota(I32, (TOP_K, TOP_K), 1),
                        1.0, 0.0).astype(BF16)

        def to_columns(rows):
            return sum(lax.dot_general(p, eye, TN, preferred_element_type=F32) for p in _split3(rows))

        onehot = [jnp.where(eidx == i, 1.0, 0.0) for i in ids]
        counts = [jnp.sum(o, axis=1, keepdims=True) for o in onehot]
        total = counts[0] + counts[1] + counts[2] + counts[3]
        padded = jnp.floor((total + (PIECE - 1)) * (1.0 / PIECE)) * PIECE
        ei = lax.broadcasted_iota(I32, (n_exp, n_exp), 0)
        ej = lax.broadcasted_iota(I32, (n_exp, n_exp), 1)
        below = jnp.where(ej < ei, 1.0, 0.0).astype(BF16)
        padded_b = jnp.broadcast_to(padded, (n_exp, LANES))
        seg_off = sum(jnp.dot(below, p, preferred_element_type=F32) for p in _split3(padded_b))[:, 0:1]
        ti = lax.broadcasted_iota(I32, (tp, tp), 0)
        tj = lax.broadcasted_iota(I32, (tp, tp), 1)
        before = jnp.where(ti < tj, 1.0, 0.0).astype(BF16)
        base = seg_off
        lpos = []
        prefixes = jnp.dot(jnp.concatenate(onehot, axis=0).astype(BF16), before,
                           preferred_element_type=F32)
        for k in range(TOP_K):
            prefix = prefixes[k * n_exp:(k + 1) * n_exp]
            lpos.append(jnp.sum(onehot[k] * (base + prefix), axis=0, keepdims=True))
            base = base + counts[k]
        pc_ref[0] = padded_b.astype(I32)

        hn_b = hn.astype(BF16)
        rc = lr // 4
        for c in range(4):
            ri = (lax.broadcasted_iota(I32, (rc, tp), 0) + c * rc).astype(F32)
            sel = jnp.zeros((rc, tp), F32)
            for k in range(TOP_K):
                sel = jnp.where(ri == lpos[k], 1.0, sel)
            xs_ref[c * rc:(c + 1) * rc, :] = jnp.dot(sel.astype(BF16), hn_b, preferred_element_type=F32).astype(BF16)

        pr_ref[...] = to_columns(jnp.concatenate([e / den for e in es], axis=0))
        lpos_ref[...] = to_columns(jnp.concatenate(lpos, axis=0)).astype(I32)


def _post(x, y_ssm, y_attn, gates, wsp, wap, wo, ffn_nw, w_rT, b_r, tm, lr, xs_rows, xs_block0, pad_steps,
          xs_prev=None):
    t = x.shape[0]
    n_exp = w_rT.shape[0]
    nt = t // tm
    last = nt - 1
    row = lambda n: pl.BlockSpec((tm, n), lambda i: (jnp.minimum(i, last), 0))
    col = row(TOP_K)
    in_specs = [row(D_MODEL), row(D_INNER), row(D_ATTN), row(2 * D_MODEL),
                _resident_spec(wsp.shape), _resident_spec(wap.shape), _resident_spec(wo.shape),
                _const_spec(ffn_nw.shape), _const_spec(w_rT.shape), _const_spec(b_r.shape)]
    args = [x, y_ssm, y_attn, gates, wsp, wap, wo, ffn_nw, w_rT, b_r]
    aliases = {}
    if xs_prev is not None:
        in_specs.append(pl.BlockSpec(memory_space=pl.ANY))
        args.append(xs_prev)
        aliases = {len(args) - 1: 1}
    return pl.pallas_call(
        functools.partial(_post_kernel, n_exp=n_exp, lr=lr, nt=nt, aliased=xs_prev is not None),
        grid=(nt + pad_steps,),
        in_specs=in_specs,
        out_specs=[row(D_MODEL), pl.BlockSpec((lr, D_MODEL), lambda i: (xs_block0 + i, 0)), col, col,
                   pl.BlockSpec((1, n_exp, LANES), lambda i: (jnp.minimum(i, last), 0, 0))],
        out_shape=[jax.ShapeDtypeStruct((t, D_MODEL), F32), jax.ShapeDtypeStruct((xs_rows, D_MODEL), BF16),
                   jax.ShapeDtypeStruct((t, TOP_K), I32), jax.ShapeDtypeStruct((t, TOP_K), F32),
                   jax.ShapeDtypeStruct((nt, n_exp, LANES), I32)],
        input_output_aliases=aliases,
        compiler_params=pltpu.CompilerParams(dimension_semantics=("arbitrary",), vmem_limit_bytes=VMEM_LIMIT),
        name="post",
    )(*args)


def _piece(ref, p):
    return ref.at[pl.ds(pl.multiple_of(p * PIECE, PIECE), PIECE)]


def _moe_kernel(te_ref, nu_ref, src_ref, xs_hbm, wg_ref, wu_ref, wd_ref, b_ref, ys_hbm, xbuf, obuf, wbf, gsem, ssem):
    i = pl.program_id(0)
    n_used = nu_ref[0]
    slot = i % 2
    tme = PIECES_PER_TILE * PIECE

    def gather(tile, s):
        for r in range(PIECES_PER_TILE):
            pltpu.make_async_copy(_piece(xs_hbm, src_ref[tile * PIECES_PER_TILE + r]),
                                  xbuf.at[s, pl.ds(r * PIECE, PIECE)], gsem.at[s]).start()

    def wait_gather(s):
        pltpu.make_async_copy(xs_hbm.at[pl.ds(0, tme)], xbuf.at[s], gsem.at[s]).wait()

    def wait_put(s):
        pltpu.make_async_copy(obuf.at[s], ys_hbm.at[pl.ds(0, tme)], ssem.at[s]).wait()

    @pl.when(i == 0)
    def _():
        gather(0, 0)

    @pl.when(i < n_used)
    def _():
        gather(i + 1, 1 - slot)
        wait_gather(slot)

        @pl.when(i >= 2)
        def _():
            wait_put(slot)

        @pl.when((i == 0) | (te_ref[i] != te_ref[jnp.maximum(i - 1, 0)]))
        def _():
            wbf[0] = wg_ref[0].astype(BF16)
            wbf[1] = wu_ref[0].astype(BF16)
            wbf[2] = wd_ref[0].astype(BF16)

        x = xbuf[slot]
        g = jnp.minimum(jnp.dot(x, wbf[0], preferred_element_type=F32) + b_ref[0, 0:1, :], SWIGLU_LIMIT)
        u = jnp.clip(jnp.dot(x, wbf[1], preferred_element_type=F32) + b_ref[0, 1:2, :], -SWIGLU_LIMIT, SWIGLU_LIMIT)
        act = ((u + 1.0) * g * jax.nn.sigmoid(SWIGLU_ALPHA * g)).astype(BF16)
        obuf[slot] = (jnp.dot(act, wbf[2], preferred_element_type=F32) + b_ref[0, 2:3, :]).astype(BF16)
        for r in range(PIECES_PER_TILE):
            pltpu.make_async_copy(obuf.at[slot, pl.ds(r * PIECE, PIECE)],
                                  _piece(ys_hbm, src_ref[i * PIECES_PER_TILE + r]), ssem.at[slot]).start()

        @pl.when(i == n_used - 1)
        def _():
            wait_put(slot)
            wait_gather(1 - slot)

            @pl.when(i >= 1)
            def _():
                wait_put(1 - slot)


def _moe(tile_expert, n_used, src, xs, wg, wu, wd, biases):
    n_tiles = tile_expert.shape[0]
    tme = PIECES_PER_TILE * PIECE
    wspec = pl.BlockSpec((1, D_MODEL, D_FF), lambda i, te, nu, sr: (te[i], 0, 0))
    grid_spec = pltpu.PrefetchScalarGridSpec(
        num_scalar_prefetch=3,
        grid=(n_tiles,),
        in_specs=[pl.BlockSpec(memory_space=pl.ANY), wspec, wspec, wspec,
                  pl.BlockSpec((1, 3, D_FF), lambda i, te, nu, sr: (te[i], 0, 0))],
        out_specs=pl.BlockSpec(memory_space=pl.ANY),
        scratch_shapes=[pltpu.VMEM((2, tme, D_MODEL), BF16), pltpu.VMEM((2, tme, D_MODEL), BF16),
                        pltpu.VMEM((3, D_MODEL, D_FF), BF16),
                        pltpu.SemaphoreType.DMA((2,)), pltpu.SemaphoreType.DMA((2,))],
    )
    return pl.pallas_call(
        _moe_kernel,
        grid_spec=grid_spec,
        out_shape=jax.ShapeDtypeStruct(xs.shape, xs.dtype),
        input_output_aliases={3: 0},
        compiler_params=pltpu.CompilerParams(dimension_semantics=("arbitrary",), vmem_limit_bytes=VMEM_LIMIT),
        name="moe",
    )(tile_expert, n_used, src, xs, wg, wu, wd, biases)


def _combine_kernel(h_ref, ys_ref, lpos_ref, pr_ref, nw_ref, o_ref):
    tp = h_ref.shape[0]
    lr = ys_ref.shape[0]
    ri = lax.broadcasted_iota(I32, (tp, lr), 1)
    lp = lpos_ref[...]
    pr = pr_ref[...]
    pw = jnp.zeros((tp, lr), F32)
    for k in range(TOP_K):
        pw = jnp.where(ri == lp[:, k:k + 1], pr[:, k:k + 1], pw)
    moe = jnp.dot(pw.astype(BF16), ys_ref[...], preferred_element_type=F32)
    h = h_ref[...] + moe
    o_ref[...] = h * lax.rsqrt(jnp.mean(h * h, axis=-1, keepdims=True) + EPS) * nw_ref[...]


def _combine(h, ys, lpos_t, probs_t, final_nw, tm, lr, ys_block0):
    t = h.shape[0]
    return pl.pallas_call(
        _combine_kernel,
        grid=(t // tm,),
        in_specs=[pl.BlockSpec((tm, D_MODEL), lambda i: (i, 0)),
                  pl.BlockSpec((lr, D_MODEL), lambda i: (ys_block0 + i, 0)),
                  pl.BlockSpec((tm, TOP_K), lambda i: (i, 0)),
                  pl.BlockSpec((tm, TOP_K), lambda i: (i, 0)),
                  _const_spec(final_nw.shape)],
        out_specs=pl.BlockSpec((tm, D_MODEL), lambda i: (i, 0)),
        out_shape=jax.ShapeDtypeStruct((t, D_MODEL), F32),
        compiler_params=pltpu.CompilerParams(dimension_semantics=("arbitrary",), vmem_limit_bytes=VMEM_LIMIT),
        name="combine",
    )(h, ys, lpos_t, probs_t, final_nw)


def _piece_table(padded_counts, tile_row0, n_tiles, spare_piece0):
    n_pieces = (padded_counts // PIECE).T
    seg_row = tile_row0[:, None] + jnp.cumsum(padded_counts, axis=1) - padded_counts
    seg_piece = (seg_row // PIECE).T
    per_expert = n_pieces.sum(axis=1)
    tiles_per = (per_expert + PIECES_PER_TILE - 1) // PIECES_PER_TILE
    tile_end = jnp.cumsum(tiles_per)
    n_used = tile_end[-1]
    slot0 = (tile_end - tiles_per) * PIECES_PER_TILE
    seg_slot = (slot0[:, None] + jnp.cumsum(n_pieces, axis=1) - n_pieces).reshape(-1)
    seg_n = n_pieces.reshape(-1)
    seg_src = seg_piece.reshape(-1)
    slots = jnp.arange(n_tiles * PIECES_PER_TILE, dtype=I32)

    def at_segment_of_slot(f):
        df = f - jnp.concatenate([jnp.zeros((1,), I32), f[:-1]])
        return jnp.cumsum(jnp.zeros(slots.shape, I32).at[seg_slot].add(df))

    real = slots < at_segment_of_slot(seg_slot + seg_n)
    padding = jnp.logical_not(real) & (slots < n_used * PIECES_PER_TILE)
    spare = spare_piece0 + jnp.where(padding, jnp.cumsum(padding.astype(I32)), 0)
    src = jnp.where(real, slots + at_segment_of_slot(seg_src - seg_slot), spare).astype(I32)
    tile_ids = jnp.arange(n_tiles, dtype=I32)
    tile_expert = jnp.sum((tile_end[None, :] <= jnp.minimum(tile_ids, n_used - 1)[:, None]).astype(I32), axis=1)
    return tile_expert.astype(I32), n_used.reshape(1).astype(I32), src


def _pick_tile(n, pref):
    while n % pref:
        pref //= 2
    return pref


def _local_rows(tm, n_exp):
    need = TOP_K * tm + n_exp * (PIECE - 1)
    return -(-need // 64) * 64


def kernel(x_prompt, x_sample, state_conv, state_ssm, cache_swa_k, cache_swa_v, attn_norm_w, w_in, conv_w, conv_b, dt_bias, a_log, d_skip, ssm_norm_w, attn_sinks, w_ssm_proj, w_attn_proj, w_o, ffn_norm_w, w_router, b_router, w_gate, b_gate, w_up, b_up, w_down, b_down, final_norm_w):
    assert w_in.shape[0] == 1, "single-layer step"
    nb, seq, _ = x_prompt.shape
    nbs = x_sample.shape[0]
    n_exp = w_router.shape[-1]
    tp, ts = nb * seq, nbs
    pad = SUBLANES

    w_packed, w_blocks = _pack_in_proj_weight(w_in[0])
    attn_nw = attn_norm_w[0].reshape(1, D_MODEL)
    a_neg = -jnp.exp(a_log[0].astype(F32))
    head_of = jnp.arange(D_INNER, dtype=I32) // SSM_HEAD_DIM
    expand = jnp.tile((jnp.arange(SSM_HEADS, dtype=I32)[:, None] == head_of[None, :]).astype(BF16), (2, 1))
    conv_consts = (conv_w[0], conv_b[0].reshape(1, CONV_DIM))
    ssd_consts = (dt_bias[0].reshape(1, SSM_HEADS), dt_bias[0].reshape(SSM_HEADS, 1),
                  a_neg.reshape(1, SSM_HEADS), a_neg.reshape(SSM_HEADS, 1),
                  d_skip[0][head_of].reshape(1, D_INNER), ssm_norm_w[0].reshape(1, D_INNER), expand)
    slopes = jnp.exp2(-8.0 * jnp.arange(1, N_Q_HEADS + 1, dtype=F32) / N_Q_HEADS)
    sinks = attn_sinks[0].astype(F32)
    wsp, wap, wo = w_ssm_proj[0].astype(BF16), w_attn_proj[0].astype(BF16), w_o[0].astype(BF16)
    ffn_nw = ffn_norm_w[0].reshape(1, D_MODEL)
    w_rT = w_router[0].T
    b_r = b_router[0].reshape(n_exp, 1)
    expert_biases = jnp.stack([b_gate[0], b_up[0], b_down[0]], axis=1)
    final_nw = final_norm_w.reshape(1, D_MODEL)

    xp = x_prompt.reshape(tp, D_MODEL)
    tm_in = _pick_tile(seq, 512)
    z, xbc, dt, q, k, v, gates, conv_tail, dtT, k_tail, v_tail = _in_proj(
        xp, attn_nw, w_packed, w_blocks, tm_in, BF16, tiles_per_seq=seq // tm_in)
    nc = seq // CHUNK
    y_ssm, ssm_p = _ssd(xbc, z, dt, dtT, jnp.zeros((nb, SSM_HEADS, SSM_HEAD_DIM, D_STATE), F32), ssd_consts,
                        nb, nc, CHUNK, CHUNK, *conv_consts, cps=_pick_tile(nc, SSD_CHUNKS_PER_STEP))
    nblk = seq // WINDOW
    y_attn = _swa(slopes, sinks, q, k, v, k, v, nb, nblk, WINDOW, False)

    xs_pad = jnp.pad(x_sample.reshape(ts, 1, D_MODEL), ((0, 0), (0, pad - 1), (0, 0))).reshape(ts * pad, D_MODEL)
    z_s, xbc_s, dt_s, q_s, k_s, v_s, gates_s = _in_proj(xs_pad, attn_nw, w_packed, w_blocks,
                                                        _pick_tile(ts * pad, 256), F32)
    dtT_s = dt_s.reshape(ts, pad, SSM_HEADS).transpose(0, 2, 1)
    spb = _pick_tile(ts, SAMPLE_SEQS_PER_STEP)
    hist_s = jnp.pad(state_conv[0], ((0, 0), (pad - (CONV_W - 1), 0), (0, 0))).reshape(ts * pad, CONV_DIM)
    y_ssm_s, ssm_s = _ssd(xbc_s, z_s, dt_s, dtT_s, state_ssm[0], ssd_consts, ts, 1, pad, 1, *conv_consts,
                          hist=hist_s, spb=spb)
    kc = cache_swa_k[0].reshape(ts * WINDOW, D_KV)
    vc = cache_swa_v[0].reshape(ts * WINDOW, D_KV)
    y_attn_s = _swa(slopes, sinks, q_s, k_s, v_s, kc, vc, ts, 1, pad, True)
    real = lambda a: a.reshape(ts, pad, -1)[:, 0]

    tm_p, tm_s = _pick_tile(tp, 512), ts
    nt_p = tp // tm_p
    lr_p, lr_s = _local_rows(tm_p, n_exp), _local_rows(tm_s, n_exp)
    if (nt_p * lr_p) % lr_s or lr_s > lr_p:
        lr_s = lr_p
    spare_rows = lr_s + (1 + n_exp * (PIECES_PER_TILE - 1)) * PIECE
    pad_steps = -(-spare_rows // lr_p)
    xs_rows = (nt_p + pad_steps) * lr_p
    block_s = nt_p * lr_p // lr_s
    post_w = (wsp, wap, wo, ffn_nw, w_rT, b_r)
    h_p, xs, lpos_p, pr_p, pc_p = _post(xp, y_ssm, y_attn, gates, *post_w, tm_p, lr_p, xs_rows, 0, pad_steps)
    h_s, xs, lpos_s, pr_s, pc_s = _post(x_sample.reshape(ts, D_MODEL), real(y_ssm_s), real(y_attn_s), real(gates_s),
                                        *post_w, tm_s, lr_s, xs_rows, block_s, 0, xs_prev=xs)

    padded_counts = jnp.concatenate([pc_p[:, :, 0], pc_s[:, :, 0]], axis=0)
    tile_row0 = jnp.concatenate([jnp.arange(nt_p, dtype=I32) * lr_p, jnp.full((1,), nt_p * lr_p, I32)])
    max_pieces = (TOP_K * (tp + ts) + (PIECE - 1) * n_exp * (nt_p + 1)) // PIECE + n_exp * (PIECES_PER_TILE - 1)
    n_tiles = -(-max_pieces // PIECES_PER_TILE) + 1
    tile_expert, n_used, src = _piece_table(padded_counts, tile_row0, n_tiles, (nt_p * lr_p + lr_s) // PIECE)
    ys = _moe(tile_expert, n_used, src, xs, w_gate[0], w_up[0], w_down[0], expert_biases)
    out_p = _combine(h_p, ys, lpos_p, pr_p, final_nw, tm_p, lr_p, 0)
    out_s = _combine(h_s, ys, lpos_s, pr_s, final_nw, tm_s, lr_s, block_s)

    y_prompt = out_p.reshape(nb, seq, D_MODEL)
    y_sample = out_s.reshape(nbs, 1, D_MODEL)
    conv_p = conv_tail[:, SUBLANES - (CONV_W - 1):][None]
    k_p = k_tail.reshape(1, nb, WINDOW, N_KV_HEADS, ATTN_HEAD_DIM)
    v_p = v_tail.reshape(1, nb, WINDOW, N_KV_HEADS, ATTN_HEAD_DIM)
    conv_s = jnp.concatenate([state_conv[0][:, 1:], real(xbc_s)[:, None]], axis=1)[None]
    k_new = real(k_s).reshape(ts, 1, N_KV_HEADS, ATTN_HEAD_DIM)
    v_new = real(v_s).reshape(ts, 1, N_KV_HEADS, ATTN_HEAD_DIM)
    ks_out = jnp.concatenate([cache_swa_k[0][:, 1:], k_new], axis=1)[None]
    vs_out = jnp.concatenate([cache_swa_v[0][:, 1:], v_new], axis=1)[None]
    return (y_prompt, y_sample, conv_p, ssm_p[None], k_p, v_p, conv_s, ssm_s[None], ks_out, vs_out)
```
